```python
import math
import jax, jax.numpy as jnp
from jax import lax
import numpy as np

D_MODEL = 1024
BATCH = 8
SEQ = 8192
DEPTH = 4

N_MIXERS = 3
N_META = 16
RMS_EPS = 1e-6
N_HEADS = 16
N_KV_HEADS = 4
HEAD_DIM = 64
GROUP = N_HEADS // N_KV_HEADS
WINDOW = 128
BLOCK = 128
N_BUCKETS = 32
MAX_DISTANCE = 128
CONV_WIDTH = 3
POOL_WINDOWS = (2, 4, 8, 16)
N_POOL_GROUPS = len(POOL_WINDOWS)
POOL_GROUP_DIM = D_MODEL // N_POOL_GROUPS
D_FF = ((8 * D_MODEL + 3 * 256 - 1) // (3 * 256)) * 256
N_ATTN = len(range(0, DEPTH, N_MIXERS))
N_CONV = len(range(1, DEPTH, N_MIXERS))
N_POOL = len(range(2, DEPTH, N_MIXERS))

kernel_name = "hybrid_swa_sink_shortconv_pool_decoder"


def rms_norm(x, g):
    xf = x.astype(jnp.float32)
    y = xf * lax.rsqrt(jnp.mean(xf * xf, axis=-1, keepdims=True) + RMS_EPS)
    return (y * g.astype(jnp.float32)).astype(x.dtype)


def rel_bucket(dist):
    max_exact = N_BUCKETS // 2
    d = jnp.maximum(dist, 0)
    df = jnp.maximum(d, 1).astype(jnp.float32)
    large = max_exact + (jnp.log(df / max_exact) / math.log(MAX_DISTANCE / max_exact)
                         * (N_BUCKETS - max_exact)).astype(jnp.int32)
    large = jnp.minimum(large, N_BUCKETS - 1)
    return jnp.where(d < max_exact, d, large)


def rel_bias(rel_table, dist):
    b = rel_table.astype(jnp.float32)[rel_bucket(dist)]
    return jnp.moveaxis(b, -1, 0)


def sliding_window_attention(h, w_qkv, b_qkv, w_o, b_o, sinks, rel_table):
    bsz, L, _ = h.shape
    S = L - N_META
    nb = S // BLOCK
    qkv = h @ w_qkv + b_qkv
    q, k, v = jnp.split(qkv, [N_HEADS * HEAD_DIM, (N_HEADS + N_KV_HEADS) * HEAD_DIM], axis=-1)
    q = q.reshape(bsz, L, N_KV_HEADS, GROUP, HEAD_DIM) * (HEAD_DIM ** -0.5)
    k = k.reshape(bsz, L, N_KV_HEADS, HEAD_DIM)
    v = v.reshape(bsz, L, N_KV_HEADS, HEAD_DIM)
    qm, qr = q[:, :N_META], q[:, N_META:]
    km, kr = k[:, :N_META], k[:, N_META:]
    vm, vr = v[:, :N_META], v[:, N_META:]
    sink = sinks.astype(jnp.float32).reshape(N_KV_HEADS, GROUP)

    im = jnp.arange(N_META)
    dist_mm = im[:, None] - im[None, :]
    s_mm = jnp.einsum('bqkgd,bmkd->bkgqm', qm, km).astype(jnp.float32)
    s_mm = s_mm + rel_bias(rel_table, dist_mm).reshape(N_KV_HEADS, GROUP, N_META, N_META)
    s_mm = jnp.where(dist_mm >= 0, s_mm, -jnp.inf)
    sink_mm = jnp.broadcast_to(sink[None, :, :, None, None], s_mm.shape[:-1] + (1,))
    p_mm = jax.nn.softmax(jnp.concatenate([s_mm, sink_mm], axis=-1), axis=-1)[..., :N_META]
    o_m = jnp.einsum('bkgqm,bmkd->bqkgd', p_mm.astype(v.dtype), vm)
    o_m = o_m.reshape(bsz, N_META, N_HEADS * HEAD_DIM)

    qb = qr.reshape(bsz, nb, BLOCK, N_KV_HEADS, GROUP, HEAD_DIM)
    kb = jnp.pad(kr, ((0, 0), (BLOCK, 0), (0, 0), (0, 0))).reshape(bsz, nb + 1, BLOCK, N_KV_HEADS, HEAD_DIM)
    vb = jnp.pad(vr, ((0, 0), (BLOCK, 0), (0, 0), (0, 0))).reshape(bsz, nb + 1, BLOCK, N_KV_HEADS, HEAD_DIM)
    k_band = jnp.concatenate([kb[:, :-1], kb[:, 1:]], axis=2)
    v_band = jnp.concatenate([vb[:, :-1], vb[:, 1:]], axis=2)

    iq = jnp.arange(BLOCK)[:, None]
    jk = jnp.arange(2 * BLOCK)[None, :]
    dist_band = BLOCK + iq - jk
    blk = jnp.arange(nb)[:, None, None]
    valid = (dist_band >= 0) & (dist_band < WINDOW) & ((blk > 0) | (jk >= BLOCK))
    bias_band = rel_bias(rel_table, dist_band).reshape(N_KV_HEADS, GROUP, BLOCK, 2 * BLOCK)

    qpos = N_META + jnp.arange(nb)[:, None] * BLOCK + jnp.arange(BLOCK)[None, :]
    dist_meta = qpos[:, :, None] - im[None, None, :]
    bias_meta = jnp.moveaxis(rel_bias(rel_table, dist_meta), 0, 1)
    bias_meta = bias_meta.reshape(nb, N_KV_HEADS, GROUP, BLOCK, N_META)

    s_band = jnp.einsum('bnqkgd,bnskd->bnkgqs', qb, k_band).astype(jnp.float32) + bias_band
    s_band = jnp.where(valid[None, :, None, None], s_band, -jnp.inf)
    s_meta = jnp.einsum('bnqkgd,bmkd->bnkgqm', qb, km).astype(jnp.float32) + bias_meta[None]
    sink_b = jnp.broadcast_to(sink[None, None, :, :, None, None], s_band.shape[:-1] + (1,))
    p = jax.nn.softmax(jnp.concatenate([s_meta, s_band, sink_b], axis=-1), axis=-1)
    p_meta = p[..., :N_META].astype(v.dtype)
    p_band = p[..., N_META:N_META + 2 * BLOCK].astype(v.dtype)
    o_r = (jnp.einsum('bnkgqm,bmkd->bnqkgd', p_meta, vm)
           + jnp.einsum('bnkgqs,bnskd->bnqkgd', p_band, v_band))
    o_r = o_r.reshape(bsz, S, N_HEADS * HEAD_DIM)

    o = jnp.concatenate([o_m, o_r], axis=1)
    return o @ w_o + b_o


def short_conv_mixer(h, w_in, conv_w, w_out):
    L = h.shape[1]
    gate_b, gate_c, u = jnp.split(h @ w_in, 3, axis=-1)
    z = gate_c * u
    zp = jnp.pad(z, ((0, 0), (CONV_WIDTH - 1, 0), (0, 0)))
    conv = sum(conv_w[t] * zp[:, t:t + L] for t in range(CONV_WIDTH))
    return (gate_b * conv) @ w_out


def pooling_mixer(h, w_pool, scale):
    bsz, L, D = h.shape
    hf = h.astype(jnp.float32).reshape(bsz, L, N_POOL_GROUPS, POOL_GROUP_DIM)
    cs = jnp.pad(lax.cumsum(hf, axis=1), ((0, 0), (1, 0), (0, 0), (0, 0)))
    t = jnp.arange(L)[:, None]
    win = jnp.array(POOL_WINDOWS, dtype=jnp.int32)[None, :]
    lo = jnp.maximum(t + 1 - win, 0)
    count = jnp.minimum(win, t + 1).astype(jnp.float32)
    lower = cs[:, lo, jnp.arange(N_POOL_GROUPS)[None, :], :]
    mix = (cs[:, 1:] - lower) / count[None, :, :, None] - hf
    out = jnp.einsum('blgc,gcd->blgd', mix.astype(h.dtype), w_pool).reshape(bsz, L, D)
    return out * scale


def swiglu(h, w_gate, w_up, w_down):
    return (jax.nn.silu(h @ w_gate) * (h @ w_up)) @ w_down


def _fwd_setup_inputs(seed: int = 0) -> dict:
    key = jax.random.key(seed)
    ks = jax.random.split(key, 20)
    f32 = jnp.float32
    qkv_out = (N_HEADS + 2 * N_KV_HEADS) * HEAD_DIM
    nrm = lambda k, shape, s: jax.random.normal(k, shape, f32) * s
    return {
        "x": nrm(ks[0], (BATCH, SEQ, D_MODEL), 1.0),
        "meta_tokens": nrm(ks[1], (N_META, D_MODEL), 1.0),
        "rel_bias_table": nrm(ks[2], (N_BUCKETS, N_HEADS), 0.5),
        "norm_mix": 1.0 + nrm(ks[3], (DEPTH, D_MODEL), 0.02),
        "norm_ffn": 1.0 + nrm(ks[4], (DEPTH, D_MODEL), 0.02),
        "norm_final": 1.0 + nrm(ks[5], (D_MODEL,), 0.02),
        "attn_w_qkv": nrm(ks[6], (N_ATTN, D_MODEL, qkv_out), D_MODEL ** -0.5),
        "attn_b_qkv": nrm(ks[7], (N_ATTN, qkv_out), 0.02),
        "attn_w_o": nrm(ks[8], (N_ATTN, N_HEADS * HEAD_DIM, D_MODEL), (N_HEADS * HEAD_DIM) ** -0.5),
        "attn_b_o": nrm(ks[9], (N_ATTN, D_MODEL), 0.02),
        "attn_sinks": nrm(ks[10], (N_ATTN, N_HEADS), 1.0),
        "conv_w_in": nrm(ks[11], (N_CONV, D_MODEL, 3 * D_MODEL), D_MODEL ** -0.5),
        "conv_w": nrm(ks[12], (N_CONV, CONV_WIDTH, D_MODEL), CONV_WIDTH ** -0.5),
        "conv_w_out": nrm(ks[13], (N_CONV, D_MODEL, D_MODEL), D_MODEL ** -0.5),
        "pool_w": nrm(ks[14], (N_POOL, N_POOL_GROUPS, POOL_GROUP_DIM, POOL_GROUP_DIM), POOL_GROUP_DIM ** -0.5),
        "pool_scale": 1.0 + nrm(ks[15], (N_POOL, D_MODEL), 0.1),
        "ffn_w_gate": nrm(ks[16], (DEPTH, D_MODEL, D_FF), D_MODEL ** -0.5),
        "ffn_w_up": nrm(ks[17], (DEPTH, D_MODEL, D_FF), D_MODEL ** -0.5),
        "ffn_w_down": nrm(ks[18], (DEPTH, D_FF, D_MODEL), D_FF ** -0.5),
    }


def _fwd_reference(x, meta_tokens, rel_bias_table, norm_mix, norm_ffn, norm_final,
              attn_w_qkv, attn_b_qkv, attn_w_o, attn_b_o, attn_sinks,
              conv_w_in, conv_w, conv_w_out,
              pool_w, pool_scale,
              ffn_w_gate, ffn_w_up, ffn_w_down):
    bsz = x.shape[0]
    meta = jnp.broadcast_to(meta_tokens.astype(x.dtype)[None], (bsz, N_META, D_MODEL))
    h = jnp.concatenate([meta, x], axis=1)
    for i in range(DEPTH):
        kind, j = i % N_MIXERS, i // N_MIXERS
        a = rms_norm(h, norm_mix[i])
        if kind == 0:
            m = sliding_window_attention(a, attn_w_qkv[j], attn_b_qkv[j], attn_w_o[j], attn_b_o[j],
                                         attn_sinks[j], rel_bias_table)
        elif kind == 1:
            m = short_conv_mixer(a, conv_w_in[j], conv_w[j], conv_w_out[j])
        else:
            m = pooling_mixer(a, pool_w[j], pool_scale[j])
        h = h + m.astype(h.dtype)
        h = h + swiglu(rms_norm(h, norm_ffn[i]), ffn_w_gate[i], ffn_w_up[i], ffn_w_down[i])
    h = rms_norm(h, norm_final)
    return h[:, N_META:]


import jax as _jax
import jax.numpy as _jnp

TWIN_FORMAT = 'train_step'
FWD_PARAMS = ['x', 'meta_tokens', 'rel_bias_table', 'norm_mix', 'norm_ffn', 'norm_final', 'attn_w_qkv', 'attn_b_qkv', 'attn_w_o', 'attn_b_o', 'attn_sinks', 'conv_w_in', 'conv_w', 'conv_w_out', 'pool_w', 'pool_scale', 'ffn_w_gate', 'ffn_w_up', 'ffn_w_down']
TWIN_WEIGHTS = ['meta_tokens', 'rel_bias_table', 'norm_mix', 'norm_ffn', 'norm_final', 'attn_w_qkv', 'attn_b_qkv', 'attn_w_o', 'attn_b_o', 'attn_sinks', 'conv_w_in', 'conv_w', 'conv_w_out', 'pool_w', 'pool_scale', 'ffn_w_gate', 'ffn_w_up', 'ffn_w_down']
TWIN_DIFF_INPUT = 'x'
TWIN_INPUTS = ['x', 'meta_tokens', 'rel_bias_table', 'norm_mix', 'norm_ffn', 'norm_final', 'attn_w_qkv', 'attn_b_qkv', 'attn_w_o', 'attn_b_o', 'attn_sinks', 'conv_w_in', 'conv_w', 'conv_w_out', 'pool_w', 'pool_scale', 'ffn_w_gate', 'ffn_w_up', 'ffn_w_down', 'loss_target', 'm_meta_tokens', 'm_rel_bias_table', 'm_norm_mix', 'm_norm_ffn', 'm_norm_final', 'm_attn_w_qkv', 'm_attn_b_qkv', 'm_attn_w_o', 'm_attn_b_o', 'm_attn_sinks', 'm_conv_w_in', 'm_conv_w', 'm_conv_w_out', 'm_pool_w', 'm_pool_scale', 'm_ffn_w_gate', 'm_ffn_w_up', 'm_ffn_w_down', 'v_meta_tokens', 'v_rel_bias_table', 'v_norm_mix', 'v_norm_ffn', 'v_norm_final', 'v_attn_w_qkv', 'v_attn_b_qkv', 'v_attn_w_o', 'v_attn_b_o', 'v_attn_sinks', 'v_conv_w_in', 'v_conv_w', 'v_conv_w_out', 'v_pool_w', 'v_pool_scale', 'v_ffn_w_gate', 'v_ffn_w_up', 'v_ffn_w_down']
TWIN_OUTPUTS = ['loss', 'grad_x', 'grad_meta_tokens', 'grad_rel_bias_table', 'grad_norm_mix', 'grad_norm_ffn', 'grad_norm_final', 'grad_attn_w_qkv', 'grad_attn_b_qkv', 'grad_attn_w_o', 'grad_attn_b_o', 'grad_attn_sinks', 'grad_conv_w_in', 'grad_conv_w', 'grad_conv_w_out', 'grad_pool_w', 'grad_pool_scale', 'grad_ffn_w_gate', 'grad_ffn_w_up', 'grad_ffn_w_down', 'delta_meta_tokens', 'delta_rel_bias_table', 'delta_norm_mix', 'delta_norm_ffn', 'delta_norm_final', 'delta_attn_w_qkv', 'delta_attn_b_qkv', 'delta_attn_w_o', 'delta_attn_b_o', 'delta_attn_sinks', 'delta_conv_w_in', 'delta_conv_w', 'delta_conv_w_out', 'delta_pool_w', 'delta_pool_scale', 'delta_ffn_w_gate', 'delta_ffn_w_up', 'delta_ffn_w_down', 'new_m_meta_tokens', 'new_m_rel_bias_table', 'new_m_norm_mix', 'new_m_norm_ffn', 'new_m_norm_final', 'new_m_attn_w_qkv', 'new_m_attn_b_qkv', 'new_m_attn_w_o', 'new_m_attn_b_o', 'new_m_attn_sinks', 'new_m_conv_w_in', 'new_m_conv_w', 'new_m_conv_w_out', 'new_m_pool_w', 'new_m_pool_scale', 'new_m_ffn_w_gate', 'new_m_ffn_w_up', 'new_m_ffn_w_down', 'new_v_meta_tokens', 'new_v_rel_bias_table', 'new_v_norm_mix', 'new_v_norm_ffn', 'new_v_norm_final', 'new_v_attn_w_qkv', 'new_v_attn_b_qkv', 'new_v_attn_w_o', 'new_v_attn_b_o', 'new_v_attn_sinks', 'new_v_conv_w_in', 'new_v_conv_w', 'new_v_conv_w_out', 'new_v_pool_w', 'new_v_pool_scale', 'new_v_ffn_w_gate', 'new_v_ffn_w_up', 'new_v_ffn_w_down']
TWIN_LEAF_KINDS = {'loss': 'loss', 'grad_x': 'grad_x', 'grad_meta_tokens': 'grad_w', 'grad_rel_bias_table': 'grad_w', 'grad_norm_mix': 'grad_w', 'grad_norm_ffn': 'grad_w', 'grad_norm_final': 'grad_w', 'grad_attn_w_qkv': 'grad_w', 'grad_attn_b_qkv': 'grad_w', 'grad_attn_w_o': 'grad_w', 'grad_attn_b_o': 'grad_w', 'grad_attn_sinks': 'grad_w', 'grad_conv_w_in': 'grad_w', 'grad_conv_w': 'grad_w', 'grad_conv_w_out': 'grad_w', 'grad_pool_w': 'grad_w', 'grad_pool_scale': 'grad_w', 'grad_ffn_w_gate': 'grad_w', 'grad_ffn_w_up': 'grad_w', 'grad_ffn_w_down': 'grad_w', 'delta_meta_tokens': 'delta_w', 'delta_rel_bias_table': 'delta_w', 'delta_norm_mix': 'delta_w', 'delta_norm_ffn': 'delta_w', 'delta_norm_final': 'delta_w', 'delta_attn_w_qkv': 'delta_w', 'delta_attn_b_qkv': 'delta_w', 'delta_attn_w_o': 'delta_w', 'delta_attn_b_o': 'delta_w', 'delta_attn_sinks': 'delta_w', 'delta_conv_w_in': 'delta_w', 'delta_conv_w': 'delta_w', 'delta_conv_w_out': 'delta_w', 'delta_pool_w': 'delta_w', 'delta_pool_scale': 'delta_w', 'delta_ffn_w_gate': 'delta_w', 'delta_ffn_w_up': 'delta_w', 'delta_ffn_w_down': 'delta_w', 'new_m_meta_tokens': 'new_m', 'new_m_rel_bias_table': 'new_m', 'new_m_norm_mix': 'new_m', 'new_m_norm_ffn': 'new_m', 'new_m_norm_final': 'new_m', 'new_m_attn_w_qkv': 'new_m', 'new_m_attn_b_qkv': 'new_m', 'new_m_attn_w_o': 'new_m', 'new_m_attn_b_o': 'new_m', 'new_m_attn_sinks': 'new_m', 'new_m_conv_w_in': 'new_m', 'new_m_conv_w': 'new_m', 'new_m_conv_w_out': 'new_m', 'new_m_pool_w': 'new_m', 'new_m_pool_scale': 'new_m', 'new_m_ffn_w_gate': 'new_m', 'new_m_ffn_w_up': 'new_m', 'new_m_ffn_w_down': 'new_m', 'new_v_meta_tokens': 'new_v', 'new_v_rel_bias_table': 'new_v', 'new_v_norm_mix': 'new_v', 'new_v_norm_ffn': 'new_v', 'new_v_norm_final': 'new_v', 'new_v_attn_w_qkv': 'new_v', 'new_v_attn_b_qkv': 'new_v', 'new_v_attn_w_o': 'new_v', 'new_v_attn_b_o': 'new_v', 'new_v_attn_sinks': 'new_v', 'new_v_conv_w_in': 'new_v', 'new_v_conv_w': 'new_v', 'new_v_conv_w_out': 'new_v', 'new_v_pool_w': 'new_v', 'new_v_pool_scale': 'new_v', 'new_v_ffn_w_gate': 'new_v', 'new_v_ffn_w_up': 'new_v', 'new_v_ffn_w_down': 'new_v'}


def _forward(args):
    return _fwd_reference(*[args[k] for k in FWD_PARAMS])


def _output_shape():
    def fwd():
        inp = _fwd_setup_inputs(0)
        return _fwd_reference(*[inp[k] for k in FWD_PARAMS])
    out = _jax.eval_shape(fwd)
    return out.shape, out.dtype

N_MICROBATCH = 1
ADAM_LR = 0.001
ADAM_B1 = 0.9
ADAM_B2 = 0.999
ADAM_EPS = 1e-08
ADAM_WD = 0.01
ADAM_STEP = 10
PER_EXAMPLE_BATCH_AXIS = {'x': 0, 'loss_target': 0}
SHARED_INPUTS = []
_WEIGHT_DTYPES = {'meta_tokens': _jnp.float32, 'rel_bias_table': _jnp.float32, 'norm_mix': _jnp.float32, 'norm_ffn': _jnp.float32, 'norm_final': _jnp.float32, 'attn_w_qkv': _jnp.float32, 'attn_b_qkv': _jnp.float32, 'attn_w_o': _jnp.float32, 'attn_b_o': _jnp.float32, 'attn_sinks': _jnp.float32, 'conv_w_in': _jnp.float32, 'conv_w': _jnp.float32, 'conv_w_out': _jnp.float32, 'pool_w': _jnp.float32, 'pool_scale': _jnp.float32, 'ffn_w_gate': _jnp.float32, 'ffn_w_up': _jnp.float32, 'ffn_w_down': _jnp.float32}
MOMENT_SCALE = {'meta_tokens': 1.655387e-02, 'rel_bias_table': 1.072535e-01, 'norm_mix': 1.990735e-01, 'norm_ffn': 1.819841e-01, 'norm_final': 6.409062e+01, 'attn_w_qkv': 8.268704e-02, 'attn_b_qkv': 2.621964e-01, 'attn_w_o': 6.198924e-02, 'attn_b_o': 3.293992e-01, 'attn_sinks': 1.379708e-02, 'conv_w_in': 1.887182e-01, 'conv_w': 1.971883e-01, 'conv_w_out': 1.887583e-01, 'pool_w': 1.340240e-01, 'pool_scale': 3.711983e-01, 'ffn_w_gate': 7.797975e-02, 'ffn_w_up': 7.558101e-02, 'ffn_w_down': 1.253128e-01}


def _to_microbatches(a, axis):
    t = _jnp.moveaxis(a, axis, 0)
    t = t.reshape((N_MICROBATCH, t.shape[0] // N_MICROBATCH) + t.shape[1:])
    return _jnp.moveaxis(t, 1, axis + 1)


def setup_inputs(seed: int = 0) -> dict:
    inp = _fwd_setup_inputs(seed)
    key = _jax.random.fold_in(_jax.random.key(seed), 7919)
    shape, _ = _output_shape()
    out = dict(inp)
    out["loss_target"] = _jax.random.normal(_jax.random.fold_in(key, 0), shape, _jnp.float32)
    for i, name in enumerate(TWIN_WEIGHTS):
        w = inp[name].astype(_jnp.float32)
        if MOMENT_SCALE is None:
            s = _jnp.sqrt(_jnp.mean(_jnp.square(w)) + 1e-30)
        else:
            s = MOMENT_SCALE[name]
        km, kv = _jax.random.split(_jax.random.fold_in(key, i + 1))
        out[name] = w
        out["m_" + name] = s * _jax.random.normal(km, w.shape, _jnp.float32)
        out["v_" + name] = (s * s) * _jax.random.uniform(kv, w.shape, _jnp.float32, 0.5, 1.5)
    if N_MICROBATCH > 1:
        for name, axis in PER_EXAMPLE_BATCH_AXIS.items():
            out[name] = _to_microbatches(out[name], axis)
    return {'x': out['x'], 'meta_tokens': out['meta_tokens'], 'rel_bias_table': out['rel_bias_table'], 'norm_mix': out['norm_mix'], 'norm_ffn': out['norm_ffn'], 'norm_final': out['norm_final'], 'attn_w_qkv': out['attn_w_qkv'], 'attn_b_qkv': out['attn_b_qkv'], 'attn_w_o': out['attn_w_o'], 'attn_b_o': out['attn_b_o'], 'attn_sinks': out['attn_sinks'], 'conv_w_in': out['conv_w_in'], 'conv_w': out['conv_w'], 'conv_w_out': out['conv_w_out'], 'pool_w': out['pool_w'], 'pool_scale': out['pool_scale'], 'ffn_w_gate': out['ffn_w_gate'], 'ffn_w_up': out['ffn_w_up'], 'ffn_w_down': out['ffn_w_down'], 'loss_target': out['loss_target'], 'm_meta_tokens': out['m_meta_tokens'], 'm_rel_bias_table': out['m_rel_bias_table'], 'm_norm_mix': out['m_norm_mix'], 'm_norm_ffn': out['m_norm_ffn'], 'm_norm_final': out['m_norm_final'], 'm_attn_w_qkv': out['m_attn_w_qkv'], 'm_attn_b_qkv': out['m_attn_b_qkv'], 'm_attn_w_o': out['m_attn_w_o'], 'm_attn_b_o': out['m_attn_b_o'], 'm_attn_sinks': out['m_attn_sinks'], 'm_conv_w_in': out['m_conv_w_in'], 'm_conv_w': out['m_conv_w'], 'm_conv_w_out': out['m_conv_w_out'], 'm_pool_w': out['m_pool_w'], 'm_pool_scale': out['m_pool_scale'], 'm_ffn_w_gate': out['m_ffn_w_gate'], 'm_ffn_w_up': out['m_ffn_w_up'], 'm_ffn_w_down': out['m_ffn_w_down'], 'v_meta_tokens': out['v_meta_tokens'], 'v_rel_bias_table': out['v_rel_bias_table'], 'v_norm_mix': out['v_norm_mix'], 'v_norm_ffn': out['v_norm_ffn'], 'v_norm_final': out['v_norm_final'], 'v_attn_w_qkv': out['v_attn_w_qkv'], 'v_attn_b_qkv': out['v_attn_b_qkv'], 'v_attn_w_o': out['v_attn_w_o'], 'v_attn_b_o': out['v_attn_b_o'], 'v_attn_sinks': out['v_attn_sinks'], 'v_conv_w_in': out['v_conv_w_in'], 'v_conv_w': out['v_conv_w'], 'v_conv_w_out': out['v_conv_w_out'], 'v_pool_w': out['v_pool_w'], 'v_pool_scale': out['v_pool_scale'], 'v_ffn_w_gate': out['v_ffn_w_gate'], 'v_ffn_w_up': out['v_ffn_w_up'], 'v_ffn_w_down': out['v_ffn_w_down']}


def _loss(weights, diff, rest, loss_target):
    with _jax.named_scope("forward"):
        args = {**rest, TWIN_DIFF_INPUT: diff, **{k: w.astype(_WEIGHT_DTYPES[k]) for k, w in weights.items()}}
        y = _forward(args)
    with _jax.named_scope("loss_head"):
        err = _jnp.square(y.astype(_jnp.float32) - loss_target)
        return 0.5 * _jnp.sum(_jnp.mean(err, axis=-1)) if err.ndim else 0.5 * err


def _adamw(w, g, m, v):
    m = ADAM_B1 * m + (1.0 - ADAM_B1) * g
    v = ADAM_B2 * v + (1.0 - ADAM_B2) * _jnp.square(g)
    m_hat = m / (1.0 - ADAM_B1 ** ADAM_STEP)
    v_hat = v / (1.0 - ADAM_B2 ** ADAM_STEP)
    delta = -ADAM_LR * (m_hat / (_jnp.sqrt(v_hat) + ADAM_EPS) + ADAM_WD * w)
    return delta, m, v


def reference(x, meta_tokens, rel_bias_table, norm_mix, norm_ffn, norm_final, attn_w_qkv, attn_b_qkv, attn_w_o, attn_b_o, attn_sinks, conv_w_in, conv_w, conv_w_out, pool_w, pool_scale, ffn_w_gate, ffn_w_up, ffn_w_down, loss_target, m_meta_tokens, m_rel_bias_table, m_norm_mix, m_norm_ffn, m_norm_final, m_attn_w_qkv, m_attn_b_qkv, m_attn_w_o, m_attn_b_o, m_attn_sinks, m_conv_w_in, m_conv_w, m_conv_w_out, m_pool_w, m_pool_scale, m_ffn_w_gate, m_ffn_w_up, m_ffn_w_down, v_meta_tokens, v_rel_bias_table, v_norm_mix, v_norm_ffn, v_norm_final, v_attn_w_qkv, v_attn_b_qkv, v_attn_w_o, v_attn_b_o, v_attn_sinks, v_conv_w_in, v_conv_w, v_conv_w_out, v_pool_w, v_pool_scale, v_ffn_w_gate, v_ffn_w_up, v_ffn_w_down):
    given = dict(x=x, meta_tokens=meta_tokens, rel_bias_table=rel_bias_table, norm_mix=norm_mix, norm_ffn=norm_ffn, norm_final=norm_final, attn_w_qkv=attn_w_qkv, attn_b_qkv=attn_b_qkv, attn_w_o=attn_w_o, attn_b_o=attn_b_o, attn_sinks=attn_sinks, conv_w_in=conv_w_in, conv_w=conv_w, conv_w_out=conv_w_out, pool_w=pool_w, pool_scale=pool_scale, ffn_w_gate=ffn_w_gate, ffn_w_up=ffn_w_up, ffn_w_down=ffn_w_down, loss_target=loss_target, m_meta_tokens=m_meta_tokens, m_rel_bias_table=m_rel_bias_table, m_norm_mix=m_norm_mix, m_norm_ffn=m_norm_ffn, m_norm_final=m_norm_final, m_attn_w_qkv=m_attn_w_qkv, m_attn_b_qkv=m_attn_b_qkv, m_attn_w_o=m_attn_w_o, m_attn_b_o=m_attn_b_o, m_attn_sinks=m_attn_sinks, m_conv_w_in=m_conv_w_in, m_conv_w=m_conv_w, m_conv_w_out=m_conv_w_out, m_pool_w=m_pool_w, m_pool_scale=m_pool_scale, m_ffn_w_gate=m_ffn_w_gate, m_ffn_w_up=m_ffn_w_up, m_ffn_w_down=m_ffn_w_down, v_meta_tokens=v_meta_tokens, v_rel_bias_table=v_rel_bias_table, v_norm_mix=v_norm_mix, v_norm_ffn=v_norm_ffn, v_norm_final=v_norm_final, v_attn_w_qkv=v_attn_w_qkv, v_attn_b_qkv=v_attn_b_qkv, v_attn_w_o=v_attn_w_o, v_attn_b_o=v_attn_b_o, v_attn_sinks=v_attn_sinks, v_conv_w_in=v_conv_w_in, v_conv_w=v_conv_w, v_conv_w_out=v_conv_w_out, v_pool_w=v_pool_w, v_pool_scale=v_pool_scale, v_ffn_w_gate=v_ffn_w_gate, v_ffn_w_up=v_ffn_w_up, v_ffn_w_down=v_ffn_w_down)
    weights = {n: given[n] for n in TWIN_WEIGHTS}
    shared = {n: given[n] for n in SHARED_INPUTS}
    per_example = {n: given[n] for n in ['x']}
    grad_fn = _jax.value_and_grad(_loss, argnums=(0, 1))

    def one_microbatch(ex, loss_target):
        ex = dict(ex)
        diff = ex.pop(TWIN_DIFF_INPUT)
        return grad_fn(weights, diff, {**shared, **ex}, loss_target)

    if N_MICROBATCH == 1:
        loss, (grad_w, grad_x) = one_microbatch(per_example, given["loss_target"])
    else:
        def body(carry, xs):
            loss_sum, grad_sum = carry
            l_k, (gw_k, gx_k) = one_microbatch(xs[0], xs[1])
            with _jax.named_scope("update"):
                return (loss_sum + l_k, _jax.tree.map(_jnp.add, grad_sum, gw_k)), gx_k

        init = (_jnp.zeros((), _jnp.float32), _jax.tree.map(_jnp.zeros_like, weights))
        (loss, grad_w), grad_x = _jax.lax.scan(body, init, (per_example, given["loss_target"]))
    with _jax.named_scope("update"):
        delta_w, new_m, new_v = {}, {}, {}
        for n in TWIN_WEIGHTS:
            delta_w[n], new_m[n], new_v[n] = _adamw(weights[n], grad_w[n], given["m_" + n], given["v_" + n])
    return (loss, grad_x, *[grad_w[n] for n in TWIN_WEIGHTS], *[delta_w[n] for n in TWIN_WEIGHTS],
            *[new_m[n] for n in TWIN_WEIGHTS], *[new_v[n] for n in TWIN_WEIGHTS])
```

```python
import functools
import math

import numpy as np
import jax
import jax.numpy as jnp
from jax import lax
from jax.experimental import pallas as pl
from jax.experimental.pallas import tpu as pltpu

F32, BF16 = jnp.float32, jnp.bfloat16
D = 1024
N_META = 16
EPS = 1e-6
H, KV, GRP, HD = 16, 4, 4, 64
BLK = 128
N_BUCKETS = 32
POOL_WINDOWS = (2, 4, 8, 16)
PG = 256
DFF = 2816
NSH = 4
FC = DFF // NSH
HALO = 16
NEG = -1e30
DEPTH = 4
LR, B1, B2, ADAM_EPS, WD, STEP = 0.001, 0.9, 0.999, 1e-08, 0.01, 10
MESH = pl.DeviceIdType.MESH
ANY = pl.BlockSpec(memory_space=pl.ANY)
VMEM_LIMIT = 48 * 1024 * 1024

_NN = (((1,), (0,)), ((), ()))
_NT = (((1,), (1,)), ((), ()))
_TN = (((0,), (0,)), ((), ()))

WEIGHTS = ['meta_tokens', 'rel_bias_table', 'norm_mix', 'norm_ffn', 'norm_final', 'attn_w_qkv', 'attn_b_qkv',
           'attn_w_o', 'attn_b_o', 'attn_sinks', 'conv_w_in', 'conv_w', 'conv_w_out', 'pool_w', 'pool_scale',
           'ffn_w_gate', 'ffn_w_up', 'ffn_w_down']
BIG = ['attn_w_qkv', 'attn_w_o', 'conv_w_in', 'conv_w_out', 'pool_w', 'ffn_w_gate', 'ffn_w_up', 'ffn_w_down']
SMALL = [w for w in WEIGHTS if w not in BIG]
COMM_SHAPE = {'attn_w_qkv': (2, 1024, 384), 'attn_w_o': (2, 256, 1024), 'conv_w_in': (2, 512, 768),
              'conv_w_out': (2, 128, 1024), 'pool_w': (2, 128, 256), 'ffn_w_gate': (2, 2048, 704),
              'ffn_w_up': (2, 2048, 704), 'ffn_w_down': (2, 1408, 1024)}


def _dot(a, b, dims=_NN):
    return lax.dot_general(a, b, dims, preferred_element_type=F32)


def _tile(n, cap=512):
    best = None
    for t in range(16, min(n, cap) + 1, 16):
        if n % t == 0:
            best = t
    assert best is not None, n
    return best


def _params(n_axes):
    return pltpu.CompilerParams(dimension_semantics=("arbitrary",) * n_axes, vmem_limit_bytes=VMEM_LIMIT)


def _rstd(x):
    return lax.rsqrt(jnp.mean(x * x, axis=-1, keepdims=True) + EPS)


def _rms_bwd(dy, x, g, r):
    u = dy * g
    dx = r * u - x * ((r * r * r) * (jnp.sum(x * u, axis=-1, keepdims=True) * (1.0 / D)))
    return dx, jnp.sum(dy * (x * r), axis=0, keepdims=True)


def _sigmoid(x):
    return 1.0 / (1.0 + jnp.exp(-x))


def _acc(ref, val, first):
    @pl.when(first)
    def _():
        ref[...] = val

    @pl.when(jnp.logical_not(first))
    def _():
        ref[...] += val


def _rms_fwd(h, g, name):
    L = h.shape[0]
    tm = _tile(L)

    def body(h_ref, g_ref, a_ref):
        x = h_ref[...]
        a_ref[...] = (x * _rstd(x) * g_ref[...]).astype(a_ref.dtype)

    return pl.pallas_call(
        body, name=name, grid=(L // tm,),
        in_specs=[pl.BlockSpec((tm, D), lambda t: (t, 0)), pl.BlockSpec((1, D), lambda t: (0, 0))],
        out_specs=pl.BlockSpec((tm, D), lambda t: (t, 0)),
        out_shape=jax.ShapeDtypeStruct((L, D), BF16), compiler_params=_params(1))(h, g)


def _linear(x, w, *, bias=None, resid=None, out_dtype=BF16, name):
    L, K = x.shape
    J, _, Nc = w.shape
    tm = _tile(L)
    has_b, has_r = bias is not None, resid is not None

    def body(*refs):
        acc = _dot(refs[0][...], refs[1][...])
        i = 2
        if has_b:
            acc = acc + refs[i][...]
            i += 1
        if has_r:
            acc = acc + refs[i][...]
            i += 1
        refs[i][...] = acc.astype(out_dtype)

    in_specs = [pl.BlockSpec((tm, K), lambda t, j: (t, 0)), pl.BlockSpec((None, K, Nc), lambda t, j: (j, 0, 0))]
    ops = [x, w]
    if has_b:
        in_specs.append(pl.BlockSpec((1, Nc), lambda t, j: (0, j)))
        ops.append(bias)
    if has_r:
        in_specs.append(pl.BlockSpec((tm, Nc), lambda t, j: (t, j)))
        ops.append(resid)
    return pl.pallas_call(
        body, name=name, grid=(L // tm, J), in_specs=in_specs,
        out_specs=pl.BlockSpec((tm, Nc), lambda t, j: (t, j)),
        out_shape=jax.ShapeDtypeStruct((L, J * Nc), out_dtype), compiler_params=_params(2))(*ops)


def _linear_t(dy, w, *, out_dtype=BF16, rms=None, name):
    L = dy.shape[0]
    J, K, Nc = w.shape
    tm = _tile(L)
    nt = L // tm
    has_rms = rms is not None

    def body(*refs):
        t, j = pl.program_id(0), pl.program_id(1)
        dy_ref, w_ref = refs[:2]
        acc_ref = refs[-1]
        part = _dot(dy_ref[...].astype(BF16), w_ref[...], _NT)
        _acc(acc_ref, part, j == 0)

        @pl.when(j == J - 1)
        def _():
            if has_rms:
                h_ref, g_ref, r_ref, out_ref, dg_ref = refs[2:7]
                x = h_ref[...]
                dx, dg = _rms_bwd(acc_ref[...], x, g_ref[...], _rstd(x))
                out_ref[...] = r_ref[...] + dx
                _acc(dg_ref, dg, t == 0)
            else:
                refs[2][...] = acc_ref[...].astype(out_dtype)

    in_specs = [pl.BlockSpec((tm, Nc), lambda t, j: (t, j)), pl.BlockSpec((None, K, Nc), lambda t, j: (j, 0, 0))]
    ops = [dy, w]
    row = pl.BlockSpec((tm, K), lambda t, j: (t, 0))
    if has_rms:
        in_specs += [row, pl.BlockSpec((1, K), lambda t, j: (0, 0)), row]
        ops += list(rms)
        out_specs = (row, pl.BlockSpec((1, K), lambda t, j: (0, 0)))
        out_shape = (jax.ShapeDtypeStruct((L, K), F32), jax.ShapeDtypeStruct((1, K), F32))
    else:
        out_specs = row
        out_shape = jax.ShapeDtypeStruct((L, K), out_dtype)
    return pl.pallas_call(
        body, name=name, grid=(nt, J), in_specs=in_specs, out_specs=out_specs, out_shape=out_shape,
        scratch_shapes=[pltpu.VMEM((tm, K), F32)], compiler_params=_params(2))(*ops)


def _wgrad(x, dy, *, mode, J, colsum=False, name):
    L = x.shape[0]
    tm = _tile(L)
    nt = L // tm
    if mode == 'col':
        K, Nc = x.shape[1], dy.shape[1] // J
        xs = pl.BlockSpec((tm, K), lambda j, t: (t, 0))
        ds = pl.BlockSpec((tm, Nc), lambda j, t: (t, j))
        oshape = (J, K, Nc)
        cs_spec = pl.BlockSpec((1, Nc), lambda j, t: (0, j))
    else:
        K, Nc = x.shape[1] // J, dy.shape[1]
        xs = pl.BlockSpec((tm, K), lambda j, t: (t, j))
        ds = pl.BlockSpec((tm, Nc), lambda j, t: (t, 0))
        oshape = (J, K, Nc)
        cs_spec = pl.BlockSpec((1, Nc), lambda j, t: (0, 0))

    def body(*refs):
        j, t = pl.program_id(0), pl.program_id(1)
        x_ref, dy_ref, out_ref = refs[:3]
        acc_ref = refs[-1]
        dyv = dy_ref[...]
        _acc(acc_ref, _dot(x_ref[...].astype(BF16), dyv.astype(BF16), _TN), t == 0)

        @pl.when(t == nt - 1)
        def _():
            out_ref[...] = acc_ref[...].astype(BF16)

        if colsum:
            s = jnp.sum(dyv.astype(F32), axis=0, keepdims=True)
            if mode == 'col':
                _acc(refs[3], s, t == 0)
            else:
                @pl.when(j == 0)
                def _():
                    _acc(refs[3], s, t == 0)

    out_specs = pl.BlockSpec((None, K, Nc), lambda j, t: (j, 0, 0))
    out_shape = jax.ShapeDtypeStruct(oshape, BF16)
    if colsum:
        out_specs = (out_specs, cs_spec)
        out_shape = (out_shape, jax.ShapeDtypeStruct((1, dy.shape[1]), F32))
    return pl.pallas_call(
        body, name=name, grid=(J, nt), in_specs=[xs, ds], out_specs=out_specs, out_shape=out_shape,
        scratch_shapes=[pltpu.VMEM((K, Nc), F32)], compiler_params=_params(2))(x, dy)


def _ffn_fwd(h, gain, wg, wu, wd, layer, name):
    L = h.shape[0]
    tm = _tile(L)

    def body(h_ref, g_ref, wg_ref, wu_ref, wd_ref, hn_ref, G_ref, U_ref, A_ref, B_ref, acc_ref, b_scr):
        j = pl.program_id(1)

        @pl.when(j == 0)
        def _():
            x = h_ref[...]
            b = (x * _rstd(x) * g_ref[...]).astype(BF16)
            b_scr[...] = b
            B_ref[...] = b
            acc_ref[...] = x

        b = b_scr[...]
        g = _dot(b, wg_ref[...])
        u = _dot(b, wu_ref[...])
        G_ref[...] = g.astype(BF16)
        U_ref[...] = u.astype(BF16)
        a = ((g * _sigmoid(g)) * u).astype(BF16)
        A_ref[...] = a
        acc_ref[...] += _dot(a, wd_ref[...])

        @pl.when(j == NSH - 1)
        def _():
            hn_ref[...] = acc_ref[...]

    row = pl.BlockSpec((tm, D), lambda t, j: (t, 0))
    chunk = pl.BlockSpec((None, tm, FC), lambda t, j: (j, t, 0))
    cshape = jax.ShapeDtypeStruct((NSH, L, FC), BF16)
    return pl.pallas_call(
        body, name=name, grid=(L // tm, NSH),
        in_specs=[row, pl.BlockSpec((1, D), lambda t, j: (0, 0)),
                  pl.BlockSpec((None, None, D, FC), lambda t, j: (j, layer, 0, 0)),
                  pl.BlockSpec((None, None, D, FC), lambda t, j: (j, layer, 0, 0)),
                  pl.BlockSpec((None, None, FC, D), lambda t, j: (j, layer, 0, 0))],
        out_specs=(row, chunk, chunk, chunk, row),
        out_shape=(jax.ShapeDtypeStruct((L, D), F32), cshape, cshape, cshape, jax.ShapeDtypeStruct((L, D), BF16)),
        scratch_shapes=[pltpu.VMEM((tm, D), F32), pltpu.VMEM((tm, D), BF16)],
        compiler_params=_params(2))(h, gain, wg, wu, wd)


def _ffn_bwd_act(dhn, h, gain, G, U, wg, wu, wd, layer, name):
    L = h.shape[0]
    tm = _tile(L)

    def body(dhn_ref, h_ref, g_ref, G_ref, U_ref, wg_ref, wu_ref, wd_ref, DG_ref, DU_ref, dh_ref, dgain_ref, db_ref):
        t, j = pl.program_id(0), pl.program_id(1)
        d_act = _dot(dhn_ref[...].astype(BF16), wd_ref[...], _NT)
        g = G_ref[...].astype(F32)
        u = U_ref[...].astype(F32)
        s = _sigmoid(g)
        dg = (d_act * u * (s * (1.0 + g * (1.0 - s)))).astype(BF16)
        du = (d_act * (g * s)).astype(BF16)
        DG_ref[...] = dg
        DU_ref[...] = du
        _acc(db_ref, _dot(dg, wg_ref[...], _NT) + _dot(du, wu_ref[...], _NT), j == 0)

        @pl.when(j == NSH - 1)
        def _():
            x = h_ref[...]
            dx, dgn = _rms_bwd(db_ref[...], x, g_ref[...], _rstd(x))
            dh_ref[...] = dhn_ref[...] + dx
            _acc(dgain_ref, dgn, t == 0)

    row = pl.BlockSpec((tm, D), lambda t, j: (t, 0))
    one = pl.BlockSpec((1, D), lambda t, j: (0, 0))
    chunk = pl.BlockSpec((None, tm, FC), lambda t, j: (j, t, 0))
    cshape = jax.ShapeDtypeStruct((NSH, L, FC), BF16)
    return pl.pallas_call(
        body, name=name, grid=(L // tm, NSH),
        in_specs=[row, row, one, chunk, chunk,
                  pl.BlockSpec((None, None, D, FC), lambda t, j: (j, layer, 0, 0)),
                  pl.BlockSpec((None, None, D, FC), lambda t, j: (j, layer, 0, 0)),
                  pl.BlockSpec((None, None, FC, D), lambda t, j: (j, layer, 0, 0))],
        out_specs=(chunk, chunk, row, one),
        out_shape=(cshape, cshape, jax.ShapeDtypeStruct((L, D), F32), jax.ShapeDtypeStruct((1, D), F32)),
        scratch_shapes=[pltpu.VMEM((tm, D), F32)],
        compiler_params=_params(2))(dhn, h, gain, G, U, wg, wu, wd)


def _ffn_bwd_w(B, A, DG, DU, dhn, name):
    L = B.shape[0]
    tm = _tile(L)
    nt = L // tm

    def body(B_ref, A_ref, DG_ref, DU_ref, dhn_ref, dwg_ref, dwu_ref, dwd_ref, ag, au, ad):
        t = pl.program_id(1)
        b = B_ref[...]
        _acc(ag, _dot(b, DG_ref[...], _TN), t == 0)
        _acc(au, _dot(b, DU_ref[...], _TN), t == 0)
        _acc(ad, _dot(A_ref[...], dhn_ref[...].astype(BF16), _TN), t == 0)

        @pl.when(t == nt - 1)
        def _():
            dwg_ref[...] = ag[...].astype(BF16)
            dwu_ref[...] = au[...].astype(BF16)
            dwd_ref[...] = ad[...].astype(BF16)

    row = pl.BlockSpec((tm, D), lambda j, t: (t, 0))
    chunk = pl.BlockSpec((None, tm, FC), lambda j, t: (j, t, 0))
    return pl.pallas_call(
        body, name=name, grid=(NSH, nt), in_specs=[row, chunk, chunk, chunk, row],
        out_specs=(pl.BlockSpec((None, D, FC), lambda j, t: (j, 0, 0)),
                   pl.BlockSpec((None, D, FC), lambda j, t: (j, 0, 0)),
                   pl.BlockSpec((None, FC, D), lambda j, t: (j, 0, 0))),
        out_shape=(jax.ShapeDtypeStruct((NSH, D, FC), BF16), jax.ShapeDtypeStruct((NSH, D, FC), BF16),
                   jax.ShapeDtypeStruct((NSH, FC, D), BF16)),
        scratch_shapes=[pltpu.VMEM((D, FC), F32), pltpu.VMEM((D, FC), F32), pltpu.VMEM((FC, D), F32)],
        compiler_params=_params(2))(B, A, DG, DU, dhn)


def _loss_head(h, gain, target):
    L = h.shape[0]
    tm = _tile(L)

    def body(h_ref, g_ref, tgt_ref, dh_ref, dg_ref, loss_ref):
        t = pl.program_id(0)
        x = h_ref[...]
        g = g_ref[...]
        r = _rstd(x)
        rows = t * tm + lax.broadcasted_iota(jnp.int32, (tm, 1), 0)
        diff = jnp.where(rows >= N_META, x * r * g - tgt_ref[...], 0.0)
        part = 0.5 * jnp.sum(jnp.sum(diff * diff, axis=-1, keepdims=True) * (1.0 / D), axis=0, keepdims=True)
        dx, dg = _rms_bwd(diff * (1.0 / D), x, g, r)
        dh_ref[...] = dx
        _acc(dg_ref, dg, t == 0)
        _acc(loss_ref, jnp.broadcast_to(part, (1, 128)), t == 0)

    row = pl.BlockSpec((tm, D), lambda t: (t, 0))
    one = pl.BlockSpec((1, D), lambda t: (0, 0))
    return pl.pallas_call(
        body, name="loss_head", grid=(L // tm,), in_specs=[row, one, row],
        out_specs=(row, one, pl.BlockSpec((1, 128), lambda t: (0, 0))),
        out_shape=(jax.ShapeDtypeStruct((L, D), F32), jax.ShapeDtypeStruct((1, D), F32),
                   jax.ShapeDtypeStruct((1, 128), F32)),
        compiler_params=_params(1))(h, gain, target)


def _halo_prev(tm):
    return lambda t: (jnp.maximum(t * (tm // HALO) - 1, 0), 0)


def _conv_fwd(z3, cw, name):
    L = z3.shape[0]
    tm = _tile(L)

    def body(b_ref, c_ref, u_ref, ch_ref, uh_ref, w_ref, y_ref, buf):
        t = pl.program_id(0)
        z = c_ref[...] * u_ref[...]
        buf[pl.ds(0, HALO), :] = jnp.where(t > 0, ch_ref[...] * uh_ref[...], 0.0)
        buf[pl.ds(HALO, tm), :] = z
        w0, w1, w2 = w_ref[0:1, :], w_ref[1:2, :], w_ref[2:3, :]
        conv = w2 * z + w1 * buf[pl.ds(HALO - 1, tm), :] + w0 * buf[pl.ds(HALO - 2, tm), :]
        y_ref[...] = (b_ref[...] * conv).astype(BF16)

    def col(k):
        return pl.BlockSpec((tm, D), lambda t: (t, k))

    def hcol(k):
        return pl.BlockSpec((HALO, D), lambda t: (jnp.maximum(t * (tm // HALO) - 1, 0), k))

    return pl.pallas_call(
        body, name=name, grid=(L // tm,),
        in_specs=[col(0), col(1), col(2), hcol(1), hcol(2), pl.BlockSpec((3, D), lambda t: (0, 0))],
        out_specs=pl.BlockSpec((tm, D), lambda t: (t, 0)),
        out_shape=jax.ShapeDtypeStruct((L, D), BF16),
        scratch_shapes=[pltpu.VMEM((HALO + tm, D), F32)], compiler_params=_params(1))(z3, z3, z3, z3, z3, cw)


def _conv_bwd(z3, cw, dy, name):
    L = z3.shape[0]
    tm = _tile(L)
    nt = L // tm
    last_h = L // HALO - 1

    def body(b_ref, c_ref, u_ref, ch_ref, uh_ref, w_ref, dy_ref, bn_ref, dyn_ref, dz3_ref, dw_ref, zbuf, dbuf):
        t = pl.program_id(0)
        w0, w1, w2 = w_ref[0:1, :], w_ref[1:2, :], w_ref[2:3, :]
        bgate, cgate, u = b_ref[...], c_ref[...], u_ref[...]
        z = cgate * u
        zbuf[pl.ds(0, HALO), :] = jnp.where(t > 0, ch_ref[...] * uh_ref[...], 0.0)
        zbuf[pl.ds(HALO, tm), :] = z
        z1 = zbuf[pl.ds(HALO - 1, tm), :]
        z2 = zbuf[pl.ds(HALO - 2, tm), :]
        conv = w2 * z + w1 * z1 + w0 * z2
        dyv = dy_ref[...]
        dconv = dyv * bgate
        dbuf[pl.ds(0, tm), :] = dconv
        dbuf[pl.ds(tm, HALO), :] = jnp.where(t < nt - 1, dyn_ref[...] * bn_ref[...], 0.0)
        dz = w2 * dconv + w1 * dbuf[pl.ds(1, tm), :] + w0 * dbuf[pl.ds(2, tm), :]
        dz3_ref[:, 0:D] = (dyv * conv).astype(BF16)
        dz3_ref[:, D:2 * D] = (dz * u).astype(BF16)
        dz3_ref[:, 2 * D:3 * D] = (dz * cgate).astype(BF16)
        for k, zk in enumerate((z2, z1, z)):
            _acc(dw_ref.at[k:k + 1], jnp.sum(dconv * zk, axis=0, keepdims=True), t == 0)

    def col(k):
        return pl.BlockSpec((tm, D), lambda t: (t, k))

    def hprev(k):
        return pl.BlockSpec((HALO, D), lambda t: (jnp.maximum(t * (tm // HALO) - 1, 0), k))

    def hnext(k):
        return pl.BlockSpec((HALO, D), lambda t: (jnp.minimum((t + 1) * (tm // HALO), last_h), k))

    return pl.pallas_call(
        body, name=name, grid=(nt,),
        in_specs=[col(0), col(1), col(2), hprev(1), hprev(2), pl.BlockSpec((3, D), lambda t: (0, 0)),
                  col(0), hnext(0), hnext(0)],
        out_specs=(pl.BlockSpec((tm, 3 * D), lambda t: (t, 0)), pl.BlockSpec((3, D), lambda t: (0, 0))),
        out_shape=(jax.ShapeDtypeStruct((L, 3 * D), BF16), jax.ShapeDtypeStruct((3, D), F32)),
        scratch_shapes=[pltpu.VMEM((HALO + tm, D), F32), pltpu.VMEM((tm + HALO, D), F32)],
        compiler_params=_params(1))(z3, z3, z3, z3, z3, cw, dy, z3, dy)


def _count_inv(pos, w):
    return 1.0 / jnp.minimum(pos + 1, w).astype(F32)


def _pool_fwd(h, gain, name):
    L = h.shape[0]
    tm = _tile(L)

    def body(h_ref, hh_ref, g_ref, mix_ref, buf):
        t = pl.program_id(0)
        g = g_ref[...]
        x = h_ref[...]
        a = x * _rstd(x) * g
        xh = hh_ref[...]
        buf[pl.ds(0, HALO), :] = jnp.where(t > 0, xh * _rstd(xh) * g, 0.0)
        buf[pl.ds(HALO, tm), :] = a
        pos = t * tm + lax.broadcasted_iota(jnp.int32, (tm, 1), 0)
        for gi, w in enumerate(POOL_WINDOWS):
            cols = pl.ds(gi * PG, PG)
            s = buf[pl.ds(HALO, tm), cols]
            for k in range(1, w):
                s = s + buf[pl.ds(HALO - k, tm), cols]
            mix_ref[:, gi * PG:(gi + 1) * PG] = (s / jnp.minimum(pos + 1, w).astype(F32)
                                                  - buf[pl.ds(HALO, tm), cols]).astype(BF16)

    return pl.pallas_call(
        body, name=name, grid=(L // tm,),
        in_specs=[pl.BlockSpec((tm, D), lambda t: (t, 0)),
                  pl.BlockSpec((HALO, D), lambda t: (jnp.maximum(t * (tm // HALO) - 1, 0), 0)),
                  pl.BlockSpec((1, D), lambda t: (0, 0))],
        out_specs=pl.BlockSpec((tm, D), lambda t: (t, 0)),
        out_shape=jax.ShapeDtypeStruct((L, D), BF16),
        scratch_shapes=[pltpu.VMEM((HALO + tm, D), F32)], compiler_params=_params(1))(h, h, gain)


def _pool_out(mix, wp, scale, h, name):
    L = mix.shape[0]
    tm = _tile(L)

    def body(mix_ref, wp_ref, s_ref, h_ref, out_ref):
        for gi in range(4):
            sl = slice(gi * PG, (gi + 1) * PG)
            pre = _dot(mix_ref[:, sl], wp_ref[gi])
            out_ref[:, sl] = h_ref[:, sl] + pre * s_ref[:, sl]

    row = pl.BlockSpec((tm, D), lambda t: (t, 0))
    return pl.pallas_call(
        body, name=name, grid=(L // tm,),
        in_specs=[row, pl.BlockSpec((4, PG, PG), lambda t: (0, 0, 0)), pl.BlockSpec((1, D), lambda t: (0, 0)), row],
        out_specs=row, out_shape=jax.ShapeDtypeStruct((L, D), F32), compiler_params=_params(1))(mix, wp, scale, h)


def _pool_out_bwd(dm, mix, wp, scale, name):
    L = mix.shape[0]
    tm = _tile(L)

    def body(dm_ref, mix_ref, wp_ref, s_ref, dmix_ref, dwp_ref, ds_ref):
        t = pl.program_id(0)
        for gi in range(4):
            sl = slice(gi * PG, (gi + 1) * PG)
            mx = mix_ref[:, sl]
            dmv = dm_ref[:, sl]
            pre = _dot(mx, wp_ref[gi])
            _acc(ds_ref.at[:, sl], jnp.sum(dmv * pre, axis=0, keepdims=True), t == 0)
            dpre = (dmv * s_ref[:, sl]).astype(BF16)
            _acc(dwp_ref.at[gi], _dot(mx, dpre, _TN), t == 0)
            dmix_ref[:, sl] = _dot(dpre, wp_ref[gi], _NT)

    row = pl.BlockSpec((tm, D), lambda t: (t, 0))
    one = pl.BlockSpec((1, D), lambda t: (0, 0))
    wsp = pl.BlockSpec((4, PG, PG), lambda t: (0, 0, 0))
    return pl.pallas_call(
        body, name=name, grid=(L // tm,), in_specs=[row, row, wsp, one],
        out_specs=(row, wsp, one),
        out_shape=(jax.ShapeDtypeStruct((L, D), F32), jax.ShapeDtypeStruct((4, PG, PG), F32),
                   jax.ShapeDtypeStruct((1, D), F32)),
        compiler_params=_params(1))(dm, mix, wp, scale)


def _pool_bwd(dmix, h, gain, resid, name):
    L = h.shape[0]
    tm = _tile(L)
    nt = L // tm
    last_h = L // HALO - 1

    def body(dm_ref, dmn_ref, h_ref, g_ref, r_ref, dh_ref, dg_ref, buf):
        t = pl.program_id(0)
        pos = t * tm + lax.broadcasted_iota(jnp.int32, (tm, 1), 0)
        posn = (t + 1) * tm + lax.broadcasted_iota(jnp.int32, (HALO, 1), 0)
        dmv = dm_ref[...]
        dmn = dmn_ref[...]
        for gi, w in enumerate(POOL_WINDOWS):
            sl = slice(gi * PG, (gi + 1) * PG)
            buf[pl.ds(0, tm), sl] = dmv[:, sl] * _count_inv(pos, w)
            buf[pl.ds(tm, HALO), sl] = jnp.where(t < nt - 1, dmn[:, sl] * _count_inv(posn, w), 0.0)
        parts = []
        for gi, w in enumerate(POOL_WINDOWS):
            cols = pl.ds(gi * PG, PG)
            s = buf[pl.ds(0, tm), cols]
            for k in range(1, w):
                s = s + buf[pl.ds(k, tm), cols]
            parts.append(s)
        da = jnp.concatenate(parts, axis=1) - dmv
        x = h_ref[...]
        dx, dg = _rms_bwd(da, x, g_ref[...], _rstd(x))
        dh_ref[...] = r_ref[...] + dx
        _acc(dg_ref, dg, t == 0)

    row = pl.BlockSpec((tm, D), lambda t: (t, 0))
    one = pl.BlockSpec((1, D), lambda t: (0, 0))
    return pl.pallas_call(
        body, name=name, grid=(nt,),
        in_specs=[row, pl.BlockSpec((HALO, D), lambda t: (jnp.minimum((t + 1) * (tm // HALO), last_h), 0)),
                  row, one, row],
        out_specs=(row, one),
        out_shape=(jax.ShapeDtypeStruct((L, D), F32), jax.ShapeDtypeStruct((1, D), F32)),
        scratch_shapes=[pltpu.VMEM((tm + HALO, D), F32)], compiler_params=_params(1))(dmix, dmix, h, gain, resid)


def _bucket_np(d):
    d = np.maximum(d, 0)
    df = np.maximum(d, 1).astype(np.float32)
    large = 16 + (np.log(df / np.float32(16)) / np.float32(math.log(128 / 16)) * np.float32(16)).astype(np.int32)
    return np.where(d < 16, d, np.minimum(large, N_BUCKETS - 1))


def _bias_index():
    iq = np.arange(BLK)[:, None]
    jk = np.arange(2 * BLK)[None, :]
    dist = BLK + iq - jk
    band = _bucket_np(dist)
    ok = (dist >= 0) & (dist < BLK)
    band1 = np.where(ok, band, N_BUCKETS)
    band0 = np.where(ok & (jk >= BLK), band, N_BUCKETS)
    im = np.arange(N_META)[None, :]
    meta0 = _bucket_np(N_META + iq - im)
    meta1 = _bucket_np(N_META + BLK + iq - im)
    dm = np.arange(N_META)[:, None] - im
    mm = np.where(dm >= 0, _bucket_np(dm), N_BUCKETS)
    segs = [band0, band1, meta0, meta1, mm]
    return np.concatenate([s.reshape(-1) for s in segs]).astype(np.int32), [s.shape for s in segs]


def _split_bias(flat):
    _, shapes = _bias_index()
    out, o = [], 0
    for s in shapes:
        n = s[0] * s[1]
        out.append(flat[:, o:o + n].reshape(H, *s))
        o += n
    return jnp.stack(out[0:2]), jnp.stack(out[2:4]), out[4]


P_CHUNK = 9984


def _onehot(idx_ref, grad):
    rows = lax.broadcasted_iota(jnp.int32, (128, P_CHUNK), 0)
    hit = (rows == idx_ref[...]).astype(F32)
    return jnp.where(rows == N_BUCKETS, -1.0, hit) if grad else hit


def _bias_lookup(table_aug, idx, name):
    P = idx.shape[1]

    def body(t_ref, idx_ref, o_ref):
        o_ref[...] = lax.dot_general(t_ref[...], _onehot(idx_ref, False), _NN, precision=lax.Precision.HIGHEST,
                                     preferred_element_type=F32)

    return pl.pallas_call(
        body, name=name, grid=(P // P_CHUNK,),
        in_specs=[pl.BlockSpec((H, 128), lambda i: (0, 0)), pl.BlockSpec((1, P_CHUNK), lambda i: (0, i))],
        out_specs=pl.BlockSpec((H, P_CHUNK), lambda i: (0, i)),
        out_shape=jax.ShapeDtypeStruct((H, P), F32), compiler_params=_params(1))(table_aug, idx)


def _bias_lookup_bwd(dbias, idx, name):
    P = idx.shape[1]

    def body(d_ref, idx_ref, o_ref):
        part = lax.dot_general(d_ref[...], _onehot(idx_ref, True), _NT, precision=lax.Precision.HIGHEST,
                               preferred_element_type=F32)
        _acc(o_ref, part, pl.program_id(0) == 0)

    return pl.pallas_call(
        body, name=name, grid=(P // P_CHUNK,),
        in_specs=[pl.BlockSpec((H, P_CHUNK), lambda i: (0, i)), pl.BlockSpec((1, P_CHUNK), lambda i: (0, i))],
        out_specs=pl.BlockSpec((H, 128), lambda i: (0, 0)),
        out_shape=jax.ShapeDtypeStruct((H, 128), F32), compiler_params=_params(1))(dbias, idx)


def _softmax_parts(q, kb, km, bias_b, bias_m, sink):
    s_b = _dot(q, kb, _NT) + bias_b
    s_m = _dot(q, km, _NT) + bias_m
    m = jnp.maximum(jnp.maximum(jnp.max(s_b, axis=-1, keepdims=True), jnp.max(s_m, axis=-1, keepdims=True)), sink)
    e_b = jnp.exp(s_b - m)
    e_m = jnp.exp(s_m - m)
    inv = 1.0 / (jnp.sum(e_b, axis=-1, keepdims=True) + jnp.sum(e_m, axis=-1, keepdims=True) + jnp.exp(sink - m))
    return e_b * inv, e_m * inv


def _attn_specs(nb):
    def qmap(kh, n):
        return (kh, jnp.minimum(n, nb - 1), 0)

    def prev(kh, n):
        return (kh, jnp.maximum(jnp.minimum(n, nb - 1) - 1, 0), 0)

    def var(kh, n):
        return (jnp.minimum(n, 1), kh, 0, 0)

    return dict(
        q=pl.BlockSpec((GRP, BLK, HD), qmap),
        cur=pl.BlockSpec((None, BLK, HD), qmap),
        prev=pl.BlockSpec((None, BLK, HD), prev),
        meta=pl.BlockSpec((None, N_META, HD), lambda kh, n: (kh, 0, 0)),
        band=pl.BlockSpec((None, GRP, BLK, 2 * BLK), var),
        bmeta=pl.BlockSpec((None, GRP, BLK, N_META), var),
        sink=pl.BlockSpec((None, GRP * BLK, 1), lambda kh, n: (kh, 0, 0)))


def _attn_fwd(qr, kr, vr, km, vm, band, bmeta, sink, name):
    S = qr.shape[1]
    nb = S // BLK
    sp = _attn_specs(nb)

    def body(q_ref, kc_ref, kp_ref, vc_ref, vp_ref, km_ref, vm_ref, band_ref, bm_ref, sink_ref, o_ref):
        q = q_ref[...].reshape(GRP * BLK, HD)
        kb = jnp.concatenate([kp_ref[...], kc_ref[...]], axis=0)
        vb = jnp.concatenate([vp_ref[...], vc_ref[...]], axis=0)
        p_b, p_m = _softmax_parts(q, kb, km_ref[...], band_ref[...].reshape(GRP * BLK, 2 * BLK),
                                  bm_ref[...].reshape(GRP * BLK, N_META), sink_ref[...])
        o = _dot(p_b.astype(BF16), vb) + _dot(p_m.astype(BF16), vm_ref[...])
        o_ref[...] = o.reshape(GRP, BLK, HD).astype(BF16)

    return pl.pallas_call(
        body, name=name, grid=(KV, nb),
        in_specs=[sp['q'], sp['cur'], sp['prev'], sp['cur'], sp['prev'], sp['meta'], sp['meta'],
                  sp['band'], sp['bmeta'], sp['sink']],
        out_specs=sp['q'], out_shape=jax.ShapeDtypeStruct((H, S, HD), BF16),
        compiler_params=_params(2))(qr, kr, kr, vr, vr, km, vm, band, bmeta, sink)


def _attn_bwd(qr, kr, vr, km, vm, band, bmeta, sink, do, name):
    S = qr.shape[1]
    nb = S // BLK
    sp = _attn_specs(nb)

    def body(q_ref, kc_ref, kp_ref, vc_ref, vp_ref, km_ref, vm_ref, band_ref, bm_ref, sink_ref, do_ref,
             dq_ref, dk_ref, dv_ref, dkm_ref, dvm_ref, dband_ref, dbm_ref, ck, cv):
        n = pl.program_id(1)

        @pl.when(n < nb)
        def _():
            q = q_ref[...].reshape(GRP * BLK, HD)
            dov = do_ref[...].reshape(GRP * BLK, HD)
            kb = jnp.concatenate([kp_ref[...], kc_ref[...]], axis=0)
            vb = jnp.concatenate([vp_ref[...], vc_ref[...]], axis=0)
            kmv, vmv = km_ref[...], vm_ref[...]
            p_b, p_m = _softmax_parts(q, kb, kmv, band_ref[...].reshape(GRP * BLK, 2 * BLK),
                                      bm_ref[...].reshape(GRP * BLK, N_META), sink_ref[...])
            dp_b = _dot(dov, vb, _NT)
            dp_m = _dot(dov, vmv, _NT)
            delta = jnp.sum(p_b * dp_b, axis=-1, keepdims=True) + jnp.sum(p_m * dp_m, axis=-1, keepdims=True)
            ds_b = p_b * (dp_b - delta)
            ds_m = p_m * (dp_m - delta)
            first = n <= 1
            _acc(dband_ref, ds_b.reshape(GRP, BLK, 2 * BLK), first)
            _acc(dbm_ref, ds_m.reshape(GRP, BLK, N_META), first)
            ds_b16, ds_m16 = ds_b.astype(BF16), ds_m.astype(BF16)
            dq = _dot(ds_b16, kb) + _dot(ds_m16, kmv)
            dq_ref[...] = dq.reshape(GRP, BLK, HD).astype(BF16)
            dkb = _dot(ds_b16, q, _TN)
            dvb = _dot(p_b.astype(BF16), dov, _TN)
            _acc(dkm_ref, _dot(ds_m16, q, _TN), n == 0)
            _acc(dvm_ref, _dot(p_m.astype(BF16), dov, _TN), n == 0)

            @pl.when(n >= 1)
            def _():
                dk_ref[...] = (ck[...] + dkb[0:BLK]).astype(BF16)
                dv_ref[...] = (cv[...] + dvb[0:BLK]).astype(BF16)

            ck[...] = dkb[BLK:2 * BLK]
            cv[...] = dvb[BLK:2 * BLK]

        @pl.when(n == nb)
        def _():
            dk_ref[...] = ck[...].astype(BF16)
            dv_ref[...] = cv[...].astype(BF16)

    kvout = pl.BlockSpec((None, BLK, HD), lambda kh, n: (kh, jnp.maximum(n - 1, 0), 0))
    mout = pl.BlockSpec((None, N_META, HD), lambda kh, n: (kh, 0, 0))
    return pl.pallas_call(
        body, name=name, grid=(KV, nb + 1),
        in_specs=[sp['q'], sp['cur'], sp['prev'], sp['cur'], sp['prev'], sp['meta'], sp['meta'],
                  sp['band'], sp['bmeta'], sp['sink'], sp['q']],
        out_specs=(sp['q'], kvout, kvout, mout, mout, sp['band'], sp['bmeta']),
        out_shape=(jax.ShapeDtypeStruct((H, S, HD), BF16), jax.ShapeDtypeStruct((KV, S, HD), BF16),
                   jax.ShapeDtypeStruct((KV, S, HD), BF16), jax.ShapeDtypeStruct((KV, N_META, HD), F32),
                   jax.ShapeDtypeStruct((KV, N_META, HD), F32), jax.ShapeDtypeStruct((2, H, BLK, 2 * BLK), F32),
                   jax.ShapeDtypeStruct((2, H, BLK, N_META), F32)),
        scratch_shapes=[pltpu.VMEM((BLK, HD), F32), pltpu.VMEM((BLK, HD), F32)],
        compiler_params=_params(2))(qr, kr, kr, vr, vr, km, vm, band, bmeta, sink, do)


def _meta_softmax(q, k, bias, sink):
    s = _dot(q, k, _NT) + bias
    m = jnp.maximum(jnp.max(s, axis=-1, keepdims=True), sink)
    e = jnp.exp(s - m)
    return e * (1.0 / (jnp.sum(e, axis=-1, keepdims=True) + jnp.exp(sink - m)))


def _attn_meta_fwd(qm, km, vm, bias, sink, name):
    def body(q_ref, k_ref, v_ref, b_ref, s_ref, o_ref):
        for h in range(H):
            p = _meta_softmax(q_ref[h], k_ref[h // GRP], b_ref[h], s_ref[h])
            o_ref[h] = _dot(p.astype(BF16), v_ref[h // GRP]).astype(BF16)

    return pl.pallas_call(body, name=name, out_shape=jax.ShapeDtypeStruct((H, N_META, HD), BF16))(
        qm, km, vm, bias, sink)


def _attn_meta_bwd(qm, km, vm, bias, sink, do, name):
    def body(q_ref, k_ref, v_ref, b_ref, s_ref, do_ref, dq_ref, dk_ref, dv_ref, db_ref):
        for kh in range(KV):
            k, v = k_ref[kh], v_ref[kh]
            dk = jnp.zeros((N_META, HD), F32)
            dv = jnp.zeros((N_META, HD), F32)
            for g in range(GRP):
                h = kh * GRP + g
                q, dov = q_ref[h], do_ref[h]
                p = _meta_softmax(q, k, b_ref[h], s_ref[h])
                dp = _dot(dov, v, _NT)
                ds = p * (dp - jnp.sum(p * dp, axis=-1, keepdims=True))
                db_ref[h] = ds
                ds16 = ds.astype(BF16)
                dq_ref[h] = _dot(ds16, k).astype(BF16)
                dk = dk + _dot(ds16, q, _TN)
                dv = dv + _dot(p.astype(BF16), dov, _TN)
            dk_ref[kh] = dk
            dv_ref[kh] = dv

    return pl.pallas_call(
        body, name=name,
        out_shape=(jax.ShapeDtypeStruct((H, N_META, HD), BF16), jax.ShapeDtypeStruct((KV, N_META, HD), F32),
                   jax.ShapeDtypeStruct((KV, N_META, HD), F32), jax.ShapeDtypeStruct((H, N_META, N_META), F32)))(
        qm, km, vm, bias, sink, do)


def _heads(t, n):
    return jnp.transpose(t.reshape(t.shape[0], n, HD), (1, 0, 2))


def _unheads(t):
    return jnp.transpose(t, (1, 0, 2)).reshape(t.shape[1], t.shape[0] * HD)


def _split_qkv(qkv):
    q = _heads(qkv[:, :H * HD], H) * jnp.asarray(HD ** -0.5, BF16)
    k = _heads(qkv[:, H * HD:(H + KV) * HD], KV)
    v = _heads(qkv[:, (H + KV) * HD:], KV)
    return (q[:, :N_META], q[:, N_META:], k[:, :N_META], k[:, N_META:], v[:, :N_META], v[:, N_META:])


def _adam(w, g, m, v, name):
    R, C = w.shape
    tr = _tile(R, 256) if R % 16 == 0 else R

    def body(w_ref, g_ref, m_ref, v_ref, d_ref, mo_ref, vo_ref):
        gv = g_ref[...]
        mn = B1 * m_ref[...] + (1.0 - B1) * gv
        vn = B2 * v_ref[...] + (1.0 - B2) * (gv * gv)
        m_hat = mn / (1.0 - B1 ** STEP)
        v_hat = vn / (1.0 - B2 ** STEP)
        d_ref[...] = -LR * (m_hat / (jnp.sqrt(v_hat) + ADAM_EPS) + WD * w_ref[...])
        mo_ref[...] = mn
        vo_ref[...] = vn

    blk = pl.BlockSpec((tr, C), lambda i: (i, 0))
    shp = jax.ShapeDtypeStruct((R, C), F32)
    return pl.pallas_call(body, name=name, grid=(R // tr,), in_specs=[blk] * 4, out_specs=(blk,) * 3,
                          out_shape=(shp,) * 3, compiler_params=_params(1))(w, g, m, v)


def _pair_add(g, r, c, name):
    _, _, R, C = g.shape
    tr = _tile(R)

    def body(c_ref, g_ref, r_ref, o_ref):
        o_ref[...] = (g_ref[...].astype(F32) + r_ref[...].astype(F32)).astype(BF16)

    grid_spec = pltpu.PrefetchScalarGridSpec(
        num_scalar_prefetch=1, grid=(NSH, R // tr),
        in_specs=[pl.BlockSpec((None, None, tr, C), lambda s, i, c_ref: (s, c_ref[0], i, 0)),
                  pl.BlockSpec((None, tr, C), lambda s, i, c_ref: (s, i, 0))],
        out_specs=pl.BlockSpec((None, tr, C), lambda s, i, c_ref: (s, i, 0)))
    return pl.pallas_call(body, name=name, grid_spec=grid_spec, out_shape=jax.ShapeDtypeStruct((NSH, R, C), BF16),
                          compiler_params=_params(2))(c, g, r)


def _sum4(r, name):
    _, R, C = r.shape
    tr = _tile(R)

    def body(r_ref, o_ref):
        acc = r_ref[0].astype(F32)
        for s in range(1, NSH):
            acc = acc + r_ref[s].astype(F32)
        o_ref[...] = acc

    return pl.pallas_call(body, name=name, grid=(R // tr,),
                          in_specs=[pl.BlockSpec((NSH, tr, C), lambda i: (0, i, 0))],
                          out_specs=pl.BlockSpec((tr, C), lambda i: (i, 0)),
                          out_shape=jax.ShapeDtypeStruct((R, C), F32), compiler_params=_params(1))(r)


def _place():
    x, y, c = lax.axis_index("x"), lax.axis_index("y"), lax.axis_index("c")
    chips = [(1 - x, y), (x, 1 - y), (1 - x, 1 - y)]
    return x, y, c, chips


def _rcopy(src, dst, ssem, rsem, dev):
    return pltpu.make_async_remote_copy(src_ref=src, dst_ref=dst, send_sem=ssem, recv_sem=rsem,
                                        device_id=dev, device_id_type=MESH)


def _gather_weights(big, small):
    nbig = len(big)

    def body(*refs):
        ins, sm = refs[:nbig], refs[nbig]
        outs, smo = refs[nbig + 1:2 * nbig + 1], refs[2 * nbig + 1]
        s_ici, r_ici, s_fwd, r_fwd, s_sm, r_sm, loc = refs[2 * nbig + 2:]
        x, y, c, chips = _place()
        me, sib = 2 * x + y, (x, y, 1 - c)
        local = [pltpu.make_async_copy(ins[k], outs[k].at[me], loc.at[k]) for k in range(nbig)]
        local.append(pltpu.make_async_copy(sm, smo.at[me], loc.at[nbig]))
        for cp in local:
            cp.start()
        sends = []
        for k in range(nbig):
            for j, (cx, cy) in enumerate(chips):
                sends.append(_rcopy(ins[k].at[c], outs[k].at[me, c], s_ici.at[3 * k + j], r_ici.at[3 * k + j],
                                    (cx, cy, c)))
        for j, (cx, cy) in enumerate(chips):
            sends.append(_rcopy(sm, smo.at[me], s_sm.at[j], r_sm.at[j], (cx, cy, c)))
        for cp in sends:
            cp.start()
        for k in range(nbig):
            for j, (cx, cy) in enumerate(chips):
                got = outs[k].at[2 * cx + cy, c]
                _rcopy(got, got, s_ici.at[3 * k + j], r_ici.at[3 * k + j], (cx, cy, c)).wait_recv()
                fwd = _rcopy(got, got, s_fwd.at[3 * k + j], r_fwd.at[3 * k + j], sib)
                fwd.start()
                sends.append(fwd)
        for k in range(nbig):
            for j, (cx, cy) in enumerate(chips):
                got = outs[k].at[2 * cx + cy, 1 - c]
                _rcopy(got, got, s_fwd.at[3 * k + j], r_fwd.at[3 * k + j], sib).wait_recv()
        for j, (cx, cy) in enumerate(chips):
            got = smo.at[2 * cx + cy]
            _rcopy(got, got, s_sm.at[j], r_sm.at[j], (cx, cy, c)).wait_recv()
        for cp in sends:
            cp.wait_send()
        for cp in local:
            cp.wait()

    out_shape = [jax.ShapeDtypeStruct((NSH,) + a.shape, a.dtype) for a in big]
    out_shape.append(jax.ShapeDtypeStruct((NSH,) + small.shape, small.dtype))
    dma = pltpu.SemaphoreType.DMA
    res = pl.pallas_call(
        body, name="gather_weights", in_specs=[ANY] * (nbig + 1), out_specs=[ANY] * (nbig + 1),
        out_shape=out_shape,
        scratch_shapes=[dma((3 * nbig,)), dma((3 * nbig,)), dma((3 * nbig,)), dma((3 * nbig,)), dma((3,)), dma((3,)),
                        dma((nbig + 1,))])(*big, small)
    return res[:nbig], res[nbig]


def _pair_exchange(grads):
    n = len(grads)

    def body(*refs):
        ins, outs = refs[:n], refs[n:2 * n]
        ssem, rsem = refs[2 * n:]
        x, y, c, _ = _place()
        sib = (x, y, 1 - c)
        cps = [_rcopy(ins[k].at[:, 1 - c], outs[k], ssem.at[k], rsem.at[k], sib) for k in range(n)]
        for cp in cps:
            cp.start()
        for cp in cps:
            cp.wait_recv()
        for cp in cps:
            cp.wait_send()

    dma = pltpu.SemaphoreType.DMA
    return pl.pallas_call(
        body, name="grad_pair_exchange", in_specs=[ANY] * n, out_specs=[ANY] * n,
        out_shape=[jax.ShapeDtypeStruct((NSH,) + g.shape[2:], g.dtype) for g in grads],
        scratch_shapes=[dma((n,)), dma((n,))])(*grads)


def _chip_exchange(parts):
    n = len(parts)

    def body(*refs):
        ins, outs = refs[:n], refs[n:2 * n]
        ssem, rsem, loc = refs[2 * n:]
        x, y, c, chips = _place()
        me = 2 * x + y
        local = [pltpu.make_async_copy(ins[k].at[me], outs[k].at[me], loc.at[k]) for k in range(n)]
        for cp in local:
            cp.start()
        sends = []
        for k in range(n):
            for j, (cx, cy) in enumerate(chips):
                sends.append(_rcopy(ins[k].at[2 * cx + cy], outs[k].at[me], ssem.at[3 * k + j], rsem.at[3 * k + j],
                                    (cx, cy, c)))
        for cp in sends:
            cp.start()
        for k in range(n):
            for j, (cx, cy) in enumerate(chips):
                got = outs[k].at[2 * cx + cy]
                _rcopy(got, got, ssem.at[3 * k + j], rsem.at[3 * k + j], (cx, cy, c)).wait_recv()
        for cp in sends:
            cp.wait_send()
        for cp in local:
            cp.wait()

    dma = pltpu.SemaphoreType.DMA
    return pl.pallas_call(
        body, name="grad_chip_exchange", in_specs=[ANY] * n, out_specs=[ANY] * n,
        out_shape=[jax.ShapeDtypeStruct(p.shape, p.dtype) for p in parts],
        scratch_shapes=[dma((3 * n,)), dma((3 * n,)), dma((n,))])(*parts)


def _pair_share(halves):
    n = len(halves)

    def body(*refs):
        ins, outs = refs[:n], refs[n:2 * n]
        ssem, rsem, loc = refs[2 * n:]
        x, y, c, _ = _place()
        sib = (x, y, 1 - c)
        local = [pltpu.make_async_copy(ins[k], outs[k].at[c], loc.at[k]) for k in range(n)]
        cps = [_rcopy(ins[k], outs[k].at[c], ssem.at[k], rsem.at[k], sib) for k in range(n)]
        for cp in local + cps:
            cp.start()
        for k in range(n):
            got = outs[k].at[1 - c]
            _rcopy(got, got, ssem.at[k], rsem.at[k], sib).wait_recv()
        for cp in cps:
            cp.wait_send()
        for cp in local:
            cp.wait()

    dma = pltpu.SemaphoreType.DMA
    return pl.pallas_call(
        body, name="grad_pair_share", in_specs=[ANY] * n, out_specs=[ANY] * n,
        out_shape=[jax.ShapeDtypeStruct((2,) + h.shape, h.dtype) for h in halves],
        scratch_shapes=[dma((n,)), dma((n,)), dma((n,))])(*halves)


def _allreduce_small(pack):
    R = pack.shape[0]

    def body(in_ref, out_ref, buf, ssem, rsem):
        x, y, c, _ = _place()
        me = 4 * x + 2 * y + c
        buf[me] = in_ref[...]
        peers = []
        for k in range(1, 8):
            fx, fy, fc = (k >> 2) & 1, (k >> 1) & 1, k & 1
            peers.append((1 - x if fx else x, 1 - y if fy else y, 1 - c if fc else c))
        cps = [_rcopy(in_ref, buf.at[me], ssem.at[k], rsem.at[k], p) for k, p in enumerate(peers)]
        for cp in cps:
            cp.start()
        for k, (px, py, pc) in enumerate(peers):
            got = buf.at[4 * px + 2 * py + pc]
            _rcopy(got, got, ssem.at[k], rsem.at[k], (px, py, pc)).wait_recv()
        for cp in cps:
            cp.wait_send()
        acc = buf[0]
        for s in range(1, 8):
            acc = acc + buf[s]
        out_ref[...] = acc

    dma = pltpu.SemaphoreType.DMA
    return pl.pallas_call(
        body, name="allreduce_small", out_shape=jax.ShapeDtypeStruct(pack.shape, F32),
        in_specs=[pl.BlockSpec(memory_space=pltpu.VMEM)], out_specs=pl.BlockSpec(memory_space=pltpu.VMEM),
        scratch_shapes=[pltpu.VMEM((8, R, 128), F32), dma((7,)), dma((7,))])(pack)


def _pack(arrs):
    flat = jnp.concatenate([a.reshape(-1).astype(F32) for a in arrs])
    n = flat.shape[0]
    rows = -(-n // 1024) * 8
    return jnp.pad(flat, (0, rows * 128 - n)).reshape(rows, 128)


def _unpack(pack, shapes):
    flat, out, o = pack.reshape(-1), [], 0
    for s in shapes:
        n = int(np.prod(s))
        out.append(flat[o:o + n].reshape(s))
        o += n
    return out


def kernel(x, meta_tokens, rel_bias_table, norm_mix, norm_ffn, norm_final, attn_w_qkv, attn_b_qkv, attn_w_o, attn_b_o, attn_sinks, conv_w_in, conv_w, conv_w_out, pool_w, pool_scale, ffn_w_gate, ffn_w_up, ffn_w_down, loss_target, m_meta_tokens, m_rel_bias_table, m_norm_mix, m_norm_ffn, m_norm_final, m_attn_w_qkv, m_attn_b_qkv, m_attn_w_o, m_attn_b_o, m_attn_sinks, m_conv_w_in, m_conv_w, m_conv_w_out, m_pool_w, m_pool_scale, m_ffn_w_gate, m_ffn_w_up, m_ffn_w_down, v_meta_tokens, v_rel_bias_table, v_norm_mix, v_norm_ffn, v_norm_final, v_attn_w_qkv, v_attn_b_qkv, v_attn_w_o, v_attn_b_o, v_attn_sinks, v_conv_w_in, v_conv_w, v_conv_w_out, v_pool_w, v_pool_scale, v_ffn_w_gate, v_ffn_w_up, v_ffn_w_down):
    args = locals()
    w = {n: args[n] for n in WEIGHTS}
    mom = {n: args['m_' + n] for n in WEIGHTS}
    var = {n: args['v_' + n] for n in WEIGHTS}
    mx, my, mc = lax.axis_index("x"), lax.axis_index("y"), lax.axis_index("c")
    chip = 2 * mx + my
    S = x.shape[1]
    L = S + N_META

    small_in = jnp.concatenate([
        jnp.pad(w['meta_tokens'], ((0, 0), (0, 128))), w['attn_b_qkv'], jnp.pad(w['attn_b_o'], ((0, 0), (0, 128))),
        jnp.pad(w['conv_w'][0], ((0, 0), (0, 128))), jnp.pad(w['pool_scale'], ((0, 0), (0, 128)))], axis=0)
    gathered, gsmall = _gather_weights([w[n].astype(BF16).reshape(COMM_SHAPE[n]) for n in BIG], small_in)
    G = dict(zip(BIG, gathered))

    def cols(rows, width):
        return jnp.transpose(rows[:, :, :width], (1, 0, 2)).reshape(rows.shape[1], NSH * width)

    meta_full = cols(gsmall[:, 0:16], 256)
    b_qkv = cols(gsmall[:, 16:18], 384)
    b_o = cols(gsmall[:, 18:20], 256)
    conv_k = cols(gsmall[:, 20:23], 256)
    p_scale = cols(gsmall[:, 23:24], 256)
    w_qkv = [G['attn_w_qkv'][:, j] for j in range(2)]
    w_o = [G['attn_w_o'][:, j].reshape(1, D, D) for j in range(2)]
    w_cin = G['conv_w_in'].reshape(NSH, D, 3 * D // NSH)
    w_cout = G['conv_w_out'].reshape(1, D, D)
    w_pool = jnp.transpose(G['pool_w'].reshape(NSH, 4, PG // NSH, PG), (1, 0, 2, 3)).reshape(4, PG, PG)
    w_gate = G['ffn_w_gate'].reshape(NSH, DEPTH, D, FC)
    w_up = G['ffn_w_up'].reshape(NSH, DEPTH, D, FC)
    w_down = G['ffn_w_down'].reshape(NSH, DEPTH, FC, D)

    idx = jnp.asarray(_bias_index()[0]).reshape(1, -1)
    table_aug = jnp.concatenate([rel_bias_table.T, jnp.full((H, 1), NEG, F32),
                                 jnp.zeros((H, 127 - N_BUCKETS), F32)], axis=1)
    band, bmeta, bias_mm = _split_bias(_bias_lookup(table_aug, idx, "bias_lookup"))

    h = jnp.concatenate([meta_full, x[0]], axis=0)
    saved = []
    for i in range(DEPTH):
        kind, j = i % 3, i // 3
        gm = norm_mix[i:i + 1]
        st = dict(h=h)
        if kind == 0:
            a = _rms_fwd(h, gm, f"rms_mix{i}")
            qkv = _linear(a, w_qkv[j], bias=b_qkv[j:j + 1], name=f"qkv{i}")
            qm, qr, km, kr, vm, vr = _split_qkv(qkv)
            sink_r = jnp.broadcast_to(attn_sinks[j].reshape(KV, GRP, 1, 1), (KV, GRP, BLK, 1)).reshape(KV, GRP * BLK, 1)
            sink_m = jnp.broadcast_to(attn_sinks[j].reshape(H, 1, 1), (H, N_META, 1))
            o_r = _attn_fwd(qr, kr, vr, km, vm, band, bmeta, sink_r, f"attn_fwd{i}")
            o_m = _attn_meta_fwd(qm, km, vm, bias_mm, sink_m, f"attn_meta_fwd{i}")
            o = _unheads(jnp.concatenate([o_m, o_r], axis=1))
            h1 = _linear(o, w_o[j], bias=b_o[j:j + 1], resid=h, out_dtype=F32, name=f"attn_out{i}")
            st.update(a=a, qs=(qr, kr, vr, km, vm, qm), sinks=(sink_r, sink_m), o=o)
        elif kind == 1:
            a = _rms_fwd(h, gm, f"rms_mix{i}")
            z3 = _linear(a, w_cin, out_dtype=F32, name=f"conv_in{i}")
            yv = _conv_fwd(z3, conv_k, f"conv_fwd{i}")
            h1 = _linear(yv, w_cout, resid=h, out_dtype=F32, name=f"conv_out{i}")
            st.update(a=a, z3=z3, y=yv)
        else:
            mix = _pool_fwd(h, gm, f"pool_fwd{i}")
            h1 = _pool_out(mix, w_pool, p_scale, h, f"pool_out{i}")
            st.update(mix=mix)
        hn, Gp, Up, Ap, Bp = _ffn_fwd(h1, norm_ffn[i:i + 1], w_gate, w_up, w_down, i, f"ffn_fwd{i}")
        st.update(h1=h1, G=Gp, U=Up, A=Ap, B=Bp)
        saved.append(st)
        h = hn

    target = jnp.pad(loss_target[0], ((N_META, 0), (0, 0)))
    dh, d_nfinal, loss_part = _loss_head(h, norm_final.reshape(1, D), target)
    d_nmix, d_nffn = [None] * DEPTH, [None] * DEPTH
    dW = {n: [] for n in BIG}
    d_bqkv, d_bo, d_sinks = [None, None], [None, None], [None, None]
    d_table = jnp.zeros((N_BUCKETS, H), F32)
    d_convk = d_pscale = None
    for i in reversed(range(DEPTH)):
        kind, j = i % 3, i // 3
        st = saved[i]
        DG, DU, dh1, d_nffn[i] = _ffn_bwd_act(dh, st['h1'], norm_ffn[i:i + 1], st['G'], st['U'], w_gate, w_up, w_down,
                                              i, f"ffn_bwd_act{i}")
        gw, uw, dw_ = _ffn_bwd_w(st['B'], st['A'], DG, DU, dh, f"ffn_bwd_w{i}")
        dW['ffn_w_gate'].insert(0, gw)
        dW['ffn_w_up'].insert(0, uw)
        dW['ffn_w_down'].insert(0, dw_)
        gm = norm_mix[i:i + 1]
        if kind == 0:
            qr, kr, vr, km, vm, qm = st['qs']
            sink_r, sink_m = st['sinks']
            gwo, d_bo[j] = _wgrad(st['o'], dh1, mode='row', J=NSH, colsum=True, name=f"attn_out_wgrad{i}")
            dW['attn_w_o'].insert(0, gwo)
            do = _heads(_linear_t(dh1, w_o[j], name=f"attn_out_bwd{i}"), H)
            dq_r, dk_r, dv_r, dkm, dvm, dband, dbmeta = _attn_bwd(qr, kr, vr, km, vm, band, bmeta, sink_r,
                                                                  do[:, N_META:], f"attn_bwd{i}")
            dq_m, dkm2, dvm2, dbmm = _attn_meta_bwd(qm, km, vm, bias_mm, sink_m, do[:, :N_META], f"attn_meta_bwd{i}")
            dbias = jnp.concatenate([dband.reshape(2, H, -1)[0], dband.reshape(2, H, -1)[1],
                                     dbmeta.reshape(2, H, -1)[0], dbmeta.reshape(2, H, -1)[1],
                                     dbmm.reshape(H, -1)], axis=1)
            dts = _bias_lookup_bwd(dbias, idx, f"bias_lookup_bwd{i}")
            d_table = d_table + dts[:, :N_BUCKETS].T
            d_sinks[j] = dts[:, N_BUCKETS]
            dq = jnp.concatenate([dq_m, dq_r], axis=1) * jnp.asarray(HD ** -0.5, BF16)
            dk = jnp.concatenate([(dkm + dkm2).astype(BF16), dk_r], axis=1)
            dv = jnp.concatenate([(dvm + dvm2).astype(BF16), dv_r], axis=1)
            dqkv = jnp.concatenate([_unheads(dq), _unheads(dk), _unheads(dv)], axis=1)
            gq, d_bqkv[j] = _wgrad(st['a'], dqkv, mode='col', J=NSH, colsum=True, name=f"qkv_wgrad{i}")
            dW['attn_w_qkv'].insert(0, gq)
            dh, d_nmix[i] = _linear_t(dqkv, w_qkv[j], rms=(st['h'], gm, dh1), name=f"qkv_bwd{i}")
        elif kind == 1:
            dW['conv_w_out'].append(_wgrad(st['y'], dh1, mode='row', J=NSH, name=f"conv_out_wgrad{i}"))
            dy = _linear_t(dh1, w_cout, out_dtype=F32, name=f"conv_out_bwd{i}")
            dz3, d_convk = _conv_bwd(st['z3'], conv_k, dy, f"conv_bwd{i}")
            dW['conv_w_in'].append(_wgrad(st['a'], dz3, mode='col', J=NSH, name=f"conv_in_wgrad{i}"))
            dh, d_nmix[i] = _linear_t(dz3, w_cin, rms=(st['h'], gm, dh1), name=f"conv_in_bwd{i}")
        else:
            dmix, dwp, d_pscale = _pool_out_bwd(dh1, st['mix'], w_pool, p_scale, f"pool_out_bwd{i}")
            dW['pool_w'].append(jnp.transpose(dwp.reshape(4, NSH, PG // NSH, PG), (1, 0, 2, 3)).astype(BF16))
            dh, d_nmix[i] = _pool_bwd(dmix, st['h'], gm, dh1, f"pool_bwd{i}")

    full = [jnp.stack(dW[n], axis=1).reshape((NSH,) + COMM_SHAPE[n]) for n in BIG]
    from_sib = _pair_exchange(full)
    core = mc.astype(jnp.int32).reshape(1)
    parts = [_pair_add(g, r, core, f"pair_add_{n}") for g, r, n in zip(full, from_sib, BIG)]
    arrived = _chip_exchange(parts)
    halves = [_sum4(r, f"chip_sum_{n}") for r, n in zip(arrived, BIG)]
    shared = _pair_share(halves)
    grads = {n: s.reshape(w[n].shape) for n, s in zip(BIG, shared)}

    small_full = [dh[:N_META], d_table, jnp.concatenate(d_nmix, axis=0), jnp.concatenate(d_nffn, axis=0), d_nfinal,
                  jnp.concatenate(d_bqkv, axis=0), jnp.concatenate(d_bo, axis=0), jnp.stack(d_sinks), d_convk,
                  d_pscale, loss_part[:, 0:1]]
    red = _unpack(_allreduce_small(_pack(small_full)), [a.shape for a in small_full])
    g_meta, g_table, g_nmix, g_nffn, g_nfinal, g_bqkv, g_bo, g_sinks, g_convk, g_pscale, loss = red

    def shard(a, width):
        return lax.dynamic_slice_in_dim(a, chip * width, width, axis=1)

    grads.update(meta_tokens=shard(g_meta, 256), rel_bias_table=g_table, norm_mix=g_nmix, norm_ffn=g_nffn,
                 norm_final=g_nfinal.reshape(D), attn_b_qkv=shard(g_bqkv, 384), attn_b_o=shard(g_bo, 256),
                 attn_sinks=g_sinks, conv_w=shard(g_convk, 256)[None], pool_scale=shard(g_pscale, 256))

    delta, new_m, new_v = {}, {}, {}
    for n in BIG:
        shp = w[n].shape
        r2 = (int(np.prod(shp[:-1])), shp[-1])
        dl, mn, vn = _adam(w[n].reshape(r2), grads[n].reshape(r2), mom[n].reshape(r2), var[n].reshape(r2), f"adam_{n}")
        delta[n], new_m[n], new_v[n] = dl.reshape(shp), mn.reshape(shp), vn.reshape(shp)
    shapes = [w[n].shape for n in SMALL]
    packed = [_pack([d[n] for n in SMALL]) for d in (w, grads, mom, var)]
    for dst, res in zip((delta, new_m, new_v), _adam(*packed, "adam_small")):
        dst.update(zip(SMALL, _unpack(res, shapes)))

    return (loss.reshape(()), dh[N_META:][None], *[grads[n] for n in WEIGHTS], *[delta[n] for n in WEIGHTS],
            *[new_m[n] for n in WEIGHTS], *[new_v[n] for n in WEIGHTS])
```

```python
import functools
import math

import numpy as np
import jax
import jax.numpy as jnp
from jax import lax
from jax.experimental import pallas as pl
from jax.experimental.pallas import tpu as pltpu

F32, BF16 = jnp.float32, jnp.bfloat16
D = 1024
N_META = 16
EPS = 1e-6
H, KV, GRP, HD = 16, 4, 4, 64
BLK = 128
NKEY = 3 * BLK
N_BUCKETS = 32
POOL_WINDOWS = (2, 4, 8, 16)
PG = 256
DFF = 2816
NSH = 4
FC = DFF // NSH
HALO = 16
NEG = -1e30
DEPTH = 4
LR, B1, B2, ADAM_EPS, WD, STEP = 0.001, 0.9, 0.999, 1e-08, 0.01, 10
MESH = pl.DeviceIdType.MESH
ANY = pl.BlockSpec(memory_space=pl.ANY)
VMEM_LIMIT = 48 * 1024 * 1024
VMEM_BIG = 58 * 1024 * 1024
BIG_TILE = 1024

_NN = (((1,), (0,)), ((), ()))
_NT = (((1,), (1,)), ((), ()))
_TN = (((0,), (0,)), ((), ()))

WEIGHTS = ['meta_tokens', 'rel_bias_table', 'norm_mix', 'norm_ffn', 'norm_final', 'attn_w_qkv', 'attn_b_qkv',
           'attn_w_o', 'attn_b_o', 'attn_sinks', 'conv_w_in', 'conv_w', 'conv_w_out', 'pool_w', 'pool_scale',
           'ffn_w_gate', 'ffn_w_up', 'ffn_w_down']
BIG = ['attn_w_qkv', 'attn_w_o', 'conv_w_in', 'conv_w_out', 'pool_w', 'ffn_w_gate', 'ffn_w_up', 'ffn_w_down']
SMALL = [w for w in WEIGHTS if w not in BIG]
COMM_SHAPE = {'attn_w_qkv': (2, 1024, 384), 'attn_w_o': (2, 256, 1024), 'conv_w_in': (2, 512, 768),
              'conv_w_out': (2, 128, 1024), 'pool_w': (2, 128, 256), 'ffn_w_gate': (2, 2048, 704),
              'ffn_w_up': (2, 2048, 704), 'ffn_w_down': (2, 1408, 1024)}


def _dot(a, b, dims=_NN):
    return lax.dot_general(a, b, dims, preferred_element_type=F32)


def _tile(n, cap=512):
    best = None
    for t in range(16, min(n, cap) + 1, 16):
        if n % t == 0:
            best = t
    assert best is not None, n
    return best


def _params(n_axes, vmem=VMEM_LIMIT):
    return pltpu.CompilerParams(dimension_semantics=("arbitrary",) * n_axes, vmem_limit_bytes=vmem)


def _rstd(x):
    return lax.rsqrt(jnp.mean(x * x, axis=-1, keepdims=True) + EPS)


def _rms_bwd(dy, x, g, r):
    u = dy * g
    dx = r * u - x * ((r * r * r) * (jnp.sum(x * u, axis=-1, keepdims=True) * (1.0 / D)))
    return dx, jnp.sum(dy * (x * r), axis=0, keepdims=True)


def _sigmoid(x):
    return 1.0 / (1.0 + jnp.exp(-x))


def _acc(ref, val, first):
    @pl.when(first)
    def _():
        ref[...] = val

    @pl.when(jnp.logical_not(first))
    def _():
        ref[...] += val


def _linear(x, w, *, gain=None, bias=None, resid=None, out_dtype=BF16, cap=BIG_TILE, name):
    L, K = x.shape
    J, _, Nc = w.shape
    tm = _tile(L, cap)
    has_g, has_b, has_r = gain is not None, bias is not None, resid is not None

    def body(*refs):
        refs = list(refs)
        x_ref, w_ref = refs[:2]
        i = 2
        g_ref = b_ref = r_ref = None
        if has_g:
            g_ref, i = refs[i], i + 1
        if has_b:
            b_ref, i = refs[i], i + 1
        if has_r:
            r_ref, i = refs[i], i + 1
        out_ref = refs[i]
        if has_g:
            xf = x_ref[...]
            xv = (xf * _rstd(xf) * g_ref[...]).astype(BF16)
            refs[i + 1][...] = xv
        else:
            xv = x_ref[...]
        for s in range(J):
            sl = slice(s * Nc, (s + 1) * Nc)
            acc = _dot(xv, w_ref[s])
            if has_b:
                acc = acc + b_ref[:, sl]
            if has_r:
                acc = acc + r_ref[:, sl]
            out_ref[:, sl] = acc.astype(out_dtype)

    row = lambda n: pl.BlockSpec((tm, n), lambda t: (t, 0))
    one = lambda n: pl.BlockSpec((1, n), lambda t: (0, 0))
    in_specs = [row(K), pl.BlockSpec((J, K, Nc), lambda t: (0, 0, 0))]
    ops = [x, w]
    if has_g:
        in_specs.append(one(K))
        ops.append(gain)
    if has_b:
        in_specs.append(one(J * Nc))
        ops.append(bias)
    if has_r:
        in_specs.append(row(J * Nc))
        ops.append(resid)
    out_specs, out_shape = row(J * Nc), jax.ShapeDtypeStruct((L, J * Nc), out_dtype)
    if has_g:
        out_specs, out_shape = (out_specs, row(K)), (out_shape, jax.ShapeDtypeStruct((L, K), BF16))
    return pl.pallas_call(body, name=name, grid=(L // tm,), in_specs=in_specs, out_specs=out_specs,
                          out_shape=out_shape, compiler_params=_params(1, VMEM_BIG))(*ops)


def _linear_t(dy, w, *, out_dtype=BF16, rms=None, cap=BIG_TILE, name):
    L = dy.shape[0]
    J, K, Nc = w.shape
    tm = _tile(L, cap)
    has_rms = rms is not None

    def body(*refs):
        t = pl.program_id(0)
        dy_ref, w_ref = refs[:2]
        acc = _dot(dy_ref[:, 0:Nc].astype(BF16), w_ref[0], _NT)
        for s in range(1, J):
            acc = acc + _dot(dy_ref[:, s * Nc:(s + 1) * Nc].astype(BF16), w_ref[s], _NT)
        if has_rms:
            h_ref, g_ref, r_ref, out_ref, dg_ref = refs[2:7]
            x = h_ref[...]
            dx, dg = _rms_bwd(acc, x, g_ref[...], _rstd(x))
            out_ref[...] = r_ref[...] + dx
            _acc(dg_ref, dg, t == 0)
        else:
            refs[2][...] = acc.astype(out_dtype)

    row = lambda n: pl.BlockSpec((tm, n), lambda t: (t, 0))
    in_specs = [row(J * Nc), pl.BlockSpec((J, K, Nc), lambda t: (0, 0, 0))]
    ops = [dy, w]
    if has_rms:
        in_specs += [row(K), pl.BlockSpec((1, K), lambda t: (0, 0)), row(K)]
        ops += list(rms)
        out_specs = (row(K), pl.BlockSpec((1, K), lambda t: (0, 0)))
        out_shape = (jax.ShapeDtypeStruct((L, K), F32), jax.ShapeDtypeStruct((1, K), F32))
    else:
        out_specs = row(K)
        out_shape = jax.ShapeDtypeStruct((L, K), out_dtype)
    return pl.pallas_call(body, name=name, grid=(L // tm,), in_specs=in_specs, out_specs=out_specs,
                          out_shape=out_shape, compiler_params=_params(1, VMEM_BIG))(*ops)


def _wgrad(x, dy, *, mode, colsum=False, cap=BIG_TILE, name):
    L, K = x.shape
    N = dy.shape[1]
    tm = _tile(L, cap)
    nt = L // tm
    oshape = (NSH, K, N // NSH) if mode == 'col' else (NSH, K // NSH, N)

    def body(*refs):
        t = pl.program_id(0)
        x_ref, dy_ref, out_ref = refs[:3]
        acc_ref = refs[-1]
        dyv = dy_ref[...]
        _acc(acc_ref, _dot(x_ref[...].astype(BF16), dyv.astype(BF16), _TN), t == 0)
        if colsum:
            _acc(refs[3], jnp.sum(dyv.astype(F32), axis=0, keepdims=True), t == 0)

        @pl.when(t == nt - 1)
        def _():
            for s in range(NSH):
                if mode == 'col':
                    out_ref[s] = acc_ref[:, s * oshape[2]:(s + 1) * oshape[2]].astype(BF16)
                else:
                    out_ref[s] = acc_ref[s * oshape[1]:(s + 1) * oshape[1], :].astype(BF16)

    out_specs = pl.BlockSpec(oshape, lambda t: (0, 0, 0))
    out_shape = jax.ShapeDtypeStruct(oshape, BF16)
    if colsum:
        out_specs = (out_specs, pl.BlockSpec((1, N), lambda t: (0, 0)))
        out_shape = (out_shape, jax.ShapeDtypeStruct((1, N), F32))
    return pl.pallas_call(
        body, name=name, grid=(nt,),
        in_specs=[pl.BlockSpec((tm, K), lambda t: (t, 0)), pl.BlockSpec((tm, N), lambda t: (t, 0))],
        out_specs=out_specs, out_shape=out_shape,
        scratch_shapes=[pltpu.VMEM((K, N), F32)], compiler_params=_params(1, VMEM_BIG))(x, dy)


def _ffn_fwd(h, gain, wg, wu, wd, layer, name):
    L = h.shape[0]
    tm = _tile(L, BIG_TILE)

    def body(h_ref, g_ref, wg_ref, wu_ref, wd_ref, hn_ref, G_ref, U_ref, A_ref, B_ref, acc_ref, b_scr):
        j = pl.program_id(1)

        @pl.when(j == 0)
        def _():
            x = h_ref[...]
            b = (x * _rstd(x) * g_ref[...]).astype(BF16)
            b_scr[...] = b
            B_ref[...] = b
            acc_ref[...] = x

        b = b_scr[...]
        g = _dot(b, wg_ref[...])
        u = _dot(b, wu_ref[...])
        G_ref[...] = g.astype(BF16)
        U_ref[...] = u.astype(BF16)
        a = ((g * _sigmoid(g)) * u).astype(BF16)
        A_ref[...] = a
        acc_ref[...] += _dot(a, wd_ref[...])

        @pl.when(j == NSH - 1)
        def _():
            hn_ref[...] = acc_ref[...]

    row = pl.BlockSpec((tm, D), lambda t, j: (t, 0))
    chunk = pl.BlockSpec((None, tm, FC), lambda t, j: (j, t, 0))
    cshape = jax.ShapeDtypeStruct((NSH, L, FC), BF16)
    return pl.pallas_call(
        body, name=name, grid=(L // tm, NSH),
        in_specs=[row, pl.BlockSpec((1, D), lambda t, j: (0, 0)),
                  pl.BlockSpec((None, None, D, FC), lambda t, j: (j, layer, 0, 0)),
                  pl.BlockSpec((None, None, D, FC), lambda t, j: (j, layer, 0, 0)),
                  pl.BlockSpec((None, None, FC, D), lambda t, j: (j, layer, 0, 0))],
        out_specs=(row, chunk, chunk, chunk, row),
        out_shape=(jax.ShapeDtypeStruct((L, D), F32), cshape, cshape, cshape, jax.ShapeDtypeStruct((L, D), BF16)),
        scratch_shapes=[pltpu.VMEM((tm, D), F32), pltpu.VMEM((tm, D), BF16)],
        compiler_params=_params(2, VMEM_BIG))(h, gain, wg, wu, wd)


def _ffn_bwd_act(dhn, h, gain, G, U, wg, wu, wd, layer, name):
    L = h.shape[0]
    tm = _tile(L)

    def body(dhn_ref, h_ref, g_ref, G_ref, U_ref, wg_ref, wu_ref, wd_ref, DG_ref, DU_ref, dh_ref, dgain_ref, db_ref):
        t, j = pl.program_id(0), pl.program_id(1)
        d_act = _dot(dhn_ref[...].astype(BF16), wd_ref[...], _NT)
        g = G_ref[...].astype(F32)
        u = U_ref[...].astype(F32)
        s = _sigmoid(g)
        dg = (d_act * u * (s * (1.0 + g * (1.0 - s)))).astype(BF16)
        du = (d_act * (g * s)).astype(BF16)
        DG_ref[...] = dg
        DU_ref[...] = du
        _acc(db_ref, _dot(dg, wg_ref[...], _NT) + _dot(du, wu_ref[...], _NT), j == 0)

        @pl.when(j == NSH - 1)
        def _():
            x = h_ref[...]
            dx, dgn = _rms_bwd(db_ref[...], x, g_ref[...], _rstd(x))
            dh_ref[...] = dhn_ref[...] + dx
            _acc(dgain_ref, dgn, t == 0)

    row = pl.BlockSpec((tm, D), lambda t, j: (t, 0))
    one = pl.BlockSpec((1, D), lambda t, j: (0, 0))
    chunk = pl.BlockSpec((None, tm, FC), lambda t, j: (j, t, 0))
    cshape = jax.ShapeDtypeStruct((NSH, L, FC), BF16)
    return pl.pallas_call(
        body, name=name, grid=(L // tm, NSH),
        in_specs=[row, row, one, chunk, chunk,
                  pl.BlockSpec((None, None, D, FC), lambda t, j: (j, layer, 0, 0)),
                  pl.BlockSpec((None, None, D, FC), lambda t, j: (j, layer, 0, 0)),
                  pl.BlockSpec((None, None, FC, D), lambda t, j: (j, layer, 0, 0))],
        out_specs=(chunk, chunk, row, one),
        out_shape=(cshape, cshape, jax.ShapeDtypeStruct((L, D), F32), jax.ShapeDtypeStruct((1, D), F32)),
        scratch_shapes=[pltpu.VMEM((tm, D), F32)],
        compiler_params=_params(2))(dhn, h, gain, G, U, wg, wu, wd)


def _ffn_bwd_w(B, A, DG, DU, dhn, name):
    L = B.shape[0]
    tm = _tile(L, BIG_TILE)
    nt = L // tm

    def body(B_ref, A_ref, DG_ref, DU_ref, dhn_ref, dwg_ref, dwu_ref, dwd_ref, ag, au, ad):
        t = pl.program_id(1)
        b = B_ref[...]
        _acc(ag, _dot(b, DG_ref[...], _TN), t == 0)
        _acc(au, _dot(b, DU_ref[...], _TN), t == 0)
        _acc(ad, _dot(A_ref[...], dhn_ref[...].astype(BF16), _TN), t == 0)

        @pl.when(t == nt - 1)
        def _():
            dwg_ref[...] = ag[...].astype(BF16)
            dwu_ref[...] = au[...].astype(BF16)
            dwd_ref[...] = ad[...].astype(BF16)

    row = pl.BlockSpec((tm, D), lambda j, t: (t, 0))
    chunk = pl.BlockSpec((None, tm, FC), lambda j, t: (j, t, 0))
    return pl.pallas_call(
        body, name=name, grid=(NSH, nt), in_specs=[row, chunk, chunk, chunk, row],
        out_specs=(pl.BlockSpec((None, D, FC), lambda j, t: (j, 0, 0)),
                   pl.BlockSpec((None, D, FC), lambda j, t: (j, 0, 0)),
                   pl.BlockSpec((None, FC, D), lambda j, t: (j, 0, 0))),
        out_shape=(jax.ShapeDtypeStruct((NSH, D, FC), BF16), jax.ShapeDtypeStruct((NSH, D, FC), BF16),
                   jax.ShapeDtypeStruct((NSH, FC, D), BF16)),
        scratch_shapes=[pltpu.VMEM((D, FC), F32), pltpu.VMEM((D, FC), F32), pltpu.VMEM((FC, D), F32)],
        compiler_params=_params(2, VMEM_BIG))(B, A, DG, DU, dhn)


def _loss_head(h, gain, target):
    L = h.shape[0]
    tm = _tile(L)

    def body(h_ref, g_ref, tgt_ref, dh_ref, dg_ref, loss_ref):
        t = pl.program_id(0)
        x = h_ref[...]
        g = g_ref[...]
        r = _rstd(x)
        rows = t * tm + lax.broadcasted_iota(jnp.int32, (tm, 1), 0)
        diff = jnp.where(rows >= N_META, x * r * g - tgt_ref[...], 0.0)
        part = 0.5 * jnp.sum(jnp.sum(diff * diff, axis=-1, keepdims=True) * (1.0 / D), axis=0, keepdims=True)
        dx, dg = _rms_bwd(diff * (1.0 / D), x, g, r)
        dh_ref[...] = dx
        _acc(dg_ref, dg, t == 0)
        _acc(loss_ref, jnp.broadcast_to(part, (1, 128)), t == 0)

    row = pl.BlockSpec((tm, D), lambda t: (t, 0))
    one = pl.BlockSpec((1, D), lambda t: (0, 0))
    return pl.pallas_call(
        body, name="loss_head", grid=(L // tm,), in_specs=[row, one, row],
        out_specs=(row, one, pl.BlockSpec((1, 128), lambda t: (0, 0))),
        out_shape=(jax.ShapeDtypeStruct((L, D), F32), jax.ShapeDtypeStruct((1, D), F32),
                   jax.ShapeDtypeStruct((1, 128), F32)),
        compiler_params=_params(1))(h, gain, target)


def _conv_fwd(z3, cw, name):
    L = z3.shape[0]
    tm = _tile(L)

    def body(b_ref, c_ref, u_ref, ch_ref, uh_ref, w_ref, y_ref, buf):
        t = pl.program_id(0)
        z = c_ref[...] * u_ref[...]
        buf[pl.ds(0, HALO), :] = jnp.where(t > 0, ch_ref[...] * uh_ref[...], 0.0)
        buf[pl.ds(HALO, tm), :] = z
        w0, w1, w2 = w_ref[0:1, :], w_ref[1:2, :], w_ref[2:3, :]
        conv = w2 * z + w1 * buf[pl.ds(HALO - 1, tm), :] + w0 * buf[pl.ds(HALO - 2, tm), :]
        y_ref[...] = (b_ref[...] * conv).astype(BF16)

    def col(k):
        return pl.BlockSpec((tm, D), lambda t: (t, k))

    def hcol(k):
        return pl.BlockSpec((HALO, D), lambda t: (jnp.maximum(t * (tm // HALO) - 1, 0), k))

    return pl.pallas_call(
        body, name=name, grid=(L // tm,),
        in_specs=[col(0), col(1), col(2), hcol(1), hcol(2), pl.BlockSpec((3, D), lambda t: (0, 0))],
        out_specs=pl.BlockSpec((tm, D), lambda t: (t, 0)),
        out_shape=jax.ShapeDtypeStruct((L, D), BF16),
        scratch_shapes=[pltpu.VMEM((HALO + tm, D), F32)], compiler_params=_params(1))(z3, z3, z3, z3, z3, cw)


def _conv_bwd(z3, cw, dy, name):
    L = z3.shape[0]
    tm = _tile(L)
    nt = L // tm
    last_h = L // HALO - 1

    def body(b_ref, c_ref, u_ref, ch_ref, uh_ref, w_ref, dy_ref, bn_ref, dyn_ref, dz3_ref, dw_ref, zbuf, dbuf):
        t = pl.program_id(0)
        w0, w1, w2 = w_ref[0:1, :], w_ref[1:2, :], w_ref[2:3, :]
        bgate, cgate, u = b_ref[...], c_ref[...], u_ref[...]
        z = cgate * u
        zbuf[pl.ds(0, HALO), :] = jnp.where(t > 0, ch_ref[...] * uh_ref[...], 0.0)
        zbuf[pl.ds(HALO, tm), :] = z
        z1 = zbuf[pl.ds(HALO - 1, tm), :]
        z2 = zbuf[pl.ds(HALO - 2, tm), :]
        conv = w2 * z + w1 * z1 + w0 * z2
        dyv = dy_ref[...]
        dconv = dyv * bgate
        dbuf[pl.ds(0, tm), :] = dconv
        dbuf[pl.ds(tm, HALO), :] = jnp.where(t < nt - 1, dyn_ref[...] * bn_ref[...], 0.0)
        dz = w2 * dconv + w1 * dbuf[pl.ds(1, tm), :] + w0 * dbuf[pl.ds(2, tm), :]
        dz3_ref[:, 0:D] = (dyv * conv).astype(BF16)
        dz3_ref[:, D:2 * D] = (dz * u).astype(BF16)
        dz3_ref[:, 2 * D:3 * D] = (dz * cgate).astype(BF16)
        for k, zk in enumerate((z2, z1, z)):
            _acc(dw_ref.at[k:k + 1], jnp.sum(dconv * zk, axis=0, keepdims=True), t == 0)

    def col(k):
        return pl.BlockSpec((tm, D), lambda t: (t, k))

    def hprev(k):
        return pl.BlockSpec((HALO, D), lambda t: (jnp.maximum(t * (tm // HALO) - 1, 0), k))

    def hnext(k):
        return pl.BlockSpec((HALO, D), lambda t: (jnp.minimum((t + 1) * (tm // HALO), last_h), k))

    return pl.pallas_call(
        body, name=name, grid=(nt,),
        in_specs=[col(0), col(1), col(2), hprev(1), hprev(2), pl.BlockSpec((3, D), lambda t: (0, 0)),
                  col(0), hnext(0), hnext(0)],
        out_specs=(pl.BlockSpec((tm, 3 * D), lambda t: (t, 0)), pl.BlockSpec((3, D), lambda t: (0, 0))),
        out_shape=(jax.ShapeDtypeStruct((L, 3 * D), BF16), jax.ShapeDtypeStruct((3, D), F32)),
        scratch_shapes=[pltpu.VMEM((HALO + tm, D), F32), pltpu.VMEM((tm + HALO, D), F32)],
        compiler_params=_params(1))(z3, z3, z3, z3, z3, cw, dy, z3, dy)


def _count_inv(pos, w):
    return 1.0 / jnp.minimum(pos + 1, w).astype(F32)


def _pool_fwd(h, gain, name):
    L = h.shape[0]
    tm = _tile(L)

    def body(h_ref, hh_ref, g_ref, mix_ref, buf):
        t = pl.program_id(0)
        g = g_ref[...]
        x = h_ref[...]
        a = x * _rstd(x) * g
        xh = hh_ref[...]
        buf[pl.ds(0, HALO), :] = jnp.where(t > 0, xh * _rstd(xh) * g, 0.0)
        buf[pl.ds(HALO, tm), :] = a
        pos = t * tm + lax.broadcasted_iota(jnp.int32, (tm, 1), 0)
        for gi, w in enumerate(POOL_WINDOWS):
            cols = pl.ds(gi * PG, PG)
            s = buf[pl.ds(HALO, tm), cols]
            for k in range(1, w):
                s = s + buf[pl.ds(HALO - k, tm), cols]
            mix_ref[:, gi * PG:(gi + 1) * PG] = (s / jnp.minimum(pos + 1, w).astype(F32)
                                                  - buf[pl.ds(HALO, tm), cols]).astype(BF16)

    return pl.pallas_call(
        body, name=name, grid=(L // tm,),
        in_specs=[pl.BlockSpec((tm, D), lambda t: (t, 0)),
                  pl.BlockSpec((HALO, D), lambda t: (jnp.maximum(t * (tm // HALO) - 1, 0), 0)),
                  pl.BlockSpec((1, D), lambda t: (0, 0))],
        out_specs=pl.BlockSpec((tm, D), lambda t: (t, 0)),
        out_shape=jax.ShapeDtypeStruct((L, D), BF16),
        scratch_shapes=[pltpu.VMEM((HALO + tm, D), F32)], compiler_params=_params(1))(h, h, gain)


def _pool_out(mix, wp, scale, h, name):
    L = mix.shape[0]
    tm = _tile(L)

    def body(mix_ref, wp_ref, s_ref, h_ref, out_ref):
        for gi in range(4):
            sl = slice(gi * PG, (gi + 1) * PG)
            pre = _dot(mix_ref[:, sl], wp_ref[gi])
            out_ref[:, sl] = h_ref[:, sl] + pre * s_ref[:, sl]

    row = pl.BlockSpec((tm, D), lambda t: (t, 0))
    return pl.pallas_call(
        body, name=name, grid=(L // tm,),
        in_specs=[row, pl.BlockSpec((4, PG, PG), lambda t: (0, 0, 0)), pl.BlockSpec((1, D), lambda t: (0, 0)), row],
        out_specs=row, out_shape=jax.ShapeDtypeStruct((L, D), F32), compiler_params=_params(1))(mix, wp, scale, h)


def _pool_out_bwd(dm, mix, wp, scale, name):
    L = mix.shape[0]
    tm = _tile(L)

    def body(dm_ref, mix_ref, wp_ref, s_ref, dmix_ref, dwp_ref, ds_ref):
        t = pl.program_id(0)
        for gi in range(4):
            sl = slice(gi * PG, (gi + 1) * PG)
            mx = mix_ref[:, sl]
            dmv = dm_ref[:, sl]
            pre = _dot(mx, wp_ref[gi])
            _acc(ds_ref.at[:, sl], jnp.sum(dmv * pre, axis=0, keepdims=True), t == 0)
            dpre = (dmv * s_ref[:, sl]).astype(BF16)
            _acc(dwp_ref.at[gi], _dot(mx, dpre, _TN), t == 0)
            dmix_ref[:, sl] = _dot(dpre, wp_ref[gi], _NT)

    row = pl.BlockSpec((tm, D), lambda t: (t, 0))
    one = pl.BlockSpec((1, D), lambda t: (0, 0))
    wsp = pl.BlockSpec((4, PG, PG), lambda t: (0, 0, 0))
    return pl.pallas_call(
        body, name=name, grid=(L // tm,), in_specs=[row, row, wsp, one],
        out_specs=(row, wsp, one),
        out_shape=(jax.ShapeDtypeStruct((L, D), F32), jax.ShapeDtypeStruct((4, PG, PG), F32),
                   jax.ShapeDtypeStruct((1, D), F32)),
        compiler_params=_params(1))(dm, mix, wp, scale)


def _pool_bwd(dmix, h, gain, resid, name):
    L = h.shape[0]
    tm = _tile(L)
    nt = L // tm
    last_h = L // HALO - 1

    def body(dm_ref, dmn_ref, h_ref, g_ref, r_ref, dh_ref, dg_ref, buf):
        t = pl.program_id(0)
        pos = t * tm + lax.broadcasted_iota(jnp.int32, (tm, 1), 0)
        posn = (t + 1) * tm + lax.broadcasted_iota(jnp.int32, (HALO, 1), 0)
        dmv = dm_ref[...]
        dmn = dmn_ref[...]
        for gi, w in enumerate(POOL_WINDOWS):
            sl = slice(gi * PG, (gi + 1) * PG)
            buf[pl.ds(0, tm), sl] = dmv[:, sl] * _count_inv(pos, w)
            buf[pl.ds(tm, HALO), sl] = jnp.where(t < nt - 1, dmn[:, sl] * _count_inv(posn, w), 0.0)
        parts = []
        for gi, w in enumerate(POOL_WINDOWS):
            cols = pl.ds(gi * PG, PG)
            s = buf[pl.ds(0, tm), cols]
            for k in range(1, w):
                s = s + buf[pl.ds(k, tm), cols]
            parts.append(s)
        da = jnp.concatenate(parts, axis=1) - dmv
        x = h_ref[...]
        dx, dg = _rms_bwd(da, x, g_ref[...], _rstd(x))
        dh_ref[...] = r_ref[...] + dx
        _acc(dg_ref, dg, t == 0)

    row = pl.BlockSpec((tm, D), lambda t: (t, 0))
    one = pl.BlockSpec((1, D), lambda t: (0, 0))
    return pl.pallas_call(
        body, name=name, grid=(nt,),
        in_specs=[row, pl.BlockSpec((HALO, D), lambda t: (jnp.minimum((t + 1) * (tm // HALO), last_h), 0)),
                  row, one, row],
        out_specs=(row, one),
        out_shape=(jax.ShapeDtypeStruct((L, D), F32), jax.ShapeDtypeStruct((1, D), F32)),
        scratch_shapes=[pltpu.VMEM((tm + HALO, D), F32)], compiler_params=_params(1))(dmix, dmix, h, gain, resid)


def _bucket_np(d):
    d = np.maximum(d, 0)
    df = np.maximum(d, 1).astype(np.float32)
    large = 16 + (np.log(df / np.float32(16)) / np.float32(math.log(128 / 16)) * np.float32(16)).astype(np.int32)
    return np.where(d < 16, d, np.minimum(large, N_BUCKETS - 1))


def _bias_index():
    iq = np.arange(BLK)[:, None]
    jk = np.arange(2 * BLK)[None, :]
    dist = BLK + iq - jk
    band = _bucket_np(dist)
    ok = (dist >= 0) & (dist < BLK)
    band1 = np.where(ok, band, N_BUCKETS)
    band0 = np.where(ok & (jk >= BLK), band, N_BUCKETS)
    im = np.arange(N_META)[None, :]
    unused = np.full((BLK, BLK - N_META), N_BUCKETS)
    var0 = np.concatenate([_bucket_np(N_META + iq - im), unused, band0], axis=1)
    var1 = np.concatenate([_bucket_np(N_META + BLK + iq - im), unused, band1], axis=1)
    dm = np.arange(N_META)[:, None] - im
    mm = np.where(dm >= 0, _bucket_np(dm), N_BUCKETS)
    segs = [var0, var1, mm]
    return np.concatenate([s.reshape(-1) for s in segs]).astype(np.int32), [s.shape for s in segs]


P_CHUNK = 9856


def _onehot(idx_ref, grad):
    rows = lax.broadcasted_iota(jnp.int32, (128, P_CHUNK), 0)
    hit = (rows == idx_ref[...]).astype(F32)
    return jnp.where(rows == N_BUCKETS, -1.0, hit) if grad else hit


def _bias_lookup(table_aug, idx, name):
    P = idx.shape[1]

    def body(t_ref, idx_ref, o_ref):
        o_ref[...] = lax.dot_general(t_ref[...], _onehot(idx_ref, False), _NN, precision=lax.Precision.HIGHEST,
                                     preferred_element_type=F32)

    return pl.pallas_call(
        body, name=name, grid=(P // P_CHUNK,),
        in_specs=[pl.BlockSpec((H, 128), lambda i: (0, 0)), pl.BlockSpec((1, P_CHUNK), lambda i: (0, i))],
        out_specs=pl.BlockSpec((H, P_CHUNK), lambda i: (0, i)),
        out_shape=jax.ShapeDtypeStruct((H, P), F32), compiler_params=_params(1))(table_aug, idx)


def _bias_lookup_bwd(dbias, idx, name):
    P = idx.shape[1]

    def body(d_ref, idx_ref, o_ref):
        part = lax.dot_general(d_ref[...], _onehot(idx_ref, True), _NT, precision=lax.Precision.HIGHEST,
                               preferred_element_type=F32)
        _acc(o_ref, part, pl.program_id(0) == 0)

    return pl.pallas_call(
        body, name=name, grid=(P // P_CHUNK,),
        in_specs=[pl.BlockSpec((H, P_CHUNK), lambda i: (0, i)), pl.BlockSpec((1, P_CHUNK), lambda i: (0, i))],
        out_specs=pl.BlockSpec((H, 128), lambda i: (0, 0)),
        out_shape=jax.ShapeDtypeStruct((H, 128), F32), compiler_params=_params(1))(dbias, idx)


def _probs(q, kbt, bias, sink):
    s = _dot(q, kbt) + bias
    m = jnp.maximum(jnp.max(s, axis=-1, keepdims=True), sink)
    e = jnp.exp(s - m)
    return e * (1.0 / (jnp.sum(e, axis=-1, keepdims=True) + jnp.exp(sink - m)))


def _attn_specs(nb):
    def cur(kh, n):
        return jnp.minimum(n, nb - 1)

    def prev(kh, n):
        return jnp.maximum(jnp.minimum(n, nb - 1) - 1, 0)

    return dict(
        q=pl.BlockSpec((GRP, BLK, HD), lambda kh, n: (kh, cur(kh, n), 0)),
        qt=pl.BlockSpec((GRP, HD, BLK), lambda kh, n: (kh, 0, cur(kh, n))),
        cur=pl.BlockSpec((None, BLK, HD), lambda kh, n: (kh, cur(kh, n), 0)),
        prev=pl.BlockSpec((None, BLK, HD), lambda kh, n: (kh, prev(kh, n), 0)),
        meta=pl.BlockSpec((None, BLK, HD), lambda kh, n: (kh, 0, 0)),
        curt=pl.BlockSpec((None, HD, BLK), lambda kh, n: (kh, 0, cur(kh, n))),
        prevt=pl.BlockSpec((None, HD, BLK), lambda kh, n: (kh, 0, prev(kh, n))),
        metat=pl.BlockSpec((None, HD, BLK), lambda kh, n: (kh, 0, 0)),
        bias=pl.BlockSpec((None, GRP, BLK, NKEY), lambda kh, n: (jnp.minimum(n, 1), kh, 0, 0)),
        sink=pl.BlockSpec((None, GRP * BLK, 1), lambda kh, n: (kh, 0, 0)))


def _attn_fwd(q, kt, kmt, v, vm, bias, sink, name):
    S = q.shape[1]
    nb = S // BLK
    sp = _attn_specs(nb)

    def body(q_ref, kct_ref, kpt_ref, kmt_ref, vc_ref, vp_ref, vm_ref, bias_ref, sink_ref, o_ref):
        kbt = jnp.concatenate([kmt_ref[...], kpt_ref[...], kct_ref[...]], axis=1)
        vb = jnp.concatenate([vm_ref[...], vp_ref[...], vc_ref[...]], axis=0)
        p = _probs(q_ref[...].reshape(GRP * BLK, HD), kbt, bias_ref[...].reshape(GRP * BLK, NKEY), sink_ref[...])
        o_ref[...] = _dot(p.astype(BF16), vb).reshape(GRP, BLK, HD).astype(BF16)

    return pl.pallas_call(
        body, name=name, grid=(KV, nb),
        in_specs=[sp['q'], sp['curt'], sp['prevt'], sp['metat'], sp['cur'], sp['prev'], sp['meta'],
                  sp['bias'], sp['sink']],
        out_specs=sp['q'], out_shape=jax.ShapeDtypeStruct((H, S, HD), BF16),
        compiler_params=_params(2))(q, kt, kt, kmt, v, v, vm, bias, sink)


def _attn_bwd(q, qt, k, km, kt, kmt, vt, vmt, bias, sink, do, dot_, name):
    S = q.shape[1]
    nb = S // BLK
    sp = _attn_specs(nb)

    def body(q_ref, qt_ref, kc_ref, kp_ref, km_ref, kct_ref, kpt_ref, kmt_ref, vct_ref, vpt_ref, vmt_ref,
             bias_ref, sink_ref, do_ref, dot_ref, dq_ref, dkt_ref, dvt_ref, dkmt_ref, dvmt_ref, dbias_ref, ck, cv):
        n = pl.program_id(1)

        @pl.when(n < nb)
        def _():
            kbt = jnp.concatenate([kmt_ref[...], kpt_ref[...], kct_ref[...]], axis=1)
            vbt = jnp.concatenate([vmt_ref[...], vpt_ref[...], vct_ref[...]], axis=1)
            kb = jnp.concatenate([km_ref[...], kp_ref[...], kc_ref[...]], axis=0)
            p = _probs(q_ref[...].reshape(GRP * BLK, HD), kbt, bias_ref[...].reshape(GRP * BLK, NKEY), sink_ref[...])
            dp = _dot(do_ref[...].reshape(GRP * BLK, HD), vbt)
            ds = p * (dp - jnp.sum(p * dp, axis=-1, keepdims=True))
            _acc(dbias_ref, ds.reshape(GRP, BLK, NKEY), n <= 1)
            ds16 = ds.astype(BF16)
            dq_ref[...] = _dot(ds16, kb).reshape(GRP, BLK, HD).astype(BF16)
            qtv = jnp.concatenate([qt_ref[g] for g in range(GRP)], axis=1)
            dotv = jnp.concatenate([dot_ref[g] for g in range(GRP)], axis=1)
            dkt = _dot(qtv, ds16)
            dvt = _dot(dotv, p.astype(BF16))
            _acc(dkmt_ref, dkt[:, 0:BLK], n == 0)
            _acc(dvmt_ref, dvt[:, 0:BLK], n == 0)

            @pl.when(n >= 1)
            def _():
                dkt_ref[...] = (ck[...] + dkt[:, BLK:2 * BLK]).astype(BF16)
                dvt_ref[...] = (cv[...] + dvt[:, BLK:2 * BLK]).astype(BF16)

            ck[...] = dkt[:, 2 * BLK:3 * BLK]
            cv[...] = dvt[:, 2 * BLK:3 * BLK]

        @pl.when(n == nb)
        def _():
            dkt_ref[...] = ck[...].astype(BF16)
            dvt_ref[...] = cv[...].astype(BF16)

    kvout = pl.BlockSpec((None, HD, BLK), lambda kh, n: (kh, 0, jnp.maximum(n - 1, 0)))
    return pl.pallas_call(
        body, name=name, grid=(KV, nb + 1),
        in_specs=[sp['q'], sp['qt'], sp['cur'], sp['prev'], sp['meta'], sp['curt'], sp['prevt'], sp['metat'],
                  sp['curt'], sp['prevt'], sp['metat'], sp['bias'], sp['sink'], sp['q'], sp['qt']],
        out_specs=(sp['q'], kvout, kvout, sp['metat'], sp['metat'], sp['bias']),
        out_shape=(jax.ShapeDtypeStruct((H, S, HD), BF16), jax.ShapeDtypeStruct((KV, HD, S), BF16),
                   jax.ShapeDtypeStruct((KV, HD, S), BF16), jax.ShapeDtypeStruct((KV, HD, BLK), F32),
                   jax.ShapeDtypeStruct((KV, HD, BLK), F32), jax.ShapeDtypeStruct((2, H, BLK, NKEY), F32)),
        scratch_shapes=[pltpu.VMEM((HD, BLK), F32), pltpu.VMEM((HD, BLK), F32)],
        compiler_params=_params(2))(q, qt, k, k, km, kt, kt, kmt, vt, vt, vmt, bias, sink, do, dot_)


def _meta_softmax(q, k, bias, sink):
    s = _dot(q, k, _NT) + bias
    m = jnp.maximum(jnp.max(s, axis=-1, keepdims=True), sink)
    e = jnp.exp(s - m)
    return e * (1.0 / (jnp.sum(e, axis=-1, keepdims=True) + jnp.exp(sink - m)))


def _attn_meta_fwd(qm, km, vm, bias, sink, name):
    def body(q_ref, k_ref, v_ref, b_ref, s_ref, o_ref):
        for h in range(H):
            p = _meta_softmax(q_ref[h], k_ref[h // GRP], b_ref[h], s_ref[h])
            o_ref[h] = _dot(p.astype(BF16), v_ref[h // GRP]).astype(BF16)

    return pl.pallas_call(body, name=name, out_shape=jax.ShapeDtypeStruct((H, N_META, HD), BF16))(
        qm, km, vm, bias, sink)


def _attn_meta_bwd(qm, km, vm, bias, sink, do, name):
    def body(q_ref, k_ref, v_ref, b_ref, s_ref, do_ref, dq_ref, dk_ref, dv_ref, db_ref):
        for kh in range(KV):
            k, v = k_ref[kh], v_ref[kh]
            dk = jnp.zeros((N_META, HD), F32)
            dv = jnp.zeros((N_META, HD), F32)
            for g in range(GRP):
                h = kh * GRP + g
                q, dov = q_ref[h], do_ref[h]
                p = _meta_softmax(q, k, b_ref[h], s_ref[h])
                dp = _dot(dov, v, _NT)
                ds = p * (dp - jnp.sum(p * dp, axis=-1, keepdims=True))
                db_ref[h] = ds
                ds16 = ds.astype(BF16)
                dq_ref[h] = _dot(ds16, k).astype(BF16)
                dk = dk + _dot(ds16, q, _TN)
                dv = dv + _dot(p.astype(BF16), dov, _TN)
            dk_ref[kh] = dk
            dv_ref[kh] = dv

    return pl.pallas_call(
        body, name=name,
        out_shape=(jax.ShapeDtypeStruct((H, N_META, HD), BF16), jax.ShapeDtypeStruct((KV, N_META, HD), F32),
                   jax.ShapeDtypeStruct((KV, N_META, HD), F32), jax.ShapeDtypeStruct((H, N_META, N_META), F32)))(
        qm, km, vm, bias, sink, do)


def _heads(t, n):
    return jnp.transpose(t.reshape(t.shape[0], n, HD), (1, 0, 2))


def _heads_t(t, n):
    return jnp.transpose(t.reshape(t.shape[0], n, HD), (1, 2, 0))


def _unheads(t):
    return jnp.transpose(t, (1, 0, 2)).reshape(t.shape[1], t.shape[0] * HD)


def _unheads_t(t):
    return jnp.transpose(t, (2, 0, 1)).reshape(t.shape[2], t.shape[0] * HD)


def _pad_block(t, axis):
    pad = [(0, 0)] * t.ndim
    pad[axis] = (0, BLK - N_META)
    return jnp.pad(t, pad)


def _adam(w, g, m, v, name):
    R, C = w.shape
    tr = _tile(R, 256) if R % 16 == 0 else R

    def body(w_ref, g_ref, m_ref, v_ref, d_ref, mo_ref, vo_ref):
        gv = g_ref[...]
        mn = B1 * m_ref[...] + (1.0 - B1) * gv
        vn = B2 * v_ref[...] + (1.0 - B2) * (gv * gv)
        m_hat = mn / (1.0 - B1 ** STEP)
        v_hat = vn / (1.0 - B2 ** STEP)
        d_ref[...] = -LR * (m_hat / (jnp.sqrt(v_hat) + ADAM_EPS) + WD * w_ref[...])
        mo_ref[...] = mn
        vo_ref[...] = vn

    blk = pl.BlockSpec((tr, C), lambda i: (i, 0))
    shp = jax.ShapeDtypeStruct((R, C), F32)
    return pl.pallas_call(body, name=name, grid=(R // tr,), in_specs=[blk] * 4, out_specs=(blk,) * 3,
                          out_shape=(shp,) * 3, compiler_params=_params(1))(w, g, m, v)


def _pair_add(g, r, c, name):
    _, _, R, C = g.shape
    tr = _tile(R)

    def body(c_ref, g_ref, r_ref, o_ref):
        o_ref[...] = (g_ref[...].astype(F32) + r_ref[...].astype(F32)).astype(BF16)

    grid_spec = pltpu.PrefetchScalarGridSpec(
        num_scalar_prefetch=1, grid=(NSH, R // tr),
        in_specs=[pl.BlockSpec((None, None, tr, C), lambda s, i, c_ref: (s, c_ref[0], i, 0)),
                  pl.BlockSpec((None, tr, C), lambda s, i, c_ref: (s, i, 0))],
        out_specs=pl.BlockSpec((None, tr, C), lambda s, i, c_ref: (s, i, 0)))
    return pl.pallas_call(body, name=name, grid_spec=grid_spec, out_shape=jax.ShapeDtypeStruct((NSH, R, C), BF16),
                          compiler_params=_params(2))(c, g, r)


def _sum4(r, name):
    _, R, C = r.shape
    tr = _tile(R)

    def body(r_ref, o_ref):
        acc = r_ref[0].astype(F32)
        for s in range(1, NSH):
            acc = acc + r_ref[s].astype(F32)
        o_ref[...] = acc

    return pl.pallas_call(body, name=name, grid=(R // tr,),
                          in_specs=[pl.BlockSpec((NSH, tr, C), lambda i: (0, i, 0))],
                          out_specs=pl.BlockSpec((tr, C), lambda i: (i, 0)),
                          out_shape=jax.ShapeDtypeStruct((R, C), F32), compiler_params=_params(1))(r)


def _place():
    x, y, c = lax.axis_index("x"), lax.axis_index("y"), lax.axis_index("c")
    chips = [(1 - x, y), (x, 1 - y), (1 - x, 1 - y)]
    return x, y, c, chips


def _rcopy(src, dst, ssem, rsem, dev):
    return pltpu.make_async_remote_copy(src_ref=src, dst_ref=dst, send_sem=ssem, recv_sem=rsem,
                                        device_id=dev, device_id_type=MESH)


def _gather_weights(big, small):
    nbig = len(big)

    def body(*refs):
        ins, sm = refs[:nbig], refs[nbig]
        outs, smo = refs[nbig + 1:2 * nbig + 1], refs[2 * nbig + 1]
        s_ici, r_ici, s_fwd, r_fwd, s_sm, r_sm, loc = refs[2 * nbig + 2:]
        x, y, c, chips = _place()
        me, sib = 2 * x + y, (x, y, 1 - c)
        local = [pltpu.make_async_copy(ins[k], outs[k].at[me], loc.at[k]) for k in range(nbig)]
        local.append(pltpu.make_async_copy(sm, smo.at[me], loc.at[nbig]))
        for cp in local:
            cp.start()
        sends = []
        for k in range(nbig):
            for j, (cx, cy) in enumerate(chips):
                sends.append(_rcopy(ins[k].at[c], outs[k].at[me, c], s_ici.at[3 * k + j], r_ici.at[3 * k + j],
                                    (cx, cy, c)))
        for j, (cx, cy) in enumerate(chips):
            sends.append(_rcopy(sm, smo.at[me], s_sm.at[j], r_sm.at[j], (cx, cy, c)))
        for cp in sends:
            cp.start()
        for k in range(nbig):
            for j, (cx, cy) in enumerate(chips):
                got = outs[k].at[2 * cx + cy, c]
                _rcopy(got, got, s_ici.at[3 * k + j], r_ici.at[3 * k + j], (cx, cy, c)).wait_recv()
                fwd = _rcopy(got, got, s_fwd.at[3 * k + j], r_fwd.at[3 * k + j], sib)
                fwd.start()
                sends.append(fwd)
        for k in range(nbig):
            for j, (cx, cy) in enumerate(chips):
                got = outs[k].at[2 * cx + cy, 1 - c]
                _rcopy(got, got, s_fwd.at[3 * k + j], r_fwd.at[3 * k + j], sib).wait_recv()
        for j, (cx, cy) in enumerate(chips):
            got = smo.at[2 * cx + cy]
            _rcopy(got, got, s_sm.at[j], r_sm.at[j], (cx, cy, c)).wait_recv()
        for cp in sends:
            cp.wait_send()
        for cp in local:
            cp.wait()

    out_shape = [jax.ShapeDtypeStruct((NSH,) + a.shape, a.dtype) for a in big]
    out_shape.append(jax.ShapeDtypeStruct((NSH,) + small.shape, small.dtype))
    dma = pltpu.SemaphoreType.DMA
    res = pl.pallas_call(
        body, name="gather_weights", in_specs=[ANY] * (nbig + 1), out_specs=[ANY] * (nbig + 1),
        out_shape=out_shape,
        scratch_shapes=[dma((3 * nbig,)), dma((3 * nbig,)), dma((3 * nbig,)), dma((3 * nbig,)), dma((3,)), dma((3,)),
                        dma((nbig + 1,))])(*big, small)
    return res[:nbig], res[nbig]


def _pair_exchange(grads):
    n = len(grads)

    def body(*refs):
        ins, outs = refs[:n], refs[n:2 * n]
        ssem, rsem = refs[2 * n:]
        x, y, c, _ = _place()
        sib = (x, y, 1 - c)
        cps = [_rcopy(ins[k].at[:, 1 - c], outs[k], ssem.at[k], rsem.at[k], sib) for k in range(n)]
        for cp in cps:
            cp.start()
        for cp in cps:
            cp.wait_recv()
        for cp in cps:
            cp.wait_send()

    dma = pltpu.SemaphoreType.DMA
    return pl.pallas_call(
        body, name="grad_pair_exchange", in_specs=[ANY] * n, out_specs=[ANY] * n,
        out_shape=[jax.ShapeDtypeStruct((NSH,) + g.shape[2:], g.dtype) for g in grads],
        scratch_shapes=[dma((n,)), dma((n,))])(*grads)


def _chip_exchange(parts):
    n = len(parts)

    def body(*refs):
        ins, outs = refs[:n], refs[n:2 * n]
        ssem, rsem, loc = refs[2 * n:]
        x, y, c, chips = _place()
        me = 2 * x + y
        local = [pltpu.make_async_copy(ins[k].at[me], outs[k].at[me], loc.at[k]) for k in range(n)]
        for cp in local:
            cp.start()
        sends = []
        for k in range(n):
            for j, (cx, cy) in enumerate(chips):
                sends.append(_rcopy(ins[k].at[2 * cx + cy], outs[k].at[me], ssem.at[3 * k + j], rsem.at[3 * k + j],
                                    (cx, cy, c)))
        for cp in sends:
            cp.start()
        for k in range(n):
            for j, (cx, cy) in enumerate(chips):
                got = outs[k].at[2 * cx + cy]
                _rcopy(got, got, ssem.at[3 * k + j], rsem.at[3 * k + j], (cx, cy, c)).wait_recv()
        for cp in sends:
            cp.wait_send()
        for cp in local:
            cp.wait()

    dma = pltpu.SemaphoreType.DMA
    return pl.pallas_call(
        body, name="grad_chip_exchange", in_specs=[ANY] * n, out_specs=[ANY] * n,
        out_shape=[jax.ShapeDtypeStruct(p.shape, p.dtype) for p in parts],
        scratch_shapes=[dma((3 * n,)), dma((3 * n,)), dma((n,))])(*parts)


def _pair_share(halves):
    n = len(halves)

    def body(*refs):
        ins, outs = refs[:n], refs[n:2 * n]
        ssem, rsem, loc = refs[2 * n:]
        x, y, c, _ = _place()
        sib = (x, y, 1 - c)
        local = [pltpu.make_async_copy(ins[k], outs[k].at[c], loc.at[k]) for k in range(n)]
        cps = [_rcopy(ins[k], outs[k].at[c], ssem.at[k], rsem.at[k], sib) for k in range(n)]
        for cp in local + cps:
            cp.start()
        for k in range(n):
            got = outs[k].at[1 - c]
            _rcopy(got, got, ssem.at[k], rsem.at[k], sib).wait_recv()
        for cp in cps:
            cp.wait_send()
        for cp in local:
            cp.wait()

    dma = pltpu.SemaphoreType.DMA
    return pl.pallas_call(
        body, name="grad_pair_share", in_specs=[ANY] * n, out_specs=[ANY] * n,
        out_shape=[jax.ShapeDtypeStruct((2,) + h.shape, h.dtype) for h in halves],
        scratch_shapes=[dma((n,)), dma((n,)), dma((n,))])(*halves)


def _allreduce_small(pack):
    R = pack.shape[0]

    def body(in_ref, out_ref, buf, ssem, rsem):
        x, y, c, _ = _place()
        me = 4 * x + 2 * y + c
        buf[me] = in_ref[...]
        peers = []
        for k in range(1, 8):
            fx, fy, fc = (k >> 2) & 1, (k >> 1) & 1, k & 1
            peers.append((1 - x if fx else x, 1 - y if fy else y, 1 - c if fc else c))
        cps = [_rcopy(in_ref, buf.at[me], ssem.at[k], rsem.at[k], p) for k, p in enumerate(peers)]
        for cp in cps:
            cp.start()
        for k, (px, py, pc) in enumerate(peers):
            got = buf.at[4 * px + 2 * py + pc]
            _rcopy(got, got, ssem.at[k], rsem.at[k], (px, py, pc)).wait_recv()
        for cp in cps:
            cp.wait_send()
        acc = buf[0]
        for s in range(1, 8):
            acc = acc + buf[s]
        out_ref[...] = acc

    dma = pltpu.SemaphoreType.DMA
    return pl.pallas_call(
        body, name="allreduce_small", out_shape=jax.ShapeDtypeStruct(pack.shape, F32),
        in_specs=[pl.BlockSpec(memory_space=pltpu.VMEM)], out_specs=pl.BlockSpec(memory_space=pltpu.VMEM),
        scratch_shapes=[pltpu.VMEM((8, R, 128), F32), dma((7,)), dma((7,))])(pack)


def _pack(arrs):
    flat = jnp.concatenate([a.reshape(-1).astype(F32) for a in arrs])
    n = flat.shape[0]
    rows = -(-n // 1024) * 8
    return jnp.pad(flat, (0, rows * 128 - n)).reshape(rows, 128)


def _unpack(pack, shapes):
    flat, out, o = pack.reshape(-1), [], 0
    for s in shapes:
        n = int(np.prod(s))
        out.append(flat[o:o + n].reshape(s))
        o += n
    return out


def kernel(x, meta_tokens, rel_bias_table, norm_mix, norm_ffn, norm_final, attn_w_qkv, attn_b_qkv, attn_w_o, attn_b_o, attn_sinks, conv_w_in, conv_w, conv_w_out, pool_w, pool_scale, ffn_w_gate, ffn_w_up, ffn_w_down, loss_target, m_meta_tokens, m_rel_bias_table, m_norm_mix, m_norm_ffn, m_norm_final, m_attn_w_qkv, m_attn_b_qkv, m_attn_w_o, m_attn_b_o, m_attn_sinks, m_conv_w_in, m_conv_w, m_conv_w_out, m_pool_w, m_pool_scale, m_ffn_w_gate, m_ffn_w_up, m_ffn_w_down, v_meta_tokens, v_rel_bias_table, v_norm_mix, v_norm_ffn, v_norm_final, v_attn_w_qkv, v_attn_b_qkv, v_attn_w_o, v_attn_b_o, v_attn_sinks, v_conv_w_in, v_conv_w, v_conv_w_out, v_pool_w, v_pool_scale, v_ffn_w_gate, v_ffn_w_up, v_ffn_w_down):
    args = locals()
    w = {n: args[n] for n in WEIGHTS}
    mom = {n: args['m_' + n] for n in WEIGHTS}
    var = {n: args['v_' + n] for n in WEIGHTS}
    mx, my, mc = lax.axis_index("x"), lax.axis_index("y"), lax.axis_index("c")
    chip = 2 * mx + my
    S = x.shape[1]
    L = S + N_META
    scale = jnp.asarray(HD ** -0.5, BF16)

    small_in = jnp.concatenate([
        jnp.pad(w['meta_tokens'], ((0, 0), (0, 128))), w['attn_b_qkv'], jnp.pad(w['attn_b_o'], ((0, 0), (0, 128))),
        jnp.pad(w['conv_w'][0], ((0, 0), (0, 128))), jnp.pad(w['pool_scale'], ((0, 0), (0, 128)))], axis=0)
    gathered, gsmall = _gather_weights([w[n].astype(BF16).reshape(COMM_SHAPE[n]) for n in BIG], small_in)
    G = dict(zip(BIG, gathered))

    def cols(rows, width):
        return jnp.transpose(rows[:, :, :width], (1, 0, 2)).reshape(rows.shape[1], NSH * width)

    meta_full = cols(gsmall[:, 0:16], 256)
    b_qkv = cols(gsmall[:, 16:18], 384)
    b_o = cols(gsmall[:, 18:20], 256)
    conv_k = cols(gsmall[:, 20:23], 256)
    p_scale = cols(gsmall[:, 23:24], 256)
    w_qkv = [G['attn_w_qkv'][:, j] for j in range(2)]
    w_o = [G['attn_w_o'][:, j].reshape(1, D, D) for j in range(2)]
    w_cin = G['conv_w_in'].reshape(NSH, D, 3 * D // NSH)
    w_cout = G['conv_w_out'].reshape(1, D, D)
    w_pool = jnp.transpose(G['pool_w'].reshape(NSH, 4, PG // NSH, PG), (1, 0, 2, 3)).reshape(4, PG, PG)
    w_gate = G['ffn_w_gate'].reshape(NSH, DEPTH, D, FC)
    w_up = G['ffn_w_up'].reshape(NSH, DEPTH, D, FC)
    w_down = G['ffn_w_down'].reshape(NSH, DEPTH, FC, D)

    idx_np, seg_shapes = _bias_index()
    idx = jnp.asarray(idx_np).reshape(1, -1)
    table_aug = jnp.concatenate([rel_bias_table.T, jnp.full((H, 1), NEG, F32),
                                 jnp.zeros((H, 127 - N_BUCKETS), F32)], axis=1)
    bias_flat = _bias_lookup(table_aug, idx, "bias_lookup")
    nblock = BLK * NKEY
    bias_blk = jnp.transpose(bias_flat[:, :2 * nblock].reshape(H, 2, BLK, NKEY), (1, 0, 2, 3))
    bias_mm = bias_flat[:, 2 * nblock:].reshape(H, N_META, N_META)

    h = jnp.concatenate([meta_full, x[0]], axis=0)
    saved = []
    for i in range(DEPTH):
        kind, j = i % 3, i // 3
        gm = norm_mix[i:i + 1]
        st = dict(h=h)
        if kind == 0:
            qkv, a = _linear(h, w_qkv[j], gain=gm, bias=b_qkv[j:j + 1], name=f"qkv{i}")
            qf, kf, vf = qkv[:, :H * HD] * scale, qkv[:, H * HD:(H + KV) * HD], qkv[:, (H + KV) * HD:]
            q, k, v = _heads(qf, H), _heads(kf, KV), _heads(vf, KV)
            lay = dict(q=q[:, N_META:], qt=_heads_t(qf[N_META:], H), qm=q[:, :N_META],
                       k=k[:, N_META:], kt=_heads_t(kf[N_META:], KV), km=k[:, :N_META],
                       kmp=_pad_block(k[:, :N_META], 1), kmt=_pad_block(_heads_t(kf[:N_META], KV), 2),
                       v=v[:, N_META:], vt=_heads_t(vf[N_META:], KV), vm=v[:, :N_META],
                       vmp=_pad_block(v[:, :N_META], 1), vmt=_pad_block(_heads_t(vf[:N_META], KV), 2))
            sink_r = jnp.broadcast_to(attn_sinks[j].reshape(KV, GRP, 1, 1), (KV, GRP, BLK, 1)).reshape(KV, GRP * BLK, 1)
            sink_m = jnp.broadcast_to(attn_sinks[j].reshape(H, 1, 1), (H, N_META, 1))
            o_r = _attn_fwd(lay['q'], lay['kt'], lay['kmt'], lay['v'], lay['vmp'], bias_blk, sink_r, f"attn_fwd{i}")
            o_m = _attn_meta_fwd(lay['qm'], lay['km'], lay['vm'], bias_mm, sink_m, f"attn_meta_fwd{i}")
            o = _unheads(jnp.concatenate([o_m, o_r], axis=1))
            h1 = _linear(o, w_o[j], bias=b_o[j:j + 1], resid=h, out_dtype=F32, name=f"attn_out{i}")
            st.update(a=a, lay=lay, sinks=(sink_r, sink_m), o=o)
        elif kind == 1:
            z3, a = _linear(h, w_cin, gain=gm, out_dtype=F32, cap=512, name=f"conv_in{i}")
            yv = _conv_fwd(z3, conv_k, f"conv_fwd{i}")
            h1 = _linear(yv, w_cout, resid=h, out_dtype=F32, name=f"conv_out{i}")
            st.update(a=a, z3=z3, y=yv)
        else:
            mix = _pool_fwd(h, gm, f"pool_fwd{i}")
            h1 = _pool_out(mix, w_pool, p_scale, h, f"pool_out{i}")
            st.update(mix=mix)
        hn, Gp, Up, Ap, Bp = _ffn_fwd(h1, norm_ffn[i:i + 1], w_gate, w_up, w_down, i, f"ffn_fwd{i}")
        st.update(h1=h1, G=Gp, U=Up, A=Ap, B=Bp)
        saved.append(st)
        h = hn

    target = jnp.pad(loss_target[0], ((N_META, 0), (0, 0)))
    dh, d_nfinal, loss_part = _loss_head(h, norm_final.reshape(1, D), target)
    d_nmix, d_nffn = [None] * DEPTH, [None] * DEPTH
    dW = {n: [] for n in BIG}
    d_bqkv, d_bo, d_sinks = [None, None], [None, None], [None, None]
    d_table = jnp.zeros((N_BUCKETS, H), F32)
    d_convk = d_pscale = None
    for i in reversed(range(DEPTH)):
        kind, j = i % 3, i // 3
        st = saved[i]
        DG, DU, dh1, d_nffn[i] = _ffn_bwd_act(dh, st['h1'], norm_ffn[i:i + 1], st['G'], st['U'], w_gate, w_up, w_down,
                                              i, f"ffn_bwd_act{i}")
        gw, uw, dw_ = _ffn_bwd_w(st['B'], st['A'], DG, DU, dh, f"ffn_bwd_w{i}")
        dW['ffn_w_gate'].insert(0, gw)
        dW['ffn_w_up'].insert(0, uw)
        dW['ffn_w_down'].insert(0, dw_)
        gm = norm_mix[i:i + 1]
        if kind == 0:
            lay = st['lay']
            sink_r, sink_m = st['sinks']
            gwo, d_bo[j] = _wgrad(st['o'], dh1, mode='row', colsum=True, name=f"attn_out_wgrad{i}")
            dW['attn_w_o'].insert(0, gwo)
            dof = _linear_t(dh1, w_o[j], name=f"attn_out_bwd{i}")
            do = _heads(dof, H)
            dq_r, dkt, dvt, dkmt, dvmt, dbias = _attn_bwd(
                lay['q'], lay['qt'], lay['k'], lay['kmp'], lay['kt'], lay['kmt'], lay['vt'], lay['vmt'], bias_blk,
                sink_r, do[:, N_META:], _heads_t(dof[N_META:], H), f"attn_bwd{i}")
            dq_m, dkm2, dvm2, dbmm = _attn_meta_bwd(lay['qm'], lay['km'], lay['vm'], bias_mm, sink_m,
                                                    do[:, :N_META], f"attn_meta_bwd{i}")
            dflat = jnp.concatenate([jnp.transpose(dbias, (1, 0, 2, 3)).reshape(H, -1), dbmm.reshape(H, -1)], axis=1)
            dts = _bias_lookup_bwd(dflat, idx, f"bias_lookup_bwd{i}")
            d_table = d_table + dts[:, :N_BUCKETS].T
            d_sinks[j] = dts[:, N_BUCKETS]
            dkm = (jnp.transpose(dkmt[:, :, :N_META], (0, 2, 1)) + dkm2).astype(BF16)
            dvm = (jnp.transpose(dvmt[:, :, :N_META], (0, 2, 1)) + dvm2).astype(BF16)
            dqkv = jnp.concatenate([
                jnp.concatenate([_unheads(dq_m), _unheads(dq_r)], axis=0) * scale,
                jnp.concatenate([_unheads(dkm), _unheads_t(dkt)], axis=0),
                jnp.concatenate([_unheads(dvm), _unheads_t(dvt)], axis=0)], axis=1)
            gq, d_bqkv[j] = _wgrad(st['a'], dqkv, mode='col', colsum=True, name=f"qkv_wgrad{i}")
            dW['attn_w_qkv'].insert(0, gq)
            dh, d_nmix[i] = _linear_t(dqkv, w_qkv[j], rms=(st['h'], gm, dh1), name=f"qkv_bwd{i}")
        elif kind == 1:
            dW['conv_w_out'].append(_wgrad(st['y'], dh1, mode='row', name=f"conv_out_wgrad{i}"))
            dy = _linear_t(dh1, w_cout, out_dtype=F32, name=f"conv_out_bwd{i}")
            dz3, d_convk = _conv_bwd(st['z3'], conv_k, dy, f"conv_bwd{i}")
            dW['conv_w_in'].append(_wgrad(st['a'], dz3, mode='col', cap=512, name=f"conv_in_wgrad{i}"))
            dh, d_nmix[i] = _linear_t(dz3, w_cin, rms=(st['h'], gm, dh1), cap=512, name=f"conv_in_bwd{i}")
        else:
            dmix, dwp, d_pscale = _pool_out_bwd(dh1, st['mix'], w_pool, p_scale, f"pool_out_bwd{i}")
            dW['pool_w'].append(jnp.transpose(dwp.reshape(4, NSH, PG // NSH, PG), (1, 0, 2, 3)).astype(BF16))
            dh, d_nmix[i] = _pool_bwd(dmix, st['h'], gm, dh1, f"pool_bwd{i}")

    full = [jnp.stack(dW[n], axis=1).reshape((NSH,) + COMM_SHAPE[n]) for n in BIG]
    from_sib = _pair_exchange(full)
    core = mc.astype(jnp.int32).reshape(1)
    parts = [_pair_add(g, r, core, f"pair_add_{n}") for g, r, n in zip(full, from_sib, BIG)]
    arrived = _chip_exchange(parts)
    halves = [_sum4(r, f"chip_sum_{n}") for r, n in zip(arrived, BIG)]
    shared = _pair_share(halves)
    grads = {n: s.reshape(w[n].shape) for n, s in zip(BIG, shared)}

    small_full = [dh[:N_META], d_table, jnp.concatenate(d_nmix, axis=0), jnp.concatenate(d_nffn, axis=0), d_nfinal,
                  jnp.concatenate(d_bqkv, axis=0), jnp.concatenate(d_bo, axis=0), jnp.stack(d_sinks), d_convk,
                  d_pscale, loss_part[:, 0:1]]
    red = _unpack(_allreduce_small(_pack(small_full)), [a.shape for a in small_full])
    g_meta, g_table, g_nmix, g_nffn, g_nfinal, g_bqkv, g_bo, g_sinks, g_convk, g_pscale, loss = red

    def shard(a, width):
        return lax.dynamic_slice_in_dim(a, chip * width, width, axis=1)

    grads.update(meta_tokens=shard(g_meta, 256), rel_bias_table=g_table, norm_mix=g_nmix, norm_ffn=g_nffn,
                 norm_final=g_nfinal.reshape(D), attn_b_qkv=shard(g_bqkv, 384), attn_b_o=shard(g_bo, 256),
                 attn_sinks=g_sinks, conv_w=shard(g_convk, 256)[None], pool_scale=shard(g_pscale, 256))

    delta, new_m, new_v = {}, {}, {}
    for n in BIG:
        shp = w[n].shape
        r2 = (int(np.prod(shp[:-1])), shp[-1])
        dl, mn, vn = _adam(w[n].reshape(r2), grads[n].reshape(r2), mom[n].reshape(r2), var[n].reshape(r2), f"adam_{n}")
        delta[n], new_m[n], new_v[n] = dl.reshape(shp), mn.reshape(shp), vn.reshape(shp)
    shapes = [w[n].shape for n in SMALL]
    packed = [_pack([d[n] for n in SMALL]) for d in (w, grads, mom, var)]
    for dst, res in zip((delta, new_m, new_v), _adam(*packed, "adam_small")):
        dst.update(zip(SMALL, _unpack(res, shapes)))

    return (loss.reshape(()), dh[N_META:][None], *[grads[n] for n in WEIGHTS], *[delta[n] for n in WEIGHTS],
            *[new_m[n] for n in WEIGHTS], *[new_v[n] for n in WEIGHTS])
```

```python
import functools
import math

import numpy as np
import jax
import jax.numpy as jnp
from jax import lax
from jax.experimental import pallas as pl
from jax.experimental.pallas import tpu as pltpu

F32, BF16 = jnp.float32, jnp.bfloat16
D = 1024
N_META = 16
EPS = 1e-6
H, KV, GRP, HD = 16, 4, 4, 64
BLK = 128
NKEY = 3 * BLK
N_BUCKETS = 32
POOL_WINDOWS = (2, 4, 8, 16)
PG = 256
DFF = 2816
NSH = 4
FC = DFF // NSH
HALO = 16
NEG = -1e30
DEPTH = 4
LR, B1, B2, ADAM_EPS, WD, STEP = 0.001, 0.9, 0.999, 1e-08, 0.01, 10
MESH = pl.DeviceIdType.MESH
ANY = pl.BlockSpec(memory_space=pl.ANY)
VMEM_LIMIT = 48 * 1024 * 1024
VMEM_BIG = 58 * 1024 * 1024
BIG_TILE = 1024

_NN = (((1,), (0,)), ((), ()))
_NT = (((1,), (1,)), ((), ()))
_TN = (((0,), (0,)), ((), ()))

WEIGHTS = ['meta_tokens', 'rel_bias_table', 'norm_mix', 'norm_ffn', 'norm_final', 'attn_w_qkv', 'attn_b_qkv',
           'attn_w_o', 'attn_b_o', 'attn_sinks', 'conv_w_in', 'conv_w', 'conv_w_out', 'pool_w', 'pool_scale',
           'ffn_w_gate', 'ffn_w_up', 'ffn_w_down']
BIG = ['attn_w_qkv', 'attn_w_o', 'conv_w_in', 'conv_w_out', 'pool_w', 'ffn_w_gate', 'ffn_w_up', 'ffn_w_down']
SMALL = [w for w in WEIGHTS if w not in BIG]
HALF_SHAPE = {'attn_w_qkv': (512, 384), 'attn_w_o': (128, 1024), 'conv_w_in': (512, 768),
              'conv_w_out': (128, 1024), 'pool_w': (128, 256), 'ffn_w_gate': (512, 704),
              'ffn_w_up': (512, 704), 'ffn_w_down': (352, 1024)}


def _dot(a, b, dims=_NN):
    return lax.dot_general(a, b, dims, preferred_element_type=F32)


def _tile(n, cap=512):
    best = None
    for t in range(16, min(n, cap) + 1, 16):
        if n % t == 0:
            best = t
    assert best is not None, n
    return best


def _params(n_axes, vmem=VMEM_LIMIT):
    return pltpu.CompilerParams(dimension_semantics=("arbitrary",) * n_axes, vmem_limit_bytes=vmem)


def _rstd(x):
    return lax.rsqrt(jnp.mean(x * x, axis=-1, keepdims=True) + EPS)


def _rms_bwd(dy, x, g, r):
    u = dy * g
    dx = r * u - x * ((r * r * r) * (jnp.sum(x * u, axis=-1, keepdims=True) * (1.0 / D)))
    return dx, jnp.sum(dy * (x * r), axis=0, keepdims=True)


def _sigmoid(x):
    return 1.0 / (1.0 + jnp.exp(-x))


def _acc(ref, val, first):
    @pl.when(first)
    def _():
        ref[...] = val

    @pl.when(jnp.logical_not(first))
    def _():
        ref[...] += val


def _linear(x, w, *, gain=None, bias=None, resid=None, out_dtype=BF16, cap=BIG_TILE, name):
    L, K = x.shape
    J, _, Nc = w.shape
    tm = _tile(L, cap)
    has_g, has_b, has_r = gain is not None, bias is not None, resid is not None

    def body(*refs):
        refs = list(refs)
        x_ref, w_ref = refs[:2]
        i = 2
        g_ref = b_ref = r_ref = None
        if has_g:
            g_ref, i = refs[i], i + 1
        if has_b:
            b_ref, i = refs[i], i + 1
        if has_r:
            r_ref, i = refs[i], i + 1
        out_ref = refs[i]
        if has_g:
            xf = x_ref[...]
            xv = (xf * _rstd(xf) * g_ref[...]).astype(BF16)
            refs[i + 1][...] = xv
        else:
            xv = x_ref[...]
        for s in range(J):
            sl = slice(s * Nc, (s + 1) * Nc)
            acc = _dot(xv, w_ref[s])
            if has_b:
                acc = acc + b_ref[:, sl]
            if has_r:
                acc = acc + r_ref[:, sl]
            out_ref[:, sl] = acc.astype(out_dtype)

    row = lambda n: pl.BlockSpec((tm, n), lambda t: (t, 0))
    one = lambda n: pl.BlockSpec((1, n), lambda t: (0, 0))
    in_specs = [row(K), pl.BlockSpec((J, K, Nc), lambda t: (0, 0, 0))]
    ops = [x, w]
    if has_g:
        in_specs.append(one(K))
        ops.append(gain)
    if has_b:
        in_specs.append(one(J * Nc))
        ops.append(bias)
    if has_r:
        in_specs.append(row(J * Nc))
        ops.append(resid)
    out_specs, out_shape = row(J * Nc), jax.ShapeDtypeStruct((L, J * Nc), out_dtype)
    if has_g:
        out_specs, out_shape = (out_specs, row(K)), (out_shape, jax.ShapeDtypeStruct((L, K), BF16))
    return pl.pallas_call(body, name=name, grid=(L // tm,), in_specs=in_specs, out_specs=out_specs,
                          out_shape=out_shape, compiler_params=_params(1, VMEM_BIG))(*ops)


def _linear_t(dy, w, *, out_dtype=BF16, rms=None, cap=BIG_TILE, name):
    L = dy.shape[0]
    J, K, Nc = w.shape
    tm = _tile(L, cap)
    has_rms = rms is not None

    def body(*refs):
        t = pl.program_id(0)
        dy_ref, w_ref = refs[:2]
        acc = _dot(dy_ref[:, 0:Nc].astype(BF16), w_ref[0], _NT)
        for s in range(1, J):
            acc = acc + _dot(dy_ref[:, s * Nc:(s + 1) * Nc].astype(BF16), w_ref[s], _NT)
        if has_rms:
            h_ref, g_ref, r_ref, out_ref, dg_ref = refs[2:7]
            x = h_ref[...]
            dx, dg = _rms_bwd(acc, x, g_ref[...], _rstd(x))
            out_ref[...] = r_ref[...] + dx
            _acc(dg_ref, dg, t == 0)
        else:
            refs[2][...] = acc.astype(out_dtype)

    row = lambda n: pl.BlockSpec((tm, n), lambda t: (t, 0))
    in_specs = [row(J * Nc), pl.BlockSpec((J, K, Nc), lambda t: (0, 0, 0))]
    ops = [dy, w]
    if has_rms:
        in_specs += [row(K), pl.BlockSpec((1, K), lambda t: (0, 0)), row(K)]
        ops += list(rms)
        out_specs = (row(K), pl.BlockSpec((1, K), lambda t: (0, 0)))
        out_shape = (jax.ShapeDtypeStruct((L, K), F32), jax.ShapeDtypeStruct((1, K), F32))
    else:
        out_specs = row(K)
        out_shape = jax.ShapeDtypeStruct((L, K), out_dtype)
    return pl.pallas_call(body, name=name, grid=(L // tm,), in_specs=in_specs, out_specs=out_specs,
                          out_shape=out_shape, compiler_params=_params(1, VMEM_BIG))(*ops)


def _wgrad(x, dy, *, mode, colsum=False, cap=BIG_TILE, name):
    L, K = x.shape
    N = dy.shape[1]
    tm = _tile(L, cap)
    nt = L // tm
    oshape = (NSH, K, N // NSH) if mode == 'col' else (NSH, K // NSH, N)

    def body(*refs):
        t = pl.program_id(0)
        x_ref, dy_ref, out_ref = refs[:3]
        acc_ref = refs[-1]
        dyv = dy_ref[...]
        _acc(acc_ref, _dot(x_ref[...].astype(BF16), dyv.astype(BF16), _TN), t == 0)
        if colsum:
            _acc(refs[3], jnp.sum(dyv.astype(F32), axis=0, keepdims=True), t == 0)

        @pl.when(t == nt - 1)
        def _():
            for s in range(NSH):
                if mode == 'col':
                    out_ref[s] = acc_ref[:, s * oshape[2]:(s + 1) * oshape[2]].astype(BF16)
                else:
                    out_ref[s] = acc_ref[s * oshape[1]:(s + 1) * oshape[1], :].astype(BF16)

    out_specs = pl.BlockSpec(oshape, lambda t: (0, 0, 0))
    out_shape = jax.ShapeDtypeStruct(oshape, BF16)
    if colsum:
        out_specs = (out_specs, pl.BlockSpec((1, N), lambda t: (0, 0)))
        out_shape = (out_shape, jax.ShapeDtypeStruct((1, N), F32))
    return pl.pallas_call(
        body, name=name, grid=(nt,),
        in_specs=[pl.BlockSpec((tm, K), lambda t: (t, 0)), pl.BlockSpec((tm, N), lambda t: (t, 0))],
        out_specs=out_specs, out_shape=out_shape,
        scratch_shapes=[pltpu.VMEM((K, N), F32)], compiler_params=_params(1, VMEM_BIG))(x, dy)


def _ffn_fwd(h, gain, wg, wu, wd, name, carry=None):
    L = h.shape[0]
    tm = _tile(L, BIG_TILE)

    def body(h_ref, g_ref, wg_ref, wu_ref, wd_ref, hn_ref, G_ref, U_ref, A_ref, B_ref, acc_ref, b_scr):
        j = pl.program_id(1)

        @pl.when(j == 0)
        def _():
            x = h_ref[...]
            b = (x * _rstd(x) * g_ref[...]).astype(BF16)
            b_scr[...] = b
            B_ref[...] = b
            acc_ref[...] = x

        b = b_scr[...]
        g = _dot(b, wg_ref[...])
        u = _dot(b, wu_ref[...])
        G_ref[...] = g.astype(BF16)
        U_ref[...] = u.astype(BF16)
        a = ((g * _sigmoid(g)) * u).astype(BF16)
        A_ref[...] = a
        acc_ref[...] += _dot(a, wd_ref[...])

        @pl.when(j == NSH - 1)
        def _():
            hn_ref[...] = acc_ref[...]

    row = pl.BlockSpec((tm, D), lambda t, j: (t, 0))
    chunk = pl.BlockSpec((None, tm, FC), lambda t, j: (j, t, 0))
    cshape = jax.ShapeDtypeStruct((NSH, L, FC), BF16)
    return _carried_call(
        body, carry, name=name, grid=(L // tm, NSH),
        in_specs=[row, pl.BlockSpec((1, D), lambda t, j: (0, 0)),
                  pl.BlockSpec((None, D, FC), lambda t, j: (j, 0, 0)),
                  pl.BlockSpec((None, D, FC), lambda t, j: (j, 0, 0)),
                  pl.BlockSpec((None, FC, D), lambda t, j: (j, 0, 0))],
        out_specs=[row, chunk, chunk, chunk, row],
        out_shape=[jax.ShapeDtypeStruct((L, D), F32), cshape, cshape, cshape, jax.ShapeDtypeStruct((L, D), BF16)],
        scratch_shapes=[pltpu.VMEM((tm, D), F32), pltpu.VMEM((tm, D), BF16)],
        compiler_params=_params(2, VMEM_BIG), operands=[h, gain, wg, wu, wd])


def _ffn_bwd_act(dhn, h, gain, G, U, wg, wu, wd, name, carry=None):
    L = h.shape[0]
    tm = _tile(L)

    def body(dhn_ref, h_ref, g_ref, G_ref, U_ref, wg_ref, wu_ref, wd_ref, DG_ref, DU_ref, dh_ref, dgain_ref, db_ref):
        t, j = pl.program_id(0), pl.program_id(1)
        d_act = _dot(dhn_ref[...].astype(BF16), wd_ref[...], _NT)
        g = G_ref[...].astype(F32)
        u = U_ref[...].astype(F32)
        s = _sigmoid(g)
        dg = (d_act * u * (s * (1.0 + g * (1.0 - s)))).astype(BF16)
        du = (d_act * (g * s)).astype(BF16)
        DG_ref[...] = dg
        DU_ref[...] = du
        _acc(db_ref, _dot(dg, wg_ref[...], _NT) + _dot(du, wu_ref[...], _NT), j == 0)

        @pl.when(j == NSH - 1)
        def _():
            x = h_ref[...]
            dx, dgn = _rms_bwd(db_ref[...], x, g_ref[...], _rstd(x))
            dh_ref[...] = dhn_ref[...] + dx
            _acc(dgain_ref, dgn, t == 0)

    row = pl.BlockSpec((tm, D), lambda t, j: (t, 0))
    one = pl.BlockSpec((1, D), lambda t, j: (0, 0))
    chunk = pl.BlockSpec((None, tm, FC), lambda t, j: (j, t, 0))
    cshape = jax.ShapeDtypeStruct((NSH, L, FC), BF16)
    return _carried_call(
        body, carry, name=name, grid=(L // tm, NSH),
        in_specs=[row, row, one, chunk, chunk,
                  pl.BlockSpec((None, D, FC), lambda t, j: (j, 0, 0)),
                  pl.BlockSpec((None, D, FC), lambda t, j: (j, 0, 0)),
                  pl.BlockSpec((None, FC, D), lambda t, j: (j, 0, 0))],
        out_specs=[chunk, chunk, row, one],
        out_shape=[cshape, cshape, jax.ShapeDtypeStruct((L, D), F32), jax.ShapeDtypeStruct((1, D), F32)],
        scratch_shapes=[pltpu.VMEM((tm, D), F32)],
        compiler_params=_params(2), operands=[dhn, h, gain, G, U, wg, wu, wd])


def _ffn_bwd_w(B, A, DG, DU, dhn, name, carry=None):
    L = B.shape[0]
    tm = _tile(L, BIG_TILE)
    nt = L // tm

    def body(B_ref, A_ref, DG_ref, DU_ref, dhn_ref, dwg_ref, dwu_ref, dwd_ref, ag, au, ad):
        t = pl.program_id(1)
        b = B_ref[...]
        _acc(ag, _dot(b, DG_ref[...], _TN), t == 0)
        _acc(au, _dot(b, DU_ref[...], _TN), t == 0)
        _acc(ad, _dot(A_ref[...], dhn_ref[...].astype(BF16), _TN), t == 0)

        @pl.when(t == nt - 1)
        def _():
            dwg_ref[...] = ag[...].astype(BF16)
            dwu_ref[...] = au[...].astype(BF16)
            dwd_ref[...] = ad[...].astype(BF16)

    row = pl.BlockSpec((tm, D), lambda j, t: (t, 0))
    chunk = pl.BlockSpec((None, tm, FC), lambda j, t: (j, t, 0))
    return _carried_call(
        body, carry, name=name, grid=(NSH, nt), in_specs=[row, chunk, chunk, chunk, row],
        out_specs=[pl.BlockSpec((None, D, FC), lambda j, t: (j, 0, 0)),
                   pl.BlockSpec((None, D, FC), lambda j, t: (j, 0, 0)),
                   pl.BlockSpec((None, FC, D), lambda j, t: (j, 0, 0))],
        out_shape=[jax.ShapeDtypeStruct((NSH, D, FC), BF16), jax.ShapeDtypeStruct((NSH, D, FC), BF16),
                   jax.ShapeDtypeStruct((NSH, FC, D), BF16)],
        scratch_shapes=[pltpu.VMEM((D, FC), F32), pltpu.VMEM((D, FC), F32), pltpu.VMEM((FC, D), F32)],
        compiler_params=_params(2, VMEM_BIG), operands=[B, A, DG, DU, dhn])


def _loss_head(h, gain, target):
    L = h.shape[0]
    tm = _tile(L)

    def body(h_ref, g_ref, tgt_ref, dh_ref, dg_ref, loss_ref):
        t = pl.program_id(0)
        x = h_ref[...]
        g = g_ref[...]
        r = _rstd(x)
        rows = t * tm + lax.broadcasted_iota(jnp.int32, (tm, 1), 0)
        diff = jnp.where(rows >= N_META, x * r * g - tgt_ref[...], 0.0)
        part = 0.5 * jnp.sum(jnp.sum(diff * diff, axis=-1, keepdims=True) * (1.0 / D), axis=0, keepdims=True)
        dx, dg = _rms_bwd(diff * (1.0 / D), x, g, r)
        dh_ref[...] = dx
        _acc(dg_ref, dg, t == 0)
        _acc(loss_ref, jnp.broadcast_to(part, (1, 128)), t == 0)

    row = pl.BlockSpec((tm, D), lambda t: (t, 0))
    one = pl.BlockSpec((1, D), lambda t: (0, 0))
    return pl.pallas_call(
        body, name="loss_head", grid=(L // tm,), in_specs=[row, one, row],
        out_specs=(row, one, pl.BlockSpec((1, 128), lambda t: (0, 0))),
        out_shape=(jax.ShapeDtypeStruct((L, D), F32), jax.ShapeDtypeStruct((1, D), F32),
                   jax.ShapeDtypeStruct((1, 128), F32)),
        compiler_params=_params(1))(h, gain, target)


def _conv_fwd(z3, cw, name):
    L = z3.shape[0]
    tm = _tile(L)

    def body(b_ref, c_ref, u_ref, ch_ref, uh_ref, w_ref, y_ref, buf):
        t = pl.program_id(0)
        z = c_ref[...] * u_ref[...]
        buf[pl.ds(0, HALO), :] = jnp.where(t > 0, ch_ref[...] * uh_ref[...], 0.0)
        buf[pl.ds(HALO, tm), :] = z
        w0, w1, w2 = w_ref[0:1, :], w_ref[1:2, :], w_ref[2:3, :]
        conv = w2 * z + w1 * buf[pl.ds(HALO - 1, tm), :] + w0 * buf[pl.ds(HALO - 2, tm), :]
        y_ref[...] = (b_ref[...] * conv).astype(BF16)

    def col(k):
        return pl.BlockSpec((tm, D), lambda t: (t, k))

    def hcol(k):
        return pl.BlockSpec((HALO, D), lambda t: (jnp.maximum(t * (tm // HALO) - 1, 0), k))

    return pl.pallas_call(
        body, name=name, grid=(L // tm,),
        in_specs=[col(0), col(1), col(2), hcol(1), hcol(2), pl.BlockSpec((3, D), lambda t: (0, 0))],
        out_specs=pl.BlockSpec((tm, D), lambda t: (t, 0)),
        out_shape=jax.ShapeDtypeStruct((L, D), BF16),
        scratch_shapes=[pltpu.VMEM((HALO + tm, D), F32)], compiler_params=_params(1))(z3, z3, z3, z3, z3, cw)


def _conv_bwd(z3, cw, dy, name):
    L = z3.shape[0]
    tm = _tile(L)
    nt = L // tm
    last_h = L // HALO - 1

    def body(b_ref, c_ref, u_ref, ch_ref, uh_ref, w_ref, dy_ref, bn_ref, dyn_ref, dz3_ref, dw_ref, zbuf, dbuf):
        t = pl.program_id(0)
        w0, w1, w2 = w_ref[0:1, :], w_ref[1:2, :], w_ref[2:3, :]
        bgate, cgate, u = b_ref[...], c_ref[...], u_ref[...]
        z = cgate * u
        zbuf[pl.ds(0, HALO), :] = jnp.where(t > 0, ch_ref[...] * uh_ref[...], 0.0)
        zbuf[pl.ds(HALO, tm), :] = z
        z1 = zbuf[pl.ds(HALO - 1, tm), :]
        z2 = zbuf[pl.ds(HALO - 2, tm), :]
        conv = w2 * z + w1 * z1 + w0 * z2
        dyv = dy_ref[...]
        dconv = dyv * bgate
        dbuf[pl.ds(0, tm), :] = dconv
        dbuf[pl.ds(tm, HALO), :] = jnp.where(t < nt - 1, dyn_ref[...] * bn_ref[...], 0.0)
        dz = w2 * dconv + w1 * dbuf[pl.ds(1, tm), :] + w0 * dbuf[pl.ds(2, tm), :]
        dz3_ref[:, 0:D] = (dyv * conv).astype(BF16)
        dz3_ref[:, D:2 * D] = (dz * u).astype(BF16)
        dz3_ref[:, 2 * D:3 * D] = (dz * cgate).astype(BF16)
        for k, zk in enumerate((z2, z1, z)):
            _acc(dw_ref.at[k:k + 1], jnp.sum(dconv * zk, axis=0, keepdims=True), t == 0)

    def col(k):
        return pl.BlockSpec((tm, D), lambda t: (t, k))

    def hprev(k):
        return pl.BlockSpec((HALO, D), lambda t: (jnp.maximum(t * (tm // HALO) - 1, 0), k))

    def hnext(k):
        return pl.BlockSpec((HALO, D), lambda t: (jnp.minimum((t + 1) * (tm // HALO), last_h), k))

    return pl.pallas_call(
        body, name=name, grid=(nt,),
        in_specs=[col(0), col(1), col(2), hprev(1), hprev(2), pl.BlockSpec((3, D), lambda t: (0, 0)),
                  col(0), hnext(0), hnext(0)],
        out_specs=(pl.BlockSpec((tm, 3 * D), lambda t: (t, 0)), pl.BlockSpec((3, D), lambda t: (0, 0))),
        out_shape=(jax.ShapeDtypeStruct((L, 3 * D), BF16), jax.ShapeDtypeStruct((3, D), F32)),
        scratch_shapes=[pltpu.VMEM((HALO + tm, D), F32), pltpu.VMEM((tm + HALO, D), F32)],
        compiler_params=_params(1))(z3, z3, z3, z3, z3, cw, dy, z3, dy)


def _count_inv(pos, w):
    return 1.0 / jnp.minimum(pos + 1, w).astype(F32)


def _pool_fwd(h, gain, name):
    L = h.shape[0]
    tm = _tile(L)

    def body(h_ref, hh_ref, g_ref, mix_ref, buf):
        t = pl.program_id(0)
        g = g_ref[...]
        x = h_ref[...]
        a = x * _rstd(x) * g
        xh = hh_ref[...]
        buf[pl.ds(0, HALO), :] = jnp.where(t > 0, xh * _rstd(xh) * g, 0.0)
        buf[pl.ds(HALO, tm), :] = a
        pos = t * tm + lax.broadcasted_iota(jnp.int32, (tm, 1), 0)
        for gi, w in enumerate(POOL_WINDOWS):
            cols = pl.ds(gi * PG, PG)
            s = buf[pl.ds(HALO, tm), cols]
            for k in range(1, w):
                s = s + buf[pl.ds(HALO - k, tm), cols]
            mix_ref[:, gi * PG:(gi + 1) * PG] = (s / jnp.minimum(pos + 1, w).astype(F32)
                                                  - buf[pl.ds(HALO, tm), cols]).astype(BF16)

    return pl.pallas_call(
        body, name=name, grid=(L // tm,),
        in_specs=[pl.BlockSpec((tm, D), lambda t: (t, 0)),
                  pl.BlockSpec((HALO, D), lambda t: (jnp.maximum(t * (tm // HALO) - 1, 0), 0)),
                  pl.BlockSpec((1, D), lambda t: (0, 0))],
        out_specs=pl.BlockSpec((tm, D), lambda t: (t, 0)),
        out_shape=jax.ShapeDtypeStruct((L, D), BF16),
        scratch_shapes=[pltpu.VMEM((HALO + tm, D), F32)], compiler_params=_params(1))(h, h, gain)


def _pool_out(mix, wp, scale, h, name):
    L = mix.shape[0]
    tm = _tile(L)

    def body(mix_ref, wp_ref, s_ref, h_ref, out_ref):
        for gi in range(4):
            sl = slice(gi * PG, (gi + 1) * PG)
            pre = _dot(mix_ref[:, sl], wp_ref[gi])
            out_ref[:, sl] = h_ref[:, sl] + pre * s_ref[:, sl]

    row = pl.BlockSpec((tm, D), lambda t: (t, 0))
    return pl.pallas_call(
        body, name=name, grid=(L // tm,),
        in_specs=[row, pl.BlockSpec((4, PG, PG), lambda t: (0, 0, 0)), pl.BlockSpec((1, D), lambda t: (0, 0)), row],
        out_specs=row, out_shape=jax.ShapeDtypeStruct((L, D), F32), compiler_params=_params(1))(mix, wp, scale, h)


def _pool_out_bwd(dm, mix, wp, scale, name):
    L = mix.shape[0]
    tm = _tile(L)

    def body(dm_ref, mix_ref, wp_ref, s_ref, dmix_ref, dwp_ref, ds_ref):
        t = pl.program_id(0)
        for gi in range(4):
            sl = slice(gi * PG, (gi + 1) * PG)
            mx = mix_ref[:, sl]
            dmv = dm_ref[:, sl]
            pre = _dot(mx, wp_ref[gi])
            _acc(ds_ref.at[:, sl], jnp.sum(dmv * pre, axis=0, keepdims=True), t == 0)
            dpre = (dmv * s_ref[:, sl]).astype(BF16)
            _acc(dwp_ref.at[gi], _dot(mx, dpre, _TN), t == 0)
            dmix_ref[:, sl] = _dot(dpre, wp_ref[gi], _NT)

    row = pl.BlockSpec((tm, D), lambda t: (t, 0))
    one = pl.BlockSpec((1, D), lambda t: (0, 0))
    wsp = pl.BlockSpec((4, PG, PG), lambda t: (0, 0, 0))
    return pl.pallas_call(
        body, name=name, grid=(L // tm,), in_specs=[row, row, wsp, one],
        out_specs=(row, wsp, one),
        out_shape=(jax.ShapeDtypeStruct((L, D), F32), jax.ShapeDtypeStruct((4, PG, PG), F32),
                   jax.ShapeDtypeStruct((1, D), F32)),
        compiler_params=_params(1))(dm, mix, wp, scale)


def _pool_bwd(dmix, h, gain, resid, name):
    L = h.shape[0]
    tm = _tile(L)
    nt = L // tm
    last_h = L // HALO - 1

    def body(dm_ref, dmn_ref, h_ref, g_ref, r_ref, dh_ref, dg_ref, buf):
        t = pl.program_id(0)
        pos = t * tm + lax.broadcasted_iota(jnp.int32, (tm, 1), 0)
        posn = (t + 1) * tm + lax.broadcasted_iota(jnp.int32, (HALO, 1), 0)
        dmv = dm_ref[...]
        dmn = dmn_ref[...]
        for gi, w in enumerate(POOL_WINDOWS):
            sl = slice(gi * PG, (gi + 1) * PG)
            buf[pl.ds(0, tm), sl] = dmv[:, sl] * _count_inv(pos, w)
            buf[pl.ds(tm, HALO), sl] = jnp.where(t < nt - 1, dmn[:, sl] * _count_inv(posn, w), 0.0)
        parts = []
        for gi, w in enumerate(POOL_WINDOWS):
            cols = pl.ds(gi * PG, PG)
            s = buf[pl.ds(0, tm), cols]
            for k in range(1, w):
                s = s + buf[pl.ds(k, tm), cols]
            parts.append(s)
        da = jnp.concatenate(parts, axis=1) - dmv
        x = h_ref[...]
        dx, dg = _rms_bwd(da, x, g_ref[...], _rstd(x))
        dh_ref[...] = r_ref[...] + dx
        _acc(dg_ref, dg, t == 0)

    row = pl.BlockSpec((tm, D), lambda t: (t, 0))
    one = pl.BlockSpec((1, D), lambda t: (0, 0))
    return pl.pallas_call(
        body, name=name, grid=(nt,),
        in_specs=[row, pl.BlockSpec((HALO, D), lambda t: (jnp.minimum((t + 1) * (tm // HALO), last_h), 0)),
                  row, one, row],
        out_specs=(row, one),
        out_shape=(jax.ShapeDtypeStruct((L, D), F32), jax.ShapeDtypeStruct((1, D), F32)),
        scratch_shapes=[pltpu.VMEM((tm + HALO, D), F32)], compiler_params=_params(1))(dmix, dmix, h, gain, resid)


def _bucket_np(d):
    d = np.maximum(d, 0)
    df = np.maximum(d, 1).astype(np.float32)
    large = 16 + (np.log(df / np.float32(16)) / np.float32(math.log(128 / 16)) * np.float32(16)).astype(np.int32)
    return np.where(d < 16, d, np.minimum(large, N_BUCKETS - 1))


def _bias_index():
    iq = np.arange(BLK)[:, None]
    jk = np.arange(2 * BLK)[None, :]
    dist = BLK + iq - jk
    band = _bucket_np(dist)
    ok = (dist >= 0) & (dist < BLK)
    band1 = np.where(ok, band, N_BUCKETS)
    band0 = np.where(ok & (jk >= BLK), band, N_BUCKETS)
    im = np.arange(N_META)[None, :]
    unused = np.full((BLK, BLK - N_META), N_BUCKETS)
    var0 = np.concatenate([_bucket_np(N_META + iq - im), unused, band0], axis=1)
    var1 = np.concatenate([_bucket_np(N_META + BLK + iq - im), unused, band1], axis=1)
    dm = np.arange(N_META)[:, None] - im
    mm = np.where(dm >= 0, _bucket_np(dm), N_BUCKETS)
    segs = [var0, var1, mm]
    return np.concatenate([s.reshape(-1) for s in segs]).astype(np.int32), [s.shape for s in segs]


P_CHUNK = 9856


def _onehot(idx_ref, grad):
    rows = lax.broadcasted_iota(jnp.int32, (128, P_CHUNK), 0)
    hit = (rows == idx_ref[...]).astype(F32)
    return jnp.where(rows == N_BUCKETS, -1.0, hit) if grad else hit


def _bias_lookup(table_aug, idx, name):
    P = idx.shape[1]

    def body(t_ref, idx_ref, o_ref):
        o_ref[...] = lax.dot_general(t_ref[...], _onehot(idx_ref, False), _NN, precision=lax.Precision.HIGHEST,
                                     preferred_element_type=F32)

    return pl.pallas_call(
        body, name=name, grid=(P // P_CHUNK,),
        in_specs=[pl.BlockSpec((H, 128), lambda i: (0, 0)), pl.BlockSpec((1, P_CHUNK), lambda i: (0, i))],
        out_specs=pl.BlockSpec((H, P_CHUNK), lambda i: (0, i)),
        out_shape=jax.ShapeDtypeStruct((H, P), F32), compiler_params=_params(1))(table_aug, idx)


def _bias_lookup_bwd(dbias, idx, name):
    P = idx.shape[1]

    def body(d_ref, idx_ref, o_ref):
        part = lax.dot_general(d_ref[...], _onehot(idx_ref, True), _NT, precision=lax.Precision.HIGHEST,
                               preferred_element_type=F32)
        _acc(o_ref, part, pl.program_id(0) == 0)

    return pl.pallas_call(
        body, name=name, grid=(P // P_CHUNK,),
        in_specs=[pl.BlockSpec((H, P_CHUNK), lambda i: (0, i)), pl.BlockSpec((1, P_CHUNK), lambda i: (0, i))],
        out_specs=pl.BlockSpec((H, 128), lambda i: (0, 0)),
        out_shape=jax.ShapeDtypeStruct((H, 128), F32), compiler_params=_params(1))(dbias, idx)


def _probs(q, kbt, bias, sink):
    s = _dot(q, kbt) + bias
    m = jnp.maximum(jnp.max(s, axis=-1, keepdims=True), sink)
    e = jnp.exp(s - m)
    return e * (1.0 / (jnp.sum(e, axis=-1, keepdims=True) + jnp.exp(sink - m)))


def _attn_specs(nb):
    def cur(kh, n):
        return jnp.minimum(n, nb - 1)

    def prev(kh, n):
        return jnp.maximum(jnp.minimum(n, nb - 1) - 1, 0)

    return dict(
        q=pl.BlockSpec((GRP, BLK, HD), lambda kh, n: (kh, cur(kh, n), 0)),
        qt=pl.BlockSpec((GRP, HD, BLK), lambda kh, n: (kh, 0, cur(kh, n))),
        cur=pl.BlockSpec((None, BLK, HD), lambda kh, n: (kh, cur(kh, n), 0)),
        prev=pl.BlockSpec((None, BLK, HD), lambda kh, n: (kh, prev(kh, n), 0)),
        meta=pl.BlockSpec((None, BLK, HD), lambda kh, n: (kh, 0, 0)),
        curt=pl.BlockSpec((None, HD, BLK), lambda kh, n: (kh, 0, cur(kh, n))),
        prevt=pl.BlockSpec((None, HD, BLK), lambda kh, n: (kh, 0, prev(kh, n))),
        metat=pl.BlockSpec((None, HD, BLK), lambda kh, n: (kh, 0, 0)),
        bias=pl.BlockSpec((None, GRP, BLK, NKEY), lambda kh, n: (jnp.minimum(n, 1), kh, 0, 0)),
        sink=pl.BlockSpec((None, GRP * BLK, 1), lambda kh, n: (kh, 0, 0)))


def _attn_fwd(q, kt, kmt, v, vm, bias, sink, name, carry=None):
    S = q.shape[1]
    nb = S // BLK
    sp = _attn_specs(nb)

    def body(q_ref, kct_ref, kpt_ref, kmt_ref, vc_ref, vp_ref, vm_ref, bias_ref, sink_ref, o_ref):
        kbt = jnp.concatenate([kmt_ref[...], kpt_ref[...], kct_ref[...]], axis=1)
        vb = jnp.concatenate([vm_ref[...], vp_ref[...], vc_ref[...]], axis=0)
        p = _probs(q_ref[...].reshape(GRP * BLK, HD), kbt, bias_ref[...].reshape(GRP * BLK, NKEY), sink_ref[...])
        o_ref[...] = _dot(p.astype(BF16), vb).reshape(GRP, BLK, HD).astype(BF16)

    return _carried_call(
        body, carry, name=name, grid=(KV, nb),
        in_specs=[sp['q'], sp['curt'], sp['prevt'], sp['metat'], sp['cur'], sp['prev'], sp['meta'],
                  sp['bias'], sp['sink']],
        out_specs=[sp['q']], out_shape=[jax.ShapeDtypeStruct((H, S, HD), BF16)], scratch_shapes=[],
        compiler_params=_params(2), operands=[q, kt, kt, kmt, v, v, vm, bias, sink])


def _attn_bwd(q, qt, k, km, kt, kmt, vt, vmt, bias, sink, do, dot_, name, carry=None):
    S = q.shape[1]
    nb = S // BLK
    sp = _attn_specs(nb)

    def body(q_ref, qt_ref, kc_ref, kp_ref, km_ref, kct_ref, kpt_ref, kmt_ref, vct_ref, vpt_ref, vmt_ref,
             bias_ref, sink_ref, do_ref, dot_ref, dq_ref, dkt_ref, dvt_ref, dkmt_ref, dvmt_ref, dbias_ref, ck, cv):
        n = pl.program_id(1)

        @pl.when(n < nb)
        def _():
            kbt = jnp.concatenate([kmt_ref[...], kpt_ref[...], kct_ref[...]], axis=1)
            vbt = jnp.concatenate([vmt_ref[...], vpt_ref[...], vct_ref[...]], axis=1)
            kb = jnp.concatenate([km_ref[...], kp_ref[...], kc_ref[...]], axis=0)
            p = _probs(q_ref[...].reshape(GRP * BLK, HD), kbt, bias_ref[...].reshape(GRP * BLK, NKEY), sink_ref[...])
            dp = _dot(do_ref[...].reshape(GRP * BLK, HD), vbt)
            ds = p * (dp - jnp.sum(p * dp, axis=-1, keepdims=True))
            _acc(dbias_ref, ds.reshape(GRP, BLK, NKEY), n <= 1)
            ds16 = ds.astype(BF16)
            dq_ref[...] = _dot(ds16, kb).reshape(GRP, BLK, HD).astype(BF16)
            qtv = jnp.concatenate([qt_ref[g] for g in range(GRP)], axis=1)
            dotv = jnp.concatenate([dot_ref[g] for g in range(GRP)], axis=1)
            dkt = _dot(qtv, ds16)
            dvt = _dot(dotv, p.astype(BF16))
            _acc(dkmt_ref, dkt[:, 0:BLK], n == 0)
            _acc(dvmt_ref, dvt[:, 0:BLK], n == 0)

            @pl.when(n >= 1)
            def _():
                dkt_ref[...] = (ck[...] + dkt[:, BLK:2 * BLK]).astype(BF16)
                dvt_ref[...] = (cv[...] + dvt[:, BLK:2 * BLK]).astype(BF16)

            ck[...] = dkt[:, 2 * BLK:3 * BLK]
            cv[...] = dvt[:, 2 * BLK:3 * BLK]

        @pl.when(n == nb)
        def _():
            dkt_ref[...] = ck[...].astype(BF16)
            dvt_ref[...] = cv[...].astype(BF16)

    kvout = pl.BlockSpec((None, HD, BLK), lambda kh, n: (kh, 0, jnp.maximum(n - 1, 0)))
    return _carried_call(
        body, carry, name=name, grid=(KV, nb + 1),
        in_specs=[sp['q'], sp['qt'], sp['cur'], sp['prev'], sp['meta'], sp['curt'], sp['prevt'], sp['metat'],
                  sp['curt'], sp['prevt'], sp['metat'], sp['bias'], sp['sink'], sp['q'], sp['qt']],
        out_specs=[sp['q'], kvout, kvout, sp['metat'], sp['metat'], sp['bias']],
        out_shape=[jax.ShapeDtypeStruct((H, S, HD), BF16), jax.ShapeDtypeStruct((KV, HD, S), BF16),
                   jax.ShapeDtypeStruct((KV, HD, S), BF16), jax.ShapeDtypeStruct((KV, HD, BLK), F32),
                   jax.ShapeDtypeStruct((KV, HD, BLK), F32), jax.ShapeDtypeStruct((2, H, BLK, NKEY), F32)],
        scratch_shapes=[pltpu.VMEM((HD, BLK), F32), pltpu.VMEM((HD, BLK), F32)],
        compiler_params=_params(2), operands=[q, qt, k, k, km, kt, kt, kmt, vt, vt, vmt, bias, sink, do, dot_])


def _meta_softmax(q, k, bias, sink):
    s = _dot(q, k, _NT) + bias
    m = jnp.maximum(jnp.max(s, axis=-1, keepdims=True), sink)
    e = jnp.exp(s - m)
    return e * (1.0 / (jnp.sum(e, axis=-1, keepdims=True) + jnp.exp(sink - m)))


def _attn_meta_fwd(qm, km, vm, bias, sink, name):
    def body(q_ref, k_ref, v_ref, b_ref, s_ref, o_ref):
        for h in range(H):
            p = _meta_softmax(q_ref[h], k_ref[h // GRP], b_ref[h], s_ref[h])
            o_ref[h] = _dot(p.astype(BF16), v_ref[h // GRP]).astype(BF16)

    return pl.pallas_call(body, name=name, out_shape=jax.ShapeDtypeStruct((H, N_META, HD), BF16))(
        qm, km, vm, bias, sink)


def _attn_meta_bwd(qm, km, vm, bias, sink, do, name):
    def body(q_ref, k_ref, v_ref, b_ref, s_ref, do_ref, dq_ref, dk_ref, dv_ref, db_ref):
        for kh in range(KV):
            k, v = k_ref[kh], v_ref[kh]
            dk = jnp.zeros((N_META, HD), F32)
            dv = jnp.zeros((N_META, HD), F32)
            for g in range(GRP):
                h = kh * GRP + g
                q, dov = q_ref[h], do_ref[h]
                p = _meta_softmax(q, k, b_ref[h], s_ref[h])
                dp = _dot(dov, v, _NT)
                ds = p * (dp - jnp.sum(p * dp, axis=-1, keepdims=True))
                db_ref[h] = ds
                ds16 = ds.astype(BF16)
                dq_ref[h] = _dot(ds16, k).astype(BF16)
                dk = dk + _dot(ds16, q, _TN)
                dv = dv + _dot(p.astype(BF16), dov, _TN)
            dk_ref[kh] = dk
            dv_ref[kh] = dv

    return pl.pallas_call(
        body, name=name,
        out_shape=(jax.ShapeDtypeStruct((H, N_META, HD), BF16), jax.ShapeDtypeStruct((KV, N_META, HD), F32),
                   jax.ShapeDtypeStruct((KV, N_META, HD), F32), jax.ShapeDtypeStruct((H, N_META, N_META), F32)))(
        qm, km, vm, bias, sink, do)


def _heads(t, n):
    return jnp.transpose(t.reshape(t.shape[0], n, HD), (1, 0, 2))


def _heads_t(t, n):
    return jnp.transpose(t.reshape(t.shape[0], n, HD), (1, 2, 0))


def _unheads(t):
    return jnp.transpose(t, (1, 0, 2)).reshape(t.shape[1], t.shape[0] * HD)


def _unheads_t(t):
    return jnp.transpose(t, (2, 0, 1)).reshape(t.shape[2], t.shape[0] * HD)


def _pad_block(t, axis):
    pad = [(0, 0)] * t.ndim
    pad[axis] = (0, BLK - N_META)
    return jnp.pad(t, pad)


def _adam(w, g, m, v, name):
    R, C = w.shape
    tr = _tile(R, 256) if R % 16 == 0 else R

    def body(w_ref, g_ref, m_ref, v_ref, d_ref, mo_ref, vo_ref):
        gv = g_ref[...]
        mn = B1 * m_ref[...] + (1.0 - B1) * gv
        vn = B2 * v_ref[...] + (1.0 - B2) * (gv * gv)
        m_hat = mn / (1.0 - B1 ** STEP)
        v_hat = vn / (1.0 - B2 ** STEP)
        d_ref[...] = -LR * (m_hat / (jnp.sqrt(v_hat) + ADAM_EPS) + WD * w_ref[...])
        mo_ref[...] = mn
        vo_ref[...] = vn

    blk = pl.BlockSpec((tr, C), lambda i: (i, 0))
    shp = jax.ShapeDtypeStruct((R, C), F32)
    return pl.pallas_call(body, name=name, grid=(R // tr,), in_specs=[blk] * 4, out_specs=(blk,) * 3,
                          out_shape=(shp,) * 3, compiler_params=_params(1))(w, g, m, v)


def _sum8(lands, name):
    n = len(lands)
    _, R, C = lands[0].shape
    tr = _tile(R, 128)

    def body(*refs):
        for k in range(n):
            acc = refs[k][0].astype(F32)
            for s in range(1, 8):
                acc = acc + refs[k][s].astype(F32)
            refs[n + k][...] = acc

    return pl.pallas_call(
        body, name=name, grid=(R // tr,),
        in_specs=[pl.BlockSpec((8, tr, C), lambda i: (0, i, 0))] * n,
        out_specs=[pl.BlockSpec((tr, C), lambda i: (i, 0))] * n,
        out_shape=[jax.ShapeDtypeStruct((R, C), F32)] * n, compiler_params=_params(1))(*lands)


def _place():
    x, y, c = lax.axis_index("x"), lax.axis_index("y"), lax.axis_index("c")
    chips = [(1 - x, y), (x, 1 - y), (1 - x, 1 - y)]
    return x, y, c, chips


def _rcopy(src, dst, ssem, rsem, dev):
    return pltpu.make_async_remote_copy(src_ref=src, dst_ref=dst, send_sem=ssem, recv_sem=rsem,
                                        device_id=dev, device_id_type=MESH)


class _Carry:
    def __init__(self, kind, arrays):
        self.kind, self.arrays, self.n = kind, list(arrays), len(arrays)
        self.per = 3 if kind == 'gather' else 7
        if kind == 'gather':
            self.out_shape = [jax.ShapeDtypeStruct((NSH,) + a.shape, a.dtype) for a in self.arrays]
        else:
            self.out_shape = [jax.ShapeDtypeStruct((8,) + a.shape[2:], a.dtype) for a in self.arrays]
        dma = pltpu.SemaphoreType.DMA
        self.scratch = [dma((self.per * self.n,)), dma((self.per * self.n,)), dma((self.n,))]

    def _copies(self, cin, cout, sems):
        ssem, rsem, loc = sems
        x, y, c, chips = _place()
        local, sends, recvs = [], [], []
        for k in range(self.n):
            if self.kind == 'gather':
                me = 2 * x + y
                local.append(pltpu.make_async_copy(cin[k], cout[k].at[me], loc.at[k]))
                for j, (cx, cy) in enumerate(chips):
                    i = 3 * k + j
                    sends.append(_rcopy(cin[k], cout[k].at[me], ssem.at[i], rsem.at[i], (cx, cy, c)))
                    got = cout[k].at[2 * cx + cy]
                    recvs.append(_rcopy(got, got, ssem.at[i], rsem.at[i], (cx, cy, c)))
            else:
                me = 4 * x + 2 * y + c
                local.append(pltpu.make_async_copy(cin[k].at[2 * x + y, c], cout[k].at[me], loc.at[k]))
                for f in range(1, 8):
                    px = 1 - x if (f >> 2) & 1 else x
                    py = 1 - y if (f >> 1) & 1 else y
                    pc = 1 - c if f & 1 else c
                    i = 7 * k + f - 1
                    sends.append(_rcopy(cin[k].at[2 * px + py, pc], cout[k].at[me], ssem.at[i], rsem.at[i],
                                        (px, py, pc)))
                    got = cout[k].at[4 * px + 2 * py + pc]
                    recvs.append(_rcopy(got, got, ssem.at[i], rsem.at[i], (px, py, pc)))
        return local, sends, recvs

    def start(self, cin, cout, sems):
        local, sends, _ = self._copies(cin, cout, sems)
        for cp in local + sends:
            cp.start()

    def finish(self, cin, cout, sems):
        local, sends, recvs = self._copies(cin, cout, sems)
        for cp in recvs:
            cp.wait_recv()
        for cp in sends:
            cp.wait_send()
        for cp in local:
            cp.wait()


def _carried_call(body, carry, *, name, grid, in_specs, out_specs, out_shape, scratch_shapes, compiler_params,
                  operands):
    n_in, n_out = len(in_specs), len(out_specs)
    if carry is None:
        return pl.pallas_call(body, name=name, grid=grid, in_specs=in_specs, out_specs=out_specs,
                              out_shape=out_shape, scratch_shapes=scratch_shapes,
                              compiler_params=compiler_params)(*operands), []
    m = carry.n

    def full(*refs):
        ins, cin = refs[:n_in], refs[n_in:n_in + m]
        outs, cout = refs[n_in + m:n_in + m + n_out], refs[n_in + m + n_out:n_in + 2 * m + n_out]
        scr, sems = refs[n_in + 2 * m + n_out:len(refs) - 3], refs[len(refs) - 3:]
        ids = [pl.program_id(a) for a in range(len(grid))]
        first = functools.reduce(jnp.logical_and, [i == 0 for i in ids])
        last = functools.reduce(jnp.logical_and, [i == g - 1 for i, g in zip(ids, grid)])

        @pl.when(first)
        def _():
            carry.start(cin, cout, sems)

        body(*ins, *outs, *scr)

        @pl.when(last)
        def _():
            carry.finish(cin, cout, sems)

    res = pl.pallas_call(
        full, name=name, grid=grid, in_specs=list(in_specs) + [ANY] * m, out_specs=list(out_specs) + [ANY] * m,
        out_shape=list(out_shape) + carry.out_shape, scratch_shapes=list(scratch_shapes) + carry.scratch,
        compiler_params=compiler_params)(*operands, *carry.arrays)
    return res[:n_out], res[n_out:]


def _flush(carry, name):
    m = carry.n

    def body(*refs):
        cin, cout, sems = refs[:m], refs[m:2 * m], refs[2 * m:]
        carry.start(cin, cout, sems)
        carry.finish(cin, cout, sems)

    return pl.pallas_call(body, name=name, in_specs=[ANY] * m, out_specs=[ANY] * m, out_shape=carry.out_shape,
                          scratch_shapes=carry.scratch)(*carry.arrays)


def _pair_share(halves, where, leaf_shapes):
    n, nl = len(halves), len(leaf_shapes)

    def body(*refs):
        ins, outs = refs[:n], refs[n:n + nl]
        ssem, rsem, loc = refs[n + nl:]
        x, y, c, _ = _place()
        sib = (x, y, 1 - c)
        local = [pltpu.make_async_copy(ins[k], outs[lf].at[ly, c], loc.at[k]) for k, (lf, ly) in enumerate(where)]
        cps = [_rcopy(ins[k], outs[lf].at[ly, c], ssem.at[k], rsem.at[k], sib) for k, (lf, ly) in enumerate(where)]
        for cp in local + cps:
            cp.start()
        for k, (lf, ly) in enumerate(where):
            got = outs[lf].at[ly, 1 - c]
            _rcopy(got, got, ssem.at[k], rsem.at[k], sib).wait_recv()
        for cp in cps:
            cp.wait_send()
        for cp in local:
            cp.wait()

    dma = pltpu.SemaphoreType.DMA
    return pl.pallas_call(
        body, name="grad_pair_share", in_specs=[ANY] * n, out_specs=[ANY] * nl,
        out_shape=[jax.ShapeDtypeStruct(s, F32) for s in leaf_shapes],
        scratch_shapes=[dma((n,)), dma((n,)), dma((n,))])(*halves)


def _allreduce_small(pack):
    R = pack.shape[0]

    def body(in_ref, out_ref, buf, ssem, rsem):
        x, y, c, _ = _place()
        me = 4 * x + 2 * y + c
        buf[me] = in_ref[...]
        peers = []
        for k in range(1, 8):
            fx, fy, fc = (k >> 2) & 1, (k >> 1) & 1, k & 1
            peers.append((1 - x if fx else x, 1 - y if fy else y, 1 - c if fc else c))
        cps = [_rcopy(in_ref, buf.at[me], ssem.at[k], rsem.at[k], p) for k, p in enumerate(peers)]
        for cp in cps:
            cp.start()
        for k, (px, py, pc) in enumerate(peers):
            got = buf.at[4 * px + 2 * py + pc]
            _rcopy(got, got, ssem.at[k], rsem.at[k], (px, py, pc)).wait_recv()
        for cp in cps:
            cp.wait_send()
        acc = buf[0]
        for s in range(1, 8):
            acc = acc + buf[s]
        out_ref[...] = acc

    dma = pltpu.SemaphoreType.DMA
    return pl.pallas_call(
        body, name="allreduce_small", out_shape=jax.ShapeDtypeStruct(pack.shape, F32),
        in_specs=[pl.BlockSpec(memory_space=pltpu.VMEM)], out_specs=pl.BlockSpec(memory_space=pltpu.VMEM),
        scratch_shapes=[pltpu.VMEM((8, R, 128), F32), dma((7,)), dma((7,))])(pack)


def _pack(arrs):
    flat = jnp.concatenate([a.reshape(-1).astype(F32) for a in arrs])
    n = flat.shape[0]
    rows = -(-n // 1024) * 8
    return jnp.pad(flat, (0, rows * 128 - n)).reshape(rows, 128)


def _unpack(pack, shapes):
    flat, out, o = pack.reshape(-1), [], 0
    for s in shapes:
        n = int(np.prod(s))
        out.append(flat[o:o + n].reshape(s))
        o += n
    return out


def _ffn_keys(i):
    return [('ffn_w_gate', i), ('ffn_w_up', i), ('ffn_w_down', i)]


GATHER_PLAN = {'attn_fwd0': _ffn_keys(0) + [('conv_w_in', 0), ('conv_w_out', 0)],
               'ffn_fwd0': _ffn_keys(1),
               'ffn_fwd1': _ffn_keys(2) + [('pool_w', 0)],
               'ffn_fwd2': [('attn_w_qkv', 1), ('attn_w_o', 1)],
               'attn_fwd3': _ffn_keys(3)}
REDUCE_PLAN = {'attn_bwd3': _ffn_keys(3),
               'ffn_bwd_act2': [('attn_w_qkv', 1), ('attn_w_o', 1)],
               'ffn_bwd_act1': _ffn_keys(2) + [('pool_w', 0)],
               'ffn_bwd_act0': _ffn_keys(1),
               'ffn_bwd_w0': [('conv_w_in', 0), ('conv_w_out', 0)],
               'attn_bwd0': _ffn_keys(0)}
SUM_GROUPS = [[('ffn_w_gate', i) for i in range(DEPTH)] + [('ffn_w_up', i) for i in range(DEPTH)],
              [('ffn_w_down', i) for i in range(DEPTH)],
              [('attn_w_qkv', 0), ('attn_w_qkv', 1)],
              [('attn_w_o', 0), ('attn_w_o', 1), ('conv_w_out', 0)],
              [('conv_w_in', 0)], [('pool_w', 0)]]


def kernel(x, meta_tokens, rel_bias_table, norm_mix, norm_ffn, norm_final, attn_w_qkv, attn_b_qkv, attn_w_o, attn_b_o, attn_sinks, conv_w_in, conv_w, conv_w_out, pool_w, pool_scale, ffn_w_gate, ffn_w_up, ffn_w_down, loss_target, m_meta_tokens, m_rel_bias_table, m_norm_mix, m_norm_ffn, m_norm_final, m_attn_w_qkv, m_attn_b_qkv, m_attn_w_o, m_attn_b_o, m_attn_sinks, m_conv_w_in, m_conv_w, m_conv_w_out, m_pool_w, m_pool_scale, m_ffn_w_gate, m_ffn_w_up, m_ffn_w_down, v_meta_tokens, v_rel_bias_table, v_norm_mix, v_norm_ffn, v_norm_final, v_attn_w_qkv, v_attn_b_qkv, v_attn_w_o, v_attn_b_o, v_attn_sinks, v_conv_w_in, v_conv_w, v_conv_w_out, v_pool_w, v_pool_scale, v_ffn_w_gate, v_ffn_w_up, v_ffn_w_down):
    args = locals()
    w = {n: args[n] for n in WEIGHTS}
    mom = {n: args['m_' + n] for n in WEIGHTS}
    var = {n: args['v_' + n] for n in WEIGHTS}
    mx, my = lax.axis_index("x"), lax.axis_index("y")
    chip = 2 * mx + my
    S = x.shape[1]
    scale = jnp.asarray(HD ** -0.5, BF16)

    GW = {}
    pending = {}
    land = {}

    def gather_carry(call):
        keys = GATHER_PLAN.get(call)
        return _Carry('gather', [w[n][l].astype(BF16) for n, l in keys]) if keys else None

    def reduce_carry(call):
        keys = REDUCE_PLAN.get(call)
        return _Carry('reduce', [pending[k] for k in keys]) if keys else None

    def pend(n, l, g):
        pending[(n, l)] = g.reshape((NSH, 2) + HALF_SHAPE[n])

    small_in = jnp.concatenate([
        jnp.pad(w['meta_tokens'], ((0, 0), (0, 128))), w['attn_b_qkv'], jnp.pad(w['attn_b_o'], ((0, 0), (0, 128))),
        jnp.pad(w['conv_w'][0], ((0, 0), (0, 128))), jnp.pad(w['pool_scale'], ((0, 0), (0, 128)))], axis=0)
    gsmall, GW[('attn_w_qkv', 0)], GW[('attn_w_o', 0)] = _flush(
        _Carry('gather', [small_in, w['attn_w_qkv'][0].astype(BF16), w['attn_w_o'][0].astype(BF16)]), "gather_first")

    def cols(rows, width):
        return jnp.transpose(rows[:, :, :width], (1, 0, 2)).reshape(rows.shape[1], NSH * width)

    meta_full = cols(gsmall[:, 0:16], 256)
    b_qkv = cols(gsmall[:, 16:18], 384)
    b_o = cols(gsmall[:, 18:20], 256)
    conv_k = cols(gsmall[:, 20:23], 256)
    p_scale = cols(gsmall[:, 23:24], 256)

    idx_np, _ = _bias_index()
    idx = jnp.asarray(idx_np).reshape(1, -1)
    table_aug = jnp.concatenate([rel_bias_table.T, jnp.full((H, 1), NEG, F32),
                                 jnp.zeros((H, 127 - N_BUCKETS), F32)], axis=1)
    bias_flat = _bias_lookup(table_aug, idx, "bias_lookup")
    nblock = BLK * NKEY
    bias_blk = jnp.transpose(bias_flat[:, :2 * nblock].reshape(H, 2, BLK, NKEY), (1, 0, 2, 3))
    bias_mm = bias_flat[:, 2 * nblock:].reshape(H, N_META, N_META)

    h = jnp.concatenate([meta_full, x[0]], axis=0)
    saved = []
    for i in range(DEPTH):
        kind, j = i % 3, i // 3
        gm = norm_mix[i:i + 1]
        st = dict(h=h)
        if kind == 0:
            w_o = GW[('attn_w_o', j)].reshape(1, D, D)
            qkv, a = _linear(h, GW[('attn_w_qkv', j)], gain=gm, bias=b_qkv[j:j + 1], name=f"qkv{i}")
            qf, kf, vf = qkv[:, :H * HD] * scale, qkv[:, H * HD:(H + KV) * HD], qkv[:, (H + KV) * HD:]
            q, k, v = _heads(qf, H), _heads(kf, KV), _heads(vf, KV)
            lay = dict(q=q[:, N_META:], qt=_heads_t(qf[N_META:], H), qm=q[:, :N_META],
                       k=k[:, N_META:], kt=_heads_t(kf[N_META:], KV), km=k[:, :N_META],
                       kmp=_pad_block(k[:, :N_META], 1), kmt=_pad_block(_heads_t(kf[:N_META], KV), 2),
                       v=v[:, N_META:], vt=_heads_t(vf[N_META:], KV), vm=v[:, :N_META],
                       vmp=_pad_block(v[:, :N_META], 1), vmt=_pad_block(_heads_t(vf[:N_META], KV), 2))
            sink_r = jnp.broadcast_to(attn_sinks[j].reshape(KV, GRP, 1, 1), (KV, GRP, BLK, 1)).reshape(KV, GRP * BLK, 1)
            sink_m = jnp.broadcast_to(attn_sinks[j].reshape(H, 1, 1), (H, N_META, 1))
            (o_r,), got = _attn_fwd(lay['q'], lay['kt'], lay['kmt'], lay['v'], lay['vmp'], bias_blk, sink_r,
                                    f"attn_fwd{i}", gather_carry(f"attn_fwd{i}"))
            GW.update(zip(GATHER_PLAN.get(f"attn_fwd{i}", []), got))
            o_m = _attn_meta_fwd(lay['qm'], lay['km'], lay['vm'], bias_mm, sink_m, f"attn_meta_fwd{i}")
            o = _unheads(jnp.concatenate([o_m, o_r], axis=1))
            h1 = _linear(o, w_o, bias=b_o[j:j + 1], resid=h, out_dtype=F32, name=f"attn_out{i}")
            st.update(a=a, lay=lay, sinks=(sink_r, sink_m), o=o, w_o=w_o)
        elif kind == 1:
            z3, a = _linear(h, GW[('conv_w_in', j)], gain=gm, out_dtype=F32, cap=512, name=f"conv_in{i}")
            yv = _conv_fwd(z3, conv_k, f"conv_fwd{i}")
            w_cout = GW[('conv_w_out', j)].reshape(1, D, D)
            h1 = _linear(yv, w_cout, resid=h, out_dtype=F32, name=f"conv_out{i}")
            st.update(a=a, z3=z3, y=yv, w_cout=w_cout)
        else:
            w_pool = jnp.transpose(GW[('pool_w', j)], (1, 0, 2, 3)).reshape(4, PG, PG)
            mix = _pool_fwd(h, gm, f"pool_fwd{i}")
            h1 = _pool_out(mix, w_pool, p_scale, h, f"pool_out{i}")
            st.update(mix=mix, w_pool=w_pool)
        ffn_w = [GW[k] for k in _ffn_keys(i)]
        (hn, Gp, Up, Ap, Bp), got = _ffn_fwd(h1, norm_ffn[i:i + 1], *ffn_w, f"ffn_fwd{i}", gather_carry(f"ffn_fwd{i}"))
        GW.update(zip(GATHER_PLAN.get(f"ffn_fwd{i}", []), got))
        st.update(h1=h1, G=Gp, U=Up, A=Ap, B=Bp, ffn_w=ffn_w)
        saved.append(st)
        h = hn

    target = jnp.pad(loss_target[0], ((N_META, 0), (0, 0)))
    dh, d_nfinal, loss_part = _loss_head(h, norm_final.reshape(1, D), target)
    d_nmix, d_nffn = [None] * DEPTH, [None] * DEPTH
    d_bqkv, d_bo, d_sinks = [None, None], [None, None], [None, None]
    d_table = jnp.zeros((N_BUCKETS, H), F32)
    d_convk = d_pscale = None
    for i in reversed(range(DEPTH)):
        kind, j = i % 3, i // 3
        st = saved[i]
        call = f"ffn_bwd_act{i}"
        (DG, DU, dh1, d_nffn[i]), got = _ffn_bwd_act(dh, st['h1'], norm_ffn[i:i + 1], st['G'], st['U'], *st['ffn_w'],
                                                     call, reduce_carry(call))
        land.update(zip(REDUCE_PLAN.get(call, []), got))
        call = f"ffn_bwd_w{i}"
        (gw, uw, dw_), got = _ffn_bwd_w(st['B'], st['A'], DG, DU, dh, call, reduce_carry(call))
        land.update(zip(REDUCE_PLAN.get(call, []), got))
        pend('ffn_w_gate', i, gw)
        pend('ffn_w_up', i, uw)
        pend('ffn_w_down', i, dw_)
        gm = norm_mix[i:i + 1]
        if kind == 0:
            lay = st['lay']
            sink_r, sink_m = st['sinks']
            gwo, d_bo[j] = _wgrad(st['o'], dh1, mode='row', colsum=True, name=f"attn_out_wgrad{i}")
            pend('attn_w_o', j, gwo)
            dof = _linear_t(dh1, st['w_o'], name=f"attn_out_bwd{i}")
            do = _heads(dof, H)
            call = f"attn_bwd{i}"
            (dq_r, dkt, dvt, dkmt, dvmt, dbias), got = _attn_bwd(
                lay['q'], lay['qt'], lay['k'], lay['kmp'], lay['kt'], lay['kmt'], lay['vt'], lay['vmt'], bias_blk,
                sink_r, do[:, N_META:], _heads_t(dof[N_META:], H), call, reduce_carry(call))
            land.update(zip(REDUCE_PLAN.get(call, []), got))
            dq_m, dkm2, dvm2, dbmm = _attn_meta_bwd(lay['qm'], lay['km'], lay['vm'], bias_mm, sink_m,
                                                    do[:, :N_META], f"attn_meta_bwd{i}")
            dflat = jnp.concatenate([jnp.transpose(dbias, (1, 0, 2, 3)).reshape(H, -1), dbmm.reshape(H, -1)], axis=1)
            dts = _bias_lookup_bwd(dflat, idx, f"bias_lookup_bwd{i}")
            d_table = d_table + dts[:, :N_BUCKETS].T
            d_sinks[j] = dts[:, N_BUCKETS]
            dkm = (jnp.transpose(dkmt[:, :, :N_META], (0, 2, 1)) + dkm2).astype(BF16)
            dvm = (jnp.transpose(dvmt[:, :, :N_META], (0, 2, 1)) + dvm2).astype(BF16)
            dqkv = jnp.concatenate([
                jnp.concatenate([_unheads(dq_m), _unheads(dq_r)], axis=0) * scale,
                jnp.concatenate([_unheads(dkm), _unheads_t(dkt)], axis=0),
                jnp.concatenate([_unheads(dvm), _unheads_t(dvt)], axis=0)], axis=1)
            gq, d_bqkv[j] = _wgrad(st['a'], dqkv, mode='col', colsum=True, name=f"qkv_wgrad{i}")
            pend('attn_w_qkv', j, gq)
            dh, d_nmix[i] = _linear_t(dqkv, GW[('attn_w_qkv', j)], rms=(st['h'], gm, dh1), name=f"qkv_bwd{i}")
        elif kind == 1:
            pend('conv_w_out', j, _wgrad(st['y'], dh1, mode='row', name=f"conv_out_wgrad{i}"))
            dy = _linear_t(dh1, st['w_cout'], out_dtype=F32, name=f"conv_out_bwd{i}")
            dz3, d_convk = _conv_bwd(st['z3'], conv_k, dy, f"conv_bwd{i}")
            pend('conv_w_in', j, _wgrad(st['a'], dz3, mode='col', cap=512, name=f"conv_in_wgrad{i}"))
            dh, d_nmix[i] = _linear_t(dz3, GW[('conv_w_in', j)], rms=(st['h'], gm, dh1), cap=512,
                                      name=f"conv_in_bwd{i}")
        else:
            dmix, dwp, d_pscale = _pool_out_bwd(dh1, st['mix'], st['w_pool'], p_scale, f"pool_out_bwd{i}")
            pend('pool_w', j, jnp.transpose(dwp.reshape(4, NSH, PG // NSH, PG), (1, 0, 2, 3)).astype(BF16))
            dh, d_nmix[i] = _pool_bwd(dmix, st['h'], gm, dh1, f"pool_bwd{i}")

    last_keys = [('attn_w_qkv', 0), ('attn_w_o', 0)]
    land.update(zip(last_keys, _flush(_Carry('reduce', [pending[k] for k in last_keys]), "reduce_last")))
    halves = {}
    for gi, keys in enumerate(SUM_GROUPS):
        halves.update(zip(keys, _sum8([land[k] for k in keys], f"sum_partials{gi}")))
    keys = [k for grp in SUM_GROUPS for k in grp]
    shared = _pair_share([halves[k] for k in keys], [(BIG.index(n), l) for n, l in keys],
                         [(w[n].shape[0], 2) + HALF_SHAPE[n] for n in BIG])
    grads = {n: s.reshape(w[n].shape) for n, s in zip(BIG, shared)}

    small_full = [dh[:N_META], d_table, jnp.concatenate(d_nmix, axis=0), jnp.concatenate(d_nffn, axis=0), d_nfinal,
                  jnp.concatenate(d_bqkv, axis=0), jnp.concatenate(d_bo, axis=0), jnp.stack(d_sinks), d_convk,
                  d_pscale, loss_part[:, 0:1]]
    red = _unpack(_allreduce_small(_pack(small_full)), [a.shape for a in small_full])
    g_meta, g_table, g_nmix, g_nffn, g_nfinal, g_bqkv, g_bo, g_sinks, g_convk, g_pscale, loss = red

    def shard(a, width):
        return lax.dynamic_slice_in_dim(a, chip * width, width, axis=1)

    grads.update(meta_tokens=shard(g_meta, 256), rel_bias_table=g_table, norm_mix=g_nmix, norm_ffn=g_nffn,
                 norm_final=g_nfinal.reshape(D), attn_b_qkv=shard(g_bqkv, 384), attn_b_o=shard(g_bo, 256),
                 attn_sinks=g_sinks, conv_w=shard(g_convk, 256)[None], pool_scale=shard(g_pscale, 256))

    delta, new_m, new_v = {}, {}, {}
    for n in BIG:
        shp = w[n].shape
        r2 = (int(np.prod(shp[:-1])), shp[-1])
        dl, mn, vn = _adam(w[n].reshape(r2), grads[n].reshape(r2), mom[n].reshape(r2), var[n].reshape(r2), f"adam_{n}")
        delta[n], new_m[n], new_v[n] = dl.reshape(shp), mn.reshape(shp), vn.reshape(shp)
    shapes = [w[n].shape for n in SMALL]
    packed = [_pack([d[n] for n in SMALL]) for d in (w, grads, mom, var)]
    for dst, res in zip((delta, new_m, new_v), _adam(*packed, "adam_small")):
        dst.update(zip(SMALL, _unpack(res, shapes)))

    return (loss.reshape(()), dh[N_META:][None], *[grads[n] for n in WEIGHTS], *[delta[n] for n in WEIGHTS],
            *[new_m[n] for n in WEIGHTS], *[new_v[n] for n in WEIGHTS])
```

```python
import functools
import math

import numpy as np
import jax
import jax.numpy as jnp
from jax import lax
from jax.experimental import pallas as pl
from jax.experimental.pallas import tpu as pltpu

F32, BF16 = jnp.float32, jnp.bfloat16
D = 1024
N_META = 16
EPS = 1e-6
H, KV, GRP, HD = 16, 4, 4, 64
BLK = 128
NKEY = 3 * BLK
N_BUCKETS = 32
POOL_WINDOWS = (2, 4, 8, 16)
PG = 256
DFF = 2816
NSH = 4
FC = DFF // NSH
HALO = 16
NEG = -1e30
DEPTH = 4
LR, B1, B2, ADAM_EPS, WD, STEP = 0.001, 0.9, 0.999, 1e-08, 0.01, 10
MESH = pl.DeviceIdType.MESH
ANY = pl.BlockSpec(memory_space=pl.ANY)
VMEM_LIMIT = 48 * 1024 * 1024
VMEM_BIG = 58 * 1024 * 1024
BIG_TILE = 1024

_NN = (((1,), (0,)), ((), ()))
_NT = (((1,), (1,)), ((), ()))
_TN = (((0,), (0,)), ((), ()))

WEIGHTS = ['meta_tokens', 'rel_bias_table', 'norm_mix', 'norm_ffn', 'norm_final', 'attn_w_qkv', 'attn_b_qkv',
           'attn_w_o', 'attn_b_o', 'attn_sinks', 'conv_w_in', 'conv_w', 'conv_w_out', 'pool_w', 'pool_scale',
           'ffn_w_gate', 'ffn_w_up', 'ffn_w_down']
BIG = ['attn_w_qkv', 'attn_w_o', 'conv_w_in', 'conv_w_out', 'pool_w', 'ffn_w_gate', 'ffn_w_up', 'ffn_w_down']
SMALL = [w for w in WEIGHTS if w not in BIG]
HALF_SHAPE = {'attn_w_qkv': (512, 384), 'attn_w_o': (128, 1024), 'conv_w_in': (512, 768),
              'conv_w_out': (128, 1024), 'pool_w': (128, 256), 'ffn_w_gate': (512, 704),
              'ffn_w_up': (512, 704), 'ffn_w_down': (352, 1024)}


def _dot(a, b, dims=_NN):
    return lax.dot_general(a, b, dims, preferred_element_type=F32)


def _tile(n, cap=512):
    best = None
    for t in range(16, min(n, cap) + 1, 16):
        if n % t == 0:
            best = t
    assert best is not None, n
    return best


def _params(n_axes, vmem=VMEM_LIMIT):
    return pltpu.CompilerParams(dimension_semantics=("arbitrary",) * n_axes, vmem_limit_bytes=vmem)


def _rstd(x):
    return lax.rsqrt(jnp.mean(x * x, axis=-1, keepdims=True) + EPS)


def _rms_bwd(dy, x, g, r):
    u = dy * g
    dx = r * u - x * ((r * r * r) * (jnp.sum(x * u, axis=-1, keepdims=True) * (1.0 / D)))
    return dx, jnp.sum(dy * (x * r), axis=0, keepdims=True)


def _sigmoid(x):
    return 1.0 / (1.0 + jnp.exp(-x))


def _acc(ref, val, first):
    @pl.when(first)
    def _():
        ref[...] = val

    @pl.when(jnp.logical_not(first))
    def _():
        ref[...] += val


def _linear(x, w, *, gain=None, bias=None, resid=None, out_dtype=BF16, cap=BIG_TILE, name):
    L, K = x.shape
    J, _, Nc = w.shape
    tm = _tile(L, cap)
    has_g, has_b, has_r = gain is not None, bias is not None, resid is not None

    def body(*refs):
        refs = list(refs)
        x_ref, w_ref = refs[:2]
        i = 2
        g_ref = b_ref = r_ref = None
        if has_g:
            g_ref, i = refs[i], i + 1
        if has_b:
            b_ref, i = refs[i], i + 1
        if has_r:
            r_ref, i = refs[i], i + 1
        out_ref = refs[i]
        if has_g:
            xf = x_ref[...]
            xv = (xf * _rstd(xf) * g_ref[...]).astype(BF16)
            refs[i + 1][...] = xv
        else:
            xv = x_ref[...]
        for s in range(J):
            sl = slice(s * Nc, (s + 1) * Nc)
            acc = _dot(xv, w_ref[s])
            if has_b:
                acc = acc + b_ref[:, sl]
            if has_r:
                acc = acc + r_ref[:, sl]
            out_ref[:, sl] = acc.astype(out_dtype)

    row = lambda n: pl.BlockSpec((tm, n), lambda t: (t, 0))
    one = lambda n: pl.BlockSpec((1, n), lambda t: (0, 0))
    in_specs = [row(K), pl.BlockSpec((J, K, Nc), lambda t: (0, 0, 0))]
    ops = [x, w]
    if has_g:
        in_specs.append(one(K))
        ops.append(gain)
    if has_b:
        in_specs.append(one(J * Nc))
        ops.append(bias)
    if has_r:
        in_specs.append(row(J * Nc))
        ops.append(resid)
    out_specs, out_shape = row(J * Nc), jax.ShapeDtypeStruct((L, J * Nc), out_dtype)
    if has_g:
        out_specs, out_shape = (out_specs, row(K)), (out_shape, jax.ShapeDtypeStruct((L, K), BF16))
    return pl.pallas_call(body, name=name, grid=(L // tm,), in_specs=in_specs, out_specs=out_specs,
                          out_shape=out_shape, compiler_params=_params(1, VMEM_BIG))(*ops)


def _linear_t(dy, w, *, out_dtype=BF16, rms=None, cap=BIG_TILE, name):
    L = dy.shape[0]
    J, K, Nc = w.shape
    tm = _tile(L, cap)
    has_rms = rms is not None

    def body(*refs):
        t = pl.program_id(0)
        dy_ref, w_ref = refs[:2]
        acc = _dot(dy_ref[:, 0:Nc].astype(BF16), w_ref[0], _NT)
        for s in range(1, J):
            acc = acc + _dot(dy_ref[:, s * Nc:(s + 1) * Nc].astype(BF16), w_ref[s], _NT)
        if has_rms:
            h_ref, g_ref, r_ref, out_ref, dg_ref = refs[2:7]
            x = h_ref[...]
            dx, dg = _rms_bwd(acc, x, g_ref[...], _rstd(x))
            out_ref[...] = r_ref[...] + dx
            _acc(dg_ref, dg, t == 0)
        else:
            refs[2][...] = acc.astype(out_dtype)

    row = lambda n: pl.BlockSpec((tm, n), lambda t: (t, 0))
    in_specs = [row(J * Nc), pl.BlockSpec((J, K, Nc), lambda t: (0, 0, 0))]
    ops = [dy, w]
    if has_rms:
        in_specs += [row(K), pl.BlockSpec((1, K), lambda t: (0, 0)), row(K)]
        ops += list(rms)
        out_specs = (row(K), pl.BlockSpec((1, K), lambda t: (0, 0)))
        out_shape = (jax.ShapeDtypeStruct((L, K), F32), jax.ShapeDtypeStruct((1, K), F32))
    else:
        out_specs = row(K)
        out_shape = jax.ShapeDtypeStruct((L, K), out_dtype)
    return pl.pallas_call(body, name=name, grid=(L // tm,), in_specs=in_specs, out_specs=out_specs,
                          out_shape=out_shape, compiler_params=_params(1, VMEM_BIG))(*ops)


def _wgrad(x, dy, *, mode, colsum=False, cap=BIG_TILE, name):
    L, K = x.shape
    N = dy.shape[1]
    tm = _tile(L, cap)
    nt = L // tm
    oshape = (NSH, K, N // NSH) if mode == 'col' else (NSH, K // NSH, N)

    def body(*refs):
        t = pl.program_id(0)
        x_ref, dy_ref, out_ref = refs[:3]
        acc_ref = refs[-1]
        dyv = dy_ref[...]
        _acc(acc_ref, _dot(x_ref[...].astype(BF16), dyv.astype(BF16), _TN), t == 0)
        if colsum:
            _acc(refs[3], jnp.sum(dyv.astype(F32), axis=0, keepdims=True), t == 0)

        @pl.when(t == nt - 1)
        def _():
            for s in range(NSH):
                if mode == 'col':
                    out_ref[s] = acc_ref[:, s * oshape[2]:(s + 1) * oshape[2]].astype(BF16)
                else:
                    out_ref[s] = acc_ref[s * oshape[1]:(s + 1) * oshape[1], :].astype(BF16)

    out_specs = pl.BlockSpec(oshape, lambda t: (0, 0, 0))
    out_shape = jax.ShapeDtypeStruct(oshape, BF16)
    if colsum:
        out_specs = (out_specs, pl.BlockSpec((1, N), lambda t: (0, 0)))
        out_shape = (out_shape, jax.ShapeDtypeStruct((1, N), F32))
    return pl.pallas_call(
        body, name=name, grid=(nt,),
        in_specs=[pl.BlockSpec((tm, K), lambda t: (t, 0)), pl.BlockSpec((tm, N), lambda t: (t, 0))],
        out_specs=out_specs, out_shape=out_shape,
        scratch_shapes=[pltpu.VMEM((K, N), F32)], compiler_params=_params(1, VMEM_BIG))(x, dy)


def _ffn_fwd(h, gain, wg, wu, wd, name, carry=None):
    L = h.shape[0]
    tm = _tile(L, BIG_TILE)

    def body(h_ref, g_ref, wg_ref, wu_ref, wd_ref, hn_ref, G_ref, U_ref, A_ref, B_ref, acc_ref, b_scr):
        j = pl.program_id(1)

        @pl.when(j == 0)
        def _():
            x = h_ref[...]
            b = (x * _rstd(x) * g_ref[...]).astype(BF16)
            b_scr[...] = b
            B_ref[...] = b
            acc_ref[...] = x

        b = b_scr[...]
        g = _dot(b, wg_ref[...])
        u = _dot(b, wu_ref[...])
        G_ref[...] = g.astype(BF16)
        U_ref[...] = u.astype(BF16)
        a = ((g * _sigmoid(g)) * u).astype(BF16)
        A_ref[...] = a
        acc_ref[...] += _dot(a, wd_ref[...])

        @pl.when(j == NSH - 1)
        def _():
            hn_ref[...] = acc_ref[...]

    row = pl.BlockSpec((tm, D), lambda t, j: (t, 0))
    chunk = pl.BlockSpec((None, tm, FC), lambda t, j: (j, t, 0))
    cshape = jax.ShapeDtypeStruct((NSH, L, FC), BF16)
    return _carried_call(
        body, carry, name=name, grid=(L // tm, NSH),
        in_specs=[row, pl.BlockSpec((1, D), lambda t, j: (0, 0)),
                  pl.BlockSpec((None, D, FC), lambda t, j: (j, 0, 0)),
                  pl.BlockSpec((None, D, FC), lambda t, j: (j, 0, 0)),
                  pl.BlockSpec((None, FC, D), lambda t, j: (j, 0, 0))],
        out_specs=[row, chunk, chunk, chunk, row],
        out_shape=[jax.ShapeDtypeStruct((L, D), F32), cshape, cshape, cshape, jax.ShapeDtypeStruct((L, D), BF16)],
        scratch_shapes=[pltpu.VMEM((tm, D), F32), pltpu.VMEM((tm, D), BF16)],
        compiler_params=_params(2, VMEM_BIG), operands=[h, gain, wg, wu, wd])


def _ffn_bwd_act(dhn, h, gain, G, U, wg, wu, wd, name, carry=None):
    L = h.shape[0]
    tm = _tile(L, BIG_TILE)

    def body(dhn_ref, h_ref, g_ref, G_ref, U_ref, wg_ref, wu_ref, wd_ref, DG_ref, DU_ref, dh_ref, dgain_ref, db_ref):
        t, j = pl.program_id(0), pl.program_id(1)
        d_act = _dot(dhn_ref[...].astype(BF16), wd_ref[...], _NT)
        g = G_ref[...].astype(F32)
        u = U_ref[...].astype(F32)
        s = _sigmoid(g)
        dg = (d_act * u * (s * (1.0 + g * (1.0 - s)))).astype(BF16)
        du = (d_act * (g * s)).astype(BF16)
        DG_ref[...] = dg
        DU_ref[...] = du
        _acc(db_ref, _dot(dg, wg_ref[...], _NT) + _dot(du, wu_ref[...], _NT), j == 0)

        @pl.when(j == NSH - 1)
        def _():
            x = h_ref[...]
            dx, dgn = _rms_bwd(db_ref[...], x, g_ref[...], _rstd(x))
            dh_ref[...] = dhn_ref[...] + dx
            _acc(dgain_ref, dgn, t == 0)

    row = pl.BlockSpec((tm, D), lambda t, j: (t, 0))
    one = pl.BlockSpec((1, D), lambda t, j: (0, 0))
    chunk = pl.BlockSpec((None, tm, FC), lambda t, j: (j, t, 0))
    cshape = jax.ShapeDtypeStruct((NSH, L, FC), BF16)
    return _carried_call(
        body, carry, name=name, grid=(L // tm, NSH),
        in_specs=[row, row, one, chunk, chunk,
                  pl.BlockSpec((None, D, FC), lambda t, j: (j, 0, 0)),
                  pl.BlockSpec((None, D, FC), lambda t, j: (j, 0, 0)),
                  pl.BlockSpec((None, FC, D), lambda t, j: (j, 0, 0))],
        out_specs=[chunk, chunk, row, one],
        out_shape=[cshape, cshape, jax.ShapeDtypeStruct((L, D), F32), jax.ShapeDtypeStruct((1, D), F32)],
        scratch_shapes=[pltpu.VMEM((tm, D), F32)],
        compiler_params=_params(2, VMEM_BIG), operands=[dhn, h, gain, G, U, wg, wu, wd])


def _ffn_bwd_w(B, A, DG, DU, dhn, name, carry=None):
    L = B.shape[0]
    tm = _tile(L, BIG_TILE)
    nt = L // tm

    def body(B_ref, A_ref, DG_ref, DU_ref, dhn_ref, dwg_ref, dwu_ref, dwd_ref, ag, au, ad):
        t = pl.program_id(1)
        b = B_ref[...]
        _acc(ag, _dot(b, DG_ref[...], _TN), t == 0)
        _acc(au, _dot(b, DU_ref[...], _TN), t == 0)
        _acc(ad, _dot(A_ref[...], dhn_ref[...].astype(BF16), _TN), t == 0)

        @pl.when(t == nt - 1)
        def _():
            dwg_ref[...] = ag[...].astype(BF16)
            dwu_ref[...] = au[...].astype(BF16)
            dwd_ref[...] = ad[...].astype(BF16)

    row = pl.BlockSpec((tm, D), lambda j, t: (t, 0))
    chunk = pl.BlockSpec((None, tm, FC), lambda j, t: (j, t, 0))
    return _carried_call(
        body, carry, name=name, grid=(NSH, nt), in_specs=[row, chunk, chunk, chunk, row],
        out_specs=[pl.BlockSpec((None, D, FC), lambda j, t: (j, 0, 0)),
                   pl.BlockSpec((None, D, FC), lambda j, t: (j, 0, 0)),
                   pl.BlockSpec((None, FC, D), lambda j, t: (j, 0, 0))],
        out_shape=[jax.ShapeDtypeStruct((NSH, D, FC), BF16), jax.ShapeDtypeStruct((NSH, D, FC), BF16),
                   jax.ShapeDtypeStruct((NSH, FC, D), BF16)],
        scratch_shapes=[pltpu.VMEM((D, FC), F32), pltpu.VMEM((D, FC), F32), pltpu.VMEM((FC, D), F32)],
        compiler_params=_params(2, VMEM_BIG), operands=[B, A, DG, DU, dhn])


def _loss_head(h, gain, target):
    L = h.shape[0]
    tm = _tile(L)

    def body(h_ref, g_ref, tgt_ref, dh_ref, dg_ref, loss_ref):
        t = pl.program_id(0)
        x = h_ref[...]
        g = g_ref[...]
        r = _rstd(x)
        rows = t * tm + lax.broadcasted_iota(jnp.int32, (tm, 1), 0)
        diff = jnp.where(rows >= N_META, x * r * g - tgt_ref[...], 0.0)
        part = 0.5 * jnp.sum(jnp.sum(diff * diff, axis=-1, keepdims=True) * (1.0 / D), axis=0, keepdims=True)
        dx, dg = _rms_bwd(diff * (1.0 / D), x, g, r)
        dh_ref[...] = dx
        _acc(dg_ref, dg, t == 0)
        _acc(loss_ref, jnp.broadcast_to(part, (1, 128)), t == 0)

    row = pl.BlockSpec((tm, D), lambda t: (t, 0))
    one = pl.BlockSpec((1, D), lambda t: (0, 0))
    return pl.pallas_call(
        body, name="loss_head", grid=(L // tm,), in_specs=[row, one, row],
        out_specs=(row, one, pl.BlockSpec((1, 128), lambda t: (0, 0))),
        out_shape=(jax.ShapeDtypeStruct((L, D), F32), jax.ShapeDtypeStruct((1, D), F32),
                   jax.ShapeDtypeStruct((1, 128), F32)),
        compiler_params=_params(1))(h, gain, target)


def _conv_fwd(z3, cw, name):
    L = z3.shape[0]
    tm = _tile(L)

    def body(b_ref, c_ref, u_ref, ch_ref, uh_ref, w_ref, y_ref, buf):
        t = pl.program_id(0)
        z = c_ref[...] * u_ref[...]
        buf[pl.ds(0, HALO), :] = jnp.where(t > 0, ch_ref[...] * uh_ref[...], 0.0)
        buf[pl.ds(HALO, tm), :] = z
        w0, w1, w2 = w_ref[0:1, :], w_ref[1:2, :], w_ref[2:3, :]
        conv = w2 * z + w1 * buf[pl.ds(HALO - 1, tm), :] + w0 * buf[pl.ds(HALO - 2, tm), :]
        y_ref[...] = (b_ref[...] * conv).astype(BF16)

    def col(k):
        return pl.BlockSpec((tm, D), lambda t: (t, k))

    def hcol(k):
        return pl.BlockSpec((HALO, D), lambda t: (jnp.maximum(t * (tm // HALO) - 1, 0), k))

    return pl.pallas_call(
        body, name=name, grid=(L // tm,),
        in_specs=[col(0), col(1), col(2), hcol(1), hcol(2), pl.BlockSpec((3, D), lambda t: (0, 0))],
        out_specs=pl.BlockSpec((tm, D), lambda t: (t, 0)),
        out_shape=jax.ShapeDtypeStruct((L, D), BF16),
        scratch_shapes=[pltpu.VMEM((HALO + tm, D), F32)], compiler_params=_params(1))(z3, z3, z3, z3, z3, cw)


def _conv_bwd(z3, cw, dy, name):
    L = z3.shape[0]
    tm = _tile(L)
    nt = L // tm
    last_h = L // HALO - 1

    def body(b_ref, c_ref, u_ref, ch_ref, uh_ref, w_ref, dy_ref, bn_ref, dyn_ref, dz3_ref, dw_ref, zbuf, dbuf):
        t = pl.program_id(0)
        w0, w1, w2 = w_ref[0:1, :], w_ref[1:2, :], w_ref[2:3, :]
        bgate, cgate, u = b_ref[...], c_ref[...], u_ref[...]
        z = cgate * u
        zbuf[pl.ds(0, HALO), :] = jnp.where(t > 0, ch_ref[...] * uh_ref[...], 0.0)
        zbuf[pl.ds(HALO, tm), :] = z
        z1 = zbuf[pl.ds(HALO - 1, tm), :]
        z2 = zbuf[pl.ds(HALO - 2, tm), :]
        conv = w2 * z + w1 * z1 + w0 * z2
        dyv = dy_ref[...]
        dconv = dyv * bgate
        dbuf[pl.ds(0, tm), :] = dconv
        dbuf[pl.ds(tm, HALO), :] = jnp.where(t < nt - 1, dyn_ref[...] * bn_ref[...], 0.0)
        dz = w2 * dconv + w1 * dbuf[pl.ds(1, tm), :] + w0 * dbuf[pl.ds(2, tm), :]
        dz3_ref[:, 0:D] = (dyv * conv).astype(BF16)
        dz3_ref[:, D:2 * D] = (dz * u).astype(BF16)
        dz3_ref[:, 2 * D:3 * D] = (dz * cgate).astype(BF16)
        for k, zk in enumerate((z2, z1, z)):
            _acc(dw_ref.at[k:k + 1], jnp.sum(dconv * zk, axis=0, keepdims=True), t == 0)

    def col(k):
        return pl.BlockSpec((tm, D), lambda t: (t, k))

    def hprev(k):
        return pl.BlockSpec((HALO, D), lambda t: (jnp.maximum(t * (tm // HALO) - 1, 0), k))

    def hnext(k):
        return pl.BlockSpec((HALO, D), lambda t: (jnp.minimum((t + 1) * (tm // HALO), last_h), k))

    return pl.pallas_call(
        body, name=name, grid=(nt,),
        in_specs=[col(0), col(1), col(2), hprev(1), hprev(2), pl.BlockSpec((3, D), lambda t: (0, 0)),
                  col(0), hnext(0), hnext(0)],
        out_specs=(pl.BlockSpec((tm, 3 * D), lambda t: (t, 0)), pl.BlockSpec((3, D), lambda t: (0, 0))),
        out_shape=(jax.ShapeDtypeStruct((L, 3 * D), BF16), jax.ShapeDtypeStruct((3, D), F32)),
        scratch_shapes=[pltpu.VMEM((HALO + tm, D), F32), pltpu.VMEM((tm + HALO, D), F32)],
        compiler_params=_params(1))(z3, z3, z3, z3, z3, cw, dy, z3, dy)


def _count_inv(pos, w):
    return 1.0 / jnp.minimum(pos + 1, w).astype(F32)


def _pool_fwd(h, gain, name):
    L = h.shape[0]
    tm = _tile(L)

    def body(h_ref, hh_ref, g_ref, mix_ref, buf):
        t = pl.program_id(0)
        g = g_ref[...]
        x = h_ref[...]
        a = x * _rstd(x) * g
        xh = hh_ref[...]
        buf[pl.ds(0, HALO), :] = jnp.where(t > 0, xh * _rstd(xh) * g, 0.0)
        buf[pl.ds(HALO, tm), :] = a
        pos = t * tm + lax.broadcasted_iota(jnp.int32, (tm, 1), 0)
        for gi, w in enumerate(POOL_WINDOWS):
            cols = pl.ds(gi * PG, PG)
            s = buf[pl.ds(HALO, tm), cols]
            for k in range(1, w):
                s = s + buf[pl.ds(HALO - k, tm), cols]
            mix_ref[:, gi * PG:(gi + 1) * PG] = (s / jnp.minimum(pos + 1, w).astype(F32)
                                                  - buf[pl.ds(HALO, tm), cols]).astype(BF16)

    return pl.pallas_call(
        body, name=name, grid=(L // tm,),
        in_specs=[pl.BlockSpec((tm, D), lambda t: (t, 0)),
                  pl.BlockSpec((HALO, D), lambda t: (jnp.maximum(t * (tm // HALO) - 1, 0), 0)),
                  pl.BlockSpec((1, D), lambda t: (0, 0))],
        out_specs=pl.BlockSpec((tm, D), lambda t: (t, 0)),
        out_shape=jax.ShapeDtypeStruct((L, D), BF16),
        scratch_shapes=[pltpu.VMEM((HALO + tm, D), F32)], compiler_params=_params(1))(h, h, gain)


def _pool_out(mix, wp, scale, h, name):
    L = mix.shape[0]
    tm = _tile(L)

    def body(mix_ref, wp_ref, s_ref, h_ref, out_ref):
        for gi in range(4):
            sl = slice(gi * PG, (gi + 1) * PG)
            pre = _dot(mix_ref[:, sl], wp_ref[gi])
            out_ref[:, sl] = h_ref[:, sl] + pre * s_ref[:, sl]

    row = pl.BlockSpec((tm, D), lambda t: (t, 0))
    return pl.pallas_call(
        body, name=name, grid=(L // tm,),
        in_specs=[row, pl.BlockSpec((4, PG, PG), lambda t: (0, 0, 0)), pl.BlockSpec((1, D), lambda t: (0, 0)), row],
        out_specs=row, out_shape=jax.ShapeDtypeStruct((L, D), F32), compiler_params=_params(1))(mix, wp, scale, h)


def _pool_out_bwd(dm, mix, wp, scale, name):
    L = mix.shape[0]
    tm = _tile(L)

    def body(dm_ref, mix_ref, wp_ref, s_ref, dmix_ref, dwp_ref, ds_ref):
        t = pl.program_id(0)
        for gi in range(4):
            sl = slice(gi * PG, (gi + 1) * PG)
            mx = mix_ref[:, sl]
            dmv = dm_ref[:, sl]
            pre = _dot(mx, wp_ref[gi])
            _acc(ds_ref.at[:, sl], jnp.sum(dmv * pre, axis=0, keepdims=True), t == 0)
            dpre = (dmv * s_ref[:, sl]).astype(BF16)
            _acc(dwp_ref.at[gi], _dot(mx, dpre, _TN), t == 0)
            dmix_ref[:, sl] = _dot(dpre, wp_ref[gi], _NT)

    row = pl.BlockSpec((tm, D), lambda t: (t, 0))
    one = pl.BlockSpec((1, D), lambda t: (0, 0))
    wsp = pl.BlockSpec((4, PG, PG), lambda t: (0, 0, 0))
    return pl.pallas_call(
        body, name=name, grid=(L // tm,), in_specs=[row, row, wsp, one],
        out_specs=(row, wsp, one),
        out_shape=(jax.ShapeDtypeStruct((L, D), F32), jax.ShapeDtypeStruct((4, PG, PG), F32),
                   jax.ShapeDtypeStruct((1, D), F32)),
        compiler_params=_params(1))(dm, mix, wp, scale)


def _pool_bwd(dmix, h, gain, resid, name):
    L = h.shape[0]
    tm = _tile(L)
    nt = L // tm
    last_h = L // HALO - 1

    def body(dm_ref, dmn_ref, h_ref, g_ref, r_ref, dh_ref, dg_ref, buf):
        t = pl.program_id(0)
        pos = t * tm + lax.broadcasted_iota(jnp.int32, (tm, 1), 0)
        posn = (t + 1) * tm + lax.broadcasted_iota(jnp.int32, (HALO, 1), 0)
        dmv = dm_ref[...]
        dmn = dmn_ref[...]
        for gi, w in enumerate(POOL_WINDOWS):
            sl = slice(gi * PG, (gi + 1) * PG)
            buf[pl.ds(0, tm), sl] = dmv[:, sl] * _count_inv(pos, w)
            buf[pl.ds(tm, HALO), sl] = jnp.where(t < nt - 1, dmn[:, sl] * _count_inv(posn, w), 0.0)
        parts = []
        for gi, w in enumerate(POOL_WINDOWS):
            cols = pl.ds(gi * PG, PG)
            s = buf[pl.ds(0, tm), cols]
            for k in range(1, w):
                s = s + buf[pl.ds(k, tm), cols]
            parts.append(s)
        da = jnp.concatenate(parts, axis=1) - dmv
        x = h_ref[...]
        dx, dg = _rms_bwd(da, x, g_ref[...], _rstd(x))
        dh_ref[...] = r_ref[...] + dx
        _acc(dg_ref, dg, t == 0)

    row = pl.BlockSpec((tm, D), lambda t: (t, 0))
    one = pl.BlockSpec((1, D), lambda t: (0, 0))
    return pl.pallas_call(
        body, name=name, grid=(nt,),
        in_specs=[row, pl.BlockSpec((HALO, D), lambda t: (jnp.minimum((t + 1) * (tm // HALO), last_h), 0)),
                  row, one, row],
        out_specs=(row, one),
        out_shape=(jax.ShapeDtypeStruct((L, D), F32), jax.ShapeDtypeStruct((1, D), F32)),
        scratch_shapes=[pltpu.VMEM((tm + HALO, D), F32)], compiler_params=_params(1))(dmix, dmix, h, gain, resid)


def _bucket_np(d):
    d = np.maximum(d, 0)
    df = np.maximum(d, 1).astype(np.float32)
    large = 16 + (np.log(df / np.float32(16)) / np.float32(math.log(128 / 16)) * np.float32(16)).astype(np.int32)
    return np.where(d < 16, d, np.minimum(large, N_BUCKETS - 1))


def _bias_index():
    iq = np.arange(BLK)[:, None]
    jk = np.arange(2 * BLK)[None, :]
    dist = BLK + iq - jk
    band = _bucket_np(dist)
    ok = (dist >= 0) & (dist < BLK)
    band1 = np.where(ok, band, N_BUCKETS)
    band0 = np.where(ok & (jk >= BLK), band, N_BUCKETS)
    im = np.arange(N_META)[None, :]
    unused = np.full((BLK, BLK - N_META), N_BUCKETS)
    var0 = np.concatenate([_bucket_np(N_META + iq - im), unused, band0], axis=1)
    var1 = np.concatenate([_bucket_np(N_META + BLK + iq - im), unused, band1], axis=1)
    dm = np.arange(N_META)[:, None] - im
    mm = np.where(dm >= 0, _bucket_np(dm), N_BUCKETS)
    segs = [var0, var1, mm]
    return np.concatenate([s.reshape(-1) for s in segs]).astype(np.int32), [s.shape for s in segs]


P_CHUNK = 9856


def _onehot(idx_ref, grad):
    rows = lax.broadcasted_iota(jnp.int32, (128, P_CHUNK), 0)
    hit = (rows == idx_ref[...]).astype(F32)
    return jnp.where(rows == N_BUCKETS, -1.0, hit) if grad else hit


def _bias_lookup(table_aug, idx, name):
    P = idx.shape[1]

    def body(t_ref, idx_ref, o_ref):
        o_ref[...] = lax.dot_general(t_ref[...], _onehot(idx_ref, False), _NN, precision=lax.Precision.HIGHEST,
                                     preferred_element_type=F32)

    return pl.pallas_call(
        body, name=name, grid=(P // P_CHUNK,),
        in_specs=[pl.BlockSpec((H, 128), lambda i: (0, 0)), pl.BlockSpec((1, P_CHUNK), lambda i: (0, i))],
        out_specs=pl.BlockSpec((H, P_CHUNK), lambda i: (0, i)),
        out_shape=jax.ShapeDtypeStruct((H, P), F32), compiler_params=_params(1))(table_aug, idx)


def _bias_lookup_bwd(dbias, idx, name):
    P = idx.shape[1]

    def body(d_ref, idx_ref, o_ref):
        part = lax.dot_general(d_ref[...], _onehot(idx_ref, True), _NT, precision=lax.Precision.HIGHEST,
                               preferred_element_type=F32)
        _acc(o_ref, part, pl.program_id(0) == 0)

    return pl.pallas_call(
        body, name=name, grid=(P // P_CHUNK,),
        in_specs=[pl.BlockSpec((H, P_CHUNK), lambda i: (0, i)), pl.BlockSpec((1, P_CHUNK), lambda i: (0, i))],
        out_specs=pl.BlockSpec((H, 128), lambda i: (0, 0)),
        out_shape=jax.ShapeDtypeStruct((H, 128), F32), compiler_params=_params(1))(dbias, idx)


def _probs(q, kbt, bias, sink):
    s = _dot(q, kbt) + bias
    m = jnp.maximum(jnp.max(s, axis=-1, keepdims=True), sink)
    e = jnp.exp(s - m)
    return e * (1.0 / (jnp.sum(e, axis=-1, keepdims=True) + jnp.exp(sink - m)))


def _attn_specs(nb):
    def cur(kh, n):
        return jnp.minimum(n, nb - 1)

    def prev(kh, n):
        return jnp.maximum(jnp.minimum(n, nb - 1) - 1, 0)

    return dict(
        q=pl.BlockSpec((GRP, BLK, HD), lambda kh, n: (kh, cur(kh, n), 0)),
        qt=pl.BlockSpec((GRP, HD, BLK), lambda kh, n: (kh, 0, cur(kh, n))),
        cur=pl.BlockSpec((None, BLK, HD), lambda kh, n: (kh, cur(kh, n), 0)),
        prev=pl.BlockSpec((None, BLK, HD), lambda kh, n: (kh, prev(kh, n), 0)),
        meta=pl.BlockSpec((None, BLK, HD), lambda kh, n: (kh, 0, 0)),
        curt=pl.BlockSpec((None, HD, BLK), lambda kh, n: (kh, 0, cur(kh, n))),
        prevt=pl.BlockSpec((None, HD, BLK), lambda kh, n: (kh, 0, prev(kh, n))),
        metat=pl.BlockSpec((None, HD, BLK), lambda kh, n: (kh, 0, 0)),
        bias=pl.BlockSpec((None, GRP, BLK, NKEY), lambda kh, n: (jnp.minimum(n, 1), kh, 0, 0)),
        sink=pl.BlockSpec((None, GRP * BLK, 1), lambda kh, n: (kh, 0, 0)))


def _attn_fwd(q, k, km, v, vm, band, bmeta, sink, name, carry=None):
    S = q.shape[1]
    nb = S // BLK
    sp = _attn_specs(nb)
    var = lambda kh, n: (jnp.minimum(n, 1), kh, 0, 0)
    meta = pl.BlockSpec((None, N_META, HD), lambda kh, n: (kh, 0, 0))

    def body(q_ref, kc_ref, kp_ref, vc_ref, vp_ref, km_ref, vm_ref, band_ref, bm_ref, sink_ref, o_ref):
        qv = q_ref[...].reshape(GRP * BLK, HD)
        kb = jnp.concatenate([kp_ref[...], kc_ref[...]], axis=0)
        vb = jnp.concatenate([vp_ref[...], vc_ref[...]], axis=0)
        sink = sink_ref[...]
        s_b = _dot(qv, kb, _NT) + band_ref[...].reshape(GRP * BLK, 2 * BLK)
        s_m = _dot(qv, km_ref[...], _NT) + bm_ref[...].reshape(GRP * BLK, N_META)
        m = jnp.maximum(jnp.maximum(jnp.max(s_b, axis=-1, keepdims=True), jnp.max(s_m, axis=-1, keepdims=True)), sink)
        e_b = jnp.exp(s_b - m)
        e_m = jnp.exp(s_m - m)
        inv = 1.0 / (jnp.sum(e_b, axis=-1, keepdims=True) + jnp.sum(e_m, axis=-1, keepdims=True) + jnp.exp(sink - m))
        o = _dot((e_b * inv).astype(BF16), vb) + _dot((e_m * inv).astype(BF16), vm_ref[...])
        o_ref[...] = o.reshape(GRP, BLK, HD).astype(BF16)

    return _carried_call(
        body, carry, name=name, grid=(KV, nb),
        in_specs=[sp['q'], sp['cur'], sp['prev'], sp['cur'], sp['prev'], meta, meta,
                  pl.BlockSpec((None, GRP, BLK, 2 * BLK), var), pl.BlockSpec((None, GRP, BLK, N_META), var),
                  sp['sink']],
        out_specs=[sp['q']], out_shape=[jax.ShapeDtypeStruct((H, S, HD), BF16)], scratch_shapes=[],
        compiler_params=_params(2), operands=[q, k, k, v, v, km, vm, band, bmeta, sink])


def _attn_bwd(q, qt, k, km, kt, kmt, vt, vmt, bias, sink, do, dot_, name, carry=None):
    S = q.shape[1]
    nb = S // BLK
    sp = _attn_specs(nb)

    def body(q_ref, qt_ref, kc_ref, kp_ref, km_ref, kct_ref, kpt_ref, kmt_ref, vct_ref, vpt_ref, vmt_ref,
             bias_ref, sink_ref, do_ref, dot_ref, dq_ref, dkt_ref, dvt_ref, dkmt_ref, dvmt_ref, dbias_ref, ck, cv):
        n = pl.program_id(1)

        @pl.when(n < nb)
        def _():
            kbt = jnp.concatenate([kmt_ref[...], kpt_ref[...], kct_ref[...]], axis=1)
            vbt = jnp.concatenate([vmt_ref[...], vpt_ref[...], vct_ref[...]], axis=1)
            kb = jnp.concatenate([km_ref[...], kp_ref[...], kc_ref[...]], axis=0)
            p = _probs(q_ref[...].reshape(GRP * BLK, HD), kbt, bias_ref[...].reshape(GRP * BLK, NKEY), sink_ref[...])
            dp = _dot(do_ref[...].reshape(GRP * BLK, HD), vbt)
            ds = p * (dp - jnp.sum(p * dp, axis=-1, keepdims=True))
            _acc(dbias_ref, ds.reshape(GRP, BLK, NKEY), n <= 1)
            ds16 = ds.astype(BF16)
            dq_ref[...] = _dot(ds16, kb).reshape(GRP, BLK, HD).astype(BF16)
            qtv = jnp.concatenate([qt_ref[g] for g in range(GRP)], axis=1)
            dotv = jnp.concatenate([dot_ref[g] for g in range(GRP)], axis=1)
            dkt = _dot(qtv, ds16)
            dvt = _dot(dotv, p.astype(BF16))
            _acc(dkmt_ref, dkt[:, 0:BLK], n == 0)
            _acc(dvmt_ref, dvt[:, 0:BLK], n == 0)

            @pl.when(n >= 1)
            def _():
                dkt_ref[...] = (ck[...] + dkt[:, BLK:2 * BLK]).astype(BF16)
                dvt_ref[...] = (cv[...] + dvt[:, BLK:2 * BLK]).astype(BF16)

            ck[...] = dkt[:, 2 * BLK:3 * BLK]
            cv[...] = dvt[:, 2 * BLK:3 * BLK]

        @pl.when(n == nb)
        def _():
            dkt_ref[...] = ck[...].astype(BF16)
            dvt_ref[...] = cv[...].astype(BF16)

    kvout = pl.BlockSpec((None, HD, BLK), lambda kh, n: (kh, 0, jnp.maximum(n - 1, 0)))
    return _carried_call(
        body, carry, name=name, grid=(KV, nb + 1),
        in_specs=[sp['q'], sp['qt'], sp['cur'], sp['prev'], sp['meta'], sp['curt'], sp['prevt'], sp['metat'],
                  sp['curt'], sp['prevt'], sp['metat'], sp['bias'], sp['sink'], sp['q'], sp['qt']],
        out_specs=[sp['q'], kvout, kvout, sp['metat'], sp['metat'], sp['bias']],
        out_shape=[jax.ShapeDtypeStruct((H, S, HD), BF16), jax.ShapeDtypeStruct((KV, HD, S), BF16),
                   jax.ShapeDtypeStruct((KV, HD, S), BF16), jax.ShapeDtypeStruct((KV, HD, BLK), F32),
                   jax.ShapeDtypeStruct((KV, HD, BLK), F32), jax.ShapeDtypeStruct((2, H, BLK, NKEY), F32)],
        scratch_shapes=[pltpu.VMEM((HD, BLK), F32), pltpu.VMEM((HD, BLK), F32)],
        compiler_params=_params(2), operands=[q, qt, k, k, km, kt, kt, kmt, vt, vt, vmt, bias, sink, do, dot_])


def _meta_softmax(q, k, bias, sink):
    s = _dot(q, k, _NT) + bias
    m = jnp.maximum(jnp.max(s, axis=-1, keepdims=True), sink)
    e = jnp.exp(s - m)
    return e * (1.0 / (jnp.sum(e, axis=-1, keepdims=True) + jnp.exp(sink - m)))


def _attn_meta_fwd(qm, km, vm, bias, sink, name):
    def body(q_ref, k_ref, v_ref, b_ref, s_ref, o_ref):
        for h in range(H):
            p = _meta_softmax(q_ref[h], k_ref[h // GRP], b_ref[h], s_ref[h])
            o_ref[h] = _dot(p.astype(BF16), v_ref[h // GRP]).astype(BF16)

    return pl.pallas_call(body, name=name, out_shape=jax.ShapeDtypeStruct((H, N_META, HD), BF16))(
        qm, km, vm, bias, sink)


def _attn_meta_bwd(qm, km, vm, bias, sink, do, name):
    def body(q_ref, k_ref, v_ref, b_ref, s_ref, do_ref, dq_ref, dk_ref, dv_ref, db_ref):
        for kh in range(KV):
            k, v = k_ref[kh], v_ref[kh]
            dk = jnp.zeros((N_META, HD), F32)
            dv = jnp.zeros((N_META, HD), F32)
            for g in range(GRP):
                h = kh * GRP + g
                q, dov = q_ref[h], do_ref[h]
                p = _meta_softmax(q, k, b_ref[h], s_ref[h])
                dp = _dot(dov, v, _NT)
                ds = p * (dp - jnp.sum(p * dp, axis=-1, keepdims=True))
                db_ref[h] = ds
                ds16 = ds.astype(BF16)
                dq_ref[h] = _dot(ds16, k).astype(BF16)
                dk = dk + _dot(ds16, q, _TN)
                dv = dv + _dot(p.astype(BF16), dov, _TN)
            dk_ref[kh] = dk
            dv_ref[kh] = dv

    return pl.pallas_call(
        body, name=name,
        out_shape=(jax.ShapeDtypeStruct((H, N_META, HD), BF16), jax.ShapeDtypeStruct((KV, N_META, HD), F32),
                   jax.ShapeDtypeStruct((KV, N_META, HD), F32), jax.ShapeDtypeStruct((H, N_META, N_META), F32)))(
        qm, km, vm, bias, sink, do)


def _heads(t, n):
    return jnp.transpose(t.reshape(t.shape[0], n, HD), (1, 0, 2))


def _heads_t(t, n):
    return jnp.transpose(t.reshape(t.shape[0], n, HD), (1, 2, 0))


def _unheads(t):
    return jnp.transpose(t, (1, 0, 2)).reshape(t.shape[1], t.shape[0] * HD)


def _unheads_t(t):
    return jnp.transpose(t, (2, 0, 1)).reshape(t.shape[2], t.shape[0] * HD)


def _pad_block(t, axis):
    pad = [(0, 0)] * t.ndim
    pad[axis] = (0, BLK - N_META)
    return jnp.pad(t, pad)


def _adam(w, g, m, v, name):
    R, C = w.shape
    tr = _tile(R, 256) if R % 16 == 0 else R

    def body(w_ref, g_ref, m_ref, v_ref, d_ref, mo_ref, vo_ref, go_ref):
        gv = g_ref[...]
        go_ref[...] = gv
        mn = B1 * m_ref[...] + (1.0 - B1) * gv
        vn = B2 * v_ref[...] + (1.0 - B2) * (gv * gv)
        m_hat = mn / (1.0 - B1 ** STEP)
        v_hat = vn / (1.0 - B2 ** STEP)
        d_ref[...] = -LR * (m_hat / (jnp.sqrt(v_hat) + ADAM_EPS) + WD * w_ref[...])
        mo_ref[...] = mn
        vo_ref[...] = vn

    blk = pl.BlockSpec((tr, C), lambda i: (i, 0))
    shp = jax.ShapeDtypeStruct((R, C), F32)
    return pl.pallas_call(body, name=name, grid=(R // tr,), in_specs=[blk] * 4, out_specs=(blk,) * 4,
                          out_shape=(shp,) * 4, compiler_params=_params(1))(w, g, m, v)


def _sum_leaf(lands, pends, ids, name):
    n = len(lands)
    _, R, C = lands[0].shape
    tr = _tile(R, 128)

    def body(ids_ref, *refs):
        out_ref = refs[2 * n]
        for k in range(n):
            acc = refs[n + k][...].astype(F32)
            for s in range(7):
                acc = acc + refs[k][s].astype(F32)
            out_ref[k] = acc

    grid_spec = pltpu.PrefetchScalarGridSpec(
        num_scalar_prefetch=1, grid=(R // tr,),
        in_specs=[pl.BlockSpec((7, tr, C), lambda i, ids: (0, i, 0))] * n
        + [pl.BlockSpec((None, None, tr, C), lambda i, ids: (ids[0], ids[1], i, 0))] * n,
        out_specs=pl.BlockSpec((n, None, tr, C), lambda i, ids: (0, ids[1], i, 0)))
    return pl.pallas_call(body, name=name, grid_spec=grid_spec, out_shape=jax.ShapeDtypeStruct((n, 2, R, C), F32),
                          compiler_params=_params(1))(ids, *lands, *pends)


def _place():
    x, y, c = lax.axis_index("x"), lax.axis_index("y"), lax.axis_index("c")
    chips = [(1 - x, y), (x, 1 - y), (1 - x, 1 - y)]
    return x, y, c, chips


def _rcopy(src, dst, ssem, rsem, dev):
    return pltpu.make_async_remote_copy(src_ref=src, dst_ref=dst, send_sem=ssem, recv_sem=rsem,
                                        device_id=dev, device_id_type=MESH)


class _Carry:
    def __init__(self, kind, arrays):
        self.kind, self.arrays, self.n = kind, list(arrays), len(arrays)
        self.per = 3 if kind == 'gather' else 7
        if kind == 'gather':
            self.out_shape = [jax.ShapeDtypeStruct((NSH,) + a.shape, a.dtype) for a in self.arrays]
        else:
            self.out_shape = [jax.ShapeDtypeStruct((7,) + a.shape[2:], a.dtype) for a in self.arrays]
        dma = pltpu.SemaphoreType.DMA
        self.scratch = [dma((self.per * self.n,)), dma((self.per * self.n,)), dma((self.n,))]
        if kind == 'gather':
            self.scratch += [pltpu.VMEM(a.shape, a.dtype) for a in self.arrays]

    def _copies(self, cin, cout, scr):
        ssem, rsem, loc = scr[:3]
        x, y, c, chips = _place()
        local, sends, recvs = [], [], []
        for k in range(self.n):
            if self.kind == 'gather':
                me = 2 * x + y
                local.append((pltpu.make_async_copy(cin[k], scr[3 + k], loc.at[k]),
                              pltpu.make_async_copy(scr[3 + k], cout[k].at[me], loc.at[k])))
                for j, (cx, cy) in enumerate(chips):
                    i = 3 * k + j
                    sends.append(_rcopy(cin[k], cout[k].at[me], ssem.at[i], rsem.at[i], (cx, cy, c)))
                    got = cout[k].at[2 * cx + cy]
                    recvs.append(_rcopy(got, got, ssem.at[i], rsem.at[i], (cx, cy, c)))
            else:
                for f in range(1, 8):
                    px = 1 - x if (f >> 2) & 1 else x
                    py = 1 - y if (f >> 1) & 1 else y
                    pc = 1 - c if f & 1 else c
                    i = 7 * k + f - 1
                    got = cout[k].at[f - 1]
                    sends.append(_rcopy(cin[k].at[2 * px + py, pc], got, ssem.at[i], rsem.at[i], (px, py, pc)))
                    recvs.append(_rcopy(got, got, ssem.at[i], rsem.at[i], (px, py, pc)))
        return local, sends, recvs

    def start(self, cin, cout, scr):
        local, sends, _ = self._copies(cin, cout, scr)
        for cp in [to_vmem for to_vmem, _ in local] + sends:
            cp.start()

    def finish(self, cin, cout, scr):
        local, sends, recvs = self._copies(cin, cout, scr)
        for to_vmem, to_slot in local:
            to_vmem.wait()
            to_slot.start()
        for cp in recvs:
            cp.wait_recv()
        for cp in sends:
            cp.wait_send()
        for _, to_slot in local:
            to_slot.wait()


def _carried_call(body, carry, *, name, grid, in_specs, out_specs, out_shape, scratch_shapes, compiler_params,
                  operands):
    n_in, n_out = len(in_specs), len(out_specs)
    if carry is None:
        return pl.pallas_call(body, name=name, grid=grid, in_specs=in_specs, out_specs=out_specs,
                              out_shape=out_shape, scratch_shapes=scratch_shapes,
                              compiler_params=compiler_params)(*operands), []
    m = carry.n

    def full(*refs):
        ins, cin = refs[:n_in], refs[n_in:n_in + m]
        outs, cout = refs[n_in + m:n_in + m + n_out], refs[n_in + m + n_out:n_in + 2 * m + n_out]
        own = len(refs) - len(carry.scratch)
        scr, sems = refs[n_in + 2 * m + n_out:own], refs[own:]
        ids = [pl.program_id(a) for a in range(len(grid))]
        first = functools.reduce(jnp.logical_and, [i == 0 for i in ids])
        last = functools.reduce(jnp.logical_and, [i == g - 1 for i, g in zip(ids, grid)])

        @pl.when(first)
        def _():
            carry.start(cin, cout, sems)

        body(*ins, *outs, *scr)

        @pl.when(last)
        def _():
            carry.finish(cin, cout, sems)

    res = pl.pallas_call(
        full, name=name, grid=grid, in_specs=list(in_specs) + [ANY] * m, out_specs=list(out_specs) + [ANY] * m,
        out_shape=list(out_shape) + carry.out_shape, scratch_shapes=list(scratch_shapes) + carry.scratch,
        compiler_params=compiler_params)(*operands, *carry.arrays)
    return res[:n_out], res[n_out:]


def _flush(carry, name):
    m = carry.n

    def body(*refs):
        cin, cout, sems = refs[:m], refs[m:2 * m], refs[2 * m:]
        carry.start(cin, cout, sems)
        carry.finish(cin, cout, sems)

    return pl.pallas_call(body, name=name, in_specs=[ANY] * m, out_specs=[ANY] * m, out_shape=carry.out_shape,
                          scratch_shapes=carry.scratch)(*carry.arrays)


def _pair_share(leaves):
    n = len(leaves)

    def body(*refs):
        ins, outs = refs[:n], refs[n:2 * n]
        ssem, rsem = refs[2 * n:]
        x, y, c, _ = _place()
        sib = (x, y, 1 - c)
        cps = [_rcopy(ins[k].at[:, c], outs[k].at[:, c], ssem.at[k], rsem.at[k], sib) for k in range(n)]
        for cp in cps:
            cp.start()
        for k in range(n):
            got = outs[k].at[:, 1 - c]
            _rcopy(got, got, ssem.at[k], rsem.at[k], sib).wait_recv()
        for cp in cps:
            cp.wait_send()

    dma = pltpu.SemaphoreType.DMA
    return pl.pallas_call(
        body, name="grad_pair_share", in_specs=[ANY] * n, out_specs=[ANY] * n,
        out_shape=[jax.ShapeDtypeStruct(a.shape, a.dtype) for a in leaves],
        input_output_aliases={k: k for k in range(n)},
        scratch_shapes=[dma((n,)), dma((n,))])(*leaves)


def _allreduce_small(pack):
    R = pack.shape[0]

    def body(in_ref, out_ref, buf, ssem, rsem):
        x, y, c, _ = _place()
        me = 4 * x + 2 * y + c
        buf[me] = in_ref[...]
        peers = []
        for k in range(1, 8):
            fx, fy, fc = (k >> 2) & 1, (k >> 1) & 1, k & 1
            peers.append((1 - x if fx else x, 1 - y if fy else y, 1 - c if fc else c))
        cps = [_rcopy(in_ref, buf.at[me], ssem.at[k], rsem.at[k], p) for k, p in enumerate(peers)]
        for cp in cps:
            cp.start()
        for k, (px, py, pc) in enumerate(peers):
            got = buf.at[4 * px + 2 * py + pc]
            _rcopy(got, got, ssem.at[k], rsem.at[k], (px, py, pc)).wait_recv()
        for cp in cps:
            cp.wait_send()
        acc = buf[0]
        for s in range(1, 8):
            acc = acc + buf[s]
        out_ref[...] = acc

    dma = pltpu.SemaphoreType.DMA
    return pl.pallas_call(
        body, name="allreduce_small", out_shape=jax.ShapeDtypeStruct(pack.shape, F32),
        in_specs=[pl.BlockSpec(memory_space=pltpu.VMEM)], out_specs=pl.BlockSpec(memory_space=pltpu.VMEM),
        scratch_shapes=[pltpu.VMEM((8, R, 128), F32), dma((7,)), dma((7,))])(pack)


def _pack(arrs):
    flat = jnp.concatenate([a.reshape(-1).astype(F32) for a in arrs])
    n = flat.shape[0]
    rows = -(-n // 1024) * 8
    return jnp.pad(flat, (0, rows * 128 - n)).reshape(rows, 128)


def _unpack(pack, shapes):
    flat, out, o = pack.reshape(-1), [], 0
    for s in shapes:
        n = int(np.prod(s))
        out.append(flat[o:o + n].reshape(s))
        o += n
    return out


def _ffn_keys(i):
    return [('ffn_w_gate', i), ('ffn_w_up', i), ('ffn_w_down', i)]


GATHER_PLAN = {'attn_fwd0': _ffn_keys(0) + [('conv_w_in', 0), ('conv_w_out', 0)],
               'ffn_fwd0': _ffn_keys(1),
               'ffn_fwd1': _ffn_keys(2) + [('pool_w', 0)],
               'ffn_fwd2': [('attn_w_qkv', 1), ('attn_w_o', 1)],
               'attn_fwd3': _ffn_keys(3)}
REDUCE_PLAN = {'attn_bwd3': _ffn_keys(3),
               'ffn_bwd_act2': [('attn_w_qkv', 1), ('attn_w_o', 1)],
               'ffn_bwd_act1': _ffn_keys(2) + [('pool_w', 0)],
               'ffn_bwd_act0': _ffn_keys(1),
               'ffn_bwd_w0': [('conv_w_in', 0), ('conv_w_out', 0)],
               'attn_bwd0': _ffn_keys(0)}


def kernel(x, meta_tokens, rel_bias_table, norm_mix, norm_ffn, norm_final, attn_w_qkv, attn_b_qkv, attn_w_o, attn_b_o, attn_sinks, conv_w_in, conv_w, conv_w_out, pool_w, pool_scale, ffn_w_gate, ffn_w_up, ffn_w_down, loss_target, m_meta_tokens, m_rel_bias_table, m_norm_mix, m_norm_ffn, m_norm_final, m_attn_w_qkv, m_attn_b_qkv, m_attn_w_o, m_attn_b_o, m_attn_sinks, m_conv_w_in, m_conv_w, m_conv_w_out, m_pool_w, m_pool_scale, m_ffn_w_gate, m_ffn_w_up, m_ffn_w_down, v_meta_tokens, v_rel_bias_table, v_norm_mix, v_norm_ffn, v_norm_final, v_attn_w_qkv, v_attn_b_qkv, v_attn_w_o, v_attn_b_o, v_attn_sinks, v_conv_w_in, v_conv_w, v_conv_w_out, v_pool_w, v_pool_scale, v_ffn_w_gate, v_ffn_w_up, v_ffn_w_down):
    args = locals()
    w = {n: args[n] for n in WEIGHTS}
    mom = {n: args['m_' + n] for n in WEIGHTS}
    var = {n: args['v_' + n] for n in WEIGHTS}
    mx, my = lax.axis_index("x"), lax.axis_index("y")
    chip = 2 * mx + my
    S = x.shape[1]
    scale = jnp.asarray(HD ** -0.5, BF16)

    GW = {}
    pending = {}
    land = {}

    def gather_carry(call):
        keys = GATHER_PLAN.get(call)
        return _Carry('gather', [w[n][l].astype(BF16) for n, l in keys]) if keys else None

    def reduce_carry(call):
        keys = REDUCE_PLAN.get(call)
        return _Carry('reduce', [pending[k] for k in keys]) if keys else None

    def pend(n, l, g):
        pending[(n, l)] = g.reshape((NSH, 2) + HALF_SHAPE[n])

    small_in = jnp.concatenate([
        jnp.pad(w['meta_tokens'], ((0, 0), (0, 128))), w['attn_b_qkv'], jnp.pad(w['attn_b_o'], ((0, 0), (0, 128))),
        jnp.pad(w['conv_w'][0], ((0, 0), (0, 128))), jnp.pad(w['pool_scale'], ((0, 0), (0, 128)))], axis=0)
    gsmall, GW[('attn_w_qkv', 0)], GW[('attn_w_o', 0)] = _flush(
        _Carry('gather', [small_in, w['attn_w_qkv'][0].astype(BF16), w['attn_w_o'][0].astype(BF16)]), "gather_first")

    def cols(rows, width):
        return jnp.transpose(rows[:, :, :width], (1, 0, 2)).reshape(rows.shape[1], NSH * width)

    meta_full = cols(gsmall[:, 0:16], 256)
    b_qkv = cols(gsmall[:, 16:18], 384)
    b_o = cols(gsmall[:, 18:20], 256)
    conv_k = cols(gsmall[:, 20:23], 256)
    p_scale = cols(gsmall[:, 23:24], 256)

    idx_np, _ = _bias_index()
    idx = jnp.asarray(idx_np).reshape(1, -1)
    table_aug = jnp.concatenate([rel_bias_table.T, jnp.full((H, 1), NEG, F32),
                                 jnp.zeros((H, 127 - N_BUCKETS), F32)], axis=1)
    bias_flat = _bias_lookup(table_aug, idx, "bias_lookup")
    nblock = BLK * NKEY
    bias_blk = jnp.transpose(bias_flat[:, :2 * nblock].reshape(H, 2, BLK, NKEY), (1, 0, 2, 3))
    bias_mm = bias_flat[:, 2 * nblock:].reshape(H, N_META, N_META)
    bias_band, bias_meta = bias_blk[..., BLK:], bias_blk[..., :N_META]

    h = jnp.concatenate([meta_full, x[0]], axis=0)
    saved = []
    for i in range(DEPTH):
        kind, j = i % 3, i // 3
        gm = norm_mix[i:i + 1]
        st = dict(h=h)
        if kind == 0:
            w_o = GW[('attn_w_o', j)].reshape(1, D, D)
            qkv, a = _linear(h, GW[('attn_w_qkv', j)], gain=gm, bias=b_qkv[j:j + 1], name=f"qkv{i}")
            qf, kf, vf = qkv[:, :H * HD] * scale, qkv[:, H * HD:(H + KV) * HD], qkv[:, (H + KV) * HD:]
            q, k, v = _heads(qf, H), _heads(kf, KV), _heads(vf, KV)
            lay = dict(q=q[:, N_META:], qt=_heads_t(qf[N_META:], H), qm=q[:, :N_META],
                       k=k[:, N_META:], kt=_heads_t(kf[N_META:], KV), km=k[:, :N_META],
                       kmp=_pad_block(k[:, :N_META], 1), kmt=_pad_block(_heads_t(kf[:N_META], KV), 2),
                       v=v[:, N_META:], vt=_heads_t(vf[N_META:], KV), vm=v[:, :N_META],
                       vmt=_pad_block(_heads_t(vf[:N_META], KV), 2))
            sink_r = jnp.broadcast_to(attn_sinks[j].reshape(KV, GRP, 1, 1), (KV, GRP, BLK, 1)).reshape(KV, GRP * BLK, 1)
            sink_m = jnp.broadcast_to(attn_sinks[j].reshape(H, 1, 1), (H, N_META, 1))
            (o_r,), got = _attn_fwd(lay['q'], lay['k'], lay['km'], lay['v'], lay['vm'], bias_band, bias_meta, sink_r,
                                    f"attn_fwd{i}", gather_carry(f"attn_fwd{i}"))
            GW.update(zip(GATHER_PLAN.get(f"attn_fwd{i}", []), got))
            o_m = _attn_meta_fwd(lay['qm'], lay['km'], lay['vm'], bias_mm, sink_m, f"attn_meta_fwd{i}")
            o = _unheads(jnp.concatenate([o_m, o_r], axis=1))
            h1 = _linear(o, w_o, bias=b_o[j:j + 1], resid=h, out_dtype=F32, name=f"attn_out{i}")
            st.update(a=a, lay=lay, sinks=(sink_r, sink_m), o=o, w_o=w_o)
        elif kind == 1:
            z3, a = _linear(h, GW[('conv_w_in', j)], gain=gm, out_dtype=F32, cap=512, name=f"conv_in{i}")
            yv = _conv_fwd(z3, conv_k, f"conv_fwd{i}")
            w_cout = GW[('conv_w_out', j)].reshape(1, D, D)
            h1 = _linear(yv, w_cout, resid=h, out_dtype=F32, name=f"conv_out{i}")
            st.update(a=a, z3=z3, y=yv, w_cout=w_cout)
        else:
            w_pool = jnp.transpose(GW[('pool_w', j)], (1, 0, 2, 3)).reshape(4, PG, PG)
            mix = _pool_fwd(h, gm, f"pool_fwd{i}")
            h1 = _pool_out(mix, w_pool, p_scale, h, f"pool_out{i}")
            st.update(mix=mix, w_pool=w_pool)
        ffn_w = [GW[k] for k in _ffn_keys(i)]
        (hn, Gp, Up, Ap, Bp), got = _ffn_fwd(h1, norm_ffn[i:i + 1], *ffn_w, f"ffn_fwd{i}", gather_carry(f"ffn_fwd{i}"))
        GW.update(zip(GATHER_PLAN.get(f"ffn_fwd{i}", []), got))
        st.update(h1=h1, G=Gp, U=Up, A=Ap, B=Bp, ffn_w=ffn_w)
        saved.append(st)
        h = hn

    target = jnp.pad(loss_target[0], ((N_META, 0), (0, 0)))
    dh, d_nfinal, loss_part = _loss_head(h, norm_final.reshape(1, D), target)
    d_nmix, d_nffn = [None] * DEPTH, [None] * DEPTH
    d_bqkv, d_bo, d_sinks = [None, None], [None, None], [None, None]
    d_table = jnp.zeros((N_BUCKETS, H), F32)
    d_convk = d_pscale = None
    for i in reversed(range(DEPTH)):
        kind, j = i % 3, i // 3
        st = saved[i]
        call = f"ffn_bwd_act{i}"
        (DG, DU, dh1, d_nffn[i]), got = _ffn_bwd_act(dh, st['h1'], norm_ffn[i:i + 1], st['G'], st['U'], *st['ffn_w'],
                                                     call, reduce_carry(call))
        land.update(zip(REDUCE_PLAN.get(call, []), got))
        call = f"ffn_bwd_w{i}"
        (gw, uw, dw_), got = _ffn_bwd_w(st['B'], st['A'], DG, DU, dh, call, reduce_carry(call))
        land.update(zip(REDUCE_PLAN.get(call, []), got))
        pend('ffn_w_gate', i, gw)
        pend('ffn_w_up', i, uw)
        pend('ffn_w_down', i, dw_)
        gm = norm_mix[i:i + 1]
        if kind == 0:
            lay = st['lay']
            sink_r, sink_m = st['sinks']
            gwo, d_bo[j] = _wgrad(st['o'], dh1, mode='row', colsum=True, name=f"attn_out_wgrad{i}")
            pend('attn_w_o', j, gwo)
            dof = _linear_t(dh1, st['w_o'], name=f"attn_out_bwd{i}")
            do = _heads(dof, H)
            call = f"attn_bwd{i}"
            (dq_r, dkt, dvt, dkmt, dvmt, dbias), got = _attn_bwd(
                lay['q'], lay['qt'], lay['k'], lay['kmp'], lay['kt'], lay['kmt'], lay['vt'], lay['vmt'], bias_blk,
                sink_r, do[:, N_META:], _heads_t(dof[N_META:], H), call, reduce_carry(call))
            land.update(zip(REDUCE_PLAN.get(call, []), got))
            dq_m, dkm2, dvm2, dbmm = _attn_meta_bwd(lay['qm'], lay['km'], lay['vm'], bias_mm, sink_m,
                                                    do[:, :N_META], f"attn_meta_bwd{i}")
            dflat = jnp.concatenate([jnp.transpose(dbias, (1, 0, 2, 3)).reshape(H, -1), dbmm.reshape(H, -1)], axis=1)
            dts = _bias_lookup_bwd(dflat, idx, f"bias_lookup_bwd{i}")
            d_table = d_table + dts[:, :N_BUCKETS].T
            d_sinks[j] = dts[:, N_BUCKETS]
            dkm = (jnp.transpose(dkmt[:, :, :N_META], (0, 2, 1)) + dkm2).astype(BF16)
            dvm = (jnp.transpose(dvmt[:, :, :N_META], (0, 2, 1)) + dvm2).astype(BF16)
            dqkv = jnp.concatenate([
                jnp.concatenate([_unheads(dq_m), _unheads(dq_r)], axis=0) * scale,
                jnp.concatenate([_unheads(dkm), _unheads_t(dkt)], axis=0),
                jnp.concatenate([_unheads(dvm), _unheads_t(dvt)], axis=0)], axis=1)
            gq, d_bqkv[j] = _wgrad(st['a'], dqkv, mode='col', colsum=True, name=f"qkv_wgrad{i}")
            pend('attn_w_qkv', j, gq)
            dh, d_nmix[i] = _linear_t(dqkv, GW[('attn_w_qkv', j)], rms=(st['h'], gm, dh1), name=f"qkv_bwd{i}")
        elif kind == 1:
            pend('conv_w_out', j, _wgrad(st['y'], dh1, mode='row', name=f"conv_out_wgrad{i}"))
            dy = _linear_t(dh1, st['w_cout'], out_dtype=F32, name=f"conv_out_bwd{i}")
            dz3, d_convk = _conv_bwd(st['z3'], conv_k, dy, f"conv_bwd{i}")
            pend('conv_w_in', j, _wgrad(st['a'], dz3, mode='col', cap=512, name=f"conv_in_wgrad{i}"))
            dh, d_nmix[i] = _linear_t(dz3, GW[('conv_w_in', j)], rms=(st['h'], gm, dh1), cap=512,
                                      name=f"conv_in_bwd{i}")
        else:
            dmix, dwp, d_pscale = _pool_out_bwd(dh1, st['mix'], st['w_pool'], p_scale, f"pool_out_bwd{i}")
            pend('pool_w', j, jnp.transpose(dwp.reshape(4, NSH, PG // NSH, PG), (1, 0, 2, 3)).astype(BF16))
            dh, d_nmix[i] = _pool_bwd(dmix, st['h'], gm, dh1, f"pool_bwd{i}")

    last_keys = [('attn_w_qkv', 0), ('attn_w_o', 0)]
    land.update(zip(last_keys, _flush(_Carry('reduce', [pending[k] for k in last_keys]), "reduce_last")))
    ids = jnp.stack([chip, lax.axis_index("c")]).astype(jnp.int32)
    shared = _pair_share([_sum_leaf([land[(n, l)] for l in range(w[n].shape[0])],
                                    [pending[(n, l)] for l in range(w[n].shape[0])], ids, f"sum_{n}") for n in BIG])
    grads = {n: s.reshape(w[n].shape) for n, s in zip(BIG, shared)}

    small_full = [dh[:N_META], d_table, jnp.concatenate(d_nmix, axis=0), jnp.concatenate(d_nffn, axis=0), d_nfinal,
                  jnp.concatenate(d_bqkv, axis=0), jnp.concatenate(d_bo, axis=0), jnp.stack(d_sinks), d_convk,
                  d_pscale, loss_part[:, 0:1]]
    red = _unpack(_allreduce_small(_pack(small_full)), [a.shape for a in small_full])
    g_meta, g_table, g_nmix, g_nffn, g_nfinal, g_bqkv, g_bo, g_sinks, g_convk, g_pscale, loss = red

    def shard(a, width):
        return lax.dynamic_slice_in_dim(a, chip * width, width, axis=1)

    grads.update(meta_tokens=shard(g_meta, 256), rel_bias_table=g_table, norm_mix=g_nmix, norm_ffn=g_nffn,
                 norm_final=g_nfinal.reshape(D), attn_b_qkv=shard(g_bqkv, 384), attn_b_o=shard(g_bo, 256),
                 attn_sinks=g_sinks, conv_w=shard(g_convk, 256)[None], pool_scale=shard(g_pscale, 256))

    delta, new_m, new_v = {}, {}, {}
    for n in BIG:
        shp = w[n].shape
        r2 = (int(np.prod(shp[:-1])), shp[-1])
        dl, mn, vn, gn = _adam(w[n].reshape(r2), grads[n].reshape(r2), mom[n].reshape(r2), var[n].reshape(r2),
                               f"adam_{n}")
        delta[n], new_m[n], new_v[n], grads[n] = dl.reshape(shp), mn.reshape(shp), vn.reshape(shp), gn.reshape(shp)
    shapes = [w[n].shape for n in SMALL]
    packed = [_pack([d[n] for n in SMALL]) for d in (w, grads, mom, var)]
    for dst, res in zip((delta, new_m, new_v), _adam(*packed, "adam_small")[:3]):
        dst.update(zip(SMALL, _unpack(res, shapes)))

    return (loss.reshape(()), dh[N_META:][None], *[grads[n] for n in WEIGHTS], *[delta[n] for n in WEIGHTS],
            *[new_m[n] for n in WEIGHTS], *[new_v[n] for n in WEIGHTS])
```

```python
import functools
import math

import numpy as np
import jax
import jax.numpy as jnp
from jax import lax
from jax.experimental import pallas as pl
from jax.experimental.pallas import tpu as pltpu

F32, BF16 = jnp.float32, jnp.bfloat16
D = 1024
N_META = 16
EPS = 1e-6
H, KV, GRP, HD = 16, 4, 4, 64
BLK = 128
NKEY = 3 * BLK
N_BUCKETS = 32
POOL_WINDOWS = (2, 4, 8, 16)
PG = 256
DFF = 2816
NSH = 4
FC = DFF // NSH
HALO = 16
NEG = -1e30
DEPTH = 4
LR, B1, B2, ADAM_EPS, WD, STEP = 0.001, 0.9, 0.999, 1e-08, 0.01, 10
MESH = pl.DeviceIdType.MESH
ANY = pl.BlockSpec(memory_space=pl.ANY)
VMEM_LIMIT = 48 * 1024 * 1024
VMEM_BIG = 58 * 1024 * 1024
BIG_TILE = 1024

_NN = (((1,), (0,)), ((), ()))
_NT = (((1,), (1,)), ((), ()))
_TN = (((0,), (0,)), ((), ()))

WEIGHTS = ['meta_tokens', 'rel_bias_table', 'norm_mix', 'norm_ffn', 'norm_final', 'attn_w_qkv', 'attn_b_qkv',
           'attn_w_o', 'attn_b_o', 'attn_sinks', 'conv_w_in', 'conv_w', 'conv_w_out', 'pool_w', 'pool_scale',
           'ffn_w_gate', 'ffn_w_up', 'ffn_w_down']
BIG = ['attn_w_qkv', 'attn_w_o', 'conv_w_in', 'conv_w_out', 'pool_w', 'ffn_w_gate', 'ffn_w_up', 'ffn_w_down']
SMALL = [w for w in WEIGHTS if w not in BIG]
HALF_SHAPE = {'attn_w_qkv': (512, 384), 'attn_w_o': (128, 1024), 'conv_w_in': (512, 768),
              'conv_w_out': (128, 1024), 'pool_w': (128, 256), 'ffn_w_gate': (512, 704),
              'ffn_w_up': (512, 704), 'ffn_w_down': (352, 1024)}


def _dot(a, b, dims=_NN):
    return lax.dot_general(a, b, dims, preferred_element_type=F32)


def _tile(n, cap=512):
    best = None
    for t in range(16, min(n, cap) + 1, 16):
        if n % t == 0:
            best = t
    assert best is not None, n
    return best


def _params(n_axes, vmem=VMEM_LIMIT):
    return pltpu.CompilerParams(dimension_semantics=("arbitrary",) * n_axes, vmem_limit_bytes=vmem)


def _rstd(x):
    return lax.rsqrt(jnp.mean(x * x, axis=-1, keepdims=True) + EPS)


def _rms_bwd(dy, x, g, r):
    u = dy * g
    dx = r * u - x * ((r * r * r) * (jnp.sum(x * u, axis=-1, keepdims=True) * (1.0 / D)))
    return dx, jnp.sum(dy * (x * r), axis=0, keepdims=True)


def _sigmoid(x):
    return 1.0 / (1.0 + jnp.exp(-x))


def _acc(ref, val, first):
    @pl.when(first)
    def _():
        ref[...] = val

    @pl.when(jnp.logical_not(first))
    def _():
        ref[...] += val


def _linear(x, w, *, gain=None, bias=None, resid=None, out_dtype=BF16, cap=BIG_TILE, name):
    L, K = x.shape
    J, _, Nc = w.shape
    tm = _tile(L, cap)
    has_g, has_b, has_r = gain is not None, bias is not None, resid is not None

    def body(*refs):
        refs = list(refs)
        x_ref, w_ref = refs[:2]
        i = 2
        g_ref = b_ref = r_ref = None
        if has_g:
            g_ref, i = refs[i], i + 1
        if has_b:
            b_ref, i = refs[i], i + 1
        if has_r:
            r_ref, i = refs[i], i + 1
        out_ref = refs[i]
        if has_g:
            xf = x_ref[...]
            xv = (xf * _rstd(xf) * g_ref[...]).astype(BF16)
            refs[i + 1][...] = xv
        else:
            xv = x_ref[...]
        for s in range(J):
            sl = slice(s * Nc, (s + 1) * Nc)
            acc = _dot(xv, w_ref[s])
            if has_b:
                acc = acc + b_ref[:, sl]
            if has_r:
                acc = acc + r_ref[:, sl]
            out_ref[:, sl] = acc.astype(out_dtype)

    row = lambda n: pl.BlockSpec((tm, n), lambda t: (t, 0))
    one = lambda n: pl.BlockSpec((1, n), lambda t: (0, 0))
    in_specs = [row(K), pl.BlockSpec((J, K, Nc), lambda t: (0, 0, 0))]
    ops = [x, w]
    if has_g:
        in_specs.append(one(K))
        ops.append(gain)
    if has_b:
        in_specs.append(one(J * Nc))
        ops.append(bias)
    if has_r:
        in_specs.append(row(J * Nc))
        ops.append(resid)
    out_specs, out_shape = row(J * Nc), jax.ShapeDtypeStruct((L, J * Nc), out_dtype)
    if has_g:
        out_specs, out_shape = (out_specs, row(K)), (out_shape, jax.ShapeDtypeStruct((L, K), BF16))
    return pl.pallas_call(body, name=name, grid=(L // tm,), in_specs=in_specs, out_specs=out_specs,
                          out_shape=out_shape, compiler_params=_params(1, VMEM_BIG))(*ops)


def _linear_t(dy, w, *, out_dtype=BF16, rms=None, cap=BIG_TILE, name):
    L = dy.shape[0]
    J, K, Nc = w.shape
    tm = _tile(L, cap)
    has_rms = rms is not None

    def body(*refs):
        t = pl.program_id(0)
        dy_ref, w_ref = refs[:2]
        acc = _dot(dy_ref[:, 0:Nc].astype(BF16), w_ref[0], _NT)
        for s in range(1, J):
            acc = acc + _dot(dy_ref[:, s * Nc:(s + 1) * Nc].astype(BF16), w_ref[s], _NT)
        if has_rms:
            h_ref, g_ref, r_ref, out_ref, dg_ref = refs[2:7]
            x = h_ref[...]
            dx, dg = _rms_bwd(acc, x, g_ref[...], _rstd(x))
            out_ref[...] = r_ref[...] + dx
            _acc(dg_ref, dg, t == 0)
        else:
            refs[2][...] = acc.astype(out_dtype)

    row = lambda n: pl.BlockSpec((tm, n), lambda t: (t, 0))
    in_specs = [row(J * Nc), pl.BlockSpec((J, K, Nc), lambda t: (0, 0, 0))]
    ops = [dy, w]
    if has_rms:
        in_specs += [row(K), pl.BlockSpec((1, K), lambda t: (0, 0)), row(K)]
        ops += list(rms)
        out_specs = (row(K), pl.BlockSpec((1, K), lambda t: (0, 0)))
        out_shape = (jax.ShapeDtypeStruct((L, K), F32), jax.ShapeDtypeStruct((1, K), F32))
    else:
        out_specs = row(K)
        out_shape = jax.ShapeDtypeStruct((L, K), out_dtype)
    return pl.pallas_call(body, name=name, grid=(L // tm,), in_specs=in_specs, out_specs=out_specs,
                          out_shape=out_shape, compiler_params=_params(1, VMEM_BIG))(*ops)


def _wgrad(x, dy, *, mode, colsum=False, cap=BIG_TILE, name):
    L, K = x.shape
    N = dy.shape[1]
    tm = _tile(L, cap)
    nt = L // tm
    oshape = (NSH, K, N // NSH) if mode == 'col' else (NSH, K // NSH, N)

    def body(*refs):
        t = pl.program_id(0)
        x_ref, dy_ref, out_ref = refs[:3]
        acc_ref = refs[-1]
        dyv = dy_ref[...]
        _acc(acc_ref, _dot(x_ref[...].astype(BF16), dyv.astype(BF16), _TN), t == 0)
        if colsum:
            _acc(refs[3], jnp.sum(dyv.astype(F32), axis=0, keepdims=True), t == 0)

        @pl.when(t == nt - 1)
        def _():
            for s in range(NSH):
                if mode == 'col':
                    out_ref[s] = acc_ref[:, s * oshape[2]:(s + 1) * oshape[2]].astype(BF16)
                else:
                    out_ref[s] = acc_ref[s * oshape[1]:(s + 1) * oshape[1], :].astype(BF16)

    out_specs = pl.BlockSpec(oshape, lambda t: (0, 0, 0))
    out_shape = jax.ShapeDtypeStruct(oshape, BF16)
    if colsum:
        out_specs = (out_specs, pl.BlockSpec((1, N), lambda t: (0, 0)))
        out_shape = (out_shape, jax.ShapeDtypeStruct((1, N), F32))
    return pl.pallas_call(
        body, name=name, grid=(nt,),
        in_specs=[pl.BlockSpec((tm, K), lambda t: (t, 0)), pl.BlockSpec((tm, N), lambda t: (t, 0))],
        out_specs=out_specs, out_shape=out_shape,
        scratch_shapes=[pltpu.VMEM((K, N), F32)], compiler_params=_params(1, VMEM_BIG))(x, dy)


def _ffn_fwd(h, gain, wg, wu, wd, name, carry=None):
    L = h.shape[0]
    tm = _tile(L, BIG_TILE)

    def body(h_ref, g_ref, wg_ref, wu_ref, wd_ref, hn_ref, G_ref, U_ref, A_ref, B_ref, acc_ref, b_scr):
        j = pl.program_id(1)

        @pl.when(j == 0)
        def _():
            x = h_ref[...]
            b = (x * _rstd(x) * g_ref[...]).astype(BF16)
            b_scr[...] = b
            B_ref[...] = b
            acc_ref[...] = x

        b = b_scr[...]
        g = _dot(b, wg_ref[...])
        u = _dot(b, wu_ref[...])
        s = _sigmoid(g)
        silu = g * s
        G_ref[...] = (u * (s * (1.0 + g * (1.0 - s)))).astype(BF16)
        U_ref[...] = silu.astype(BF16)
        a = (silu * u).astype(BF16)
        A_ref[...] = a
        acc_ref[...] += _dot(a, wd_ref[...])

        @pl.when(j == NSH - 1)
        def _():
            hn_ref[...] = acc_ref[...]

    row = pl.BlockSpec((tm, D), lambda t, j: (t, 0))
    chunk = pl.BlockSpec((None, tm, FC), lambda t, j: (j, t, 0))
    cshape = jax.ShapeDtypeStruct((NSH, L, FC), BF16)
    return _carried_call(
        body, carry, name=name, grid=(L // tm, NSH),
        in_specs=[row, pl.BlockSpec((1, D), lambda t, j: (0, 0)),
                  pl.BlockSpec((None, D, FC), lambda t, j: (j, 0, 0)),
                  pl.BlockSpec((None, D, FC), lambda t, j: (j, 0, 0)),
                  pl.BlockSpec((None, FC, D), lambda t, j: (j, 0, 0))],
        out_specs=[row, chunk, chunk, chunk, row],
        out_shape=[jax.ShapeDtypeStruct((L, D), F32), cshape, cshape, cshape, jax.ShapeDtypeStruct((L, D), BF16)],
        scratch_shapes=[pltpu.VMEM((tm, D), F32), pltpu.VMEM((tm, D), BF16)],
        compiler_params=_params(2, VMEM_BIG), operands=[h, gain, wg, wu, wd])


def _ffn_bwd_act(dhn, h, gain, G, U, wg, wu, wd, name, carry=None):
    L = h.shape[0]
    tm = _tile(L, BIG_TILE)

    def body(dhn_ref, h_ref, g_ref, G_ref, U_ref, wg_ref, wu_ref, wd_ref, DG_ref, DU_ref, dh_ref, dgain_ref, db_ref):
        t, j = pl.program_id(0), pl.program_id(1)
        d_act = _dot(dhn_ref[...].astype(BF16), wd_ref[...], _NT)
        dg = (d_act * G_ref[...].astype(F32)).astype(BF16)
        du = (d_act * U_ref[...].astype(F32)).astype(BF16)
        DG_ref[...] = dg
        DU_ref[...] = du
        _acc(db_ref, _dot(dg, wg_ref[...], _NT) + _dot(du, wu_ref[...], _NT), j == 0)

        @pl.when(j == NSH - 1)
        def _():
            x = h_ref[...]
            dx, dgn = _rms_bwd(db_ref[...], x, g_ref[...], _rstd(x))
            dh_ref[...] = dhn_ref[...] + dx
            _acc(dgain_ref, dgn, t == 0)

    row = pl.BlockSpec((tm, D), lambda t, j: (t, 0))
    one = pl.BlockSpec((1, D), lambda t, j: (0, 0))
    chunk = pl.BlockSpec((None, tm, FC), lambda t, j: (j, t, 0))
    cshape = jax.ShapeDtypeStruct((NSH, L, FC), BF16)
    return _carried_call(
        body, carry, name=name, grid=(L // tm, NSH),
        in_specs=[row, row, one, chunk, chunk,
                  pl.BlockSpec((None, D, FC), lambda t, j: (j, 0, 0)),
                  pl.BlockSpec((None, D, FC), lambda t, j: (j, 0, 0)),
                  pl.BlockSpec((None, FC, D), lambda t, j: (j, 0, 0))],
        out_specs=[chunk, chunk, row, one],
        out_shape=[cshape, cshape, jax.ShapeDtypeStruct((L, D), F32), jax.ShapeDtypeStruct((1, D), F32)],
        scratch_shapes=[pltpu.VMEM((tm, D), F32)],
        compiler_params=_params(2, VMEM_BIG), operands=[dhn, h, gain, G, U, wg, wu, wd])


def _ffn_bwd_w(B, A, DG, DU, dhn, name, carry=None):
    L = B.shape[0]
    tm = _tile(L, BIG_TILE)
    nt = L // tm

    def body(B_ref, A_ref, DG_ref, DU_ref, dhn_ref, dwg_ref, dwu_ref, dwd_ref, ag, au, ad):
        t = pl.program_id(1)
        b = B_ref[...]
        _acc(ag, _dot(b, DG_ref[...], _TN), t == 0)
        _acc(au, _dot(b, DU_ref[...], _TN), t == 0)
        _acc(ad, _dot(A_ref[...], dhn_ref[...].astype(BF16), _TN), t == 0)

        @pl.when(t == nt - 1)
        def _():
            dwg_ref[...] = ag[...].astype(BF16)
            dwu_ref[...] = au[...].astype(BF16)
            dwd_ref[...] = ad[...].astype(BF16)

    row = pl.BlockSpec((tm, D), lambda j, t: (t, 0))
    chunk = pl.BlockSpec((None, tm, FC), lambda j, t: (j, t, 0))
    return _carried_call(
        body, carry, name=name, grid=(NSH, nt), in_specs=[row, chunk, chunk, chunk, row],
        out_specs=[pl.BlockSpec((None, D, FC), lambda j, t: (j, 0, 0)),
                   pl.BlockSpec((None, D, FC), lambda j, t: (j, 0, 0)),
                   pl.BlockSpec((None, FC, D), lambda j, t: (j, 0, 0))],
        out_shape=[jax.ShapeDtypeStruct((NSH, D, FC), BF16), jax.ShapeDtypeStruct((NSH, D, FC), BF16),
                   jax.ShapeDtypeStruct((NSH, FC, D), BF16)],
        scratch_shapes=[pltpu.VMEM((D, FC), F32), pltpu.VMEM((D, FC), F32), pltpu.VMEM((FC, D), F32)],
        compiler_params=_params(2, VMEM_BIG), operands=[B, A, DG, DU, dhn])


def _loss_head(h, gain, target):
    L = h.shape[0]
    tm = _tile(L)

    def body(h_ref, g_ref, tgt_ref, dh_ref, dg_ref, loss_ref):
        t = pl.program_id(0)
        x = h_ref[...]
        g = g_ref[...]
        r = _rstd(x)
        rows = t * tm + lax.broadcasted_iota(jnp.int32, (tm, 1), 0)
        diff = jnp.where(rows >= N_META, x * r * g - tgt_ref[...], 0.0)
        part = 0.5 * jnp.sum(jnp.sum(diff * diff, axis=-1, keepdims=True) * (1.0 / D), axis=0, keepdims=True)
        dx, dg = _rms_bwd(diff * (1.0 / D), x, g, r)
        dh_ref[...] = dx
        _acc(dg_ref, dg, t == 0)
        _acc(loss_ref, jnp.broadcast_to(part, (1, 128)), t == 0)

    row = pl.BlockSpec((tm, D), lambda t: (t, 0))
    one = pl.BlockSpec((1, D), lambda t: (0, 0))
    return pl.pallas_call(
        body, name="loss_head", grid=(L // tm,), in_specs=[row, one, row],
        out_specs=(row, one, pl.BlockSpec((1, 128), lambda t: (0, 0))),
        out_shape=(jax.ShapeDtypeStruct((L, D), F32), jax.ShapeDtypeStruct((1, D), F32),
                   jax.ShapeDtypeStruct((1, 128), F32)),
        compiler_params=_params(1))(h, gain, target)


def _conv_fwd(z3, cw, name):
    L = z3.shape[0]
    tm = _tile(L)

    def body(b_ref, c_ref, u_ref, ch_ref, uh_ref, w_ref, y_ref, buf):
        t = pl.program_id(0)
        z = c_ref[...].astype(F32) * u_ref[...].astype(F32)
        buf[pl.ds(0, HALO), :] = jnp.where(t > 0, ch_ref[...].astype(F32) * uh_ref[...].astype(F32), 0.0)
        buf[pl.ds(HALO, tm), :] = z
        w0, w1, w2 = w_ref[0:1, :], w_ref[1:2, :], w_ref[2:3, :]
        conv = w2 * z + w1 * buf[pl.ds(HALO - 1, tm), :] + w0 * buf[pl.ds(HALO - 2, tm), :]
        y_ref[...] = (b_ref[...].astype(F32) * conv).astype(BF16)

    def col(k):
        return pl.BlockSpec((tm, D), lambda t: (t, k))

    def hcol(k):
        return pl.BlockSpec((HALO, D), lambda t: (jnp.maximum(t * (tm // HALO) - 1, 0), k))

    return pl.pallas_call(
        body, name=name, grid=(L // tm,),
        in_specs=[col(0), col(1), col(2), hcol(1), hcol(2), pl.BlockSpec((3, D), lambda t: (0, 0))],
        out_specs=pl.BlockSpec((tm, D), lambda t: (t, 0)),
        out_shape=jax.ShapeDtypeStruct((L, D), BF16),
        scratch_shapes=[pltpu.VMEM((HALO + tm, D), F32)], compiler_params=_params(1))(z3, z3, z3, z3, z3, cw)


def _conv_bwd(z3, cw, dy, name):
    L = z3.shape[0]
    tm = _tile(L)
    nt = L // tm
    last_h = L // HALO - 1

    def body(b_ref, c_ref, u_ref, ch_ref, uh_ref, w_ref, dy_ref, bn_ref, dyn_ref, dz3_ref, dw_ref, zbuf, dbuf):
        t = pl.program_id(0)
        w0, w1, w2 = w_ref[0:1, :], w_ref[1:2, :], w_ref[2:3, :]
        bgate, cgate, u = b_ref[...].astype(F32), c_ref[...].astype(F32), u_ref[...].astype(F32)
        z = cgate * u
        zbuf[pl.ds(0, HALO), :] = jnp.where(t > 0, ch_ref[...].astype(F32) * uh_ref[...].astype(F32), 0.0)
        zbuf[pl.ds(HALO, tm), :] = z
        z1 = zbuf[pl.ds(HALO - 1, tm), :]
        z2 = zbuf[pl.ds(HALO - 2, tm), :]
        conv = w2 * z + w1 * z1 + w0 * z2
        dyv = dy_ref[...]
        dconv = dyv * bgate
        dbuf[pl.ds(0, tm), :] = dconv
        dbuf[pl.ds(tm, HALO), :] = jnp.where(t < nt - 1, dyn_ref[...] * bn_ref[...].astype(F32), 0.0)
        dz = w2 * dconv + w1 * dbuf[pl.ds(1, tm), :] + w0 * dbuf[pl.ds(2, tm), :]
        dz3_ref[:, 0:D] = (dyv * conv).astype(BF16)
        dz3_ref[:, D:2 * D] = (dz * u).astype(BF16)
        dz3_ref[:, 2 * D:3 * D] = (dz * cgate).astype(BF16)
        for k, zk in enumerate((z2, z1, z)):
            _acc(dw_ref.at[k:k + 1], jnp.sum(dconv * zk, axis=0, keepdims=True), t == 0)

    def col(k):
        return pl.BlockSpec((tm, D), lambda t: (t, k))

    def hprev(k):
        return pl.BlockSpec((HALO, D), lambda t: (jnp.maximum(t * (tm // HALO) - 1, 0), k))

    def hnext(k):
        return pl.BlockSpec((HALO, D), lambda t: (jnp.minimum((t + 1) * (tm // HALO), last_h), k))

    return pl.pallas_call(
        body, name=name, grid=(nt,),
        in_specs=[col(0), col(1), col(2), hprev(1), hprev(2), pl.BlockSpec((3, D), lambda t: (0, 0)),
                  col(0), hnext(0), hnext(0)],
        out_specs=(pl.BlockSpec((tm, 3 * D), lambda t: (t, 0)), pl.BlockSpec((3, D), lambda t: (0, 0))),
        out_shape=(jax.ShapeDtypeStruct((L, 3 * D), BF16), jax.ShapeDtypeStruct((3, D), F32)),
        scratch_shapes=[pltpu.VMEM((HALO + tm, D), F32), pltpu.VMEM((tm + HALO, D), F32)],
        compiler_params=_params(1))(z3, z3, z3, z3, z3, cw, dy, z3, dy)


def _count_inv(pos, w):
    return 1.0 / jnp.minimum(pos + 1, w).astype(F32)


def _pool_fwd(h, gain, name):
    L = h.shape[0]
    tm = _tile(L)

    def body(h_ref, hh_ref, g_ref, mix_ref, buf):
        t = pl.program_id(0)
        g = g_ref[...]
        x = h_ref[...]
        a = x * _rstd(x) * g
        xh = hh_ref[...]
        buf[pl.ds(0, HALO), :] = jnp.where(t > 0, xh * _rstd(xh) * g, 0.0)
        buf[pl.ds(HALO, tm), :] = a
        pos = t * tm + lax.broadcasted_iota(jnp.int32, (tm, 1), 0)
        for gi, w in enumerate(POOL_WINDOWS):
            cols = pl.ds(gi * PG, PG)
            s = buf[pl.ds(HALO, tm), cols]
            for k in range(1, w):
                s = s + buf[pl.ds(HALO - k, tm), cols]
            mix_ref[:, gi * PG:(gi + 1) * PG] = (s / jnp.minimum(pos + 1, w).astype(F32)
                                                  - buf[pl.ds(HALO, tm), cols]).astype(BF16)

    return pl.pallas_call(
        body, name=name, grid=(L // tm,),
        in_specs=[pl.BlockSpec((tm, D), lambda t: (t, 0)),
                  pl.BlockSpec((HALO, D), lambda t: (jnp.maximum(t * (tm // HALO) - 1, 0), 0)),
                  pl.BlockSpec((1, D), lambda t: (0, 0))],
        out_specs=pl.BlockSpec((tm, D), lambda t: (t, 0)),
        out_shape=jax.ShapeDtypeStruct((L, D), BF16),
        scratch_shapes=[pltpu.VMEM((HALO + tm, D), F32)], compiler_params=_params(1))(h, h, gain)


def _pool_out(mix, wp, scale, h, name):
    L = mix.shape[0]
    tm = _tile(L)

    def body(mix_ref, wp_ref, s_ref, h_ref, out_ref):
        for gi in range(4):
            sl = slice(gi * PG, (gi + 1) * PG)
            pre = _dot(mix_ref[:, sl], wp_ref[gi])
            out_ref[:, sl] = h_ref[:, sl] + pre * s_ref[:, sl]

    row = pl.BlockSpec((tm, D), lambda t: (t, 0))
    return pl.pallas_call(
        body, name=name, grid=(L // tm,),
        in_specs=[row, pl.BlockSpec((4, PG, PG), lambda t: (0, 0, 0)), pl.BlockSpec((1, D), lambda t: (0, 0)), row],
        out_specs=row, out_shape=jax.ShapeDtypeStruct((L, D), F32), compiler_params=_params(1))(mix, wp, scale, h)


def _pool_out_bwd(dm, mix, wp, scale, name):
    L = mix.shape[0]
    tm = _tile(L)

    def body(dm_ref, mix_ref, wp_ref, s_ref, dmix_ref, dwp_ref, ds_ref):
        t = pl.program_id(0)
        for gi in range(4):
            sl = slice(gi * PG, (gi + 1) * PG)
            mx = mix_ref[:, sl]
            dmv = dm_ref[:, sl]
            pre = _dot(mx, wp_ref[gi])
            _acc(ds_ref.at[:, sl], jnp.sum(dmv * pre, axis=0, keepdims=True), t == 0)
            dpre = (dmv * s_ref[:, sl]).astype(BF16)
            _acc(dwp_ref.at[gi], _dot(mx, dpre, _TN), t == 0)
            dmix_ref[:, sl] = _dot(dpre, wp_ref[gi], _NT)

    row = pl.BlockSpec((tm, D), lambda t: (t, 0))
    one = pl.BlockSpec((1, D), lambda t: (0, 0))
    wsp = pl.BlockSpec((4, PG, PG), lambda t: (0, 0, 0))
    return pl.pallas_call(
        body, name=name, grid=(L // tm,), in_specs=[row, row, wsp, one],
        out_specs=(row, wsp, one),
        out_shape=(jax.ShapeDtypeStruct((L, D), F32), jax.ShapeDtypeStruct((4, PG, PG), F32),
                   jax.ShapeDtypeStruct((1, D), F32)),
        compiler_params=_params(1))(dm, mix, wp, scale)


def _pool_bwd(dmix, h, gain, resid, name):
    L = h.shape[0]
    tm = _tile(L)
    nt = L // tm
    last_h = L // HALO - 1

    def body(dm_ref, dmn_ref, h_ref, g_ref, r_ref, dh_ref, dg_ref, buf):
        t = pl.program_id(0)
        pos = t * tm + lax.broadcasted_iota(jnp.int32, (tm, 1), 0)
        posn = (t + 1) * tm + lax.broadcasted_iota(jnp.int32, (HALO, 1), 0)
        dmv = dm_ref[...]
        dmn = dmn_ref[...]
        for gi, w in enumerate(POOL_WINDOWS):
            sl = slice(gi * PG, (gi + 1) * PG)
            buf[pl.ds(0, tm), sl] = dmv[:, sl] * _count_inv(pos, w)
            buf[pl.ds(tm, HALO), sl] = jnp.where(t < nt - 1, dmn[:, sl] * _count_inv(posn, w), 0.0)
        parts = []
        for gi, w in enumerate(POOL_WINDOWS):
            cols = pl.ds(gi * PG, PG)
            s = buf[pl.ds(0, tm), cols]
            for k in range(1, w):
                s = s + buf[pl.ds(k, tm), cols]
            parts.append(s)
        da = jnp.concatenate(parts, axis=1) - dmv
        x = h_ref[...]
        dx, dg = _rms_bwd(da, x, g_ref[...], _rstd(x))
        dh_ref[...] = r_ref[...] + dx
        _acc(dg_ref, dg, t == 0)

    row = pl.BlockSpec((tm, D), lambda t: (t, 0))
    one = pl.BlockSpec((1, D), lambda t: (0, 0))
    return pl.pallas_call(
        body, name=name, grid=(nt,),
        in_specs=[row, pl.BlockSpec((HALO, D), lambda t: (jnp.minimum((t + 1) * (tm // HALO), last_h), 0)),
                  row, one, row],
        out_specs=(row, one),
        out_shape=(jax.ShapeDtypeStruct((L, D), F32), jax.ShapeDtypeStruct((1, D), F32)),
        scratch_shapes=[pltpu.VMEM((tm + HALO, D), F32)], compiler_params=_params(1))(dmix, dmix, h, gain, resid)


def _bucket_np(d):
    d = np.maximum(d, 0)
    df = np.maximum(d, 1).astype(np.float32)
    large = 16 + (np.log(df / np.float32(16)) / np.float32(math.log(128 / 16)) * np.float32(16)).astype(np.int32)
    return np.where(d < 16, d, np.minimum(large, N_BUCKETS - 1))


def _bias_index():
    iq = np.arange(BLK)[:, None]
    jk = np.arange(2 * BLK)[None, :]
    dist = BLK + iq - jk
    band = _bucket_np(dist)
    ok = (dist >= 0) & (dist < BLK)
    band1 = np.where(ok, band, N_BUCKETS)
    band0 = np.where(ok & (jk >= BLK), band, N_BUCKETS)
    im = np.arange(N_META)[None, :]
    unused = np.full((BLK, BLK - N_META), N_BUCKETS)
    var0 = np.concatenate([_bucket_np(N_META + iq - im), unused, band0], axis=1)
    var1 = np.concatenate([_bucket_np(N_META + BLK + iq - im), unused, band1], axis=1)
    dm = np.arange(N_META)[:, None] - im
    mm = np.where(dm >= 0, _bucket_np(dm), N_BUCKETS)
    segs = [var0, var1, mm]
    return np.concatenate([s.reshape(-1) for s in segs]).astype(np.int32), [s.shape for s in segs]


P_CHUNK = 9856


def _onehot(idx_ref, grad):
    rows = lax.broadcasted_iota(jnp.int32, (128, P_CHUNK), 0)
    hit = (rows == idx_ref[...]).astype(F32)
    return jnp.where(rows == N_BUCKETS, -1.0, hit) if grad else hit


def _bias_lookup(table_aug, idx, name):
    P = idx.shape[1]

    def body(t_ref, idx_ref, o_ref):
        o_ref[...] = lax.dot_general(t_ref[...], _onehot(idx_ref, False), _NN, precision=lax.Precision.HIGHEST,
                                     preferred_element_type=F32)

    return pl.pallas_call(
        body, name=name, grid=(P // P_CHUNK,),
        in_specs=[pl.BlockSpec((H, 128), lambda i: (0, 0)), pl.BlockSpec((1, P_CHUNK), lambda i: (0, i))],
        out_specs=pl.BlockSpec((H, P_CHUNK), lambda i: (0, i)),
        out_shape=jax.ShapeDtypeStruct((H, P), F32), compiler_params=_params(1))(table_aug, idx)


def _bias_lookup_bwd(dbias, idx, name):
    P = idx.shape[1]

    def body(d_ref, idx_ref, o_ref):
        part = lax.dot_general(d_ref[...], _onehot(idx_ref, True), _NT, precision=lax.Precision.HIGHEST,
                               preferred_element_type=F32)
        _acc(o_ref, part, pl.program_id(0) == 0)

    return pl.pallas_call(
        body, name=name, grid=(P // P_CHUNK,),
        in_specs=[pl.BlockSpec((H, P_CHUNK), lambda i: (0, i)), pl.BlockSpec((1, P_CHUNK), lambda i: (0, i))],
        out_specs=pl.BlockSpec((H, 128), lambda i: (0, 0)),
        out_shape=jax.ShapeDtypeStruct((H, 128), F32), compiler_params=_params(1))(dbias, idx)


def _probs(q, kbt, bias, sink):
    s = _dot(q, kbt) + bias
    m = jnp.maximum(jnp.max(s, axis=-1, keepdims=True), sink)
    e = jnp.exp(s - m)
    return e * (1.0 / (jnp.sum(e, axis=-1, keepdims=True) + jnp.exp(sink - m)))


def _attn_specs(nb):
    def cur(kh, n):
        return jnp.minimum(n, nb - 1)

    def prev(kh, n):
        return jnp.maximum(jnp.minimum(n, nb - 1) - 1, 0)

    def rows(heads, blk):
        return pl.BlockSpec((pl.Element(heads), pl.Element(BLK), pl.Element(HD)),
                            lambda kh, n: (kh * heads, pl.multiple_of(N_META + blk(kh, n) * BLK, N_META), 0))

    return dict(
        q=rows(GRP, cur),
        qo=pl.BlockSpec((GRP, BLK, HD), lambda kh, n: (kh, cur(kh, n), 0)),
        qt=pl.BlockSpec((GRP, HD, BLK), lambda kh, n: (kh, 0, cur(kh, n))),
        cur=rows(1, cur),
        prev=rows(1, prev),
        meta=pl.BlockSpec((None, BLK, HD), lambda kh, n: (kh, 0, 0)),
        curt=pl.BlockSpec((None, HD, BLK), lambda kh, n: (kh, 0, cur(kh, n))),
        prevt=pl.BlockSpec((None, HD, BLK), lambda kh, n: (kh, 0, prev(kh, n))),
        metat=pl.BlockSpec((None, HD, BLK), lambda kh, n: (kh, 0, 0)),
        bias=pl.BlockSpec((None, GRP, BLK, NKEY), lambda kh, n: (jnp.minimum(n, 1), kh, 0, 0)),
        sink=pl.BlockSpec((None, GRP * BLK, 1), lambda kh, n: (kh, 0, 0)))


def _attn_fwd(q, k, km, v, vm, band, bmeta, sink, name, carry=None):
    S = q.shape[1] - N_META
    nb = S // BLK
    sp = _attn_specs(nb)
    var = lambda kh, n: (jnp.minimum(n, 1), kh, 0, 0)
    meta = pl.BlockSpec((None, N_META, HD), lambda kh, n: (kh, 0, 0))

    def body(q_ref, kc_ref, kp_ref, vc_ref, vp_ref, km_ref, vm_ref, band_ref, bm_ref, sink_ref, o_ref):
        qv = q_ref[...].reshape(GRP * BLK, HD)
        kb = jnp.concatenate([kp_ref[0], kc_ref[0]], axis=0)
        vb = jnp.concatenate([vp_ref[0], vc_ref[0]], axis=0)
        sink = sink_ref[...]
        s_b = _dot(qv, kb, _NT) + band_ref[...].reshape(GRP * BLK, 2 * BLK)
        s_m = _dot(qv, km_ref[...], _NT) + bm_ref[...].reshape(GRP * BLK, N_META)
        m = jnp.maximum(jnp.maximum(jnp.max(s_b, axis=-1, keepdims=True), jnp.max(s_m, axis=-1, keepdims=True)), sink)
        e_b = jnp.exp(s_b - m)
        e_m = jnp.exp(s_m - m)
        inv = 1.0 / (jnp.sum(e_b, axis=-1, keepdims=True) + jnp.sum(e_m, axis=-1, keepdims=True) + jnp.exp(sink - m))
        o = _dot((e_b * inv).astype(BF16), vb) + _dot((e_m * inv).astype(BF16), vm_ref[...])
        o_ref[...] = o.reshape(GRP, BLK, HD).astype(BF16)

    return _carried_call(
        body, carry, name=name, grid=(KV, nb),
        in_specs=[sp['q'], sp['cur'], sp['prev'], sp['cur'], sp['prev'], meta, meta,
                  pl.BlockSpec((None, GRP, BLK, 2 * BLK), var), pl.BlockSpec((None, GRP, BLK, N_META), var),
                  sp['sink']],
        out_specs=[sp['qo']], out_shape=[jax.ShapeDtypeStruct((H, S, HD), BF16)], scratch_shapes=[],
        compiler_params=_params(2), operands=[q, k, k, v, v, km, vm, band, bmeta, sink])


def _attn_bwd(q, qt, k, km, kt, kmt, vt, vmt, bias, sink, do, dot_, name, carry=None):
    S = q.shape[1] - N_META
    nb = S // BLK
    sp = _attn_specs(nb)

    def body(q_ref, qt_ref, kc_ref, kp_ref, km_ref, kct_ref, kpt_ref, kmt_ref, vct_ref, vpt_ref, vmt_ref,
             bias_ref, sink_ref, do_ref, dot_ref, dq_ref, dkt_ref, dvt_ref, dkmt_ref, dvmt_ref, dbias_ref, ck, cv):
        n = pl.program_id(1)

        @pl.when(n < nb)
        def _():
            kbt = jnp.concatenate([kmt_ref[...], kpt_ref[...], kct_ref[...]], axis=1)
            vbt = jnp.concatenate([vmt_ref[...], vpt_ref[...], vct_ref[...]], axis=1)
            kb = jnp.concatenate([km_ref[...], kp_ref[0], kc_ref[0]], axis=0)
            p = _probs(q_ref[...].reshape(GRP * BLK, HD), kbt, bias_ref[...].reshape(GRP * BLK, NKEY), sink_ref[...])
            dp = _dot(do_ref[...].reshape(GRP * BLK, HD), vbt)
            ds = p * (dp - jnp.sum(p * dp, axis=-1, keepdims=True))
            _acc(dbias_ref, ds.reshape(GRP, BLK, NKEY), n <= 1)
            ds16 = ds.astype(BF16)
            dq_ref[...] = _dot(ds16, kb).reshape(GRP, BLK, HD).astype(BF16)
            qtv = jnp.concatenate([qt_ref[g] for g in range(GRP)], axis=1)
            dotv = jnp.concatenate([dot_ref[g] for g in range(GRP)], axis=1)
            dkt = _dot(qtv, ds16)
            dvt = _dot(dotv, p.astype(BF16))
            _acc(dkmt_ref, dkt[:, 0:BLK], n == 0)
            _acc(dvmt_ref, dvt[:, 0:BLK], n == 0)

            @pl.when(n >= 1)
            def _():
                dkt_ref[...] = (ck[...] + dkt[:, BLK:2 * BLK]).astype(BF16)
                dvt_ref[...] = (cv[...] + dvt[:, BLK:2 * BLK]).astype(BF16)

            ck[...] = dkt[:, 2 * BLK:3 * BLK]
            cv[...] = dvt[:, 2 * BLK:3 * BLK]

        @pl.when(n == nb)
        def _():
            dkt_ref[...] = ck[...].astype(BF16)
            dvt_ref[...] = cv[...].astype(BF16)

    kvout = pl.BlockSpec((None, HD, BLK), lambda kh, n: (kh, 0, jnp.maximum(n - 1, 0)))
    return _carried_call(
        body, carry, name=name, grid=(KV, nb + 1),
        in_specs=[sp['q'], sp['qt'], sp['cur'], sp['prev'], sp['meta'], sp['curt'], sp['prevt'], sp['metat'],
                  sp['curt'], sp['prevt'], sp['metat'], sp['bias'], sp['sink'], sp['q'], sp['qt']],
        out_specs=[sp['qo'], kvout, kvout, sp['metat'], sp['metat'], sp['bias']],
        out_shape=[jax.ShapeDtypeStruct((H, S, HD), BF16), jax.ShapeDtypeStruct((KV, HD, S), BF16),
                   jax.ShapeDtypeStruct((KV, HD, S), BF16), jax.ShapeDtypeStruct((KV, HD, BLK), F32),
                   jax.ShapeDtypeStruct((KV, HD, BLK), F32), jax.ShapeDtypeStruct((2, H, BLK, NKEY), F32)],
        scratch_shapes=[pltpu.VMEM((HD, BLK), F32), pltpu.VMEM((HD, BLK), F32)],
        compiler_params=_params(2), operands=[q, qt, k, k, km, kt, kt, kmt, vt, vt, vmt, bias, sink, do, dot_])


def _meta_softmax(q, k, bias, sink):
    s = _dot(q, k, _NT) + bias
    m = jnp.maximum(jnp.max(s, axis=-1, keepdims=True), sink)
    e = jnp.exp(s - m)
    return e * (1.0 / (jnp.sum(e, axis=-1, keepdims=True) + jnp.exp(sink - m)))


def _attn_meta_fwd(qm, km, vm, bias, sink, name):
    def body(q_ref, k_ref, v_ref, b_ref, s_ref, o_ref):
        for h in range(H):
            p = _meta_softmax(q_ref[h], k_ref[h // GRP], b_ref[h], s_ref[h])
            o_ref[h] = _dot(p.astype(BF16), v_ref[h // GRP]).astype(BF16)

    return pl.pallas_call(body, name=name, out_shape=jax.ShapeDtypeStruct((H, N_META, HD), BF16))(
        qm, km, vm, bias, sink)


def _attn_meta_bwd(qm, km, vm, bias, sink, do, name):
    def body(q_ref, k_ref, v_ref, b_ref, s_ref, do_ref, dq_ref, dk_ref, dv_ref, db_ref):
        for kh in range(KV):
            k, v = k_ref[kh], v_ref[kh]
            dk = jnp.zeros((N_META, HD), F32)
            dv = jnp.zeros((N_META, HD), F32)
            for g in range(GRP):
                h = kh * GRP + g
                q, dov = q_ref[h], do_ref[h]
                p = _meta_softmax(q, k, b_ref[h], s_ref[h])
                dp = _dot(dov, v, _NT)
                ds = p * (dp - jnp.sum(p * dp, axis=-1, keepdims=True))
                db_ref[h] = ds
                ds16 = ds.astype(BF16)
                dq_ref[h] = _dot(ds16, k).astype(BF16)
                dk = dk + _dot(ds16, q, _TN)
                dv = dv + _dot(p.astype(BF16), dov, _TN)
            dk_ref[kh] = dk
            dv_ref[kh] = dv

    return pl.pallas_call(
        body, name=name,
        out_shape=(jax.ShapeDtypeStruct((H, N_META, HD), BF16), jax.ShapeDtypeStruct((KV, N_META, HD), F32),
                   jax.ShapeDtypeStruct((KV, N_META, HD), F32), jax.ShapeDtypeStruct((H, N_META, N_META), F32)))(
        qm, km, vm, bias, sink, do)


def _heads(t, n):
    return jnp.transpose(t.reshape(t.shape[0], n, HD), (1, 0, 2))


def _heads_t(t, n):
    return jnp.transpose(t.reshape(t.shape[0], n, HD), (1, 2, 0))


def _unheads(t):
    return jnp.transpose(t, (1, 0, 2)).reshape(t.shape[1], t.shape[0] * HD)


def _unheads_t(t):
    return jnp.transpose(t, (2, 0, 1)).reshape(t.shape[2], t.shape[0] * HD)


def _pad_block(t, axis):
    pad = [(0, 0)] * t.ndim
    pad[axis] = (0, BLK - N_META)
    return jnp.pad(t, pad)


def _adam(w, g, m, v, name):
    R, C = w.shape
    tr = _tile(R, 256) if R % 16 == 0 else R

    def body(w_ref, g_ref, m_ref, v_ref, d_ref, mo_ref, vo_ref, go_ref):
        gv = g_ref[...]
        go_ref[...] = gv
        mn = B1 * m_ref[...] + (1.0 - B1) * gv
        vn = B2 * v_ref[...] + (1.0 - B2) * (gv * gv)
        m_hat = mn / (1.0 - B1 ** STEP)
        v_hat = vn / (1.0 - B2 ** STEP)
        d_ref[...] = -LR * (m_hat / (jnp.sqrt(v_hat) + ADAM_EPS) + WD * w_ref[...])
        mo_ref[...] = mn
        vo_ref[...] = vn

    blk = pl.BlockSpec((tr, C), lambda i: (i, 0))
    shp = jax.ShapeDtypeStruct((R, C), F32)
    return pl.pallas_call(body, name=name, grid=(R // tr,), in_specs=[blk] * 4, out_specs=(blk,) * 4,
                          out_shape=(shp,) * 4, compiler_params=_params(1))(w, g, m, v)


def _sum_leaf(lands, pends, ids, name):
    n = len(lands)
    _, R, C = lands[0].shape
    tr = _tile(R, 128)

    def body(ids_ref, *refs):
        out_ref = refs[2 * n]
        for k in range(n):
            acc = refs[n + k][...].astype(F32)
            for s in range(7):
                acc = acc + refs[k][s].astype(F32)
            out_ref[k] = acc

    grid_spec = pltpu.PrefetchScalarGridSpec(
        num_scalar_prefetch=1, grid=(R // tr,),
        in_specs=[pl.BlockSpec((7, tr, C), lambda i, ids: (0, i, 0))] * n
        + [pl.BlockSpec((None, None, tr, C), lambda i, ids: (ids[0], ids[1], i, 0))] * n,
        out_specs=pl.BlockSpec((n, None, tr, C), lambda i, ids: (0, ids[1], i, 0)))
    return pl.pallas_call(body, name=name, grid_spec=grid_spec, out_shape=jax.ShapeDtypeStruct((n, 2, R, C), F32),
                          compiler_params=_params(1))(ids, *lands, *pends)


def _place():
    x, y, c = lax.axis_index("x"), lax.axis_index("y"), lax.axis_index("c")
    chips = [(1 - x, y), (x, 1 - y), (1 - x, 1 - y)]
    return x, y, c, chips


def _rcopy(src, dst, ssem, rsem, dev):
    return pltpu.make_async_remote_copy(src_ref=src, dst_ref=dst, send_sem=ssem, recv_sem=rsem,
                                        device_id=dev, device_id_type=MESH)


class _Carry:
    def __init__(self, kind, arrays):
        self.kind, self.arrays, self.n = kind, list(arrays), len(arrays)
        self.per = 3 if kind == 'gather' else 7
        if kind == 'gather':
            self.out_shape = [jax.ShapeDtypeStruct((NSH,) + a.shape, a.dtype) for a in self.arrays]
        else:
            self.out_shape = [jax.ShapeDtypeStruct((7,) + a.shape[2:], a.dtype) for a in self.arrays]
        dma = pltpu.SemaphoreType.DMA
        self.scratch = [dma((self.per * self.n,)), dma((self.per * self.n,)), dma((self.n,))]
        if kind == 'gather':
            self.scratch += [pltpu.VMEM(a.shape, a.dtype) for a in self.arrays]

    def _copies(self, cin, cout, scr):
        ssem, rsem, loc = scr[:3]
        x, y, c, chips = _place()
        local, sends, recvs = [], [], []
        for k in range(self.n):
            if self.kind == 'gather':
                me = 2 * x + y
                local.append((pltpu.make_async_copy(cin[k], scr[3 + k], loc.at[k]),
                              pltpu.make_async_copy(scr[3 + k], cout[k].at[me], loc.at[k])))
                for j, (cx, cy) in enumerate(chips):
                    i = 3 * k + j
                    sends.append(_rcopy(cin[k], cout[k].at[me], ssem.at[i], rsem.at[i], (cx, cy, c)))
                    got = cout[k].at[2 * cx + cy]
                    recvs.append(_rcopy(got, got, ssem.at[i], rsem.at[i], (cx, cy, c)))
            else:
                for f in range(1, 8):
                    px = 1 - x if (f >> 2) & 1 else x
                    py = 1 - y if (f >> 1) & 1 else y
                    pc = 1 - c if f & 1 else c
                    i = 7 * k + f - 1
                    got = cout[k].at[f - 1]
                    sends.append(_rcopy(cin[k].at[2 * px + py, pc], got, ssem.at[i], rsem.at[i], (px, py, pc)))
                    recvs.append(_rcopy(got, got, ssem.at[i], rsem.at[i], (px, py, pc)))
        return local, sends, recvs

    def start(self, cin, cout, scr):
        local, sends, _ = self._copies(cin, cout, scr)
        for cp in [to_vmem for to_vmem, _ in local] + sends:
            cp.start()

    def finish(self, cin, cout, scr):
        local, sends, recvs = self._copies(cin, cout, scr)
        for to_vmem, to_slot in local:
            to_vmem.wait()
            to_slot.start()
        for cp in recvs:
            cp.wait_recv()
        for cp in sends:
            cp.wait_send()
        for _, to_slot in local:
            to_slot.wait()


def _carried_call(body, carry, *, name, grid, in_specs, out_specs, out_shape, scratch_shapes, compiler_params,
                  operands):
    n_in, n_out = len(in_specs), len(out_specs)
    if carry is None:
        return pl.pallas_call(body, name=name, grid=grid, in_specs=in_specs, out_specs=out_specs,
                              out_shape=out_shape, scratch_shapes=scratch_shapes,
                              compiler_params=compiler_params)(*operands), []
    m = carry.n

    def full(*refs):
        ins, cin = refs[:n_in], refs[n_in:n_in + m]
        outs, cout = refs[n_in + m:n_in + m + n_out], refs[n_in + m + n_out:n_in + 2 * m + n_out]
        own = len(refs) - len(carry.scratch)
        scr, sems = refs[n_in + 2 * m + n_out:own], refs[own:]
        ids = [pl.program_id(a) for a in range(len(grid))]
        first = functools.reduce(jnp.logical_and, [i == 0 for i in ids])
        last = functools.reduce(jnp.logical_and, [i == g - 1 for i, g in zip(ids, grid)])

        @pl.when(first)
        def _():
            carry.start(cin, cout, sems)

        body(*ins, *outs, *scr)

        @pl.when(last)
        def _():
            carry.finish(cin, cout, sems)

    res = pl.pallas_call(
        full, name=name, grid=grid, in_specs=list(in_specs) + [ANY] * m, out_specs=list(out_specs) + [ANY] * m,
        out_shape=list(out_shape) + carry.out_shape, scratch_shapes=list(scratch_shapes) + carry.scratch,
        compiler_params=compiler_params)(*operands, *carry.arrays)
    return res[:n_out], res[n_out:]


def _flush(carry, name):
    m = carry.n

    def body(*refs):
        cin, cout, sems = refs[:m], refs[m:2 * m], refs[2 * m:]
        carry.start(cin, cout, sems)
        carry.finish(cin, cout, sems)

    return pl.pallas_call(body, name=name, in_specs=[ANY] * m, out_specs=[ANY] * m, out_shape=carry.out_shape,
                          scratch_shapes=carry.scratch)(*carry.arrays)


def _pair_share(leaves):
    n = len(leaves)

    def body(*refs):
        ins, outs = refs[:n], refs[n:2 * n]
        ssem, rsem = refs[2 * n:]
        x, y, c, _ = _place()
        sib = (x, y, 1 - c)
        cps = [_rcopy(ins[k].at[:, c], outs[k].at[:, c], ssem.at[k], rsem.at[k], sib) for k in range(n)]
        for cp in cps:
            cp.start()
        for k in range(n):
            got = outs[k].at[:, 1 - c]
            _rcopy(got, got, ssem.at[k], rsem.at[k], sib).wait_recv()
        for cp in cps:
            cp.wait_send()

    dma = pltpu.SemaphoreType.DMA
    return pl.pallas_call(
        body, name="grad_pair_share", in_specs=[ANY] * n, out_specs=[ANY] * n,
        out_shape=[jax.ShapeDtypeStruct(a.shape, a.dtype) for a in leaves],
        input_output_aliases={k: k for k in range(n)},
        scratch_shapes=[dma((n,)), dma((n,))])(*leaves)


def _allreduce_small(pack):
    R = pack.shape[0]

    def body(in_ref, out_ref, buf, ssem, rsem):
        x, y, c, _ = _place()
        me = 4 * x + 2 * y + c
        buf[me] = in_ref[...]
        peers = []
        for k in range(1, 8):
            fx, fy, fc = (k >> 2) & 1, (k >> 1) & 1, k & 1
            peers.append((1 - x if fx else x, 1 - y if fy else y, 1 - c if fc else c))
        cps = [_rcopy(in_ref, buf.at[me], ssem.at[k], rsem.at[k], p) for k, p in enumerate(peers)]
        for cp in cps:
            cp.start()
        for k, (px, py, pc) in enumerate(peers):
            got = buf.at[4 * px + 2 * py + pc]
            _rcopy(got, got, ssem.at[k], rsem.at[k], (px, py, pc)).wait_recv()
        for cp in cps:
            cp.wait_send()
        acc = buf[0]
        for s in range(1, 8):
            acc = acc + buf[s]
        out_ref[...] = acc

    dma = pltpu.SemaphoreType.DMA
    return pl.pallas_call(
        body, name="allreduce_small", out_shape=jax.ShapeDtypeStruct(pack.shape, F32),
        in_specs=[pl.BlockSpec(memory_space=pltpu.VMEM)], out_specs=pl.BlockSpec(memory_space=pltpu.VMEM),
        scratch_shapes=[pltpu.VMEM((8, R, 128), F32), dma((7,)), dma((7,))])(pack)


def _pack(arrs):
    flat = jnp.concatenate([a.reshape(-1).astype(F32) for a in arrs])
    n = flat.shape[0]
    rows = -(-n // 1024) * 8
    return jnp.pad(flat, (0, rows * 128 - n)).reshape(rows, 128)


def _unpack(pack, shapes):
    flat, out, o = pack.reshape(-1), [], 0
    for s in shapes:
        n = int(np.prod(s))
        out.append(flat[o:o + n].reshape(s))
        o += n
    return out


def _ffn_keys(i):
    return [('ffn_w_gate', i), ('ffn_w_up', i), ('ffn_w_down', i)]


GATHER_PLAN = {'attn_fwd0': _ffn_keys(0) + [('conv_w_in', 0), ('conv_w_out', 0)],
               'ffn_fwd0': _ffn_keys(1),
               'ffn_fwd1': _ffn_keys(2) + [('pool_w', 0)],
               'ffn_fwd2': [('attn_w_qkv', 1), ('attn_w_o', 1)],
               'attn_fwd3': _ffn_keys(3)}
REDUCE_PLAN = {'attn_bwd3': _ffn_keys(3),
               'ffn_bwd_act2': [('attn_w_qkv', 1), ('attn_w_o', 1)],
               'ffn_bwd_act1': _ffn_keys(2) + [('pool_w', 0)],
               'ffn_bwd_act0': _ffn_keys(1),
               'ffn_bwd_w0': [('conv_w_in', 0), ('conv_w_out', 0)],
               'attn_bwd0': _ffn_keys(0)}


def kernel(x, meta_tokens, rel_bias_table, norm_mix, norm_ffn, norm_final, attn_w_qkv, attn_b_qkv, attn_w_o, attn_b_o, attn_sinks, conv_w_in, conv_w, conv_w_out, pool_w, pool_scale, ffn_w_gate, ffn_w_up, ffn_w_down, loss_target, m_meta_tokens, m_rel_bias_table, m_norm_mix, m_norm_ffn, m_norm_final, m_attn_w_qkv, m_attn_b_qkv, m_attn_w_o, m_attn_b_o, m_attn_sinks, m_conv_w_in, m_conv_w, m_conv_w_out, m_pool_w, m_pool_scale, m_ffn_w_gate, m_ffn_w_up, m_ffn_w_down, v_meta_tokens, v_rel_bias_table, v_norm_mix, v_norm_ffn, v_norm_final, v_attn_w_qkv, v_attn_b_qkv, v_attn_w_o, v_attn_b_o, v_attn_sinks, v_conv_w_in, v_conv_w, v_conv_w_out, v_pool_w, v_pool_scale, v_ffn_w_gate, v_ffn_w_up, v_ffn_w_down):
    args = locals()
    w = {n: args[n] for n in WEIGHTS}
    mom = {n: args['m_' + n] for n in WEIGHTS}
    var = {n: args['v_' + n] for n in WEIGHTS}
    mx, my = lax.axis_index("x"), lax.axis_index("y")
    chip = 2 * mx + my
    S = x.shape[1]
    scale = jnp.asarray(HD ** -0.5, BF16)

    GW = {}
    pending = {}
    land = {}

    def gather_carry(call):
        keys = GATHER_PLAN.get(call)
        return _Carry('gather', [w[n][l].astype(BF16) for n, l in keys]) if keys else None

    def reduce_carry(call):
        keys = REDUCE_PLAN.get(call)
        return _Carry('reduce', [pending[k] for k in keys]) if keys else None

    def pend(n, l, g):
        pending[(n, l)] = g.reshape((NSH, 2) + HALF_SHAPE[n])

    small_in = jnp.concatenate([
        jnp.pad(w['meta_tokens'], ((0, 0), (0, 128))), w['attn_b_qkv'], jnp.pad(w['attn_b_o'], ((0, 0), (0, 128))),
        jnp.pad(w['conv_w'][0], ((0, 0), (0, 128))), jnp.pad(w['pool_scale'], ((0, 0), (0, 128)))], axis=0)
    gsmall, GW[('attn_w_qkv', 0)], GW[('attn_w_o', 0)] = _flush(
        _Carry('gather', [small_in, w['attn_w_qkv'][0].astype(BF16), w['attn_w_o'][0].astype(BF16)]), "gather_first")

    def cols(rows, width):
        return jnp.transpose(rows[:, :, :width], (1, 0, 2)).reshape(rows.shape[1], NSH * width)

    meta_full = cols(gsmall[:, 0:16], 256)
    b_qkv = cols(gsmall[:, 16:18], 384)
    b_o = cols(gsmall[:, 18:20], 256)
    conv_k = cols(gsmall[:, 20:23], 256)
    p_scale = cols(gsmall[:, 23:24], 256)

    idx_np, _ = _bias_index()
    idx = jnp.asarray(idx_np).reshape(1, -1)
    table_aug = jnp.concatenate([rel_bias_table.T, jnp.full((H, 1), NEG, F32),
                                 jnp.zeros((H, 127 - N_BUCKETS), F32)], axis=1)
    bias_flat = _bias_lookup(table_aug, idx, "bias_lookup")
    nblock = BLK * NKEY
    bias_blk = jnp.transpose(bias_flat[:, :2 * nblock].reshape(H, 2, BLK, NKEY), (1, 0, 2, 3))
    bias_mm = bias_flat[:, 2 * nblock:].reshape(H, N_META, N_META)
    bias_band, bias_meta = bias_blk[..., BLK:], bias_blk[..., :N_META]

    h = jnp.concatenate([meta_full, x[0]], axis=0)
    saved = []
    for i in range(DEPTH):
        kind, j = i % 3, i // 3
        gm = norm_mix[i:i + 1]
        st = dict(h=h)
        if kind == 0:
            w_o = GW[('attn_w_o', j)].reshape(1, D, D)
            qkv, a = _linear(h, GW[('attn_w_qkv', j)], gain=gm, bias=b_qkv[j:j + 1], name=f"qkv{i}")
            qf, kf, vf = qkv[:, :H * HD] * scale, qkv[:, H * HD:(H + KV) * HD], qkv[:, (H + KV) * HD:]
            q, k, v = _heads(qf, H), _heads(kf, KV), _heads(vf, KV)
            lay = dict(q=q, qt=_heads_t(qf[N_META:], H), qm=q[:, :N_META],
                       k=k, kt=_heads_t(kf[N_META:], KV), km=k[:, :N_META],
                       kmp=_pad_block(k[:, :N_META], 1), kmt=_pad_block(_heads_t(kf[:N_META], KV), 2),
                       v=v, vt=_heads_t(vf[N_META:], KV), vm=v[:, :N_META],
                       vmt=_pad_block(_heads_t(vf[:N_META], KV), 2))
            sink_r = jnp.broadcast_to(attn_sinks[j].reshape(KV, GRP, 1, 1), (KV, GRP, BLK, 1)).reshape(KV, GRP * BLK, 1)
            sink_m = jnp.broadcast_to(attn_sinks[j].reshape(H, 1, 1), (H, N_META, 1))
            (o_r,), got = _attn_fwd(lay['q'], lay['k'], lay['km'], lay['v'], lay['vm'], bias_band, bias_meta, sink_r,
                                    f"attn_fwd{i}", gather_carry(f"attn_fwd{i}"))
            GW.update(zip(GATHER_PLAN.get(f"attn_fwd{i}", []), got))
            o_m = _attn_meta_fwd(lay['qm'], lay['km'], lay['vm'], bias_mm, sink_m, f"attn_meta_fwd{i}")
            o = _unheads(jnp.concatenate([o_m, o_r], axis=1))
            h1 = _linear(o, w_o, bias=b_o[j:j + 1], resid=h, out_dtype=F32, name=f"attn_out{i}")
            st.update(a=a, lay=lay, sinks=(sink_r, sink_m), o=o, w_o=w_o)
        elif kind == 1:
            z3, a = _linear(h, GW[('conv_w_in', j)], gain=gm, name=f"conv_in{i}")
            yv = _conv_fwd(z3, conv_k, f"conv_fwd{i}")
            w_cout = GW[('conv_w_out', j)].reshape(1, D, D)
            h1 = _linear(yv, w_cout, resid=h, out_dtype=F32, name=f"conv_out{i}")
            st.update(a=a, z3=z3, y=yv, w_cout=w_cout)
        else:
            w_pool = jnp.transpose(GW[('pool_w', j)], (1, 0, 2, 3)).reshape(4, PG, PG)
            mix = _pool_fwd(h, gm, f"pool_fwd{i}")
            h1 = _pool_out(mix, w_pool, p_scale, h, f"pool_out{i}")
            st.update(mix=mix, w_pool=w_pool)
        ffn_w = [GW[k] for k in _ffn_keys(i)]
        (hn, Gp, Up, Ap, Bp), got = _ffn_fwd(h1, norm_ffn[i:i + 1], *ffn_w, f"ffn_fwd{i}", gather_carry(f"ffn_fwd{i}"))
        GW.update(zip(GATHER_PLAN.get(f"ffn_fwd{i}", []), got))
        st.update(h1=h1, G=Gp, U=Up, A=Ap, B=Bp, ffn_w=ffn_w)
        saved.append(st)
        h = hn

    target = jnp.pad(loss_target[0], ((N_META, 0), (0, 0)))
    dh, d_nfinal, loss_part = _loss_head(h, norm_final.reshape(1, D), target)
    d_nmix, d_nffn = [None] * DEPTH, [None] * DEPTH
    d_bqkv, d_bo, d_sinks = [None, None], [None, None], [None, None]
    d_table = jnp.zeros((N_BUCKETS, H), F32)
    d_convk = d_pscale = None
    for i in reversed(range(DEPTH)):
        kind, j = i % 3, i // 3
        st = saved[i]
        call = f"ffn_bwd_act{i}"
        (DG, DU, dh1, d_nffn[i]), got = _ffn_bwd_act(dh, st['h1'], norm_ffn[i:i + 1], st['G'], st['U'], *st['ffn_w'],
                                                     call, reduce_carry(call))
        land.update(zip(REDUCE_PLAN.get(call, []), got))
        call = f"ffn_bwd_w{i}"
        (gw, uw, dw_), got = _ffn_bwd_w(st['B'], st['A'], DG, DU, dh, call, reduce_carry(call))
        land.update(zip(REDUCE_PLAN.get(call, []), got))
        pend('ffn_w_gate', i, gw)
        pend('ffn_w_up', i, uw)
        pend('ffn_w_down', i, dw_)
        gm = norm_mix[i:i + 1]
        if kind == 0:
            lay = st['lay']
            sink_r, sink_m = st['sinks']
            gwo, d_bo[j] = _wgrad(st['o'], dh1, mode='row', colsum=True, name=f"attn_out_wgrad{i}")
            pend('attn_w_o', j, gwo)
            dof = _linear_t(dh1, st['w_o'], name=f"attn_out_bwd{i}")
            do = _heads(dof, H)
            call = f"attn_bwd{i}"
            (dq_r, dkt, dvt, dkmt, dvmt, dbias), got = _attn_bwd(
                lay['q'], lay['qt'], lay['k'], lay['kmp'], lay['kt'], lay['kmt'], lay['vt'], lay['vmt'], bias_blk,
                sink_r, do, _heads_t(dof[N_META:], H), call, reduce_carry(call))
            land.update(zip(REDUCE_PLAN.get(call, []), got))
            dq_m, dkm2, dvm2, dbmm = _attn_meta_bwd(lay['qm'], lay['km'], lay['vm'], bias_mm, sink_m,
                                                    do[:, :N_META], f"attn_meta_bwd{i}")
            dflat = jnp.concatenate([jnp.transpose(dbias, (1, 0, 2, 3)).reshape(H, -1), dbmm.reshape(H, -1)], axis=1)
            dts = _bias_lookup_bwd(dflat, idx, f"bias_lookup_bwd{i}")
            d_table = d_table + dts[:, :N_BUCKETS].T
            d_sinks[j] = dts[:, N_BUCKETS]
            dkm = (jnp.transpose(dkmt[:, :, :N_META], (0, 2, 1)) + dkm2).astype(BF16)
            dvm = (jnp.transpose(dvmt[:, :, :N_META], (0, 2, 1)) + dvm2).astype(BF16)
            dqkv = jnp.concatenate([
                jnp.concatenate([_unheads(dq_m), _unheads(dq_r)], axis=0) * scale,
                jnp.concatenate([_unheads(dkm), _unheads_t(dkt)], axis=0),
                jnp.concatenate([_unheads(dvm), _unheads_t(dvt)], axis=0)], axis=1)
            gq, d_bqkv[j] = _wgrad(st['a'], dqkv, mode='col', colsum=True, name=f"qkv_wgrad{i}")
            pend('attn_w_qkv', j, gq)
            dh, d_nmix[i] = _linear_t(dqkv, GW[('attn_w_qkv', j)], rms=(st['h'], gm, dh1), name=f"qkv_bwd{i}")
        elif kind == 1:
            pend('conv_w_out', j, _wgrad(st['y'], dh1, mode='row', name=f"conv_out_wgrad{i}"))
            dy = _linear_t(dh1, st['w_cout'], out_dtype=F32, name=f"conv_out_bwd{i}")
            dz3, d_convk = _conv_bwd(st['z3'], conv_k, dy, f"conv_bwd{i}")
            pend('conv_w_in', j, _wgrad(st['a'], dz3, mode='col', name=f"conv_in_wgrad{i}"))
            dh, d_nmix[i] = _linear_t(dz3, GW[('conv_w_in', j)], rms=(st['h'], gm, dh1), name=f"conv_in_bwd{i}")
        else:
            dmix, dwp, d_pscale = _pool_out_bwd(dh1, st['mix'], st['w_pool'], p_scale, f"pool_out_bwd{i}")
            pend('pool_w', j, jnp.transpose(dwp.reshape(4, NSH, PG // NSH, PG), (1, 0, 2, 3)).astype(BF16))
            dh, d_nmix[i] = _pool_bwd(dmix, st['h'], gm, dh1, f"pool_bwd{i}")

    last_keys = [('attn_w_qkv', 0), ('attn_w_o', 0)]
    land.update(zip(last_keys, _flush(_Carry('reduce', [pending[k] for k in last_keys]), "reduce_last")))
    ids = jnp.stack([chip, lax.axis_index("c")]).astype(jnp.int32)
    shared = _pair_share([_sum_leaf([land[(n, l)] for l in range(w[n].shape[0])],
                                    [pending[(n, l)] for l in range(w[n].shape[0])], ids, f"sum_{n}") for n in BIG])
    grads = {n: s.reshape(w[n].shape) for n, s in zip(BIG, shared)}

    small_full = [dh[:N_META], d_table, jnp.concatenate(d_nmix, axis=0), jnp.concatenate(d_nffn, axis=0), d_nfinal,
                  jnp.concatenate(d_bqkv, axis=0), jnp.concatenate(d_bo, axis=0), jnp.stack(d_sinks), d_convk,
                  d_pscale, loss_part[:, 0:1]]
    red = _unpack(_allreduce_small(_pack(small_full)), [a.shape for a in small_full])
    g_meta, g_table, g_nmix, g_nffn, g_nfinal, g_bqkv, g_bo, g_sinks, g_convk, g_pscale, loss = red

    def shard(a, width):
        return lax.dynamic_slice_in_dim(a, chip * width, width, axis=1)

    grads.update(meta_tokens=shard(g_meta, 256), rel_bias_table=g_table, norm_mix=g_nmix, norm_ffn=g_nffn,
                 norm_final=g_nfinal.reshape(D), attn_b_qkv=shard(g_bqkv, 384), attn_b_o=shard(g_bo, 256),
                 attn_sinks=g_sinks, conv_w=shard(g_convk, 256)[None], pool_scale=shard(g_pscale, 256))

    delta, new_m, new_v = {}, {}, {}
    for n in BIG:
        shp = w[n].shape
        r2 = (int(np.prod(shp[:-1])), shp[-1])
        dl, mn, vn, gn = _adam(w[n].reshape(r2), grads[n].reshape(r2), mom[n].reshape(r2), var[n].reshape(r2),
                               f"adam_{n}")
        delta[n], new_m[n], new_v[n], grads[n] = dl.reshape(shp), mn.reshape(shp), vn.reshape(shp), gn.reshape(shp)
    shapes = [w[n].shape for n in SMALL]
    packed = [_pack([d[n] for n in SMALL]) for d in (w, grads, mom, var)]
    for dst, res in zip((delta, new_m, new_v), _adam(*packed, "adam_small")[:3]):
        dst.update(zip(SMALL, _unpack(res, shapes)))

    return (loss.reshape(()), dh[N_META:][None], *[grads[n] for n in WEIGHTS], *[delta[n] for n in WEIGHTS],
            *[new_m[n] for n in WEIGHTS], *[new_v[n] for n in WEIGHTS])
```

```python
import functools
import math

import numpy as np
import jax
import jax.numpy as jnp
from jax import lax
from jax.experimental import pallas as pl
from jax.experimental.pallas import tpu as pltpu

F32, BF16 = jnp.float32, jnp.bfloat16
D = 1024
N_META = 16
EPS = 1e-6
H, KV, GRP, HD = 16, 4, 4, 64
BLK = 128
NKEY = 3 * BLK
N_BUCKETS = 32
POOL_WINDOWS = (2, 4, 8, 16)
PG = 256
DFF = 2816
NSH = 4
FC = DFF // NSH
HALO = 16
NEG = -1e30
DEPTH = 4
LR, B1, B2, ADAM_EPS, WD, STEP = 0.001, 0.9, 0.999, 1e-08, 0.01, 10
MESH = pl.DeviceIdType.MESH
ANY = pl.BlockSpec(memory_space=pl.ANY)
VMEM_LIMIT = 48 * 1024 * 1024
VMEM_BIG = 58 * 1024 * 1024
BIG_TILE = 1024

_NN = (((1,), (0,)), ((), ()))
_NT = (((1,), (1,)), ((), ()))
_TN = (((0,), (0,)), ((), ()))

WEIGHTS = ['meta_tokens', 'rel_bias_table', 'norm_mix', 'norm_ffn', 'norm_final', 'attn_w_qkv', 'attn_b_qkv',
           'attn_w_o', 'attn_b_o', 'attn_sinks', 'conv_w_in', 'conv_w', 'conv_w_out', 'pool_w', 'pool_scale',
           'ffn_w_gate', 'ffn_w_up', 'ffn_w_down']
BIG = ['attn_w_qkv', 'attn_w_o', 'conv_w_in', 'conv_w_out', 'pool_w', 'ffn_w_gate', 'ffn_w_up', 'ffn_w_down']
SMALL = [w for w in WEIGHTS if w not in BIG]
HALF_SHAPE = {'attn_w_qkv': (512, 384), 'attn_w_o': (128, 1024), 'conv_w_in': (512, 768),
              'conv_w_out': (128, 1024), 'pool_w': (128, 256), 'ffn_w_gate': (512, 704),
              'ffn_w_up': (512, 704), 'ffn_w_down': (352, 1024)}


def _dot(a, b, dims=_NN):
    return lax.dot_general(a, b, dims, preferred_element_type=F32)


def _tile(n, cap=512):
    best = None
    for t in range(16, min(n, cap) + 1, 16):
        if n % t == 0:
            best = t
    assert best is not None, n
    return best


def _params(n_axes, vmem=VMEM_LIMIT):
    return pltpu.CompilerParams(dimension_semantics=("arbitrary",) * n_axes, vmem_limit_bytes=vmem)


def _rstd(x):
    return lax.rsqrt(jnp.mean(x * x, axis=-1, keepdims=True) + EPS)


def _rms_bwd(dy, x, g, r):
    u = dy * g
    dx = r * u - x * ((r * r * r) * (jnp.sum(x * u, axis=-1, keepdims=True) * (1.0 / D)))
    return dx, jnp.sum(dy * (x * r), axis=0, keepdims=True)


def _sigmoid(x):
    return 1.0 / (1.0 + jnp.exp(-x))


def _acc(ref, val, first):
    @pl.when(first)
    def _():
        ref[...] = val

    @pl.when(jnp.logical_not(first))
    def _():
        ref[...] += val


def _linear(x, w, *, gain=None, bias=None, resid=None, out_dtype=BF16, qkv_heads=False, cap=BIG_TILE, name):
    L, K = x.shape
    J, _, Nc = w.shape
    tm = _tile(L, cap)
    has_g, has_b, has_r = gain is not None, bias is not None, resid is not None
    n_main = 3 if qkv_heads else 1

    def body(*refs):
        refs = list(refs)
        x_ref, w_ref = refs[:2]
        i = 2
        g_ref = b_ref = r_ref = None
        if has_g:
            g_ref, i = refs[i], i + 1
        if has_b:
            b_ref, i = refs[i], i + 1
        if has_r:
            r_ref, i = refs[i], i + 1
        out_ref = refs[i]
        if has_g:
            xf = x_ref[...]
            xv = (xf * _rstd(xf) * g_ref[...]).astype(BF16)
            refs[i + n_main][...] = xv
        else:
            xv = x_ref[...]
        for s in range(J):
            sl = slice(s * Nc, (s + 1) * Nc)
            acc = _dot(xv, w_ref[s])
            if has_b:
                acc = acc + b_ref[:, sl]
            if has_r:
                acc = acc + r_ref[:, sl]
            if not qkv_heads:
                out_ref[:, sl] = acc.astype(out_dtype)
                continue
            for r in range(Nc // HD):
                hd, blk = s * (Nc // HD) + r, acc[:, r * HD:(r + 1) * HD]
                if hd < H:
                    refs[i][hd] = (blk * HD ** -0.5).astype(BF16)
                elif hd < H + KV:
                    refs[i + 1][hd - H] = blk.astype(BF16)
                else:
                    refs[i + 2][hd - H - KV] = blk.astype(BF16)

    row = lambda n: pl.BlockSpec((tm, n), lambda t: (t, 0))
    one = lambda n: pl.BlockSpec((1, n), lambda t: (0, 0))
    in_specs = [row(K), pl.BlockSpec((J, K, Nc), lambda t: (0, 0, 0))]
    ops = [x, w]
    if has_g:
        in_specs.append(one(K))
        ops.append(gain)
    if has_b:
        in_specs.append(one(J * Nc))
        ops.append(bias)
    if has_r:
        in_specs.append(row(J * Nc))
        ops.append(resid)
    if qkv_heads:
        heads = lambda n: pl.BlockSpec((n, tm, HD), lambda t: (0, t, 0))
        out_specs = [heads(H), heads(KV), heads(KV)]
        out_shape = [jax.ShapeDtypeStruct((n, L, HD), BF16) for n in (H, KV, KV)]
    else:
        out_specs, out_shape = [row(J * Nc)], [jax.ShapeDtypeStruct((L, J * Nc), out_dtype)]
    if has_g:
        out_specs, out_shape = out_specs + [row(K)], out_shape + [jax.ShapeDtypeStruct((L, K), BF16)]
    res = pl.pallas_call(body, name=name, grid=(L // tm,), in_specs=in_specs, out_specs=out_specs,
                         out_shape=out_shape, compiler_params=_params(1, VMEM_BIG))(*ops)
    return res[0] if len(res) == 1 else res


def _linear_t(dy, w, *, out_dtype=BF16, rms=None, to_heads=False, carry=None, cap=BIG_TILE, name):
    L = dy.shape[0]
    J, K, Nc = w.shape
    tm = _tile(L, cap)
    has_rms = rms is not None

    def body(*refs):
        t = pl.program_id(0)
        dy_ref, w_ref = refs[:2]
        acc = _dot(dy_ref[:, 0:Nc].astype(BF16), w_ref[0], _NT)
        for s in range(1, J):
            acc = acc + _dot(dy_ref[:, s * Nc:(s + 1) * Nc].astype(BF16), w_ref[s], _NT)
        if has_rms:
            h_ref, g_ref, r_ref, out_ref, dg_ref = refs[2:7]
            x = h_ref[...]
            dx, dg = _rms_bwd(acc, x, g_ref[...], _rstd(x))
            out_ref[...] = r_ref[...] + dx
            _acc(dg_ref, dg, t == 0)
        elif to_heads:
            for hd in range(K // HD):
                refs[2][hd] = acc[:, hd * HD:(hd + 1) * HD].astype(out_dtype)
        else:
            refs[2][...] = acc.astype(out_dtype)

    row = lambda n: pl.BlockSpec((tm, n), lambda t: (t, 0))
    in_specs = [row(J * Nc), pl.BlockSpec((J, K, Nc), lambda t: (0, 0, 0))]
    ops = [dy, w]
    if has_rms:
        in_specs += [row(K), pl.BlockSpec((1, K), lambda t: (0, 0)), row(K)]
        ops += list(rms)
        out_specs = [row(K), pl.BlockSpec((1, K), lambda t: (0, 0))]
        out_shape = [jax.ShapeDtypeStruct((L, K), F32), jax.ShapeDtypeStruct((1, K), F32)]
    elif to_heads:
        out_specs = [pl.BlockSpec((K // HD, tm, HD), lambda t: (0, t, 0))]
        out_shape = [jax.ShapeDtypeStruct((K // HD, L, HD), out_dtype)]
    else:
        out_specs = [row(K)]
        out_shape = [jax.ShapeDtypeStruct((L, K), out_dtype)]
    res, got = _carried_call(body, carry, name=name, grid=(L // tm,), in_specs=in_specs, out_specs=out_specs,
                             out_shape=out_shape, scratch_shapes=[], compiler_params=_params(1, VMEM_BIG),
                             operands=ops)
    res = res[0] if len(res) == 1 else res
    return res if carry is None else (res, got)


def _wgrad(x, dy, *, mode, colsum=False, cap=BIG_TILE, name):
    L, K = x.shape
    N = dy.shape[1]
    tm = _tile(L, cap)
    nt = L // tm
    oshape = (NSH, K, N // NSH) if mode == 'col' else (NSH, K // NSH, N)

    def body(*refs):
        t = pl.program_id(0)
        x_ref, dy_ref, out_ref = refs[:3]
        acc_ref = refs[-1]
        dyv = dy_ref[...]
        _acc(acc_ref, _dot(x_ref[...].astype(BF16), dyv.astype(BF16), _TN), t == 0)
        if colsum:
            _acc(refs[3], jnp.sum(dyv.astype(F32), axis=0, keepdims=True), t == 0)

        @pl.when(t == nt - 1)
        def _():
            for s in range(NSH):
                if mode == 'col':
                    out_ref[s] = acc_ref[:, s * oshape[2]:(s + 1) * oshape[2]].astype(BF16)
                else:
                    out_ref[s] = acc_ref[s * oshape[1]:(s + 1) * oshape[1], :].astype(BF16)

    out_specs = pl.BlockSpec(oshape, lambda t: (0, 0, 0))
    out_shape = jax.ShapeDtypeStruct(oshape, BF16)
    if colsum:
        out_specs = (out_specs, pl.BlockSpec((1, N), lambda t: (0, 0)))
        out_shape = (out_shape, jax.ShapeDtypeStruct((1, N), F32))
    return pl.pallas_call(
        body, name=name, grid=(nt,),
        in_specs=[pl.BlockSpec((tm, K), lambda t: (t, 0)), pl.BlockSpec((tm, N), lambda t: (t, 0))],
        out_specs=out_specs, out_shape=out_shape,
        scratch_shapes=[pltpu.VMEM((K, N), F32)], compiler_params=_params(1, VMEM_BIG))(x, dy)


def _ffn_fwd(h, gain, wg, wu, wd, name, carry=None):
    L = h.shape[0]
    tm = _tile(L, BIG_TILE)

    def body(h_ref, g_ref, wg_ref, wu_ref, wd_ref, hn_ref, G_ref, U_ref, A_ref, B_ref, acc_ref, b_scr):
        j = pl.program_id(1)

        @pl.when(j == 0)
        def _():
            x = h_ref[...]
            b = (x * _rstd(x) * g_ref[...]).astype(BF16)
            b_scr[...] = b
            B_ref[...] = b
            acc_ref[...] = x

        b = b_scr[...]
        g = _dot(b, wg_ref[...])
        u = _dot(b, wu_ref[...])
        s = _sigmoid(g)
        silu = g * s
        G_ref[...] = (u * (s * (1.0 + g * (1.0 - s)))).astype(BF16)
        U_ref[...] = silu.astype(BF16)
        a = (silu * u).astype(BF16)
        A_ref[...] = a
        acc_ref[...] += _dot(a, wd_ref[...])

        @pl.when(j == NSH - 1)
        def _():
            hn_ref[...] = acc_ref[...]

    row = pl.BlockSpec((tm, D), lambda t, j: (t, 0))
    chunk = pl.BlockSpec((None, tm, FC), lambda t, j: (j, t, 0))
    cshape = jax.ShapeDtypeStruct((NSH, L, FC), BF16)
    return _carried_call(
        body, carry, name=name, grid=(L // tm, NSH),
        in_specs=[row, pl.BlockSpec((1, D), lambda t, j: (0, 0)),
                  pl.BlockSpec((None, D, FC), lambda t, j: (j, 0, 0)),
                  pl.BlockSpec((None, D, FC), lambda t, j: (j, 0, 0)),
                  pl.BlockSpec((None, FC, D), lambda t, j: (j, 0, 0))],
        out_specs=[row, chunk, chunk, chunk, row],
        out_shape=[jax.ShapeDtypeStruct((L, D), F32), cshape, cshape, cshape, jax.ShapeDtypeStruct((L, D), BF16)],
        scratch_shapes=[pltpu.VMEM((tm, D), F32), pltpu.VMEM((tm, D), BF16)],
        compiler_params=_params(2, VMEM_BIG), operands=[h, gain, wg, wu, wd])


def _ffn_bwd_act(dhn, h, gain, G, U, wg, wu, wd, name, carry=None):
    L = h.shape[0]
    tm = _tile(L, BIG_TILE)

    def body(dhn_ref, h_ref, g_ref, G_ref, U_ref, wg_ref, wu_ref, wd_ref, DG_ref, DU_ref, dh_ref, dgain_ref, db_ref):
        t, j = pl.program_id(0), pl.program_id(1)
        d_act = _dot(dhn_ref[...].astype(BF16), wd_ref[...], _NT)
        dg = (d_act * G_ref[...].astype(F32)).astype(BF16)
        du = (d_act * U_ref[...].astype(F32)).astype(BF16)
        DG_ref[...] = dg
        DU_ref[...] = du
        _acc(db_ref, _dot(dg, wg_ref[...], _NT) + _dot(du, wu_ref[...], _NT), j == 0)

        @pl.when(j == NSH - 1)
        def _():
            x = h_ref[...]
            dx, dgn = _rms_bwd(db_ref[...], x, g_ref[...], _rstd(x))
            dh_ref[...] = dhn_ref[...] + dx
            _acc(dgain_ref, dgn, t == 0)

    row = pl.BlockSpec((tm, D), lambda t, j: (t, 0))
    one = pl.BlockSpec((1, D), lambda t, j: (0, 0))
    chunk = pl.BlockSpec((None, tm, FC), lambda t, j: (j, t, 0))
    cshape = jax.ShapeDtypeStruct((NSH, L, FC), BF16)
    return _carried_call(
        body, carry, name=name, grid=(L // tm, NSH),
        in_specs=[row, row, one, chunk, chunk,
                  pl.BlockSpec((None, D, FC), lambda t, j: (j, 0, 0)),
                  pl.BlockSpec((None, D, FC), lambda t, j: (j, 0, 0)),
                  pl.BlockSpec((None, FC, D), lambda t, j: (j, 0, 0))],
        out_specs=[chunk, chunk, row, one],
        out_shape=[cshape, cshape, jax.ShapeDtypeStruct((L, D), F32), jax.ShapeDtypeStruct((1, D), F32)],
        scratch_shapes=[pltpu.VMEM((tm, D), F32)],
        compiler_params=_params(2, VMEM_BIG), operands=[dhn, h, gain, G, U, wg, wu, wd])


def _ffn_bwd_w(B, A, DG, DU, dhn, name, carry=None):
    L = B.shape[0]
    tm = _tile(L, BIG_TILE)
    nt = L // tm

    def body(B_ref, A_ref, DG_ref, DU_ref, dhn_ref, dwg_ref, dwu_ref, dwd_ref, ag, au, ad):
        t = pl.program_id(1)
        b = B_ref[...]
        _acc(ag, _dot(b, DG_ref[...], _TN), t == 0)
        _acc(au, _dot(b, DU_ref[...], _TN), t == 0)
        _acc(ad, _dot(A_ref[...], dhn_ref[...].astype(BF16), _TN), t == 0)

        @pl.when(t == nt - 1)
        def _():
            dwg_ref[...] = ag[...].astype(BF16)
            dwu_ref[...] = au[...].astype(BF16)
            dwd_ref[...] = ad[...].astype(BF16)

    row = pl.BlockSpec((tm, D), lambda j, t: (t, 0))
    chunk = pl.BlockSpec((None, tm, FC), lambda j, t: (j, t, 0))
    return _carried_call(
        body, carry, name=name, grid=(NSH, nt), in_specs=[row, chunk, chunk, chunk, row],
        out_specs=[pl.BlockSpec((None, D, FC), lambda j, t: (j, 0, 0)),
                   pl.BlockSpec((None, D, FC), lambda j, t: (j, 0, 0)),
                   pl.BlockSpec((None, FC, D), lambda j, t: (j, 0, 0))],
        out_shape=[jax.ShapeDtypeStruct((NSH, D, FC), BF16), jax.ShapeDtypeStruct((NSH, D, FC), BF16),
                   jax.ShapeDtypeStruct((NSH, FC, D), BF16)],
        scratch_shapes=[pltpu.VMEM((D, FC), F32), pltpu.VMEM((D, FC), F32), pltpu.VMEM((FC, D), F32)],
        compiler_params=_params(2, VMEM_BIG), operands=[B, A, DG, DU, dhn])


def _loss_head(h, gain, target):
    L = h.shape[0]
    tm = _tile(L)

    def body(h_ref, g_ref, tgt_ref, dh_ref, dg_ref, loss_ref):
        t = pl.program_id(0)
        x = h_ref[...]
        g = g_ref[...]
        r = _rstd(x)
        rows = t * tm + lax.broadcasted_iota(jnp.int32, (tm, 1), 0)
        diff = jnp.where(rows >= N_META, x * r * g - tgt_ref[...], 0.0)
        part = 0.5 * jnp.sum(jnp.sum(diff * diff, axis=-1, keepdims=True) * (1.0 / D), axis=0, keepdims=True)
        dx, dg = _rms_bwd(diff * (1.0 / D), x, g, r)
        dh_ref[...] = dx
        _acc(dg_ref, dg, t == 0)
        _acc(loss_ref, jnp.broadcast_to(part, (1, 128)), t == 0)

    row = pl.BlockSpec((tm, D), lambda t: (t, 0))
    one = pl.BlockSpec((1, D), lambda t: (0, 0))
    return pl.pallas_call(
        body, name="loss_head", grid=(L // tm,), in_specs=[row, one, row],
        out_specs=(row, one, pl.BlockSpec((1, 128), lambda t: (0, 0))),
        out_shape=(jax.ShapeDtypeStruct((L, D), F32), jax.ShapeDtypeStruct((1, D), F32),
                   jax.ShapeDtypeStruct((1, 128), F32)),
        compiler_params=_params(1))(h, gain, target)


def _conv_fwd(z3, cw, name):
    L = z3.shape[0]
    tm = _tile(L)

    def body(b_ref, c_ref, u_ref, ch_ref, uh_ref, w_ref, y_ref, buf):
        t = pl.program_id(0)
        z = c_ref[...].astype(F32) * u_ref[...].astype(F32)
        buf[pl.ds(0, HALO), :] = jnp.where(t > 0, ch_ref[...].astype(F32) * uh_ref[...].astype(F32), 0.0)
        buf[pl.ds(HALO, tm), :] = z
        w0, w1, w2 = w_ref[0:1, :], w_ref[1:2, :], w_ref[2:3, :]
        conv = w2 * z + w1 * buf[pl.ds(HALO - 1, tm), :] + w0 * buf[pl.ds(HALO - 2, tm), :]
        y_ref[...] = (b_ref[...].astype(F32) * conv).astype(BF16)

    def col(k):
        return pl.BlockSpec((tm, D), lambda t: (t, k))

    def hcol(k):
        return pl.BlockSpec((HALO, D), lambda t: (jnp.maximum(t * (tm // HALO) - 1, 0), k))

    return pl.pallas_call(
        body, name=name, grid=(L // tm,),
        in_specs=[col(0), col(1), col(2), hcol(1), hcol(2), pl.BlockSpec((3, D), lambda t: (0, 0))],
        out_specs=pl.BlockSpec((tm, D), lambda t: (t, 0)),
        out_shape=jax.ShapeDtypeStruct((L, D), BF16),
        scratch_shapes=[pltpu.VMEM((HALO + tm, D), F32)], compiler_params=_params(1))(z3, z3, z3, z3, z3, cw)


def _conv_bwd(z3, cw, dy, name):
    L = z3.shape[0]
    tm = _tile(L)
    nt = L // tm
    last_h = L // HALO - 1

    def body(b_ref, c_ref, u_ref, ch_ref, uh_ref, w_ref, dy_ref, bn_ref, dyn_ref, dz3_ref, dw_ref, zbuf, dbuf):
        t = pl.program_id(0)
        w0, w1, w2 = w_ref[0:1, :], w_ref[1:2, :], w_ref[2:3, :]
        bgate, cgate, u = b_ref[...].astype(F32), c_ref[...].astype(F32), u_ref[...].astype(F32)
        z = cgate * u
        zbuf[pl.ds(0, HALO), :] = jnp.where(t > 0, ch_ref[...].astype(F32) * uh_ref[...].astype(F32), 0.0)
        zbuf[pl.ds(HALO, tm), :] = z
        z1 = zbuf[pl.ds(HALO - 1, tm), :]
        z2 = zbuf[pl.ds(HALO - 2, tm), :]
        conv = w2 * z + w1 * z1 + w0 * z2
        dyv = dy_ref[...]
        dconv = dyv * bgate
        dbuf[pl.ds(0, tm), :] = dconv
        dbuf[pl.ds(tm, HALO), :] = jnp.where(t < nt - 1, dyn_ref[...] * bn_ref[...].astype(F32), 0.0)
        dz = w2 * dconv + w1 * dbuf[pl.ds(1, tm), :] + w0 * dbuf[pl.ds(2, tm), :]
        dz3_ref[:, 0:D] = (dyv * conv).astype(BF16)
        dz3_ref[:, D:2 * D] = (dz * u).astype(BF16)
        dz3_ref[:, 2 * D:3 * D] = (dz * cgate).astype(BF16)
        for k, zk in enumerate((z2, z1, z)):
            _acc(dw_ref.at[k:k + 1], jnp.sum(dconv * zk, axis=0, keepdims=True), t == 0)

    def col(k):
        return pl.BlockSpec((tm, D), lambda t: (t, k))

    def hprev(k):
        return pl.BlockSpec((HALO, D), lambda t: (jnp.maximum(t * (tm // HALO) - 1, 0), k))

    def hnext(k):
        return pl.BlockSpec((HALO, D), lambda t: (jnp.minimum((t + 1) * (tm // HALO), last_h), k))

    return pl.pallas_call(
        body, name=name, grid=(nt,),
        in_specs=[col(0), col(1), col(2), hprev(1), hprev(2), pl.BlockSpec((3, D), lambda t: (0, 0)),
                  col(0), hnext(0), hnext(0)],
        out_specs=(pl.BlockSpec((tm, 3 * D), lambda t: (t, 0)), pl.BlockSpec((3, D), lambda t: (0, 0))),
        out_shape=(jax.ShapeDtypeStruct((L, 3 * D), BF16), jax.ShapeDtypeStruct((3, D), F32)),
        scratch_shapes=[pltpu.VMEM((HALO + tm, D), F32), pltpu.VMEM((tm + HALO, D), F32)],
        compiler_params=_params(1))(z3, z3, z3, z3, z3, cw, dy, z3, dy)


def _count_inv(pos, w):
    return 1.0 / jnp.minimum(pos + 1, w).astype(F32)


def _pool_fwd(h, gain, name):
    L = h.shape[0]
    tm = _tile(L)

    def body(h_ref, hh_ref, g_ref, mix_ref, buf):
        t = pl.program_id(0)
        g = g_ref[...]
        x = h_ref[...]
        a = x * _rstd(x) * g
        xh = hh_ref[...]
        buf[pl.ds(0, HALO), :] = jnp.where(t > 0, xh * _rstd(xh) * g, 0.0)
        buf[pl.ds(HALO, tm), :] = a
        pos = t * tm + lax.broadcasted_iota(jnp.int32, (tm, 1), 0)
        for gi, w in enumerate(POOL_WINDOWS):
            cols = pl.ds(gi * PG, PG)
            s = buf[pl.ds(HALO, tm), cols]
            for k in range(1, w):
                s = s + buf[pl.ds(HALO - k, tm), cols]
            mix_ref[:, gi * PG:(gi + 1) * PG] = (s / jnp.minimum(pos + 1, w).astype(F32)
                                                  - buf[pl.ds(HALO, tm), cols]).astype(BF16)

    return pl.pallas_call(
        body, name=name, grid=(L // tm,),
        in_specs=[pl.BlockSpec((tm, D), lambda t: (t, 0)),
                  pl.BlockSpec((HALO, D), lambda t: (jnp.maximum(t * (tm // HALO) - 1, 0), 0)),
                  pl.BlockSpec((1, D), lambda t: (0, 0))],
        out_specs=pl.BlockSpec((tm, D), lambda t: (t, 0)),
        out_shape=jax.ShapeDtypeStruct((L, D), BF16),
        scratch_shapes=[pltpu.VMEM((HALO + tm, D), F32)], compiler_params=_params(1))(h, h, gain)


def _pool_out(mix, wp, scale, h, name):
    L = mix.shape[0]
    tm = _tile(L)

    def body(mix_ref, wp_ref, s_ref, h_ref, out_ref):
        for gi in range(4):
            sl = slice(gi * PG, (gi + 1) * PG)
            pre = _dot(mix_ref[:, sl], wp_ref[gi])
            out_ref[:, sl] = h_ref[:, sl] + pre * s_ref[:, sl]

    row = pl.BlockSpec((tm, D), lambda t: (t, 0))
    return pl.pallas_call(
        body, name=name, grid=(L // tm,),
        in_specs=[row, pl.BlockSpec((4, PG, PG), lambda t: (0, 0, 0)), pl.BlockSpec((1, D), lambda t: (0, 0)), row],
        out_specs=row, out_shape=jax.ShapeDtypeStruct((L, D), F32), compiler_params=_params(1))(mix, wp, scale, h)


def _pool_out_bwd(dm, mix, wp, scale, name):
    L = mix.shape[0]
    tm = _tile(L)

    def body(dm_ref, mix_ref, wp_ref, s_ref, dmix_ref, dwp_ref, ds_ref):
        t = pl.program_id(0)
        for gi in range(4):
            sl = slice(gi * PG, (gi + 1) * PG)
            mx = mix_ref[:, sl]
            dmv = dm_ref[:, sl]
            pre = _dot(mx, wp_ref[gi])
            _acc(ds_ref.at[:, sl], jnp.sum(dmv * pre, axis=0, keepdims=True), t == 0)
            dpre = (dmv * s_ref[:, sl]).astype(BF16)
            _acc(dwp_ref.at[gi], _dot(mx, dpre, _TN), t == 0)
            dmix_ref[:, sl] = _dot(dpre, wp_ref[gi], _NT)

    row = pl.BlockSpec((tm, D), lambda t: (t, 0))
    one = pl.BlockSpec((1, D), lambda t: (0, 0))
    wsp = pl.BlockSpec((4, PG, PG), lambda t: (0, 0, 0))
    return pl.pallas_call(
        body, name=name, grid=(L // tm,), in_specs=[row, row, wsp, one],
        out_specs=(row, wsp, one),
        out_shape=(jax.ShapeDtypeStruct((L, D), F32), jax.ShapeDtypeStruct((4, PG, PG), F32),
                   jax.ShapeDtypeStruct((1, D), F32)),
        compiler_params=_params(1))(dm, mix, wp, scale)


def _pool_bwd(dmix, h, gain, resid, name):
    L = h.shape[0]
    tm = _tile(L)
    nt = L // tm
    last_h = L // HALO - 1

    def body(dm_ref, dmn_ref, h_ref, g_ref, r_ref, dh_ref, dg_ref, buf):
        t = pl.program_id(0)
        pos = t * tm + lax.broadcasted_iota(jnp.int32, (tm, 1), 0)
        posn = (t + 1) * tm + lax.broadcasted_iota(jnp.int32, (HALO, 1), 0)
        dmv = dm_ref[...]
        dmn = dmn_ref[...]
        for gi, w in enumerate(POOL_WINDOWS):
            sl = slice(gi * PG, (gi + 1) * PG)
            buf[pl.ds(0, tm), sl] = dmv[:, sl] * _count_inv(pos, w)
            buf[pl.ds(tm, HALO), sl] = jnp.where(t < nt - 1, dmn[:, sl] * _count_inv(posn, w), 0.0)
        parts = []
        for gi, w in enumerate(POOL_WINDOWS):
            cols = pl.ds(gi * PG, PG)
            s = buf[pl.ds(0, tm), cols]
            for k in range(1, w):
                s = s + buf[pl.ds(k, tm), cols]
            parts.append(s)
        da = jnp.concatenate(parts, axis=1) - dmv
        x = h_ref[...]
        dx, dg = _rms_bwd(da, x, g_ref[...], _rstd(x))
        dh_ref[...] = r_ref[...] + dx
        _acc(dg_ref, dg, t == 0)

    row = pl.BlockSpec((tm, D), lambda t: (t, 0))
    one = pl.BlockSpec((1, D), lambda t: (0, 0))
    return pl.pallas_call(
        body, name=name, grid=(nt,),
        in_specs=[row, pl.BlockSpec((HALO, D), lambda t: (jnp.minimum((t + 1) * (tm // HALO), last_h), 0)),
                  row, one, row],
        out_specs=(row, one),
        out_shape=(jax.ShapeDtypeStruct((L, D), F32), jax.ShapeDtypeStruct((1, D), F32)),
        scratch_shapes=[pltpu.VMEM((tm + HALO, D), F32)], compiler_params=_params(1))(dmix, dmix, h, gain, resid)


def _bucket_np(d):
    d = np.maximum(d, 0)
    df = np.maximum(d, 1).astype(np.float32)
    large = 16 + (np.log(df / np.float32(16)) / np.float32(math.log(128 / 16)) * np.float32(16)).astype(np.int32)
    return np.where(d < 16, d, np.minimum(large, N_BUCKETS - 1))


def _bias_index():
    iq = np.arange(BLK)[:, None]
    jk = np.arange(2 * BLK)[None, :]
    dist = BLK + iq - jk
    band = _bucket_np(dist)
    ok = (dist >= 0) & (dist < BLK)
    band1 = np.where(ok, band, N_BUCKETS)
    band0 = np.where(ok & (jk >= BLK), band, N_BUCKETS)
    im = np.arange(N_META)[None, :]
    unused = np.full((BLK, BLK - N_META), N_BUCKETS)
    var0 = np.concatenate([_bucket_np(N_META + iq - im), unused, band0], axis=1)
    var1 = np.concatenate([_bucket_np(N_META + BLK + iq - im), unused, band1], axis=1)
    dm = np.arange(N_META)[:, None] - im
    mm = np.where(dm >= 0, _bucket_np(dm), N_BUCKETS)
    segs = [var0, var1, mm]
    return np.concatenate([s.reshape(-1) for s in segs]).astype(np.int32), [s.shape for s in segs]


P_CHUNK = 9856


def _onehot(idx_ref, grad):
    rows = lax.broadcasted_iota(jnp.int32, (128, P_CHUNK), 0)
    hit = (rows == idx_ref[...]).astype(F32)
    return jnp.where(rows == N_BUCKETS, -1.0, hit) if grad else hit


def _bias_lookup(table_aug, idx, name):
    P = idx.shape[1]

    def body(t_ref, idx_ref, o_ref):
        o_ref[...] = lax.dot_general(t_ref[...], _onehot(idx_ref, False), _NN, precision=lax.Precision.HIGHEST,
                                     preferred_element_type=F32)

    return pl.pallas_call(
        body, name=name, grid=(P // P_CHUNK,),
        in_specs=[pl.BlockSpec((H, 128), lambda i: (0, 0)), pl.BlockSpec((1, P_CHUNK), lambda i: (0, i))],
        out_specs=pl.BlockSpec((H, P_CHUNK), lambda i: (0, i)),
        out_shape=jax.ShapeDtypeStruct((H, P), F32), compiler_params=_params(1))(table_aug, idx)


def _bias_lookup_bwd(dbias, idx, name):
    rows, P = dbias.shape

    def body(d_ref, idx_ref, o_ref):
        part = lax.dot_general(d_ref[...], _onehot(idx_ref, True), _NT, precision=lax.Precision.HIGHEST,
                               preferred_element_type=F32)
        _acc(o_ref, part, pl.program_id(0) == 0)

    return pl.pallas_call(
        body, name=name, grid=(P // P_CHUNK,),
        in_specs=[pl.BlockSpec((rows, P_CHUNK), lambda i: (0, i)), pl.BlockSpec((1, P_CHUNK), lambda i: (0, i))],
        out_specs=pl.BlockSpec((rows, 128), lambda i: (0, 0)),
        out_shape=jax.ShapeDtypeStruct((rows, 128), F32), compiler_params=_params(1))(dbias, idx)


def _probs(q, kbt, bias, sink):
    s = _dot(q, kbt) + bias
    m = jnp.maximum(jnp.max(s, axis=-1, keepdims=True), sink)
    e = jnp.exp(s - m)
    return e * (1.0 / (jnp.sum(e, axis=-1, keepdims=True) + jnp.exp(sink - m)))


def _attn_specs(nb):
    def cur(kh, n):
        return jnp.minimum(n, nb - 1)

    def prev(kh, n):
        return jnp.maximum(jnp.minimum(n, nb - 1) - 1, 0)

    def rows(heads, blk):
        return pl.BlockSpec((pl.Element(heads), pl.Element(BLK), pl.Element(HD)),
                            lambda kh, n: (kh * heads, pl.multiple_of(N_META + blk(kh, n) * BLK, N_META), 0))

    return dict(
        q=rows(GRP, cur),
        qo=pl.BlockSpec((GRP, BLK, HD), lambda kh, n: (kh, cur(kh, n), 0)),
        qt=pl.BlockSpec((GRP, HD, BLK), lambda kh, n: (kh, 0, cur(kh, n))),
        cur=rows(1, cur),
        prev=rows(1, prev),
        meta=pl.BlockSpec((None, BLK, HD), lambda kh, n: (kh, 0, 0)),
        curt=pl.BlockSpec((None, HD, BLK), lambda kh, n: (kh, 0, cur(kh, n))),
        prevt=pl.BlockSpec((None, HD, BLK), lambda kh, n: (kh, 0, prev(kh, n))),
        metat=pl.BlockSpec((None, HD, BLK), lambda kh, n: (kh, 0, 0)),
        bias=pl.BlockSpec((None, GRP, BLK, NKEY), lambda kh, n: (jnp.minimum(n, 1), kh, 0, 0)),
        sink=pl.BlockSpec((None, GRP * BLK, 1), lambda kh, n: (kh, 0, 0)))


def _attn_fwd(q, k, km, v, vm, band, bmeta, sink, name, carry=None):
    S = q.shape[1] - N_META
    nb = S // BLK
    sp = _attn_specs(nb)
    var = lambda kh, n: (jnp.minimum(n, 1), kh, 0, 0)
    meta = pl.BlockSpec((None, N_META, HD), lambda kh, n: (kh, 0, 0))

    def body(q_ref, kc_ref, kp_ref, vc_ref, vp_ref, km_ref, vm_ref, band_ref, bm_ref, sink_ref, o_ref):
        qv = q_ref[...].reshape(GRP * BLK, HD)
        kb = jnp.concatenate([kp_ref[0], kc_ref[0]], axis=0)
        vb = jnp.concatenate([vp_ref[0], vc_ref[0]], axis=0)
        sink = sink_ref[...]
        s_b = _dot(qv, kb, _NT) + band_ref[...].reshape(GRP * BLK, 2 * BLK)
        s_m = _dot(qv, km_ref[...], _NT) + bm_ref[...].reshape(GRP * BLK, N_META)
        m = jnp.maximum(jnp.maximum(jnp.max(s_b, axis=-1, keepdims=True), jnp.max(s_m, axis=-1, keepdims=True)), sink)
        e_b = jnp.exp(s_b - m)
        e_m = jnp.exp(s_m - m)
        inv = 1.0 / (jnp.sum(e_b, axis=-1, keepdims=True) + jnp.sum(e_m, axis=-1, keepdims=True) + jnp.exp(sink - m))
        o = _dot((e_b * inv).astype(BF16), vb) + _dot((e_m * inv).astype(BF16), vm_ref[...])
        o_ref[...] = o.reshape(GRP, BLK, HD).astype(BF16)

    return _carried_call(
        body, carry, name=name, grid=(KV, nb),
        in_specs=[sp['q'], sp['cur'], sp['prev'], sp['cur'], sp['prev'], meta, meta,
                  pl.BlockSpec((None, GRP, BLK, 2 * BLK), var), pl.BlockSpec((None, GRP, BLK, N_META), var),
                  sp['sink']],
        out_specs=[sp['qo']], out_shape=[jax.ShapeDtypeStruct((H, S, HD), BF16)], scratch_shapes=[],
        compiler_params=_params(2), operands=[q, k, k, v, v, km, vm, band, bmeta, sink])


def _attn_bwd(q, qt, k, km, kt, kmt, vt, vmt, bias, sink, do, dot_, name, carry=None):
    S = q.shape[1] - N_META
    nb = S // BLK
    sp = _attn_specs(nb)

    def body(q_ref, qt_ref, kc_ref, kp_ref, km_ref, kct_ref, kpt_ref, kmt_ref, vct_ref, vpt_ref, vmt_ref,
             bias_ref, sink_ref, do_ref, dot_ref, dq_ref, dkt_ref, dvt_ref, dkmt_ref, dvmt_ref, dbias_ref, ck, cv):
        n = pl.program_id(1)

        @pl.when(n < nb)
        def _():
            kbt = jnp.concatenate([kmt_ref[...], kpt_ref[...], kct_ref[...]], axis=1)
            vbt = jnp.concatenate([vmt_ref[...], vpt_ref[...], vct_ref[...]], axis=1)
            kb = jnp.concatenate([km_ref[...], kp_ref[0], kc_ref[0]], axis=0)
            p = _probs(q_ref[...].reshape(GRP * BLK, HD), kbt, bias_ref[...].reshape(GRP * BLK, NKEY), sink_ref[...])
            dp = _dot(do_ref[...].reshape(GRP * BLK, HD), vbt)
            ds = p * (dp - jnp.sum(p * dp, axis=-1, keepdims=True))
            _acc(dbias_ref, ds.reshape(GRP, BLK, NKEY), n <= 1)
            ds16 = ds.astype(BF16)
            dq_ref[...] = _dot(ds16, kb).reshape(GRP, BLK, HD).astype(BF16)
            qtv = jnp.concatenate([qt_ref[g] for g in range(GRP)], axis=1)
            dotv = jnp.concatenate([dot_ref[g] for g in range(GRP)], axis=1)
            dkt = _dot(qtv, ds16)
            dvt = _dot(dotv, p.astype(BF16))
            _acc(dkmt_ref, dkt[:, 0:BLK], n == 0)
            _acc(dvmt_ref, dvt[:, 0:BLK], n == 0)

            @pl.when(n >= 1)
            def _():
                dkt_ref[...] = (ck[...] + dkt[:, BLK:2 * BLK]).astype(BF16)
                dvt_ref[...] = (cv[...] + dvt[:, BLK:2 * BLK]).astype(BF16)

            ck[...] = dkt[:, 2 * BLK:3 * BLK]
            cv[...] = dvt[:, 2 * BLK:3 * BLK]

        @pl.when(n == nb)
        def _():
            dkt_ref[...] = ck[...].astype(BF16)
            dvt_ref[...] = cv[...].astype(BF16)

    kvout = pl.BlockSpec((None, HD, BLK), lambda kh, n: (kh, 0, jnp.maximum(n - 1, 0)))
    return _carried_call(
        body, carry, name=name, grid=(KV, nb + 1),
        in_specs=[sp['q'], sp['qt'], sp['cur'], sp['prev'], sp['meta'], sp['curt'], sp['prevt'], sp['metat'],
                  sp['curt'], sp['prevt'], sp['metat'], sp['bias'], sp['sink'], sp['q'], sp['qt']],
        out_specs=[sp['qo'], kvout, kvout, sp['metat'], sp['metat'], sp['bias']],
        out_shape=[jax.ShapeDtypeStruct((H, S, HD), BF16), jax.ShapeDtypeStruct((KV, HD, S), BF16),
                   jax.ShapeDtypeStruct((KV, HD, S), BF16), jax.ShapeDtypeStruct((KV, HD, BLK), F32),
                   jax.ShapeDtypeStruct((KV, HD, BLK), F32), jax.ShapeDtypeStruct((2, H, BLK, NKEY), F32)],
        scratch_shapes=[pltpu.VMEM((HD, BLK), F32), pltpu.VMEM((HD, BLK), F32)],
        compiler_params=_params(2), operands=[q, qt, k, k, km, kt, kt, kmt, vt, vt, vmt, bias, sink, do, dot_])


def _meta_softmax(q, k, bias, sink):
    s = _dot(q, k, _NT) + bias
    m = jnp.maximum(jnp.max(s, axis=-1, keepdims=True), sink)
    e = jnp.exp(s - m)
    return e * (1.0 / (jnp.sum(e, axis=-1, keepdims=True) + jnp.exp(sink - m)))


def _attn_meta_fwd(qm, km, vm, bias, sink, name):
    def body(q_ref, k_ref, v_ref, b_ref, s_ref, o_ref):
        for h in range(H):
            p = _meta_softmax(q_ref[h], k_ref[h // GRP], b_ref[h], s_ref[h])
            o_ref[h] = _dot(p.astype(BF16), v_ref[h // GRP]).astype(BF16)

    return pl.pallas_call(body, name=name, out_shape=jax.ShapeDtypeStruct((H, N_META, HD), BF16))(
        qm, km, vm, bias, sink)


def _attn_meta_bwd(qm, km, vm, bias, sink, do, name):
    def body(q_ref, k_ref, v_ref, b_ref, s_ref, do_ref, dq_ref, dk_ref, dv_ref, db_ref):
        for kh in range(KV):
            k, v = k_ref[kh], v_ref[kh]
            dk = jnp.zeros((N_META, HD), F32)
            dv = jnp.zeros((N_META, HD), F32)
            for g in range(GRP):
                h = kh * GRP + g
                q, dov = q_ref[h], do_ref[h]
                p = _meta_softmax(q, k, b_ref[h], s_ref[h])
                dp = _dot(dov, v, _NT)
                ds = p * (dp - jnp.sum(p * dp, axis=-1, keepdims=True))
                db_ref[h] = ds
                ds16 = ds.astype(BF16)
                dq_ref[h] = _dot(ds16, k).astype(BF16)
                dk = dk + _dot(ds16, q, _TN)
                dv = dv + _dot(p.astype(BF16), dov, _TN)
            dk_ref[kh] = dk
            dv_ref[kh] = dv

    return pl.pallas_call(
        body, name=name,
        out_shape=(jax.ShapeDtypeStruct((H, N_META, HD), BF16), jax.ShapeDtypeStruct((KV, N_META, HD), F32),
                   jax.ShapeDtypeStruct((KV, N_META, HD), F32), jax.ShapeDtypeStruct((H, N_META, N_META), F32)))(
        qm, km, vm, bias, sink, do)


def _unheads(t):
    return jnp.transpose(t, (1, 0, 2)).reshape(t.shape[1], t.shape[0] * HD)


def _unheads_t(t):
    return jnp.transpose(t, (2, 0, 1)).reshape(t.shape[2], t.shape[0] * HD)


def _pad_block(t, axis):
    pad = [(0, 0)] * t.ndim
    pad[axis] = (0, BLK - N_META)
    return jnp.pad(t, pad)


def _adam(w, g, m, v, name):
    lead, R, C = w.shape
    tr = _tile(R, 256) if R % 16 == 0 else R

    def body(w_ref, g_ref, m_ref, v_ref, d_ref, mo_ref, vo_ref, go_ref):
        gv = g_ref[...]
        go_ref[...] = gv
        mn = B1 * m_ref[...] + (1.0 - B1) * gv
        vn = B2 * v_ref[...] + (1.0 - B2) * (gv * gv)
        m_hat = mn / (1.0 - B1 ** STEP)
        v_hat = vn / (1.0 - B2 ** STEP)
        d_ref[...] = -LR * (m_hat / (jnp.sqrt(v_hat) + ADAM_EPS) + WD * w_ref[...])
        mo_ref[...] = mn
        vo_ref[...] = vn

    blk = pl.BlockSpec((None, tr, C), lambda l, i: (l, i, 0))
    shp = jax.ShapeDtypeStruct((lead, R, C), F32)
    return pl.pallas_call(body, name=name, grid=(lead, R // tr), in_specs=[blk] * 4, out_specs=(blk,) * 4,
                          out_shape=(shp,) * 4, compiler_params=_params(2))(w, g, m, v)


def _sum_leaf(lands, pends, ids, name):
    n = len(lands)
    _, R, C = lands[0].shape
    tr = _tile(R, 128)

    def body(ids_ref, *refs):
        out_ref = refs[2 * n]
        for k in range(n):
            acc = refs[n + k][...].astype(F32)
            for s in range(7):
                acc = acc + refs[k][s].astype(F32)
            out_ref[k] = acc

    grid_spec = pltpu.PrefetchScalarGridSpec(
        num_scalar_prefetch=1, grid=(R // tr,),
        in_specs=[pl.BlockSpec((7, tr, C), lambda i, ids: (0, i, 0))] * n
        + [pl.BlockSpec((None, None, tr, C), lambda i, ids: (ids[0], ids[1], i, 0))] * n,
        out_specs=pl.BlockSpec((n, None, tr, C), lambda i, ids: (0, ids[1], i, 0)))
    return pl.pallas_call(body, name=name, grid_spec=grid_spec, out_shape=jax.ShapeDtypeStruct((n, 2, R, C), F32),
                          compiler_params=_params(1))(ids, *lands, *pends)


def _place():
    x, y, c = lax.axis_index("x"), lax.axis_index("y"), lax.axis_index("c")
    chips = [(1 - x, y), (x, 1 - y), (1 - x, 1 - y)]
    return x, y, c, chips


def _rcopy(src, dst, ssem, rsem, dev):
    return pltpu.make_async_remote_copy(src_ref=src, dst_ref=dst, send_sem=ssem, recv_sem=rsem,
                                        device_id=dev, device_id_type=MESH)


class _Carry:
    def __init__(self, kind, arrays):
        self.kind, self.arrays, self.n = kind, list(arrays), len(arrays)
        self.per = 3 if kind == 'gather' else 7
        if kind == 'gather':
            self.out_shape = [jax.ShapeDtypeStruct((NSH,) + a.shape, a.dtype) for a in self.arrays]
        else:
            self.out_shape = [jax.ShapeDtypeStruct((7,) + a.shape[2:], a.dtype) for a in self.arrays]
        dma = pltpu.SemaphoreType.DMA
        self.scratch = [dma((self.per * self.n,)), dma((self.per * self.n,)), dma((self.n,))]
        if kind == 'gather':
            self.scratch += [pltpu.VMEM(a.shape, a.dtype) for a in self.arrays]

    def _copies(self, cin, cout, scr):
        ssem, rsem, loc = scr[:3]
        x, y, c, chips = _place()
        local, sends, recvs = [], [], []
        for k in range(self.n):
            if self.kind == 'gather':
                me = 2 * x + y
                local.append((pltpu.make_async_copy(cin[k], scr[3 + k], loc.at[k]),
                              pltpu.make_async_copy(scr[3 + k], cout[k].at[me], loc.at[k])))
                for j, (cx, cy) in enumerate(chips):
                    i = 3 * k + j
                    sends.append(_rcopy(cin[k], cout[k].at[me], ssem.at[i], rsem.at[i], (cx, cy, c)))
                    got = cout[k].at[2 * cx + cy]
                    recvs.append(_rcopy(got, got, ssem.at[i], rsem.at[i], (cx, cy, c)))
            else:
                for f in range(1, 8):
                    px = 1 - x if (f >> 2) & 1 else x
                    py = 1 - y if (f >> 1) & 1 else y
                    pc = 1 - c if f & 1 else c
                    i = 7 * k + f - 1
                    got = cout[k].at[f - 1]
                    sends.append(_rcopy(cin[k].at[2 * px + py, pc], got, ssem.at[i], rsem.at[i], (px, py, pc)))
                    recvs.append(_rcopy(got, got, ssem.at[i], rsem.at[i], (px, py, pc)))
        return local, sends, recvs

    def start(self, cin, cout, scr):
        local, sends, _ = self._copies(cin, cout, scr)
        for cp in [to_vmem for to_vmem, _ in local] + sends:
            cp.start()

    def finish(self, cin, cout, scr):
        local, sends, recvs = self._copies(cin, cout, scr)
        for to_vmem, to_slot in local:
            to_vmem.wait()
            to_slot.start()
        for cp in recvs:
            cp.wait_recv()
        for cp in sends:
            cp.wait_send()
        for _, to_slot in local:
            to_slot.wait()


def _carried_call(body, carry, *, name, grid, in_specs, out_specs, out_shape, scratch_shapes, compiler_params,
                  operands):
    n_in, n_out = len(in_specs), len(out_specs)
    if carry is None:
        return pl.pallas_call(body, name=name, grid=grid, in_specs=in_specs, out_specs=out_specs,
                              out_shape=out_shape, scratch_shapes=scratch_shapes,
                              compiler_params=compiler_params)(*operands), []
    m = carry.n

    def full(*refs):
        ins, cin = refs[:n_in], refs[n_in:n_in + m]
        outs, cout = refs[n_in + m:n_in + m + n_out], refs[n_in + m + n_out:n_in + 2 * m + n_out]
        own = len(refs) - len(carry.scratch)
        scr, sems = refs[n_in + 2 * m + n_out:own], refs[own:]
        ids = [pl.program_id(a) for a in range(len(grid))]
        first = functools.reduce(jnp.logical_and, [i == 0 for i in ids])
        last = functools.reduce(jnp.logical_and, [i == g - 1 for i, g in zip(ids, grid)])

        @pl.when(first)
        def _():
            carry.start(cin, cout, sems)

        body(*ins, *outs, *scr)

        @pl.when(last)
        def _():
            carry.finish(cin, cout, sems)

    res = pl.pallas_call(
        full, name=name, grid=grid, in_specs=list(in_specs) + [ANY] * m, out_specs=list(out_specs) + [ANY] * m,
        out_shape=list(out_shape) + carry.out_shape, scratch_shapes=list(scratch_shapes) + carry.scratch,
        compiler_params=compiler_params)(*operands, *carry.arrays)
    return res[:n_out], res[n_out:]


def _flush(carry, name):
    m = carry.n

    def body(*refs):
        cin, cout, sems = refs[:m], refs[m:2 * m], refs[2 * m:]
        carry.start(cin, cout, sems)
        carry.finish(cin, cout, sems)

    return pl.pallas_call(body, name=name, in_specs=[ANY] * m, out_specs=[ANY] * m, out_shape=carry.out_shape,
                          scratch_shapes=carry.scratch)(*carry.arrays)


def _pair_share(leaves):
    n = len(leaves)

    def body(*refs):
        ins, outs = refs[:n], refs[n:2 * n]
        ssem, rsem = refs[2 * n:]
        x, y, c, _ = _place()
        sib = (x, y, 1 - c)
        cps = [_rcopy(ins[k].at[:, c], outs[k].at[:, c], ssem.at[k], rsem.at[k], sib) for k in range(n)]
        for cp in cps:
            cp.start()
        for k in range(n):
            got = outs[k].at[:, 1 - c]
            _rcopy(got, got, ssem.at[k], rsem.at[k], sib).wait_recv()
        for cp in cps:
            cp.wait_send()

    dma = pltpu.SemaphoreType.DMA
    return pl.pallas_call(
        body, name="grad_pair_share", in_specs=[ANY] * n, out_specs=[ANY] * n,
        out_shape=[jax.ShapeDtypeStruct(a.shape, a.dtype) for a in leaves],
        input_output_aliases={k: k for k in range(n)},
        scratch_shapes=[dma((n,)), dma((n,))])(*leaves)


def _allreduce_small(pack):
    R = pack.shape[0]

    def body(in_ref, out_ref, buf, ssem, rsem):
        x, y, c, _ = _place()
        me = 4 * x + 2 * y + c
        buf[me] = in_ref[...]
        peers = []
        for k in range(1, 8):
            fx, fy, fc = (k >> 2) & 1, (k >> 1) & 1, k & 1
            peers.append((1 - x if fx else x, 1 - y if fy else y, 1 - c if fc else c))
        cps = [_rcopy(in_ref, buf.at[me], ssem.at[k], rsem.at[k], p) for k, p in enumerate(peers)]
        for cp in cps:
            cp.start()
        for k, (px, py, pc) in enumerate(peers):
            got = buf.at[4 * px + 2 * py + pc]
            _rcopy(got, got, ssem.at[k], rsem.at[k], (px, py, pc)).wait_recv()
        for cp in cps:
            cp.wait_send()
        acc = buf[0]
        for s in range(1, 8):
            acc = acc + buf[s]
        out_ref[...] = acc

    dma = pltpu.SemaphoreType.DMA
    return pl.pallas_call(
        body, name="allreduce_small", out_shape=jax.ShapeDtypeStruct(pack.shape, F32),
        in_specs=[pl.BlockSpec(memory_space=pltpu.VMEM)], out_specs=pl.BlockSpec(memory_space=pltpu.VMEM),
        scratch_shapes=[pltpu.VMEM((8, R, 128), F32), dma((7,)), dma((7,))])(pack)


def _pack(arrs):
    flat = jnp.concatenate([a.reshape(-1).astype(F32) for a in arrs])
    n = flat.shape[0]
    rows = -(-n // 1024) * 8
    return jnp.pad(flat, (0, rows * 128 - n)).reshape(rows, 128)


def _unpack(pack, shapes):
    flat, out, o = pack.reshape(-1), [], 0
    for s in shapes:
        n = int(np.prod(s))
        out.append(flat[o:o + n].reshape(s))
        o += n
    return out


def _ffn_keys(i):
    return [('ffn_w_gate', i), ('ffn_w_up', i), ('ffn_w_down', i)]


GATHER_PLAN = {'attn_fwd0': _ffn_keys(0) + [('conv_w_in', 0), ('conv_w_out', 0)],
               'ffn_fwd0': _ffn_keys(1),
               'ffn_fwd1': _ffn_keys(2) + [('pool_w', 0)],
               'ffn_fwd2': [('attn_w_qkv', 1), ('attn_w_o', 1)],
               'attn_fwd3': _ffn_keys(3)}
REDUCE_PLAN = {'attn_bwd3': _ffn_keys(3),
               'ffn_bwd_act2': [('attn_w_qkv', 1), ('attn_w_o', 1)],
               'ffn_bwd_act1': _ffn_keys(2) + [('pool_w', 0)],
               'ffn_bwd_act0': _ffn_keys(1),
               'ffn_bwd_w0': [('conv_w_in', 0), ('conv_w_out', 0)],
               'attn_bwd0': _ffn_keys(0) + [('attn_w_o', 0)],
               'qkv_bwd0': [('attn_w_qkv', 0)]}


def kernel(x, meta_tokens, rel_bias_table, norm_mix, norm_ffn, norm_final, attn_w_qkv, attn_b_qkv, attn_w_o, attn_b_o, attn_sinks, conv_w_in, conv_w, conv_w_out, pool_w, pool_scale, ffn_w_gate, ffn_w_up, ffn_w_down, loss_target, m_meta_tokens, m_rel_bias_table, m_norm_mix, m_norm_ffn, m_norm_final, m_attn_w_qkv, m_attn_b_qkv, m_attn_w_o, m_attn_b_o, m_attn_sinks, m_conv_w_in, m_conv_w, m_conv_w_out, m_pool_w, m_pool_scale, m_ffn_w_gate, m_ffn_w_up, m_ffn_w_down, v_meta_tokens, v_rel_bias_table, v_norm_mix, v_norm_ffn, v_norm_final, v_attn_w_qkv, v_attn_b_qkv, v_attn_w_o, v_attn_b_o, v_attn_sinks, v_conv_w_in, v_conv_w, v_conv_w_out, v_pool_w, v_pool_scale, v_ffn_w_gate, v_ffn_w_up, v_ffn_w_down):
    args = locals()
    w = {n: args[n] for n in WEIGHTS}
    mom = {n: args['m_' + n] for n in WEIGHTS}
    var = {n: args['v_' + n] for n in WEIGHTS}
    mx, my = lax.axis_index("x"), lax.axis_index("y")
    chip = 2 * mx + my
    S = x.shape[1]
    scale = jnp.asarray(HD ** -0.5, BF16)

    GW = {}
    pending = {}
    land = {}

    def gather_carry(call):
        keys = GATHER_PLAN.get(call)
        return _Carry('gather', [w[n][l].astype(BF16) for n, l in keys]) if keys else None

    def reduce_carry(call):
        keys = REDUCE_PLAN.get(call)
        return _Carry('reduce', [pending[k] for k in keys]) if keys else None

    def pend(n, l, g):
        pending[(n, l)] = g.reshape((NSH, 2) + HALF_SHAPE[n])

    small_in = jnp.concatenate([
        jnp.pad(w['meta_tokens'], ((0, 0), (0, 128))), w['attn_b_qkv'], jnp.pad(w['attn_b_o'], ((0, 0), (0, 128))),
        jnp.pad(w['conv_w'][0], ((0, 0), (0, 128))), jnp.pad(w['pool_scale'], ((0, 0), (0, 128)))], axis=0)
    gsmall, GW[('attn_w_qkv', 0)], GW[('attn_w_o', 0)] = _flush(
        _Carry('gather', [small_in, w['attn_w_qkv'][0].astype(BF16), w['attn_w_o'][0].astype(BF16)]), "gather_first")

    def cols(rows, width):
        return jnp.transpose(rows[:, :, :width], (1, 0, 2)).reshape(rows.shape[1], NSH * width)

    meta_full = cols(gsmall[:, 0:16], 256)
    b_qkv = cols(gsmall[:, 16:18], 384)
    b_o = cols(gsmall[:, 18:20], 256)
    conv_k = cols(gsmall[:, 20:23], 256)
    p_scale = cols(gsmall[:, 23:24], 256)

    idx_np, _ = _bias_index()
    idx = jnp.asarray(idx_np).reshape(1, -1)
    table_aug = jnp.concatenate([rel_bias_table.T, jnp.full((H, 1), NEG, F32),
                                 jnp.zeros((H, 127 - N_BUCKETS), F32)], axis=1)
    bias_flat = _bias_lookup(table_aug, idx, "bias_lookup")
    nblock = BLK * NKEY
    bias_blk = jnp.transpose(bias_flat[:, :2 * nblock].reshape(H, 2, BLK, NKEY), (1, 0, 2, 3))
    bias_mm = bias_flat[:, 2 * nblock:].reshape(H, N_META, N_META)
    bias_band, bias_meta = bias_blk[..., BLK:], bias_blk[..., :N_META]

    h = jnp.concatenate([meta_full, x[0]], axis=0)
    saved = []
    for i in range(DEPTH):
        kind, j = i % 3, i // 3
        gm = norm_mix[i:i + 1]
        st = dict(h=h)
        if kind == 0:
            w_o = GW[('attn_w_o', j)].reshape(1, D, D)
            q, k, v, a = _linear(h, GW[('attn_w_qkv', j)], gain=gm, bias=b_qkv[j:j + 1], qkv_heads=True,
                                 name=f"qkv{i}")
            tr = lambda t: jnp.swapaxes(t, 1, 2)
            lay = dict(q=q, qt=tr(q[:, N_META:]), qm=q[:, :N_META],
                       k=k, kt=tr(k[:, N_META:]), km=k[:, :N_META],
                       kmp=_pad_block(k[:, :N_META], 1), kmt=_pad_block(tr(k[:, :N_META]), 2),
                       v=v, vt=tr(v[:, N_META:]), vm=v[:, :N_META], vmt=_pad_block(tr(v[:, :N_META]), 2))
            sink_r = jnp.broadcast_to(attn_sinks[j].reshape(KV, GRP, 1, 1), (KV, GRP, BLK, 1)).reshape(KV, GRP * BLK, 1)
            sink_m = jnp.broadcast_to(attn_sinks[j].reshape(H, 1, 1), (H, N_META, 1))
            (o_r,), got = _attn_fwd(lay['q'], lay['k'], lay['km'], lay['v'], lay['vm'], bias_band, bias_meta, sink_r,
                                    f"attn_fwd{i}", gather_carry(f"attn_fwd{i}"))
            GW.update(zip(GATHER_PLAN.get(f"attn_fwd{i}", []), got))
            o_m = _attn_meta_fwd(lay['qm'], lay['km'], lay['vm'], bias_mm, sink_m, f"attn_meta_fwd{i}")
            o = _unheads(jnp.concatenate([o_m, o_r], axis=1))
            h1 = _linear(o, w_o, bias=b_o[j:j + 1], resid=h, out_dtype=F32, name=f"attn_out{i}")
            st.update(a=a, lay=lay, sinks=(sink_r, sink_m), o=o, w_o=w_o)
        elif kind == 1:
            z3, a = _linear(h, GW[('conv_w_in', j)], gain=gm, name=f"conv_in{i}")
            yv = _conv_fwd(z3, conv_k, f"conv_fwd{i}")
            w_cout = GW[('conv_w_out', j)].reshape(1, D, D)
            h1 = _linear(yv, w_cout, resid=h, out_dtype=F32, name=f"conv_out{i}")
            st.update(a=a, z3=z3, y=yv, w_cout=w_cout)
        else:
            w_pool = jnp.transpose(GW[('pool_w', j)], (1, 0, 2, 3)).reshape(4, PG, PG)
            mix = _pool_fwd(h, gm, f"pool_fwd{i}")
            h1 = _pool_out(mix, w_pool, p_scale, h, f"pool_out{i}")
            st.update(mix=mix, w_pool=w_pool)
        ffn_w = [GW[k] for k in _ffn_keys(i)]
        (hn, Gp, Up, Ap, Bp), got = _ffn_fwd(h1, norm_ffn[i:i + 1], *ffn_w, f"ffn_fwd{i}", gather_carry(f"ffn_fwd{i}"))
        GW.update(zip(GATHER_PLAN.get(f"ffn_fwd{i}", []), got))
        st.update(h1=h1, G=Gp, U=Up, A=Ap, B=Bp, ffn_w=ffn_w)
        saved.append(st)
        h = hn

    target = jnp.pad(loss_target[0], ((N_META, 0), (0, 0)))
    dh, d_nfinal, loss_part = _loss_head(h, norm_final.reshape(1, D), target)
    d_nmix, d_nffn = [None] * DEPTH, [None] * DEPTH
    d_bqkv, d_bo, dflats = [None, None], [None, None], [None, None]
    d_convk = d_pscale = None
    for i in reversed(range(DEPTH)):
        kind, j = i % 3, i // 3
        st = saved[i]
        call = f"ffn_bwd_act{i}"
        (DG, DU, dh1, d_nffn[i]), got = _ffn_bwd_act(dh, st['h1'], norm_ffn[i:i + 1], st['G'], st['U'], *st['ffn_w'],
                                                     call, reduce_carry(call))
        land.update(zip(REDUCE_PLAN.get(call, []), got))
        call = f"ffn_bwd_w{i}"
        (gw, uw, dw_), got = _ffn_bwd_w(st['B'], st['A'], DG, DU, dh, call, reduce_carry(call))
        land.update(zip(REDUCE_PLAN.get(call, []), got))
        pend('ffn_w_gate', i, gw)
        pend('ffn_w_up', i, uw)
        pend('ffn_w_down', i, dw_)
        gm = norm_mix[i:i + 1]
        if kind == 0:
            lay = st['lay']
            sink_r, sink_m = st['sinks']
            gwo, d_bo[j] = _wgrad(st['o'], dh1, mode='row', colsum=True, name=f"attn_out_wgrad{i}")
            pend('attn_w_o', j, gwo)
            do = _linear_t(dh1, st['w_o'], to_heads=True, name=f"attn_out_bwd{i}")
            call = f"attn_bwd{i}"
            (dq_r, dkt, dvt, dkmt, dvmt, dbias), got = _attn_bwd(
                lay['q'], lay['qt'], lay['k'], lay['kmp'], lay['kt'], lay['kmt'], lay['vt'], lay['vmt'], bias_blk,
                sink_r, do, jnp.swapaxes(do[:, N_META:], 1, 2), call, reduce_carry(call))
            land.update(zip(REDUCE_PLAN.get(call, []), got))
            dq_m, dkm2, dvm2, dbmm = _attn_meta_bwd(lay['qm'], lay['km'], lay['vm'], bias_mm, sink_m,
                                                    do[:, :N_META], f"attn_meta_bwd{i}")
            dflats[j] = jnp.concatenate([jnp.transpose(dbias, (1, 0, 2, 3)).reshape(H, -1), dbmm.reshape(H, -1)], axis=1)
            dkm = (jnp.transpose(dkmt[:, :, :N_META], (0, 2, 1)) + dkm2).astype(BF16)
            dvm = (jnp.transpose(dvmt[:, :, :N_META], (0, 2, 1)) + dvm2).astype(BF16)
            dqkv = jnp.concatenate([
                jnp.concatenate([_unheads(dq_m), _unheads(dq_r)], axis=0) * scale,
                jnp.concatenate([_unheads(dkm), _unheads_t(dkt)], axis=0),
                jnp.concatenate([_unheads(dvm), _unheads_t(dvt)], axis=0)], axis=1)
            gq, d_bqkv[j] = _wgrad(st['a'], dqkv, mode='col', colsum=True, name=f"qkv_wgrad{i}")
            pend('attn_w_qkv', j, gq)
            call = f"qkv_bwd{i}"
            res = _linear_t(dqkv, GW[('attn_w_qkv', j)], rms=(st['h'], gm, dh1), carry=reduce_carry(call), name=call)
            if call in REDUCE_PLAN:
                res, got = res
                land.update(zip(REDUCE_PLAN[call], got))
            dh, d_nmix[i] = res
        elif kind == 1:
            pend('conv_w_out', j, _wgrad(st['y'], dh1, mode='row', name=f"conv_out_wgrad{i}"))
            dy = _linear_t(dh1, st['w_cout'], out_dtype=F32, name=f"conv_out_bwd{i}")
            dz3, d_convk = _conv_bwd(st['z3'], conv_k, dy, f"conv_bwd{i}")
            pend('conv_w_in', j, _wgrad(st['a'], dz3, mode='col', name=f"conv_in_wgrad{i}"))
            dh, d_nmix[i] = _linear_t(dz3, GW[('conv_w_in', j)], rms=(st['h'], gm, dh1), name=f"conv_in_bwd{i}")
        else:
            dmix, dwp, d_pscale = _pool_out_bwd(dh1, st['mix'], st['w_pool'], p_scale, f"pool_out_bwd{i}")
            pend('pool_w', j, jnp.transpose(dwp.reshape(4, NSH, PG // NSH, PG), (1, 0, 2, 3)).astype(BF16))
            dh, d_nmix[i] = _pool_bwd(dmix, st['h'], gm, dh1, f"pool_bwd{i}")

    dts = _bias_lookup_bwd(jnp.concatenate(dflats, axis=0), idx, "bias_lookup_bwd")
    d_table = dts[:H, :N_BUCKETS].T + dts[H:, :N_BUCKETS].T
    d_sinks = dts[:, N_BUCKETS].reshape(2, H)

    ids = jnp.stack([chip, lax.axis_index("c")]).astype(jnp.int32)
    shared = _pair_share([_sum_leaf([land[(n, l)] for l in range(w[n].shape[0])],
                                    [pending[(n, l)] for l in range(w[n].shape[0])], ids, f"sum_{n}") for n in BIG])
    grads = {n: s.reshape(w[n].shape) for n, s in zip(BIG, shared)}

    small_full = [dh[:N_META], d_table, jnp.concatenate(d_nmix, axis=0), jnp.concatenate(d_nffn, axis=0), d_nfinal,
                  jnp.concatenate(d_bqkv, axis=0), jnp.concatenate(d_bo, axis=0), d_sinks, d_convk,
                  d_pscale, loss_part[:, 0:1]]
    red = _unpack(_allreduce_small(_pack(small_full)), [a.shape for a in small_full])
    g_meta, g_table, g_nmix, g_nffn, g_nfinal, g_bqkv, g_bo, g_sinks, g_convk, g_pscale, loss = red

    def shard(a, width):
        return lax.dynamic_slice_in_dim(a, chip * width, width, axis=1)

    grads.update(meta_tokens=shard(g_meta, 256), rel_bias_table=g_table, norm_mix=g_nmix, norm_ffn=g_nffn,
                 norm_final=g_nfinal.reshape(D), attn_b_qkv=shard(g_bqkv, 384), attn_b_o=shard(g_bo, 256),
                 attn_sinks=g_sinks, conv_w=shard(g_convk, 256)[None], pool_scale=shard(g_pscale, 256))

    delta, new_m, new_v = {}, {}, {}
    for n in BIG:
        shp = w[n].shape
        r3 = (int(np.prod(shp[:-2])),) + shp[-2:]
        dl, mn, vn, gn = _adam(w[n].reshape(r3), grads[n].reshape(r3), mom[n].reshape(r3), var[n].reshape(r3),
                               f"adam_{n}")
        delta[n], new_m[n], new_v[n], grads[n] = dl.reshape(shp), mn.reshape(shp), vn.reshape(shp), gn.reshape(shp)
    shapes = [w[n].shape for n in SMALL]
    packed = [_pack([d[n] for n in SMALL])[None] for d in (w, grads, mom, var)]
    for dst, res in zip((delta, new_m, new_v), _adam(*packed, "adam_small")[:3]):
        dst.update(zip(SMALL, _unpack(res[0], shapes)))

    return (loss.reshape(()), dh[N_META:][None], *[grads[n] for n in WEIGHTS], *[delta[n] for n in WEIGHTS],
            *[new_m[n] for n in WEIGHTS], *[new_v[n] for n in WEIGHTS])
```

```python
import functools
import math

import numpy as np
import jax
import jax.numpy as jnp
from jax import lax
from jax.experimental import pallas as pl
from jax.experimental.pallas import tpu as pltpu

F32, BF16 = jnp.float32, jnp.bfloat16
D = 1024
N_META = 16
EPS = 1e-6
H, KV, GRP, HD = 16, 4, 4, 64
BLK = 128
NKEY = 3 * BLK
N_BUCKETS = 32
POOL_WINDOWS = (2, 4, 8, 16)
PG = 256
DFF = 2816
NSH = 4
FC = DFF // NSH
HALO = 16
NEG = -1e30
DEPTH = 4
LR, B1, B2, ADAM_EPS, WD, STEP = 0.001, 0.9, 0.999, 1e-08, 0.01, 10
MESH = pl.DeviceIdType.MESH
ANY = pl.BlockSpec(memory_space=pl.ANY)
VMEM_LIMIT = 48 * 1024 * 1024
VMEM_BIG = 58 * 1024 * 1024
BIG_TILE = 1024

_NN = (((1,), (0,)), ((), ()))
_NT = (((1,), (1,)), ((), ()))
_TN = (((0,), (0,)), ((), ()))

WEIGHTS = ['meta_tokens', 'rel_bias_table', 'norm_mix', 'norm_ffn', 'norm_final', 'attn_w_qkv', 'attn_b_qkv',
           'attn_w_o', 'attn_b_o', 'attn_sinks', 'conv_w_in', 'conv_w', 'conv_w_out', 'pool_w', 'pool_scale',
           'ffn_w_gate', 'ffn_w_up', 'ffn_w_down']
BIG = ['attn_w_qkv', 'attn_w_o', 'conv_w_in', 'conv_w_out', 'pool_w', 'ffn_w_gate', 'ffn_w_up', 'ffn_w_down']
SMALL = [w for w in WEIGHTS if w not in BIG]
HALF_SHAPE = {'attn_w_qkv': (512, 384), 'attn_w_o': (128, 1024), 'conv_w_in': (512, 768),
              'conv_w_out': (128, 1024), 'pool_w': (128, 256), 'ffn_w_gate': (512, 704),
              'ffn_w_up': (512, 704), 'ffn_w_down': (352, 1024)}


def _dot(a, b, dims=_NN):
    return lax.dot_general(a, b, dims, preferred_element_type=F32)


def _tile(n, cap=512):
    best = None
    for t in range(16, min(n, cap) + 1, 16):
        if n % t == 0:
            best = t
    assert best is not None, n
    return best


def _params(n_axes, vmem=VMEM_LIMIT):
    return pltpu.CompilerParams(dimension_semantics=("arbitrary",) * n_axes, vmem_limit_bytes=vmem)


def _rstd(x):
    return lax.rsqrt(jnp.mean(x * x, axis=-1, keepdims=True) + EPS)


def _rms_bwd(dy, x, g, r):
    u = dy * g
    dx = r * u - x * ((r * r * r) * (jnp.sum(x * u, axis=-1, keepdims=True) * (1.0 / D)))
    return dx, jnp.sum(dy * (x * r), axis=0, keepdims=True)


def _sigmoid(x):
    return 1.0 / (1.0 + jnp.exp(-x))


def _acc(ref, val, first):
    @pl.when(first)
    def _():
        ref[...] = val

    @pl.when(jnp.logical_not(first))
    def _():
        ref[...] += val


def _linear(x, w, *, gain=None, bias=None, resid=None, out_dtype=BF16, qkv_heads=False, cap=BIG_TILE, name):
    L, K = x.shape
    J, _, Nc = w.shape
    tm = _tile(L, cap)
    has_g, has_b, has_r = gain is not None, bias is not None, resid is not None
    n_main = 3 if qkv_heads else 1

    def body(*refs):
        refs = list(refs)
        x_ref, w_ref = refs[:2]
        i = 2
        g_ref = b_ref = r_ref = None
        if has_g:
            g_ref, i = refs[i], i + 1
        if has_b:
            b_ref, i = refs[i], i + 1
        if has_r:
            r_ref, i = refs[i], i + 1
        out_ref = refs[i]
        if has_g:
            xf = x_ref[...]
            xv = (xf * _rstd(xf) * g_ref[...]).astype(BF16)
            refs[i + n_main][...] = xv
        else:
            xv = x_ref[...]
        for s in range(J):
            sl = slice(s * Nc, (s + 1) * Nc)
            acc = _dot(xv, w_ref[s])
            if has_b:
                acc = acc + b_ref[:, sl]
            if has_r:
                acc = acc + r_ref[:, sl]
            if not qkv_heads:
                out_ref[:, sl] = acc.astype(out_dtype)
                continue
            for r in range(Nc // HD):
                hd, blk = s * (Nc // HD) + r, acc[:, r * HD:(r + 1) * HD]
                if hd < H:
                    refs[i][hd] = (blk * HD ** -0.5).astype(BF16)
                elif hd < H + KV:
                    refs[i + 1][hd - H] = blk.astype(BF16)
                else:
                    refs[i + 2][hd - H - KV] = blk.astype(BF16)

    row = lambda n: pl.BlockSpec((tm, n), lambda t: (t, 0))
    one = lambda n: pl.BlockSpec((1, n), lambda t: (0, 0))
    in_specs = [row(K), pl.BlockSpec((J, K, Nc), lambda t: (0, 0, 0))]
    ops = [x, w]
    if has_g:
        in_specs.append(one(K))
        ops.append(gain)
    if has_b:
        in_specs.append(one(J * Nc))
        ops.append(bias)
    if has_r:
        in_specs.append(row(J * Nc))
        ops.append(resid)
    if qkv_heads:
        heads = lambda n: pl.BlockSpec((n, tm, HD), lambda t: (0, t, 0))
        out_specs = [heads(H), heads(KV), heads(KV)]
        out_shape = [jax.ShapeDtypeStruct((n, L, HD), BF16) for n in (H, KV, KV)]
    else:
        out_specs, out_shape = [row(J * Nc)], [jax.ShapeDtypeStruct((L, J * Nc), out_dtype)]
    if has_g:
        out_specs, out_shape = out_specs + [row(K)], out_shape + [jax.ShapeDtypeStruct((L, K), BF16)]
    res = pl.pallas_call(body, name=name, grid=(L // tm,), in_specs=in_specs, out_specs=out_specs,
                         out_shape=out_shape, compiler_params=_params(1, VMEM_BIG))(*ops)
    return res[0] if len(res) == 1 else res


def _linear_t(dy, w, *, out_dtype=BF16, rms=None, to_heads=False, carry=None, cap=BIG_TILE, name):
    L = dy.shape[0]
    J, K, Nc = w.shape
    tm = _tile(L, cap)
    has_rms = rms is not None

    def body(*refs):
        t = pl.program_id(0)
        dy_ref, w_ref = refs[:2]
        acc = _dot(dy_ref[:, 0:Nc].astype(BF16), w_ref[0], _NT)
        for s in range(1, J):
            acc = acc + _dot(dy_ref[:, s * Nc:(s + 1) * Nc].astype(BF16), w_ref[s], _NT)
        if has_rms:
            h_ref, g_ref, r_ref, out_ref, dg_ref = refs[2:7]
            x = h_ref[...]
            dx, dg = _rms_bwd(acc, x, g_ref[...], _rstd(x))
            out_ref[...] = r_ref[...] + dx
            _acc(dg_ref, dg, t == 0)
        elif to_heads:
            for hd in range(K // HD):
                refs[2][hd] = acc[:, hd * HD:(hd + 1) * HD].astype(out_dtype)
        else:
            refs[2][...] = acc.astype(out_dtype)

    row = lambda n: pl.BlockSpec((tm, n), lambda t: (t, 0))
    in_specs = [row(J * Nc), pl.BlockSpec((J, K, Nc), lambda t: (0, 0, 0))]
    ops = [dy, w]
    if has_rms:
        in_specs += [row(K), pl.BlockSpec((1, K), lambda t: (0, 0)), row(K)]
        ops += list(rms)
        out_specs = [row(K), pl.BlockSpec((1, K), lambda t: (0, 0))]
        out_shape = [jax.ShapeDtypeStruct((L, K), F32), jax.ShapeDtypeStruct((1, K), F32)]
    elif to_heads:
        out_specs = [pl.BlockSpec((K // HD, tm, HD), lambda t: (0, t, 0))]
        out_shape = [jax.ShapeDtypeStruct((K // HD, L, HD), out_dtype)]
    else:
        out_specs = [row(K)]
        out_shape = [jax.ShapeDtypeStruct((L, K), out_dtype)]
    res, got = _carried_call(body, carry, name=name, grid=(L // tm,), in_specs=in_specs, out_specs=out_specs,
                             out_shape=out_shape, scratch_shapes=[], compiler_params=_params(1, VMEM_BIG),
                             operands=ops)
    res = res[0] if len(res) == 1 else res
    return res if carry is None else (res, got)


def _wgrad(x, dy, *, mode, colsum=False, cap=BIG_TILE, name):
    L, K = x.shape
    N = dy.shape[1]
    tm = _tile(L, cap)
    nt = L // tm
    oshape = (NSH, K, N // NSH) if mode == 'col' else (NSH, K // NSH, N)

    def body(*refs):
        t = pl.program_id(0)
        x_ref, dy_ref, out_ref = refs[:3]
        acc_ref = refs[-1]
        dyv = dy_ref[...]
        _acc(acc_ref, _dot(x_ref[...].astype(BF16), dyv.astype(BF16), _TN), t == 0)
        if colsum:
            _acc(refs[3], jnp.sum(dyv.astype(F32), axis=0, keepdims=True), t == 0)

        @pl.when(t == nt - 1)
        def _():
            for s in range(NSH):
                if mode == 'col':
                    out_ref[s] = acc_ref[:, s * oshape[2]:(s + 1) * oshape[2]].astype(BF16)
                else:
                    out_ref[s] = acc_ref[s * oshape[1]:(s + 1) * oshape[1], :].astype(BF16)

    out_specs = pl.BlockSpec(oshape, lambda t: (0, 0, 0))
    out_shape = jax.ShapeDtypeStruct(oshape, BF16)
    if colsum:
        out_specs = (out_specs, pl.BlockSpec((1, N), lambda t: (0, 0)))
        out_shape = (out_shape, jax.ShapeDtypeStruct((1, N), F32))
    return pl.pallas_call(
        body, name=name, grid=(nt,),
        in_specs=[pl.BlockSpec((tm, K), lambda t: (t, 0)), pl.BlockSpec((tm, N), lambda t: (t, 0))],
        out_specs=out_specs, out_shape=out_shape,
        scratch_shapes=[pltpu.VMEM((K, N), F32)], compiler_params=_params(1, VMEM_BIG))(x, dy)


def _ffn_fwd(h, gain, wg, wu, wd, name, carry=None):
    L = h.shape[0]
    tm = _tile(L, BIG_TILE)

    def body(h_ref, g_ref, wg_ref, wu_ref, wd_ref, hn_ref, G_ref, U_ref, A_ref, B_ref, acc_ref, b_scr):
        j = pl.program_id(1)

        @pl.when(j == 0)
        def _():
            x = h_ref[...]
            b = (x * _rstd(x) * g_ref[...]).astype(BF16)
            b_scr[...] = b
            B_ref[...] = b
            acc_ref[...] = x

        b = b_scr[...]
        g = _dot(b, wg_ref[...])
        u = _dot(b, wu_ref[...])
        s = _sigmoid(g)
        silu = g * s
        G_ref[...] = (u * (s * (1.0 + g * (1.0 - s)))).astype(BF16)
        U_ref[...] = silu.astype(BF16)
        a = (silu * u).astype(BF16)
        A_ref[...] = a
        acc_ref[...] += _dot(a, wd_ref[...])

        @pl.when(j == NSH - 1)
        def _():
            hn_ref[...] = acc_ref[...]

    row = pl.BlockSpec((tm, D), lambda t, j: (t, 0))
    chunk = pl.BlockSpec((None, tm, FC), lambda t, j: (j, t, 0))
    cshape = jax.ShapeDtypeStruct((NSH, L, FC), BF16)
    return _carried_call(
        body, carry, name=name, grid=(L // tm, NSH),
        in_specs=[row, pl.BlockSpec((1, D), lambda t, j: (0, 0)),
                  pl.BlockSpec((None, D, FC), lambda t, j: (j, 0, 0)),
                  pl.BlockSpec((None, D, FC), lambda t, j: (j, 0, 0)),
                  pl.BlockSpec((None, FC, D), lambda t, j: (j, 0, 0))],
        out_specs=[row, chunk, chunk, chunk, row],
        out_shape=[jax.ShapeDtypeStruct((L, D), F32), cshape, cshape, cshape, jax.ShapeDtypeStruct((L, D), BF16)],
        scratch_shapes=[pltpu.VMEM((tm, D), F32), pltpu.VMEM((tm, D), BF16)],
        compiler_params=_params(2, VMEM_BIG), operands=[h, gain, wg, wu, wd])


def _ffn_bwd_act(dhn, h, gain, G, U, wg, wu, wd, name, carry=None):
    L = h.shape[0]
    tm = _tile(L, BIG_TILE)

    def body(dhn_ref, h_ref, g_ref, G_ref, U_ref, wg_ref, wu_ref, wd_ref, DG_ref, DU_ref, dh_ref, dgain_ref, db_ref):
        t, j = pl.program_id(0), pl.program_id(1)
        d_act = _dot(dhn_ref[...].astype(BF16), wd_ref[...], _NT)
        dg = (d_act * G_ref[...].astype(F32)).astype(BF16)
        du = (d_act * U_ref[...].astype(F32)).astype(BF16)
        DG_ref[...] = dg
        DU_ref[...] = du
        _acc(db_ref, _dot(dg, wg_ref[...], _NT) + _dot(du, wu_ref[...], _NT), j == 0)

        @pl.when(j == NSH - 1)
        def _():
            x = h_ref[...]
            dx, dgn = _rms_bwd(db_ref[...], x, g_ref[...], _rstd(x))
            dh_ref[...] = dhn_ref[...] + dx
            _acc(dgain_ref, dgn, t == 0)

    row = pl.BlockSpec((tm, D), lambda t, j: (t, 0))
    one = pl.BlockSpec((1, D), lambda t, j: (0, 0))
    chunk = pl.BlockSpec((None, tm, FC), lambda t, j: (j, t, 0))
    cshape = jax.ShapeDtypeStruct((NSH, L, FC), BF16)
    return _carried_call(
        body, carry, name=name, grid=(L // tm, NSH),
        in_specs=[row, row, one, chunk, chunk,
                  pl.BlockSpec((None, D, FC), lambda t, j: (j, 0, 0)),
                  pl.BlockSpec((None, D, FC), lambda t, j: (j, 0, 0)),
                  pl.BlockSpec((None, FC, D), lambda t, j: (j, 0, 0))],
        out_specs=[chunk, chunk, row, one],
        out_shape=[cshape, cshape, jax.ShapeDtypeStruct((L, D), F32), jax.ShapeDtypeStruct((1, D), F32)],
        scratch_shapes=[pltpu.VMEM((tm, D), F32)],
        compiler_params=_params(2, VMEM_BIG), operands=[dhn, h, gain, G, U, wg, wu, wd])


def _ffn_bwd_w(B, A, DG, DU, dhn, name, carry=None):
    L = B.shape[0]
    tm = _tile(L, BIG_TILE)
    nt = L // tm

    def body(B_ref, A_ref, DG_ref, DU_ref, dhn_ref, dwg_ref, dwu_ref, dwd_ref, ag, au, ad):
        t = pl.program_id(1)
        b = B_ref[...]
        _acc(ag, _dot(b, DG_ref[...], _TN), t == 0)
        _acc(au, _dot(b, DU_ref[...], _TN), t == 0)
        _acc(ad, _dot(A_ref[...], dhn_ref[...].astype(BF16), _TN), t == 0)

        @pl.when(t == nt - 1)
        def _():
            dwg_ref[...] = ag[...].astype(BF16)
            dwu_ref[...] = au[...].astype(BF16)
            dwd_ref[...] = ad[...].astype(BF16)

    row = pl.BlockSpec((tm, D), lambda j, t: (t, 0))
    chunk = pl.BlockSpec((None, tm, FC), lambda j, t: (j, t, 0))
    return _carried_call(
        body, carry, name=name, grid=(NSH, nt), in_specs=[row, chunk, chunk, chunk, row],
        out_specs=[pl.BlockSpec((None, D, FC), lambda j, t: (j, 0, 0)),
                   pl.BlockSpec((None, D, FC), lambda j, t: (j, 0, 0)),
                   pl.BlockSpec((None, FC, D), lambda j, t: (j, 0, 0))],
        out_shape=[jax.ShapeDtypeStruct((NSH, D, FC), BF16), jax.ShapeDtypeStruct((NSH, D, FC), BF16),
                   jax.ShapeDtypeStruct((NSH, FC, D), BF16)],
        scratch_shapes=[pltpu.VMEM((D, FC), F32), pltpu.VMEM((D, FC), F32), pltpu.VMEM((FC, D), F32)],
        compiler_params=_params(2, VMEM_BIG), operands=[B, A, DG, DU, dhn])


def _loss_head(h, gain, target):
    L = h.shape[0]
    tm = _tile(L)

    def body(h_ref, g_ref, tgt_ref, dh_ref, dg_ref, loss_ref):
        t = pl.program_id(0)
        x = h_ref[...]
        g = g_ref[...]
        r = _rstd(x)
        rows = t * tm + lax.broadcasted_iota(jnp.int32, (tm, 1), 0)
        diff = jnp.where(rows >= N_META, x * r * g - tgt_ref[...], 0.0)
        part = 0.5 * jnp.sum(jnp.sum(diff * diff, axis=-1, keepdims=True) * (1.0 / D), axis=0, keepdims=True)
        dx, dg = _rms_bwd(diff * (1.0 / D), x, g, r)
        dh_ref[...] = dx
        _acc(dg_ref, dg, t == 0)
        _acc(loss_ref, jnp.broadcast_to(part, (1, 128)), t == 0)

    row = pl.BlockSpec((tm, D), lambda t: (t, 0))
    one = pl.BlockSpec((1, D), lambda t: (0, 0))
    return pl.pallas_call(
        body, name="loss_head", grid=(L // tm,), in_specs=[row, one, row],
        out_specs=(row, one, pl.BlockSpec((1, 128), lambda t: (0, 0))),
        out_shape=(jax.ShapeDtypeStruct((L, D), F32), jax.ShapeDtypeStruct((1, D), F32),
                   jax.ShapeDtypeStruct((1, 128), F32)),
        compiler_params=_params(1))(h, gain, target)


def _conv_fwd(z3, cw, name):
    L = z3.shape[0]
    tm = _tile(L)

    def body(b_ref, c_ref, u_ref, ch_ref, uh_ref, w_ref, y_ref, buf):
        t = pl.program_id(0)
        z = c_ref[...].astype(F32) * u_ref[...].astype(F32)
        buf[pl.ds(0, HALO), :] = jnp.where(t > 0, ch_ref[...].astype(F32) * uh_ref[...].astype(F32), 0.0)
        buf[pl.ds(HALO, tm), :] = z
        w0, w1, w2 = w_ref[0:1, :], w_ref[1:2, :], w_ref[2:3, :]
        conv = w2 * z + w1 * buf[pl.ds(HALO - 1, tm), :] + w0 * buf[pl.ds(HALO - 2, tm), :]
        y_ref[...] = (b_ref[...].astype(F32) * conv).astype(BF16)

    def col(k):
        return pl.BlockSpec((tm, D), lambda t: (t, k))

    def hcol(k):
        return pl.BlockSpec((HALO, D), lambda t: (jnp.maximum(t * (tm // HALO) - 1, 0), k))

    return pl.pallas_call(
        body, name=name, grid=(L // tm,),
        in_specs=[col(0), col(1), col(2), hcol(1), hcol(2), pl.BlockSpec((3, D), lambda t: (0, 0))],
        out_specs=pl.BlockSpec((tm, D), lambda t: (t, 0)),
        out_shape=jax.ShapeDtypeStruct((L, D), BF16),
        scratch_shapes=[pltpu.VMEM((HALO + tm, D), F32)], compiler_params=_params(1))(z3, z3, z3, z3, z3, cw)


def _conv_bwd(z3, cw, dy, name):
    L = z3.shape[0]
    tm = _tile(L)
    nt = L // tm
    last_h = L // HALO - 1

    def body(b_ref, c_ref, u_ref, ch_ref, uh_ref, w_ref, dy_ref, bn_ref, dyn_ref, dz3_ref, dw_ref, zbuf, dbuf):
        t = pl.program_id(0)
        w0, w1, w2 = w_ref[0:1, :], w_ref[1:2, :], w_ref[2:3, :]
        bgate, cgate, u = b_ref[...].astype(F32), c_ref[...].astype(F32), u_ref[...].astype(F32)
        z = cgate * u
        zbuf[pl.ds(0, HALO), :] = jnp.where(t > 0, ch_ref[...].astype(F32) * uh_ref[...].astype(F32), 0.0)
        zbuf[pl.ds(HALO, tm), :] = z
        z1 = zbuf[pl.ds(HALO - 1, tm), :]
        z2 = zbuf[pl.ds(HALO - 2, tm), :]
        conv = w2 * z + w1 * z1 + w0 * z2
        dyv = dy_ref[...]
        dconv = dyv * bgate
        dbuf[pl.ds(0, tm), :] = dconv
        dbuf[pl.ds(tm, HALO), :] = jnp.where(t < nt - 1, dyn_ref[...] * bn_ref[...].astype(F32), 0.0)
        dz = w2 * dconv + w1 * dbuf[pl.ds(1, tm), :] + w0 * dbuf[pl.ds(2, tm), :]
        dz3_ref[:, 0:D] = (dyv * conv).astype(BF16)
        dz3_ref[:, D:2 * D] = (dz * u).astype(BF16)
        dz3_ref[:, 2 * D:3 * D] = (dz * cgate).astype(BF16)
        for k, zk in enumerate((z2, z1, z)):
            _acc(dw_ref.at[k:k + 1], jnp.sum(dconv * zk, axis=0, keepdims=True), t == 0)

    def col(k):
        return pl.BlockSpec((tm, D), lambda t: (t, k))

    def hprev(k):
        return pl.BlockSpec((HALO, D), lambda t: (jnp.maximum(t * (tm // HALO) - 1, 0), k))

    def hnext(k):
        return pl.BlockSpec((HALO, D), lambda t: (jnp.minimum((t + 1) * (tm // HALO), last_h), k))

    return pl.pallas_call(
        body, name=name, grid=(nt,),
        in_specs=[col(0), col(1), col(2), hprev(1), hprev(2), pl.BlockSpec((3, D), lambda t: (0, 0)),
                  col(0), hnext(0), hnext(0)],
        out_specs=(pl.BlockSpec((tm, 3 * D), lambda t: (t, 0)), pl.BlockSpec((3, D), lambda t: (0, 0))),
        out_shape=(jax.ShapeDtypeStruct((L, 3 * D), BF16), jax.ShapeDtypeStruct((3, D), F32)),
        scratch_shapes=[pltpu.VMEM((HALO + tm, D), F32), pltpu.VMEM((tm + HALO, D), F32)],
        compiler_params=_params(1))(z3, z3, z3, z3, z3, cw, dy, z3, dy)


def _count_inv(pos, w):
    return 1.0 / jnp.minimum(pos + 1, w).astype(F32)


def _pool_fwd(h, gain, name):
    L = h.shape[0]
    tm = _tile(L)

    def body(h_ref, hh_ref, g_ref, mix_ref, buf):
        t = pl.program_id(0)
        g = g_ref[...]
        x = h_ref[...]
        a = x * _rstd(x) * g
        xh = hh_ref[...]
        buf[pl.ds(0, HALO), :] = jnp.where(t > 0, xh * _rstd(xh) * g, 0.0)
        buf[pl.ds(HALO, tm), :] = a
        pos = t * tm + lax.broadcasted_iota(jnp.int32, (tm, 1), 0)
        for gi, w in enumerate(POOL_WINDOWS):
            cols = pl.ds(gi * PG, PG)
            s = buf[pl.ds(HALO, tm), cols]
            for k in range(1, w):
                s = s + buf[pl.ds(HALO - k, tm), cols]
            mix_ref[:, gi * PG:(gi + 1) * PG] = (s / jnp.minimum(pos + 1, w).astype(F32)
                                                  - buf[pl.ds(HALO, tm), cols]).astype(BF16)

    return pl.pallas_call(
        body, name=name, grid=(L // tm,),
        in_specs=[pl.BlockSpec((tm, D), lambda t: (t, 0)),
                  pl.BlockSpec((HALO, D), lambda t: (jnp.maximum(t * (tm // HALO) - 1, 0), 0)),
                  pl.BlockSpec((1, D), lambda t: (0, 0))],
        out_specs=pl.BlockSpec((tm, D), lambda t: (t, 0)),
        out_shape=jax.ShapeDtypeStruct((L, D), BF16),
        scratch_shapes=[pltpu.VMEM((HALO + tm, D), F32)], compiler_params=_params(1))(h, h, gain)


def _pool_out(mix, wp, scale, h, name):
    L = mix.shape[0]
    tm = _tile(L)

    def body(mix_ref, wp_ref, s_ref, h_ref, out_ref):
        for gi in range(4):
            sl = slice(gi * PG, (gi + 1) * PG)
            pre = _dot(mix_ref[:, sl], wp_ref[gi])
            out_ref[:, sl] = h_ref[:, sl] + pre * s_ref[:, sl]

    row = pl.BlockSpec((tm, D), lambda t: (t, 0))
    return pl.pallas_call(
        body, name=name, grid=(L // tm,),
        in_specs=[row, pl.BlockSpec((4, PG, PG), lambda t: (0, 0, 0)), pl.BlockSpec((1, D), lambda t: (0, 0)), row],
        out_specs=row, out_shape=jax.ShapeDtypeStruct((L, D), F32), compiler_params=_params(1))(mix, wp, scale, h)


def _pool_out_bwd(dm, mix, wp, scale, name):
    L = mix.shape[0]
    tm = _tile(L)

    def body(dm_ref, mix_ref, wp_ref, s_ref, dmix_ref, dwp_ref, ds_ref):
        t = pl.program_id(0)
        for gi in range(4):
            sl = slice(gi * PG, (gi + 1) * PG)
            mx = mix_ref[:, sl]
            dmv = dm_ref[:, sl]
            pre = _dot(mx, wp_ref[gi])
            _acc(ds_ref.at[:, sl], jnp.sum(dmv * pre, axis=0, keepdims=True), t == 0)
            dpre = (dmv * s_ref[:, sl]).astype(BF16)
            _acc(dwp_ref.at[gi], _dot(mx, dpre, _TN), t == 0)
            dmix_ref[:, sl] = _dot(dpre, wp_ref[gi], _NT)

    row = pl.BlockSpec((tm, D), lambda t: (t, 0))
    one = pl.BlockSpec((1, D), lambda t: (0, 0))
    wsp = pl.BlockSpec((4, PG, PG), lambda t: (0, 0, 0))
    return pl.pallas_call(
        body, name=name, grid=(L // tm,), in_specs=[row, row, wsp, one],
        out_specs=(row, wsp, one),
        out_shape=(jax.ShapeDtypeStruct((L, D), F32), jax.ShapeDtypeStruct((4, PG, PG), F32),
                   jax.ShapeDtypeStruct((1, D), F32)),
        compiler_params=_params(1))(dm, mix, wp, scale)


def _pool_bwd(dmix, h, gain, resid, name):
    L = h.shape[0]
    tm = _tile(L)
    nt = L // tm
    last_h = L // HALO - 1

    def body(dm_ref, dmn_ref, h_ref, g_ref, r_ref, dh_ref, dg_ref, buf):
        t = pl.program_id(0)
        pos = t * tm + lax.broadcasted_iota(jnp.int32, (tm, 1), 0)
        posn = (t + 1) * tm + lax.broadcasted_iota(jnp.int32, (HALO, 1), 0)
        dmv = dm_ref[...]
        dmn = dmn_ref[...]
        for gi, w in enumerate(POOL_WINDOWS):
            sl = slice(gi * PG, (gi + 1) * PG)
            buf[pl.ds(0, tm), sl] = dmv[:, sl] * _count_inv(pos, w)
            buf[pl.ds(tm, HALO), sl] = jnp.where(t < nt - 1, dmn[:, sl] * _count_inv(posn, w), 0.0)
        parts = []
        for gi, w in enumerate(POOL_WINDOWS):
            cols = pl.ds(gi * PG, PG)
            s = buf[pl.ds(0, tm), cols]
            for k in range(1, w):
                s = s + buf[pl.ds(k, tm), cols]
            parts.append(s)
        da = jnp.concatenate(parts, axis=1) - dmv
        x = h_ref[...]
        dx, dg = _rms_bwd(da, x, g_ref[...], _rstd(x))
        dh_ref[...] = r_ref[...] + dx
        _acc(dg_ref, dg, t == 0)

    row = pl.BlockSpec((tm, D), lambda t: (t, 0))
    one = pl.BlockSpec((1, D), lambda t: (0, 0))
    return pl.pallas_call(
        body, name=name, grid=(nt,),
        in_specs=[row, pl.BlockSpec((HALO, D), lambda t: (jnp.minimum((t + 1) * (tm // HALO), last_h), 0)),
                  row, one, row],
        out_specs=(row, one),
        out_shape=(jax.ShapeDtypeStruct((L, D), F32), jax.ShapeDtypeStruct((1, D), F32)),
        scratch_shapes=[pltpu.VMEM((tm + HALO, D), F32)], compiler_params=_params(1))(dmix, dmix, h, gain, resid)


def _bucket_np(d):
    d = np.maximum(d, 0)
    df = np.maximum(d, 1).astype(np.float32)
    large = 16 + (np.log(df / np.float32(16)) / np.float32(math.log(128 / 16)) * np.float32(16)).astype(np.int32)
    return np.where(d < 16, d, np.minimum(large, N_BUCKETS - 1))


def _bias_index():
    iq = np.arange(BLK)[:, None]
    jk = np.arange(2 * BLK)[None, :]
    dist = BLK + iq - jk
    band = _bucket_np(dist)
    ok = (dist >= 0) & (dist < BLK)
    band1 = np.where(ok, band, N_BUCKETS)
    band0 = np.where(ok & (jk >= BLK), band, N_BUCKETS)
    im = np.arange(N_META)[None, :]
    unused = np.full((BLK, BLK - N_META), N_BUCKETS)
    var0 = np.concatenate([_bucket_np(N_META + iq - im), unused, band0], axis=1)
    var1 = np.concatenate([_bucket_np(N_META + BLK + iq - im), unused, band1], axis=1)
    dm = np.arange(N_META)[:, None] - im
    mm = np.where(dm >= 0, _bucket_np(dm), N_BUCKETS)
    segs = [var0, var1, mm]
    return np.concatenate([s.reshape(-1) for s in segs]).astype(np.int32), [s.shape for s in segs]


P_CHUNK = 9856


def _onehot(idx_ref, grad):
    rows = lax.broadcasted_iota(jnp.int32, (128, P_CHUNK), 0)
    hit = (rows == idx_ref[...]).astype(F32)
    return jnp.where(rows == N_BUCKETS, -1.0, hit) if grad else hit


def _bias_lookup(table_aug, idx, name):
    P = idx.shape[1]

    def body(t_ref, idx_ref, o_ref):
        o_ref[...] = lax.dot_general(t_ref[...], _onehot(idx_ref, False), _NN, precision=lax.Precision.HIGHEST,
                                     preferred_element_type=F32)

    return pl.pallas_call(
        body, name=name, grid=(P // P_CHUNK,),
        in_specs=[pl.BlockSpec((H, 128), lambda i: (0, 0)), pl.BlockSpec((1, P_CHUNK), lambda i: (0, i))],
        out_specs=pl.BlockSpec((H, P_CHUNK), lambda i: (0, i)),
        out_shape=jax.ShapeDtypeStruct((H, P), F32), compiler_params=_params(1))(table_aug, idx)


def _bias_lookup_bwd(dbias, idx, name):
    rows, P = dbias.shape

    def body(d_ref, idx_ref, o_ref):
        part = lax.dot_general(d_ref[...], _onehot(idx_ref, True), _NT, precision=lax.Precision.HIGHEST,
                               preferred_element_type=F32)
        _acc(o_ref, part, pl.program_id(0) == 0)

    return pl.pallas_call(
        body, name=name, grid=(P // P_CHUNK,),
        in_specs=[pl.BlockSpec((rows, P_CHUNK), lambda i: (0, i)), pl.BlockSpec((1, P_CHUNK), lambda i: (0, i))],
        out_specs=pl.BlockSpec((rows, 128), lambda i: (0, 0)),
        out_shape=jax.ShapeDtypeStruct((rows, 128), F32), compiler_params=_params(1))(dbias, idx)


def _probs(q, kbt, bias, sink):
    s = _dot(q, kbt) + bias
    m = jnp.maximum(jnp.max(s, axis=-1, keepdims=True), sink)
    e = jnp.exp(s - m)
    return e * (1.0 / (jnp.sum(e, axis=-1, keepdims=True) + jnp.exp(sink - m)))


def _attn_specs(nb):
    def cur(kh, n):
        return jnp.minimum(n, nb - 1)

    def prev(kh, n):
        return jnp.maximum(jnp.minimum(n, nb - 1) - 1, 0)

    def rows(heads, blk):
        return pl.BlockSpec((pl.Element(heads), pl.Element(BLK), pl.Element(HD)),
                            lambda kh, n: (kh * heads, pl.multiple_of(N_META + blk(kh, n) * BLK, N_META), 0))

    return dict(
        q=rows(GRP, cur),
        qo=pl.BlockSpec((GRP, BLK, HD), lambda kh, n: (kh, cur(kh, n), 0)),
        qt=pl.BlockSpec((GRP, HD, BLK), lambda kh, n: (kh, 0, cur(kh, n))),
        cur=rows(1, cur),
        prev=rows(1, prev),
        meta=pl.BlockSpec((None, BLK, HD), lambda kh, n: (kh, 0, 0)),
        curt=pl.BlockSpec((None, HD, BLK), lambda kh, n: (kh, 0, cur(kh, n))),
        prevt=pl.BlockSpec((None, HD, BLK), lambda kh, n: (kh, 0, prev(kh, n))),
        metat=pl.BlockSpec((None, HD, BLK), lambda kh, n: (kh, 0, 0)),
        bias=pl.BlockSpec((None, GRP, BLK, NKEY), lambda kh, n: (jnp.minimum(n, 1), kh, 0, 0)),
        sink=pl.BlockSpec((None, GRP * BLK, 1), lambda kh, n: (kh, 0, 0)))


def _attn_fwd(q, k, km, v, vm, band, bmeta, sink, name, carry=None):
    S = q.shape[1] - N_META
    nb = S // BLK
    sp = _attn_specs(nb)
    var = lambda kh, n: (jnp.minimum(n, 1), kh, 0, 0)
    meta = pl.BlockSpec((None, N_META, HD), lambda kh, n: (kh, 0, 0))

    def body(q_ref, kc_ref, kp_ref, vc_ref, vp_ref, km_ref, vm_ref, band_ref, bm_ref, sink_ref, o_ref):
        qv = q_ref[...].reshape(GRP * BLK, HD)
        kb = jnp.concatenate([kp_ref[0], kc_ref[0]], axis=0)
        vb = jnp.concatenate([vp_ref[0], vc_ref[0]], axis=0)
        sink = sink_ref[...]
        s_b = _dot(qv, kb, _NT) + band_ref[...].reshape(GRP * BLK, 2 * BLK)
        s_m = _dot(qv, km_ref[...], _NT) + bm_ref[...].reshape(GRP * BLK, N_META)
        m = jnp.maximum(jnp.maximum(jnp.max(s_b, axis=-1, keepdims=True), jnp.max(s_m, axis=-1, keepdims=True)), sink)
        e_b = jnp.exp(s_b - m)
        e_m = jnp.exp(s_m - m)
        inv = 1.0 / (jnp.sum(e_b, axis=-1, keepdims=True) + jnp.sum(e_m, axis=-1, keepdims=True) + jnp.exp(sink - m))
        o = _dot((e_b * inv).astype(BF16), vb) + _dot((e_m * inv).astype(BF16), vm_ref[...])
        o_ref[...] = o.reshape(GRP, BLK, HD).astype(BF16)

    return _carried_call(
        body, carry, name=name, grid=(KV, nb),
        in_specs=[sp['q'], sp['cur'], sp['prev'], sp['cur'], sp['prev'], meta, meta,
                  pl.BlockSpec((None, GRP, BLK, 2 * BLK), var), pl.BlockSpec((None, GRP, BLK, N_META), var),
                  sp['sink']],
        out_specs=[sp['qo']], out_shape=[jax.ShapeDtypeStruct((H, S, HD), BF16)], scratch_shapes=[],
        compiler_params=_params(2), operands=[q, k, k, v, v, km, vm, band, bmeta, sink])


def _attn_bwd(q, qt, k, km, kt, kmt, vt, vmt, bias, sink, do, dot_, name, carry=None):
    S = q.shape[1] - N_META
    nb = S // BLK
    sp = _attn_specs(nb)

    def body(q_ref, qt_ref, kc_ref, kp_ref, km_ref, kct_ref, kpt_ref, kmt_ref, vct_ref, vpt_ref, vmt_ref,
             bias_ref, sink_ref, do_ref, dot_ref, dq_ref, dkt_ref, dvt_ref, dkmt_ref, dvmt_ref, dbias_ref, ck, cv):
        n = pl.program_id(1)

        @pl.when(n < nb)
        def _():
            kbt = jnp.concatenate([kmt_ref[...], kpt_ref[...], kct_ref[...]], axis=1)
            vbt = jnp.concatenate([vmt_ref[...], vpt_ref[...], vct_ref[...]], axis=1)
            kb = jnp.concatenate([km_ref[...], kp_ref[0], kc_ref[0]], axis=0)
            p = _probs(q_ref[...].reshape(GRP * BLK, HD), kbt, bias_ref[...].reshape(GRP * BLK, NKEY), sink_ref[...])
            dp = _dot(do_ref[...].reshape(GRP * BLK, HD), vbt)
            ds = p * (dp - jnp.sum(p * dp, axis=-1, keepdims=True))
            _acc(dbias_ref, ds.reshape(GRP, BLK, NKEY), n <= 1)
            ds16 = ds.astype(BF16)
            dq_ref[...] = _dot(ds16, kb).reshape(GRP, BLK, HD).astype(BF16)
            qtv = jnp.concatenate([qt_ref[g] for g in range(GRP)], axis=1)
            dotv = jnp.concatenate([dot_ref[g] for g in range(GRP)], axis=1)
            dkt = _dot(qtv, ds16)
            dvt = _dot(dotv, p.astype(BF16))
            _acc(dkmt_ref, dkt[:, 0:BLK], n == 0)
            _acc(dvmt_ref, dvt[:, 0:BLK], n == 0)

            @pl.when(n >= 1)
            def _():
                dkt_ref[...] = (ck[...] + dkt[:, BLK:2 * BLK]).astype(BF16)
                dvt_ref[...] = (cv[...] + dvt[:, BLK:2 * BLK]).astype(BF16)

            ck[...] = dkt[:, 2 * BLK:3 * BLK]
            cv[...] = dvt[:, 2 * BLK:3 * BLK]

        @pl.when(n == nb)
        def _():
            dkt_ref[...] = ck[...].astype(BF16)
            dvt_ref[...] = cv[...].astype(BF16)

    kvout = pl.BlockSpec((None, HD, BLK), lambda kh, n: (kh, 0, jnp.maximum(n - 1, 0)))
    return _carried_call(
        body, carry, name=name, grid=(KV, nb + 1),
        in_specs=[sp['q'], sp['qt'], sp['cur'], sp['prev'], sp['meta'], sp['curt'], sp['prevt'], sp['metat'],
                  sp['curt'], sp['prevt'], sp['metat'], sp['bias'], sp['sink'], sp['q'], sp['qt']],
        out_specs=[sp['qo'], kvout, kvout, sp['metat'], sp['metat'], sp['bias']],
        out_shape=[jax.ShapeDtypeStruct((H, S, HD), BF16), jax.ShapeDtypeStruct((KV, HD, S), BF16),
                   jax.ShapeDtypeStruct((KV, HD, S), BF16), jax.ShapeDtypeStruct((KV, HD, BLK), F32),
                   jax.ShapeDtypeStruct((KV, HD, BLK), F32), jax.ShapeDtypeStruct((2, H, BLK, NKEY), F32)],
        scratch_shapes=[pltpu.VMEM((HD, BLK), F32), pltpu.VMEM((HD, BLK), F32)],
        compiler_params=_params(2), operands=[q, qt, k, k, km, kt, kt, kmt, vt, vt, vmt, bias, sink, do, dot_])


def _meta_softmax(q, k, bias, sink):
    s = _dot(q, k, _NT) + bias
    m = jnp.maximum(jnp.max(s, axis=-1, keepdims=True), sink)
    e = jnp.exp(s - m)
    return e * (1.0 / (jnp.sum(e, axis=-1, keepdims=True) + jnp.exp(sink - m)))


def _attn_meta_fwd(qm, km, vm, bias, sink, name):
    def body(q_ref, k_ref, v_ref, b_ref, s_ref, o_ref):
        for h in range(H):
            p = _meta_softmax(q_ref[h], k_ref[h // GRP], b_ref[h], s_ref[h])
            o_ref[h] = _dot(p.astype(BF16), v_ref[h // GRP]).astype(BF16)

    return pl.pallas_call(body, name=name, out_shape=jax.ShapeDtypeStruct((H, N_META, HD), BF16))(
        qm, km, vm, bias, sink)


def _attn_meta_bwd(qm, km, vm, bias, sink, do, name):
    def body(q_ref, k_ref, v_ref, b_ref, s_ref, do_ref, dq_ref, dk_ref, dv_ref, db_ref):
        for kh in range(KV):
            k, v = k_ref[kh], v_ref[kh]
            dk = jnp.zeros((N_META, HD), F32)
            dv = jnp.zeros((N_META, HD), F32)
            for g in range(GRP):
                h = kh * GRP + g
                q, dov = q_ref[h], do_ref[h]
                p = _meta_softmax(q, k, b_ref[h], s_ref[h])
                dp = _dot(dov, v, _NT)
                ds = p * (dp - jnp.sum(p * dp, axis=-1, keepdims=True))
                db_ref[h] = ds
                ds16 = ds.astype(BF16)
                dq_ref[h] = _dot(ds16, k).astype(BF16)
                dk = dk + _dot(ds16, q, _TN)
                dv = dv + _dot(p.astype(BF16), dov, _TN)
            dk_ref[kh] = dk
            dv_ref[kh] = dv

    return pl.pallas_call(
        body, name=name,
        out_shape=(jax.ShapeDtypeStruct((H, N_META, HD), BF16), jax.ShapeDtypeStruct((KV, N_META, HD), F32),
                   jax.ShapeDtypeStruct((KV, N_META, HD), F32), jax.ShapeDtypeStruct((H, N_META, N_META), F32)))(
        qm, km, vm, bias, sink, do)


def _unheads(t):
    return jnp.transpose(t, (1, 0, 2)).reshape(t.shape[1], t.shape[0] * HD)


def _unheads_t(t):
    return jnp.transpose(t, (2, 0, 1)).reshape(t.shape[2], t.shape[0] * HD)


def _pad_block(t, axis):
    pad = [(0, 0)] * t.ndim
    pad[axis] = (0, BLK - N_META)
    return jnp.pad(t, pad)


def _adam(w, g, m, v, name):
    lead, R, C = w.shape
    tr = _tile(R, 256) if R % 16 == 0 else R

    def body(w_ref, g_ref, m_ref, v_ref, d_ref, mo_ref, vo_ref, go_ref):
        gv = g_ref[...]
        go_ref[...] = gv
        mn = B1 * m_ref[...] + (1.0 - B1) * gv
        vn = B2 * v_ref[...] + (1.0 - B2) * (gv * gv)
        m_hat = mn / (1.0 - B1 ** STEP)
        v_hat = vn / (1.0 - B2 ** STEP)
        d_ref[...] = -LR * (m_hat / (jnp.sqrt(v_hat) + ADAM_EPS) + WD * w_ref[...])
        mo_ref[...] = mn
        vo_ref[...] = vn

    blk = pl.BlockSpec((None, tr, C), lambda l, i: (l, i, 0))
    shp = jax.ShapeDtypeStruct((lead, R, C), F32)
    return pl.pallas_call(body, name=name, grid=(lead, R // tr), in_specs=[blk] * 4, out_specs=(blk,) * 4,
                          out_shape=(shp,) * 4, compiler_params=_params(2))(w, g, m, v)


def _sum_leaf(lands, pends, ids, name):
    n = len(lands)
    _, R, C = lands[0].shape
    tr = _tile(R, 128)

    def body(ids_ref, *refs):
        out_ref = refs[2 * n]
        for k in range(n):
            acc = refs[n + k][...].astype(F32)
            for s in range(7):
                acc = acc + refs[k][s].astype(F32)
            out_ref[k] = acc

    grid_spec = pltpu.PrefetchScalarGridSpec(
        num_scalar_prefetch=1, grid=(R // tr,),
        in_specs=[pl.BlockSpec((7, tr, C), lambda i, ids: (0, i, 0))] * n
        + [pl.BlockSpec((None, None, tr, C), lambda i, ids: (ids[0], ids[1], i, 0))] * n,
        out_specs=pl.BlockSpec((n, None, tr, C), lambda i, ids: (0, ids[1], i, 0)))
    return pl.pallas_call(body, name=name, grid_spec=grid_spec, out_shape=jax.ShapeDtypeStruct((n, 2, R, C), F32),
                          compiler_params=_params(1))(ids, *lands, *pends)


def _place():
    x, y, c = lax.axis_index("x"), lax.axis_index("y"), lax.axis_index("c")
    chips = [(1 - x, y), (x, 1 - y), (1 - x, 1 - y)]
    return x, y, c, chips


def _rcopy(src, dst, ssem, rsem, dev):
    return pltpu.make_async_remote_copy(src_ref=src, dst_ref=dst, send_sem=ssem, recv_sem=rsem,
                                        device_id=dev, device_id_type=MESH)


class _Carry:
    def __init__(self, kind, arrays):
        self.kind, self.arrays, self.n = kind, list(arrays), len(arrays)
        self.per = 3 if kind == 'gather' else 7
        if kind == 'gather':
            self.out_shape = [jax.ShapeDtypeStruct((NSH,) + a.shape, a.dtype) for a in self.arrays]
        else:
            self.out_shape = [jax.ShapeDtypeStruct((7,) + a.shape[2:], a.dtype) for a in self.arrays]
        dma = pltpu.SemaphoreType.DMA
        self.scratch = [dma((self.per * self.n,)), dma((self.per * self.n,)), dma((self.n,))]
        if kind == 'gather':
            self.scratch += [pltpu.VMEM(a.shape, a.dtype) for a in self.arrays]

    def _copies(self, cin, cout, scr):
        ssem, rsem, loc = scr[:3]
        x, y, c, chips = _place()
        local, sends, recvs = [], [], []
        for k in range(self.n):
            if self.kind == 'gather':
                me = 2 * x + y
                local.append((pltpu.make_async_copy(cin[k], scr[3 + k], loc.at[k]),
                              pltpu.make_async_copy(scr[3 + k], cout[k].at[me], loc.at[k])))
                for j, (cx, cy) in enumerate(chips):
                    i = 3 * k + j
                    sends.append(_rcopy(cin[k], cout[k].at[me], ssem.at[i], rsem.at[i], (cx, cy, c)))
                    got = cout[k].at[2 * cx + cy]
                    recvs.append(_rcopy(got, got, ssem.at[i], rsem.at[i], (cx, cy, c)))
            else:
                for f in range(1, 8):
                    px = 1 - x if (f >> 2) & 1 else x
                    py = 1 - y if (f >> 1) & 1 else y
                    pc = 1 - c if f & 1 else c
                    i = 7 * k + f - 1
                    got = cout[k].at[f - 1]
                    sends.append(_rcopy(cin[k].at[2 * px + py, pc], got, ssem.at[i], rsem.at[i], (px, py, pc)))
                    recvs.append(_rcopy(got, got, ssem.at[i], rsem.at[i], (px, py, pc)))
        return local, sends, recvs

    def start(self, cin, cout, scr):
        local, sends, _ = self._copies(cin, cout, scr)
        for cp in [to_vmem for to_vmem, _ in local] + sends:
            cp.start()

    def finish(self, cin, cout, scr):
        local, sends, recvs = self._copies(cin, cout, scr)
        for to_vmem, to_slot in local:
            to_vmem.wait()
            to_slot.start()
        for cp in recvs:
            cp.wait_recv()
        for cp in sends:
            cp.wait_send()
        for _, to_slot in local:
            to_slot.wait()


def _carried_call(body, carry, *, name, grid, in_specs, out_specs, out_shape, scratch_shapes, compiler_params,
                  operands):
    n_in, n_out = len(in_specs), len(out_specs)
    if carry is None:
        return pl.pallas_call(body, name=name, grid=grid, in_specs=in_specs, out_specs=out_specs,
                              out_shape=out_shape, scratch_shapes=scratch_shapes,
                              compiler_params=compiler_params)(*operands), []
    m = carry.n

    def full(*refs):
        ins, cin = refs[:n_in], refs[n_in:n_in + m]
        outs, cout = refs[n_in + m:n_in + m + n_out], refs[n_in + m + n_out:n_in + 2 * m + n_out]
        own = len(refs) - len(carry.scratch)
        scr, sems = refs[n_in + 2 * m + n_out:own], refs[own:]
        ids = [pl.program_id(a) for a in range(len(grid))]
        first = functools.reduce(jnp.logical_and, [i == 0 for i in ids])
        last = functools.reduce(jnp.logical_and, [i == g - 1 for i, g in zip(ids, grid)])

        @pl.when(first)
        def _():
            carry.start(cin, cout, sems)

        body(*ins, *outs, *scr)

        @pl.when(last)
        def _():
            carry.finish(cin, cout, sems)

    res = pl.pallas_call(
        full, name=name, grid=grid, in_specs=list(in_specs) + [ANY] * m, out_specs=list(out_specs) + [ANY] * m,
        out_shape=list(out_shape) + carry.out_shape, scratch_shapes=list(scratch_shapes) + carry.scratch,
        compiler_params=compiler_params)(*operands, *carry.arrays)
    return res[:n_out], res[n_out:]


def _flush(carry, name):
    m = carry.n

    def body(*refs):
        cin, cout, sems = refs[:m], refs[m:2 * m], refs[2 * m:]
        carry.start(cin, cout, sems)
        carry.finish(cin, cout, sems)

    return pl.pallas_call(body, name=name, in_specs=[ANY] * m, out_specs=[ANY] * m, out_shape=carry.out_shape,
                          scratch_shapes=carry.scratch)(*carry.arrays)


def _pair_share(leaves):
    n = len(leaves)

    def body(*refs):
        ins, outs = refs[:n], refs[n:2 * n]
        ssem, rsem = refs[2 * n:]
        x, y, c, _ = _place()
        sib = (x, y, 1 - c)
        cps = [_rcopy(ins[k].at[:, c], outs[k].at[:, c], ssem.at[k], rsem.at[k], sib) for k in range(n)]
        for cp in cps:
            cp.start()
        for k in range(n):
            got = outs[k].at[:, 1 - c]
            _rcopy(got, got, ssem.at[k], rsem.at[k], sib).wait_recv()
        for cp in cps:
            cp.wait_send()

    dma = pltpu.SemaphoreType.DMA
    return pl.pallas_call(
        body, name="grad_pair_share", in_specs=[ANY] * n, out_specs=[ANY] * n,
        out_shape=[jax.ShapeDtypeStruct(a.shape, a.dtype) for a in leaves],
        input_output_aliases={k: k for k in range(n)},
        scratch_shapes=[dma((n,)), dma((n,))])(*leaves)


def _allreduce_small(pack):
    R = pack.shape[0]

    def body(in_ref, out_ref, buf, ssem, rsem):
        x, y, c, _ = _place()
        me = 4 * x + 2 * y + c
        buf[me] = in_ref[...]
        peers = []
        for k in range(1, 8):
            fx, fy, fc = (k >> 2) & 1, (k >> 1) & 1, k & 1
            peers.append((1 - x if fx else x, 1 - y if fy else y, 1 - c if fc else c))
        cps = [_rcopy(in_ref, buf.at[me], ssem.at[k], rsem.at[k], p) for k, p in enumerate(peers)]
        for cp in cps:
            cp.start()
        for k, (px, py, pc) in enumerate(peers):
            got = buf.at[4 * px + 2 * py + pc]
            _rcopy(got, got, ssem.at[k], rsem.at[k], (px, py, pc)).wait_recv()
        for cp in cps:
            cp.wait_send()
        acc = buf[0]
        for s in range(1, 8):
            acc = acc + buf[s]
        out_ref[...] = acc

    dma = pltpu.SemaphoreType.DMA
    return pl.pallas_call(
        body, name="allreduce_small", out_shape=jax.ShapeDtypeStruct(pack.shape, F32),
        in_specs=[pl.BlockSpec(memory_space=pltpu.VMEM)], out_specs=pl.BlockSpec(memory_space=pltpu.VMEM),
        scratch_shapes=[pltpu.VMEM((8, R, 128), F32), dma((7,)), dma((7,))])(pack)


def _pack(arrs):
    flat = jnp.concatenate([a.reshape(-1).astype(F32) for a in arrs])
    n = flat.shape[0]
    rows = -(-n // 1024) * 8
    return jnp.pad(flat, (0, rows * 128 - n)).reshape(rows, 128)


def _unpack(pack, shapes):
    flat, out, o = pack.reshape(-1), [], 0
    for s in shapes:
        n = int(np.prod(s))
        out.append(flat[o:o + n].reshape(s))
        o += n
    return out


def _ffn_keys(i):
    return [('ffn_w_gate', i), ('ffn_w_up', i), ('ffn_w_down', i)]


GATHER_PLAN = {'attn_fwd0': _ffn_keys(0) + [('conv_w_in', 0), ('conv_w_out', 0)],
               'ffn_fwd0': _ffn_keys(1),
               'ffn_fwd1': _ffn_keys(2) + [('pool_w', 0)],
               'ffn_fwd2': [('attn_w_qkv', 1), ('attn_w_o', 1)],
               'attn_fwd3': _ffn_keys(3)}
REDUCE_PLAN = {'attn_bwd3': _ffn_keys(3),
               'ffn_bwd_act2': [('attn_w_qkv', 1), ('attn_w_o', 1)],
               'ffn_bwd_act1': _ffn_keys(2) + [('pool_w', 0)],
               'ffn_bwd_act0': _ffn_keys(1),
               'ffn_bwd_w0': [('conv_w_in', 0), ('conv_w_out', 0)],
               'attn_bwd0': _ffn_keys(0) + [('attn_w_o', 0)],
               'qkv_bwd0': [('attn_w_qkv', 0)]}


def kernel(x, meta_tokens, rel_bias_table, norm_mix, norm_ffn, norm_final, attn_w_qkv, attn_b_qkv, attn_w_o, attn_b_o, attn_sinks, conv_w_in, conv_w, conv_w_out, pool_w, pool_scale, ffn_w_gate, ffn_w_up, ffn_w_down, loss_target, m_meta_tokens, m_rel_bias_table, m_norm_mix, m_norm_ffn, m_norm_final, m_attn_w_qkv, m_attn_b_qkv, m_attn_w_o, m_attn_b_o, m_attn_sinks, m_conv_w_in, m_conv_w, m_conv_w_out, m_pool_w, m_pool_scale, m_ffn_w_gate, m_ffn_w_up, m_ffn_w_down, v_meta_tokens, v_rel_bias_table, v_norm_mix, v_norm_ffn, v_norm_final, v_attn_w_qkv, v_attn_b_qkv, v_attn_w_o, v_attn_b_o, v_attn_sinks, v_conv_w_in, v_conv_w, v_conv_w_out, v_pool_w, v_pool_scale, v_ffn_w_gate, v_ffn_w_up, v_ffn_w_down):
    args = locals()
    w = {n: args[n] for n in WEIGHTS}
    mom = {n: args['m_' + n] for n in WEIGHTS}
    var = {n: args['v_' + n] for n in WEIGHTS}
    mx, my = lax.axis_index("x"), lax.axis_index("y")
    chip = 2 * mx + my
    S = x.shape[1]
    scale = jnp.asarray(HD ** -0.5, BF16)

    GW = {}
    pending = {}
    land = {}

    def gather_carry(call):
        keys = GATHER_PLAN.get(call)
        return _Carry('gather', [w[n][l].astype(BF16) for n, l in keys]) if keys else None

    def reduce_carry(call):
        keys = REDUCE_PLAN.get(call)
        return _Carry('reduce', [pending[k] for k in keys]) if keys else None

    def pend(n, l, g):
        pending[(n, l)] = g.reshape((NSH, 2) + HALF_SHAPE[n])

    small_in = jnp.concatenate([
        jnp.pad(w['meta_tokens'], ((0, 0), (0, 128))), w['attn_b_qkv'], jnp.pad(w['attn_b_o'], ((0, 0), (0, 128))),
        jnp.pad(w['conv_w'][0], ((0, 0), (0, 128))), jnp.pad(w['pool_scale'], ((0, 0), (0, 128)))], axis=0)
    gsmall, GW[('attn_w_qkv', 0)], GW[('attn_w_o', 0)] = _flush(
        _Carry('gather', [small_in, w['attn_w_qkv'][0].astype(BF16), w['attn_w_o'][0].astype(BF16)]), "gather_first")

    def cols(rows, width):
        return jnp.transpose(rows[:, :, :width], (1, 0, 2)).reshape(rows.shape[1], NSH * width)

    meta_full = cols(gsmall[:, 0:16], 256)
    b_qkv = cols(gsmall[:, 16:18], 384)
    b_o = cols(gsmall[:, 18:20], 256)
    conv_k = cols(gsmall[:, 20:23], 256)
    p_scale = cols(gsmall[:, 23:24], 256)

    idx_np, _ = _bias_index()
    idx = jnp.asarray(idx_np).reshape(1, -1)
    table_aug = jnp.concatenate([rel_bias_table.T, jnp.full((H, 1), NEG, F32),
                                 jnp.zeros((H, 127 - N_BUCKETS), F32)], axis=1)
    bias_flat = _bias_lookup(table_aug, idx, "bias_lookup")
    nblock = BLK * NKEY
    bias_blk = jnp.transpose(bias_flat[:, :2 * nblock].reshape(H, 2, BLK, NKEY), (1, 0, 2, 3))
    bias_mm = bias_flat[:, 2 * nblock:].reshape(H, N_META, N_META)
    bias_band, bias_meta = bias_blk[..., BLK:], bias_blk[..., :N_META]

    h = jnp.concatenate([meta_full, x[0]], axis=0)
    saved = []
    for i in range(DEPTH):
        kind, j = i % 3, i // 3
        gm = norm_mix[i:i + 1]
        st = dict(h=h)
        if kind == 0:
            w_o = GW[('attn_w_o', j)].reshape(1, D, D)
            q, k, v, a = _linear(h, GW[('attn_w_qkv', j)], gain=gm, bias=b_qkv[j:j + 1], qkv_heads=True,
                                 name=f"qkv{i}")
            tr = lambda t: jnp.swapaxes(t, 1, 2)
            lay = dict(q=q, qt=tr(q[:, N_META:]), qm=q[:, :N_META],
                       k=k, kt=tr(k[:, N_META:]), km=k[:, :N_META],
                       kmp=_pad_block(k[:, :N_META], 1), kmt=_pad_block(tr(k[:, :N_META]), 2),
                       v=v, vt=tr(v[:, N_META:]), vm=v[:, :N_META], vmt=_pad_block(tr(v[:, :N_META]), 2))
            sink_r = jnp.broadcast_to(attn_sinks[j].reshape(KV, GRP, 1, 1), (KV, GRP, BLK, 1)).reshape(KV, GRP * BLK, 1)
            sink_m = jnp.broadcast_to(attn_sinks[j].reshape(H, 1, 1), (H, N_META, 1))
            (o_r,), got = _attn_fwd(lay['q'], lay['k'], lay['km'], lay['v'], lay['vm'], bias_band, bias_meta, sink_r,
                                    f"attn_fwd{i}", gather_carry(f"attn_fwd{i}"))
            GW.update(zip(GATHER_PLAN.get(f"attn_fwd{i}", []), got))
            o_m = _attn_meta_fwd(lay['qm'], lay['km'], lay['vm'], bias_mm, sink_m, f"attn_meta_fwd{i}")
            o = _unheads(jnp.concatenate([o_m, o_r], axis=1))
            h1 = _linear(o, w_o, bias=b_o[j:j + 1], resid=h, out_dtype=F32, name=f"attn_out{i}")
            st.update(a=a, lay=lay, sinks=(sink_r, sink_m), o=o, w_o=w_o)
        elif kind == 1:
            z3, a = _linear(h, GW[('conv_w_in', j)], gain=gm, name=f"conv_in{i}")
            yv = _conv_fwd(z3, conv_k, f"conv_fwd{i}")
            w_cout = GW[('conv_w_out', j)].reshape(1, D, D)
            h1 = _linear(yv, w_cout, resid=h, out_dtype=F32, name=f"conv_out{i}")
            st.update(a=a, z3=z3, y=yv, w_cout=w_cout)
        else:
            w_pool = jnp.transpose(GW[('pool_w', j)], (1, 0, 2, 3)).reshape(4, PG, PG)
            mix = _pool_fwd(h, gm, f"pool_fwd{i}")
            h1 = _pool_out(mix, w_pool, p_scale, h, f"pool_out{i}")
            st.update(mix=mix, w_pool=w_pool)
        ffn_w = [GW[k] for k in _ffn_keys(i)]
        (hn, Gp, Up, Ap, Bp), got = _ffn_fwd(h1, norm_ffn[i:i + 1], *ffn_w, f"ffn_fwd{i}", gather_carry(f"ffn_fwd{i}"))
        GW.update(zip(GATHER_PLAN.get(f"ffn_fwd{i}", []), got))
        st.update(h1=h1, G=Gp, U=Up, A=Ap, B=Bp, ffn_w=ffn_w)
        saved.append(st)
        h = hn

    target = jnp.pad(loss_target[0], ((N_META, 0), (0, 0)))
    dh, d_nfinal, loss_part = _loss_head(h, norm_final.reshape(1, D), target)
    d_nmix, d_nffn = [None] * DEPTH, [None] * DEPTH
    d_bqkv, d_bo, dflats = [None, None], [None, None], [None, None]
    d_convk = d_pscale = None
    for i in reversed(range(DEPTH)):
        kind, j = i % 3, i // 3
        st = saved[i]
        call = f"ffn_bwd_act{i}"
        (DG, DU, dh1, d_nffn[i]), got = _ffn_bwd_act(dh, st['h1'], norm_ffn[i:i + 1], st['G'], st['U'], *st['ffn_w'],
                                                     call, reduce_carry(call))
        land.update(zip(REDUCE_PLAN.get(call, []), got))
        call = f"ffn_bwd_w{i}"
        (gw, uw, dw_), got = _ffn_bwd_w(st['B'], st['A'], DG, DU, dh, call, reduce_carry(call))
        land.update(zip(REDUCE_PLAN.get(call, []), got))
        pend('ffn_w_gate', i, gw)
        pend('ffn_w_up', i, uw)
        pend('ffn_w_down', i, dw_)
        gm = norm_mix[i:i + 1]
        if kind == 0:
            lay = st['lay']
            sink_r, sink_m = st['sinks']
            gwo, d_bo[j] = _wgrad(st['o'], dh1, mode='row', colsum=True, name=f"attn_out_wgrad{i}")
            pend('attn_w_o', j, gwo)
            do = _linear_t(dh1, st['w_o'], to_heads=True, name=f"attn_out_bwd{i}")
            call = f"attn_bwd{i}"
            (dq_r, dkt, dvt, dkmt, dvmt, dbias), got = _attn_bwd(
                lay['q'], lay['qt'], lay['k'], lay['kmp'], lay['kt'], lay['kmt'], lay['vt'], lay['vmt'], bias_blk,
                sink_r, do, jnp.swapaxes(do[:, N_META:], 1, 2), call, reduce_carry(call))
            land.update(zip(REDUCE_PLAN.get(call, []), got))
            dq_m, dkm2, dvm2, dbmm = _attn_meta_bwd(lay['qm'], lay['km'], lay['vm'], bias_mm, sink_m,
                                                    do[:, :N_META], f"attn_meta_bwd{i}")
            dflats[j] = jnp.concatenate([jnp.transpose(dbias, (1, 0, 2, 3)).reshape(H, -1), dbmm.reshape(H, -1)], axis=1)
            dkm = (jnp.transpose(dkmt[:, :, :N_META], (0, 2, 1)) + dkm2).astype(BF16)
            dvm = (jnp.transpose(dvmt[:, :, :N_META], (0, 2, 1)) + dvm2).astype(BF16)
            dqkv = jnp.concatenate([
                jnp.concatenate([_unheads(dq_m), _unheads(dq_r)], axis=0) * scale,
                jnp.concatenate([_unheads(dkm), _unheads_t(dkt)], axis=0),
                jnp.concatenate([_unheads(dvm), _unheads_t(dvt)], axis=0)], axis=1)
            gq, d_bqkv[j] = _wgrad(st['a'], dqkv, mode='col', colsum=True, name=f"qkv_wgrad{i}")
            pend('attn_w_qkv', j, gq)
            call = f"qkv_bwd{i}"
            res = _linear_t(dqkv, GW[('attn_w_qkv', j)], rms=(st['h'], gm, dh1), carry=reduce_carry(call), name=call)
            if call in REDUCE_PLAN:
                res, got = res
                land.update(zip(REDUCE_PLAN[call], got))
            dh, d_nmix[i] = res
        elif kind == 1:
            pend('conv_w_out', j, _wgrad(st['y'], dh1, mode='row', name=f"conv_out_wgrad{i}"))
            dy = _linear_t(dh1, st['w_cout'], out_dtype=F32, name=f"conv_out_bwd{i}")
            dz3, d_convk = _conv_bwd(st['z3'], conv_k, dy, f"conv_bwd{i}")
            pend('conv_w_in', j, _wgrad(st['a'], dz3, mode='col', name=f"conv_in_wgrad{i}"))
            dh, d_nmix[i] = _linear_t(dz3, GW[('conv_w_in', j)], rms=(st['h'], gm, dh1), name=f"conv_in_bwd{i}")
        else:
            dmix, dwp, d_pscale = _pool_out_bwd(dh1, st['mix'], st['w_pool'], p_scale, f"pool_out_bwd{i}")
            pend('pool_w', j, jnp.transpose(dwp.reshape(4, NSH, PG // NSH, PG), (1, 0, 2, 3)).astype(BF16))
            dh, d_nmix[i] = _pool_bwd(dmix, st['h'], gm, dh1, f"pool_bwd{i}")

    dts = _bias_lookup_bwd(jnp.concatenate(dflats, axis=0), idx, "bias_lookup_bwd")
    d_table = dts[:H, :N_BUCKETS].T + dts[H:, :N_BUCKETS].T
    d_sinks = dts[:, N_BUCKETS].reshape(2, H)

    ids = jnp.stack([chip, lax.axis_index("c")]).astype(jnp.int32)
    shared = _pair_share([_sum_leaf([land[(n, l)] for l in range(w[n].shape[0])],
                                    [pending[(n, l)] for l in range(w[n].shape[0])], ids, f"sum_{n}") for n in BIG])
    grads = {n: s.reshape(w[n].shape) for n, s in zip(BIG, shared)}

    small_full = [dh[:N_META], d_table, jnp.concatenate(d_nmix, axis=0), jnp.concatenate(d_nffn, axis=0), d_nfinal,
                  jnp.concatenate(d_bqkv, axis=0), jnp.concatenate(d_bo, axis=0), d_sinks, d_convk,
                  d_pscale, loss_part[:, 0:1]]
    red = _unpack(_allreduce_small(_pack(small_full)), [a.shape for a in small_full])
    g_meta, g_table, g_nmix, g_nffn, g_nfinal, g_bqkv, g_bo, g_sinks, g_convk, g_pscale, loss = red

    def shard(a, width):
        return lax.dynamic_slice_in_dim(a, chip * width, width, axis=1)

    grads.update(meta_tokens=shard(g_meta, 256), rel_bias_table=g_table, norm_mix=g_nmix, norm_ffn=g_nffn,
                 norm_final=g_nfinal.reshape(D), attn_b_qkv=shard(g_bqkv, 384), attn_b_o=shard(g_bo, 256),
                 attn_sinks=g_sinks, conv_w=shard(g_convk, 256)[None], pool_scale=shard(g_pscale, 256))

    delta, new_m, new_v = {}, {}, {}
    for n in BIG:
        shp = w[n].shape
        r3 = (int(np.prod(shp[:-2])),) + shp[-2:]
        swap = (lambda a: jnp.swapaxes(a, 1, 2)) if shp[-1] % 128 else (lambda a: a)
        res = _adam(*[swap(a.reshape(r3)) for a in (w[n], grads[n], mom[n], var[n])], f"adam_{n}")
        delta[n], new_m[n], new_v[n], grads[n] = [swap(a).reshape(shp) for a in res]
    shapes = [w[n].shape for n in SMALL]
    packed = [_pack([d[n] for n in SMALL])[None] for d in (w, grads, mom, var)]
    for dst, res in zip((delta, new_m, new_v), _adam(*packed, "adam_small")[:3]):
        dst.update(zip(SMALL, _unpack(res[0], shapes)))

    return (loss.reshape(()), dh[N_META:][None], *[grads[n] for n in WEIGHTS], *[delta[n] for n in WEIGHTS],
            *[new_m[n] for n in WEIGHTS], *[new_v[n] for n in WEIGHTS])
```

```python
import functools
import math

import numpy as np
import jax
import jax.numpy as jnp
from jax import lax
from jax.experimental import pallas as pl
from jax.experimental.pallas import tpu as pltpu

F32, BF16 = jnp.float32, jnp.bfloat16
D = 1024
N_META = 16
EPS = 1e-6
H, KV, GRP, HD = 16, 4, 4, 64
BLK = 128
NKEY = 3 * BLK
N_BUCKETS = 32
POOL_WINDOWS = (2, 4, 8, 16)
PG = 256
DFF = 2816
NSH = 4
FC = DFF // NSH
HALO = 16
NEG = -1e30
DEPTH = 4
LR, B1, B2, ADAM_EPS, WD, STEP = 0.001, 0.9, 0.999, 1e-08, 0.01, 10
MESH = pl.DeviceIdType.MESH
ANY = pl.BlockSpec(memory_space=pl.ANY)
VMEM_LIMIT = 48 * 1024 * 1024
VMEM_BIG = 58 * 1024 * 1024
BIG_TILE = 1024

_NN = (((1,), (0,)), ((), ()))
_NT = (((1,), (1,)), ((), ()))
_TN = (((0,), (0,)), ((), ()))

WEIGHTS = ['meta_tokens', 'rel_bias_table', 'norm_mix', 'norm_ffn', 'norm_final', 'attn_w_qkv', 'attn_b_qkv',
           'attn_w_o', 'attn_b_o', 'attn_sinks', 'conv_w_in', 'conv_w', 'conv_w_out', 'pool_w', 'pool_scale',
           'ffn_w_gate', 'ffn_w_up', 'ffn_w_down']
BIG = ['attn_w_qkv', 'attn_w_o', 'conv_w_in', 'conv_w_out', 'pool_w', 'ffn_w_gate', 'ffn_w_up', 'ffn_w_down']
SMALL = [w for w in WEIGHTS if w not in BIG]
HALF_SHAPE = {'attn_w_qkv': (512, 384), 'attn_w_o': (128, 1024), 'conv_w_in': (512, 768),
              'conv_w_out': (128, 1024), 'pool_w': (128, 256), 'ffn_w_gate': (512, 704),
              'ffn_w_up': (512, 704), 'ffn_w_down': (352, 1024)}


def _dot(a, b, dims=_NN):
    return lax.dot_general(a, b, dims, preferred_element_type=F32)


def _tile(n, cap=512):
    best = None
    for t in range(16, min(n, cap) + 1, 16):
        if n % t == 0:
            best = t
    assert best is not None, n
    return best


def _params(n_axes, vmem=VMEM_LIMIT):
    return pltpu.CompilerParams(dimension_semantics=("arbitrary",) * n_axes, vmem_limit_bytes=vmem)


def _rstd(x):
    return lax.rsqrt(jnp.mean(x * x, axis=-1, keepdims=True) + EPS)


def _rms_bwd(dy, x, g, r):
    u = dy * g
    dx = r * u - x * ((r * r * r) * (jnp.sum(x * u, axis=-1, keepdims=True) * (1.0 / D)))
    return dx, jnp.sum(dy * (x * r), axis=0, keepdims=True)


def _sigmoid(x):
    return 1.0 / (1.0 + jnp.exp(-x))


def _acc(ref, val, first):
    @pl.when(first)
    def _():
        ref[...] = val

    @pl.when(jnp.logical_not(first))
    def _():
        ref[...] += val


def _linear(x, w, *, gain=None, bias=None, resid=None, out_dtype=BF16, qkv_heads=False, cap=BIG_TILE, name):
    L, K = x.shape
    J, _, Nc = w.shape
    tm = _tile(L, cap)
    has_g, has_b, has_r = gain is not None, bias is not None, resid is not None
    n_main = 3 if qkv_heads else 1

    def body(*refs):
        refs = list(refs)
        x_ref, w_ref = refs[:2]
        i = 2
        g_ref = b_ref = r_ref = None
        if has_g:
            g_ref, i = refs[i], i + 1
        if has_b:
            b_ref, i = refs[i], i + 1
        if has_r:
            r_ref, i = refs[i], i + 1
        out_ref = refs[i]
        if has_g:
            xf = x_ref[...]
            xv = (xf * _rstd(xf) * g_ref[...]).astype(BF16)
            refs[i + n_main][...] = xv
        else:
            xv = x_ref[...]
        for s in range(J):
            sl = slice(s * Nc, (s + 1) * Nc)
            acc = _dot(xv, w_ref[s])
            if has_b:
                acc = acc + b_ref[:, sl]
            if has_r:
                acc = acc + r_ref[:, sl]
            if not qkv_heads:
                out_ref[:, sl] = acc.astype(out_dtype)
                continue
            for r in range(Nc // HD):
                hd, blk = s * (Nc // HD) + r, acc[:, r * HD:(r + 1) * HD]
                if hd < H:
                    refs[i][hd] = (blk * HD ** -0.5).astype(BF16)
                elif hd < H + KV:
                    refs[i + 1][hd - H] = blk.astype(BF16)
                else:
                    refs[i + 2][hd - H - KV] = blk.astype(BF16)

    row = lambda n: pl.BlockSpec((tm, n), lambda t: (t, 0))
    one = lambda n: pl.BlockSpec((1, n), lambda t: (0, 0))
    in_specs = [row(K), pl.BlockSpec((J, K, Nc), lambda t: (0, 0, 0))]
    ops = [x, w]
    if has_g:
        in_specs.append(one(K))
        ops.append(gain)
    if has_b:
        in_specs.append(one(J * Nc))
        ops.append(bias)
    if has_r:
        in_specs.append(row(J * Nc))
        ops.append(resid)
    if qkv_heads:
        heads = lambda n: pl.BlockSpec((n, tm, HD), lambda t: (0, t, 0))
        out_specs = [heads(H), heads(KV), heads(KV)]
        out_shape = [jax.ShapeDtypeStruct((n, L, HD), BF16) for n in (H, KV, KV)]
    else:
        out_specs, out_shape = [row(J * Nc)], [jax.ShapeDtypeStruct((L, J * Nc), out_dtype)]
    if has_g:
        out_specs, out_shape = out_specs + [row(K)], out_shape + [jax.ShapeDtypeStruct((L, K), BF16)]
    res = pl.pallas_call(body, name=name, grid=(L // tm,), in_specs=in_specs, out_specs=out_specs,
                         out_shape=out_shape, compiler_params=_params(1, VMEM_BIG))(*ops)
    return res[0] if len(res) == 1 else res


def _linear_t(dy, w, *, out_dtype=BF16, rms=None, to_heads=False, carry=None, cap=BIG_TILE, name):
    L = dy.shape[0]
    J, K, Nc = w.shape
    tm = _tile(L, cap)
    has_rms = rms is not None

    def body(*refs):
        t = pl.program_id(0)
        dy_ref, w_ref = refs[:2]
        acc = _dot(dy_ref[:, 0:Nc].astype(BF16), w_ref[0], _NT)
        for s in range(1, J):
            acc = acc + _dot(dy_ref[:, s * Nc:(s + 1) * Nc].astype(BF16), w_ref[s], _NT)
        if has_rms:
            h_ref, g_ref, r_ref, out_ref, dg_ref = refs[2:7]
            x = h_ref[...]
            dx, dg = _rms_bwd(acc, x, g_ref[...], _rstd(x))
            out_ref[...] = r_ref[...] + dx
            _acc(dg_ref, dg, t == 0)
        elif to_heads:
            for hd in range(K // HD):
                refs[2][hd] = acc[:, hd * HD:(hd + 1) * HD].astype(out_dtype)
        else:
            refs[2][...] = acc.astype(out_dtype)

    row = lambda n: pl.BlockSpec((tm, n), lambda t: (t, 0))
    in_specs = [row(J * Nc), pl.BlockSpec((J, K, Nc), lambda t: (0, 0, 0))]
    ops = [dy, w]
    if has_rms:
        in_specs += [row(K), pl.BlockSpec((1, K), lambda t: (0, 0)), row(K)]
        ops += list(rms)
        out_specs = [row(K), pl.BlockSpec((1, K), lambda t: (0, 0))]
        out_shape = [jax.ShapeDtypeStruct((L, K), F32), jax.ShapeDtypeStruct((1, K), F32)]
    elif to_heads:
        out_specs = [pl.BlockSpec((K // HD, tm, HD), lambda t: (0, t, 0))]
        out_shape = [jax.ShapeDtypeStruct((K // HD, L, HD), out_dtype)]
    else:
        out_specs = [row(K)]
        out_shape = [jax.ShapeDtypeStruct((L, K), out_dtype)]
    res, got = _carried_call(body, carry, name=name, grid=(L // tm,), in_specs=in_specs, out_specs=out_specs,
                             out_shape=out_shape, scratch_shapes=[], compiler_params=_params(1, VMEM_BIG),
                             operands=ops)
    res = res[0] if len(res) == 1 else res
    return res if carry is None else (res, got)


def _wgrad(x, dy, *, mode, colsum=False, cap=BIG_TILE, name):
    L, K = x.shape
    N = dy.shape[1]
    tm = _tile(L, cap)
    nt = L // tm
    oshape = (NSH, K, N // NSH) if mode == 'col' else (NSH, K // NSH, N)

    def body(*refs):
        t = pl.program_id(0)
        x_ref, dy_ref, out_ref = refs[:3]
        acc_ref = refs[-1]
        dyv = dy_ref[...]
        _acc(acc_ref, _dot(x_ref[...].astype(BF16), dyv.astype(BF16), _TN), t == 0)
        if colsum:
            _acc(refs[3], jnp.sum(dyv.astype(F32), axis=0, keepdims=True), t == 0)

        @pl.when(t == nt - 1)
        def _():
            for s in range(NSH):
                if mode == 'col':
                    out_ref[s] = acc_ref[:, s * oshape[2]:(s + 1) * oshape[2]].astype(BF16)
                else:
                    out_ref[s] = acc_ref[s * oshape[1]:(s + 1) * oshape[1], :].astype(BF16)

    out_specs = pl.BlockSpec(oshape, lambda t: (0, 0, 0))
    out_shape = jax.ShapeDtypeStruct(oshape, BF16)
    if colsum:
        out_specs = (out_specs, pl.BlockSpec((1, N), lambda t: (0, 0)))
        out_shape = (out_shape, jax.ShapeDtypeStruct((1, N), F32))
    return pl.pallas_call(
        body, name=name, grid=(nt,),
        in_specs=[pl.BlockSpec((tm, K), lambda t: (t, 0)), pl.BlockSpec((tm, N), lambda t: (t, 0))],
        out_specs=out_specs, out_shape=out_shape,
        scratch_shapes=[pltpu.VMEM((K, N), F32)], compiler_params=_params(1, VMEM_BIG))(x, dy)


def _ffn_fwd(h, gain, wg, wu, wd, name, carry=None):
    L = h.shape[0]
    tm = _tile(L, BIG_TILE)

    def body(h_ref, g_ref, wg_ref, wu_ref, wd_ref, hn_ref, G_ref, U_ref, A_ref, B_ref, acc_ref, b_scr):
        j = pl.program_id(1)

        @pl.when(j == 0)
        def _():
            x = h_ref[...]
            b = (x * _rstd(x) * g_ref[...]).astype(BF16)
            b_scr[...] = b
            B_ref[...] = b
            acc_ref[...] = x

        b = b_scr[...]
        g = _dot(b, wg_ref[...])
        u = _dot(b, wu_ref[...])
        s = _sigmoid(g)
        silu = g * s
        G_ref[...] = (u * (s * (1.0 + g * (1.0 - s)))).astype(BF16)
        U_ref[...] = silu.astype(BF16)
        a = (silu * u).astype(BF16)
        A_ref[...] = a
        acc_ref[...] += _dot(a, wd_ref[...])

        @pl.when(j == NSH - 1)
        def _():
            hn_ref[...] = acc_ref[...]

    row = pl.BlockSpec((tm, D), lambda t, j: (t, 0))
    chunk = pl.BlockSpec((None, tm, FC), lambda t, j: (j, t, 0))
    cshape = jax.ShapeDtypeStruct((NSH, L, FC), BF16)
    return _carried_call(
        body, carry, name=name, grid=(L // tm, NSH),
        in_specs=[row, pl.BlockSpec((1, D), lambda t, j: (0, 0)),
                  pl.BlockSpec((None, D, FC), lambda t, j: (j, 0, 0)),
                  pl.BlockSpec((None, D, FC), lambda t, j: (j, 0, 0)),
                  pl.BlockSpec((None, FC, D), lambda t, j: (j, 0, 0))],
        out_specs=[row, chunk, chunk, chunk, row],
        out_shape=[jax.ShapeDtypeStruct((L, D), F32), cshape, cshape, cshape, jax.ShapeDtypeStruct((L, D), BF16)],
        scratch_shapes=[pltpu.VMEM((tm, D), F32), pltpu.VMEM((tm, D), BF16)],
        compiler_params=_params(2, VMEM_BIG), operands=[h, gain, wg, wu, wd])


def _ffn_bwd_act(dhn, h, gain, G, U, wg, wu, wd, name, carry=None):
    L = h.shape[0]
    tm = _tile(L, BIG_TILE)

    def body(dhn_ref, h_ref, g_ref, G_ref, U_ref, wg_ref, wu_ref, wd_ref, DG_ref, DU_ref, dh_ref, dgain_ref, db_ref):
        t, j = pl.program_id(0), pl.program_id(1)
        d_act = _dot(dhn_ref[...].astype(BF16), wd_ref[...], _NT)
        dg = (d_act * G_ref[...].astype(F32)).astype(BF16)
        du = (d_act * U_ref[...].astype(F32)).astype(BF16)
        DG_ref[...] = dg
        DU_ref[...] = du
        _acc(db_ref, _dot(dg, wg_ref[...], _NT) + _dot(du, wu_ref[...], _NT), j == 0)

        @pl.when(j == NSH - 1)
        def _():
            x = h_ref[...]
            dx, dgn = _rms_bwd(db_ref[...], x, g_ref[...], _rstd(x))
            dh_ref[...] = dhn_ref[...] + dx
            _acc(dgain_ref, dgn, t == 0)

    row = pl.BlockSpec((tm, D), lambda t, j: (t, 0))
    one = pl.BlockSpec((1, D), lambda t, j: (0, 0))
    chunk = pl.BlockSpec((None, tm, FC), lambda t, j: (j, t, 0))
    cshape = jax.ShapeDtypeStruct((NSH, L, FC), BF16)
    return _carried_call(
        body, carry, name=name, grid=(L // tm, NSH),
        in_specs=[row, row, one, chunk, chunk,
                  pl.BlockSpec((None, D, FC), lambda t, j: (j, 0, 0)),
                  pl.BlockSpec((None, D, FC), lambda t, j: (j, 0, 0)),
                  pl.BlockSpec((None, FC, D), lambda t, j: (j, 0, 0))],
        out_specs=[chunk, chunk, row, one],
        out_shape=[cshape, cshape, jax.ShapeDtypeStruct((L, D), F32), jax.ShapeDtypeStruct((1, D), F32)],
        scratch_shapes=[pltpu.VMEM((tm, D), F32)],
        compiler_params=_params(2, VMEM_BIG), operands=[dhn, h, gain, G, U, wg, wu, wd])


def _ffn_bwd_w(B, A, DG, DU, dhn, name, carry=None):
    L = B.shape[0]
    tm = _tile(L, BIG_TILE)
    nt = L // tm

    def body(B_ref, A_ref, DG_ref, DU_ref, dhn_ref, dwg_ref, dwu_ref, dwd_ref, ag, au, ad):
        t = pl.program_id(1)
        b = B_ref[...]
        _acc(ag, _dot(b, DG_ref[...], _TN), t == 0)
        _acc(au, _dot(b, DU_ref[...], _TN), t == 0)
        _acc(ad, _dot(A_ref[...], dhn_ref[...].astype(BF16), _TN), t == 0)

        @pl.when(t == nt - 1)
        def _():
            dwg_ref[...] = ag[...].astype(BF16)
            dwu_ref[...] = au[...].astype(BF16)
            dwd_ref[...] = ad[...].astype(BF16)

    row = pl.BlockSpec((tm, D), lambda j, t: (t, 0))
    chunk = pl.BlockSpec((None, tm, FC), lambda j, t: (j, t, 0))
    return _carried_call(
        body, carry, name=name, grid=(NSH, nt), in_specs=[row, chunk, chunk, chunk, row],
        out_specs=[pl.BlockSpec((None, D, FC), lambda j, t: (j, 0, 0)),
                   pl.BlockSpec((None, D, FC), lambda j, t: (j, 0, 0)),
                   pl.BlockSpec((None, FC, D), lambda j, t: (j, 0, 0))],
        out_shape=[jax.ShapeDtypeStruct((NSH, D, FC), BF16), jax.ShapeDtypeStruct((NSH, D, FC), BF16),
                   jax.ShapeDtypeStruct((NSH, FC, D), BF16)],
        scratch_shapes=[pltpu.VMEM((D, FC), F32), pltpu.VMEM((D, FC), F32), pltpu.VMEM((FC, D), F32)],
        compiler_params=_params(2, VMEM_BIG), operands=[B, A, DG, DU, dhn])


def _loss_head(h, gain, target):
    L = h.shape[0]
    tm = _tile(L)

    def body(h_ref, g_ref, tgt_ref, dh_ref, dg_ref, loss_ref):
        t = pl.program_id(0)
        x = h_ref[...]
        g = g_ref[...]
        r = _rstd(x)
        rows = t * tm + lax.broadcasted_iota(jnp.int32, (tm, 1), 0)
        diff = jnp.where(rows >= N_META, x * r * g - tgt_ref[...], 0.0)
        part = 0.5 * jnp.sum(jnp.sum(diff * diff, axis=-1, keepdims=True) * (1.0 / D), axis=0, keepdims=True)
        dx, dg = _rms_bwd(diff * (1.0 / D), x, g, r)
        dh_ref[...] = dx
        _acc(dg_ref, dg, t == 0)
        _acc(loss_ref, jnp.broadcast_to(part, (1, 128)), t == 0)

    row = pl.BlockSpec((tm, D), lambda t: (t, 0))
    one = pl.BlockSpec((1, D), lambda t: (0, 0))
    return pl.pallas_call(
        body, name="loss_head", grid=(L // tm,), in_specs=[row, one, row],
        out_specs=(row, one, pl.BlockSpec((1, 128), lambda t: (0, 0))),
        out_shape=(jax.ShapeDtypeStruct((L, D), F32), jax.ShapeDtypeStruct((1, D), F32),
                   jax.ShapeDtypeStruct((1, 128), F32)),
        compiler_params=_params(1))(h, gain, target)


def _conv_fwd(z3, cw, name):
    L = z3.shape[0]
    tm = _tile(L)

    def body(b_ref, c_ref, u_ref, ch_ref, uh_ref, w_ref, y_ref, buf):
        t = pl.program_id(0)
        z = c_ref[...].astype(F32) * u_ref[...].astype(F32)
        buf[pl.ds(0, HALO), :] = jnp.where(t > 0, ch_ref[...].astype(F32) * uh_ref[...].astype(F32), 0.0)
        buf[pl.ds(HALO, tm), :] = z
        w0, w1, w2 = w_ref[0:1, :], w_ref[1:2, :], w_ref[2:3, :]
        conv = w2 * z + w1 * buf[pl.ds(HALO - 1, tm), :] + w0 * buf[pl.ds(HALO - 2, tm), :]
        y_ref[...] = (b_ref[...].astype(F32) * conv).astype(BF16)

    def col(k):
        return pl.BlockSpec((tm, D), lambda t: (t, k))

    def hcol(k):
        return pl.BlockSpec((HALO, D), lambda t: (jnp.maximum(t * (tm // HALO) - 1, 0), k))

    return pl.pallas_call(
        body, name=name, grid=(L // tm,),
        in_specs=[col(0), col(1), col(2), hcol(1), hcol(2), pl.BlockSpec((3, D), lambda t: (0, 0))],
        out_specs=pl.BlockSpec((tm, D), lambda t: (t, 0)),
        out_shape=jax.ShapeDtypeStruct((L, D), BF16),
        scratch_shapes=[pltpu.VMEM((HALO + tm, D), F32)], compiler_params=_params(1))(z3, z3, z3, z3, z3, cw)


def _conv_bwd(z3, cw, dy, name):
    L = z3.shape[0]
    tm = _tile(L)
    nt = L // tm
    last_h = L // HALO - 1

    def body(b_ref, c_ref, u_ref, ch_ref, uh_ref, w_ref, dy_ref, bn_ref, dyn_ref, dz3_ref, dw_ref, zbuf, dbuf):
        t = pl.program_id(0)
        w0, w1, w2 = w_ref[0:1, :], w_ref[1:2, :], w_ref[2:3, :]
        bgate, cgate, u = b_ref[...].astype(F32), c_ref[...].astype(F32), u_ref[...].astype(F32)
        z = cgate * u
        zbuf[pl.ds(0, HALO), :] = jnp.where(t > 0, ch_ref[...].astype(F32) * uh_ref[...].astype(F32), 0.0)
        zbuf[pl.ds(HALO, tm), :] = z
        z1 = zbuf[pl.ds(HALO - 1, tm), :]
        z2 = zbuf[pl.ds(HALO - 2, tm), :]
        conv = w2 * z + w1 * z1 + w0 * z2
        dyv = dy_ref[...]
        dconv = dyv * bgate
        dbuf[pl.ds(0, tm), :] = dconv
        dbuf[pl.ds(tm, HALO), :] = jnp.where(t < nt - 1, dyn_ref[...] * bn_ref[...].astype(F32), 0.0)
        dz = w2 * dconv + w1 * dbuf[pl.ds(1, tm), :] + w0 * dbuf[pl.ds(2, tm), :]
        dz3_ref[:, 0:D] = (dyv * conv).astype(BF16)
        dz3_ref[:, D:2 * D] = (dz * u).astype(BF16)
        dz3_ref[:, 2 * D:3 * D] = (dz * cgate).astype(BF16)
        for k, zk in enumerate((z2, z1, z)):
            _acc(dw_ref.at[k:k + 1], jnp.sum(dconv * zk, axis=0, keepdims=True), t == 0)

    def col(k):
        return pl.BlockSpec((tm, D), lambda t: (t, k))

    def hprev(k):
        return pl.BlockSpec((HALO, D), lambda t: (jnp.maximum(t * (tm // HALO) - 1, 0), k))

    def hnext(k):
        return pl.BlockSpec((HALO, D), lambda t: (jnp.minimum((t + 1) * (tm // HALO), last_h), k))

    return pl.pallas_call(
        body, name=name, grid=(nt,),
        in_specs=[col(0), col(1), col(2), hprev(1), hprev(2), pl.BlockSpec((3, D), lambda t: (0, 0)),
                  col(0), hnext(0), hnext(0)],
        out_specs=(pl.BlockSpec((tm, 3 * D), lambda t: (t, 0)), pl.BlockSpec((3, D), lambda t: (0, 0))),
        out_shape=(jax.ShapeDtypeStruct((L, 3 * D), BF16), jax.ShapeDtypeStruct((3, D), F32)),
        scratch_shapes=[pltpu.VMEM((HALO + tm, D), F32), pltpu.VMEM((tm + HALO, D), F32)],
        compiler_params=_params(1))(z3, z3, z3, z3, z3, cw, dy, z3, dy)


def _count_inv(pos, w):
    return 1.0 / jnp.minimum(pos + 1, w).astype(F32)


def _pool_fwd(h, gain, name):
    L = h.shape[0]
    tm = _tile(L)

    def body(h_ref, hh_ref, g_ref, mix_ref, buf):
        t = pl.program_id(0)
        g = g_ref[...]
        x = h_ref[...]
        a = x * _rstd(x) * g
        xh = hh_ref[...]
        buf[pl.ds(0, HALO), :] = jnp.where(t > 0, xh * _rstd(xh) * g, 0.0)
        buf[pl.ds(HALO, tm), :] = a
        pos = t * tm + lax.broadcasted_iota(jnp.int32, (tm, 1), 0)
        for gi, w in enumerate(POOL_WINDOWS):
            cols = pl.ds(gi * PG, PG)
            s = buf[pl.ds(HALO, tm), cols]
            for k in range(1, w):
                s = s + buf[pl.ds(HALO - k, tm), cols]
            mix_ref[:, gi * PG:(gi + 1) * PG] = (s / jnp.minimum(pos + 1, w).astype(F32)
                                                  - buf[pl.ds(HALO, tm), cols]).astype(BF16)

    return pl.pallas_call(
        body, name=name, grid=(L // tm,),
        in_specs=[pl.BlockSpec((tm, D), lambda t: (t, 0)),
                  pl.BlockSpec((HALO, D), lambda t: (jnp.maximum(t * (tm // HALO) - 1, 0), 0)),
                  pl.BlockSpec((1, D), lambda t: (0, 0))],
        out_specs=pl.BlockSpec((tm, D), lambda t: (t, 0)),
        out_shape=jax.ShapeDtypeStruct((L, D), BF16),
        scratch_shapes=[pltpu.VMEM((HALO + tm, D), F32)], compiler_params=_params(1))(h, h, gain)


def _pool_out(mix, wp, scale, h, name):
    L = mix.shape[0]
    tm = _tile(L)

    def body(mix_ref, wp_ref, s_ref, h_ref, out_ref):
        for gi in range(4):
            sl = slice(gi * PG, (gi + 1) * PG)
            pre = _dot(mix_ref[:, sl], wp_ref[gi])
            out_ref[:, sl] = h_ref[:, sl] + pre * s_ref[:, sl]

    row = pl.BlockSpec((tm, D), lambda t: (t, 0))
    return pl.pallas_call(
        body, name=name, grid=(L // tm,),
        in_specs=[row, pl.BlockSpec((4, PG, PG), lambda t: (0, 0, 0)), pl.BlockSpec((1, D), lambda t: (0, 0)), row],
        out_specs=row, out_shape=jax.ShapeDtypeStruct((L, D), F32), compiler_params=_params(1))(mix, wp, scale, h)


def _pool_out_bwd(dm, mix, wp, scale, name):
    L = mix.shape[0]
    tm = _tile(L)

    def body(dm_ref, mix_ref, wp_ref, s_ref, dmix_ref, dwp_ref, ds_ref):
        t = pl.program_id(0)
        for gi in range(4):
            sl = slice(gi * PG, (gi + 1) * PG)
            mx = mix_ref[:, sl]
            dmv = dm_ref[:, sl]
            pre = _dot(mx, wp_ref[gi])
            _acc(ds_ref.at[:, sl], jnp.sum(dmv * pre, axis=0, keepdims=True), t == 0)
            dpre = (dmv * s_ref[:, sl]).astype(BF16)
            _acc(dwp_ref.at[gi], _dot(mx, dpre, _TN), t == 0)
            dmix_ref[:, sl] = _dot(dpre, wp_ref[gi], _NT)

    row = pl.BlockSpec((tm, D), lambda t: (t, 0))
    one = pl.BlockSpec((1, D), lambda t: (0, 0))
    wsp = pl.BlockSpec((4, PG, PG), lambda t: (0, 0, 0))
    return pl.pallas_call(
        body, name=name, grid=(L // tm,), in_specs=[row, row, wsp, one],
        out_specs=(row, wsp, one),
        out_shape=(jax.ShapeDtypeStruct((L, D), F32), jax.ShapeDtypeStruct((4, PG, PG), F32),
                   jax.ShapeDtypeStruct((1, D), F32)),
        compiler_params=_params(1))(dm, mix, wp, scale)


def _pool_bwd(dmix, h, gain, resid, name):
    L = h.shape[0]
    tm = _tile(L)
    nt = L // tm
    last_h = L // HALO - 1

    def body(dm_ref, dmn_ref, h_ref, g_ref, r_ref, dh_ref, dg_ref, buf):
        t = pl.program_id(0)
        pos = t * tm + lax.broadcasted_iota(jnp.int32, (tm, 1), 0)
        posn = (t + 1) * tm + lax.broadcasted_iota(jnp.int32, (HALO, 1), 0)
        dmv = dm_ref[...]
        dmn = dmn_ref[...]
        for gi, w in enumerate(POOL_WINDOWS):
            sl = slice(gi * PG, (gi + 1) * PG)
            buf[pl.ds(0, tm), sl] = dmv[:, sl] * _count_inv(pos, w)
            buf[pl.ds(tm, HALO), sl] = jnp.where(t < nt - 1, dmn[:, sl] * _count_inv(posn, w), 0.0)
        parts = []
        for gi, w in enumerate(POOL_WINDOWS):
            cols = pl.ds(gi * PG, PG)
            s = buf[pl.ds(0, tm), cols]
            for k in range(1, w):
                s = s + buf[pl.ds(k, tm), cols]
            parts.append(s)
        da = jnp.concatenate(parts, axis=1) - dmv
        x = h_ref[...]
        dx, dg = _rms_bwd(da, x, g_ref[...], _rstd(x))
        dh_ref[...] = r_ref[...] + dx
        _acc(dg_ref, dg, t == 0)

    row = pl.BlockSpec((tm, D), lambda t: (t, 0))
    one = pl.BlockSpec((1, D), lambda t: (0, 0))
    return pl.pallas_call(
        body, name=name, grid=(nt,),
        in_specs=[row, pl.BlockSpec((HALO, D), lambda t: (jnp.minimum((t + 1) * (tm // HALO), last_h), 0)),
                  row, one, row],
        out_specs=(row, one),
        out_shape=(jax.ShapeDtypeStruct((L, D), F32), jax.ShapeDtypeStruct((1, D), F32)),
        scratch_shapes=[pltpu.VMEM((tm + HALO, D), F32)], compiler_params=_params(1))(dmix, dmix, h, gain, resid)


def _bucket_np(d):
    d = np.maximum(d, 0)
    df = np.maximum(d, 1).astype(np.float32)
    large = 16 + (np.log(df / np.float32(16)) / np.float32(math.log(128 / 16)) * np.float32(16)).astype(np.int32)
    return np.where(d < 16, d, np.minimum(large, N_BUCKETS - 1))


def _bias_index():
    iq = np.arange(BLK)[:, None]
    jk = np.arange(2 * BLK)[None, :]
    dist = BLK + iq - jk
    band = _bucket_np(dist)
    ok = (dist >= 0) & (dist < BLK)
    band1 = np.where(ok, band, N_BUCKETS)
    band0 = np.where(ok & (jk >= BLK), band, N_BUCKETS)
    im = np.arange(N_META)[None, :]
    unused = np.full((BLK, BLK - N_META), N_BUCKETS)
    var0 = np.concatenate([_bucket_np(N_META + iq - im), unused, band0], axis=1)
    var1 = np.concatenate([_bucket_np(N_META + BLK + iq - im), unused, band1], axis=1)
    dm = np.arange(N_META)[:, None] - im
    mm = np.where(dm >= 0, _bucket_np(dm), N_BUCKETS)
    segs = [var0, var1, mm]
    return np.concatenate([s.reshape(-1) for s in segs]).astype(np.int32), [s.shape for s in segs]


P_CHUNK = 9856


def _onehot(idx_ref, grad):
    rows = lax.broadcasted_iota(jnp.int32, (128, P_CHUNK), 0)
    hit = (rows == idx_ref[...]).astype(F32)
    return jnp.where(rows == N_BUCKETS, -1.0, hit) if grad else hit


def _bias_lookup(table_aug, idx, name, carry=None):
    P = idx.shape[1]

    def body(t_ref, idx_ref, o_ref):
        o_ref[...] = lax.dot_general(t_ref[...], _onehot(idx_ref, False), _NN, precision=lax.Precision.HIGHEST,
                                     preferred_element_type=F32)

    return _carried_call(
        body, carry, name=name, grid=(P // P_CHUNK,),
        in_specs=[pl.BlockSpec((H, 128), lambda i: (0, 0)), pl.BlockSpec((1, P_CHUNK), lambda i: (0, i))],
        out_specs=[pl.BlockSpec((H, P_CHUNK), lambda i: (0, i))],
        out_shape=[jax.ShapeDtypeStruct((H, P), F32)], scratch_shapes=[], compiler_params=_params(1),
        operands=[table_aug, idx])


def _bias_lookup_bwd(dbias, idx, name):
    rows, P = dbias.shape

    def body(d_ref, idx_ref, o_ref):
        part = lax.dot_general(d_ref[...], _onehot(idx_ref, True), _NT, precision=lax.Precision.HIGHEST,
                               preferred_element_type=F32)
        _acc(o_ref, part, pl.program_id(0) == 0)

    return pl.pallas_call(
        body, name=name, grid=(P // P_CHUNK,),
        in_specs=[pl.BlockSpec((rows, P_CHUNK), lambda i: (0, i)), pl.BlockSpec((1, P_CHUNK), lambda i: (0, i))],
        out_specs=pl.BlockSpec((rows, 128), lambda i: (0, 0)),
        out_shape=jax.ShapeDtypeStruct((rows, 128), F32), compiler_params=_params(1))(dbias, idx)


def _probs(q, kbt, bias, sink):
    s = _dot(q, kbt) + bias
    m = jnp.maximum(jnp.max(s, axis=-1, keepdims=True), sink)
    e = jnp.exp(s - m)
    return e * (1.0 / (jnp.sum(e, axis=-1, keepdims=True) + jnp.exp(sink - m)))


def _attn_specs(nb):
    def cur(kh, n):
        return jnp.minimum(n, nb - 1)

    def prev(kh, n):
        return jnp.maximum(jnp.minimum(n, nb - 1) - 1, 0)

    def rows(heads, blk):
        return pl.BlockSpec((pl.Element(heads), pl.Element(BLK), pl.Element(HD)),
                            lambda kh, n: (kh * heads, pl.multiple_of(N_META + blk(kh, n) * BLK, N_META), 0))

    return dict(
        q=rows(GRP, cur),
        qo=pl.BlockSpec((GRP, BLK, HD), lambda kh, n: (kh, cur(kh, n), 0)),
        qt=pl.BlockSpec((GRP, HD, BLK), lambda kh, n: (kh, 0, cur(kh, n))),
        cur=rows(1, cur),
        prev=rows(1, prev),
        meta=pl.BlockSpec((None, BLK, HD), lambda kh, n: (kh, 0, 0)),
        curt=pl.BlockSpec((None, HD, BLK), lambda kh, n: (kh, 0, cur(kh, n))),
        prevt=pl.BlockSpec((None, HD, BLK), lambda kh, n: (kh, 0, prev(kh, n))),
        metat=pl.BlockSpec((None, HD, BLK), lambda kh, n: (kh, 0, 0)),
        bias=pl.BlockSpec((None, GRP, BLK, NKEY), lambda kh, n: (jnp.minimum(n, 1), kh, 0, 0)),
        sink=pl.BlockSpec((None, GRP * BLK, 1), lambda kh, n: (kh, 0, 0)))


def _attn_fwd(q, k, km, v, vm, band, bmeta, sink, name, carry=None):
    S = q.shape[1] - N_META
    nb = S // BLK
    sp = _attn_specs(nb)
    var = lambda kh, n: (jnp.minimum(n, 1), kh, 0, 0)
    meta = pl.BlockSpec((None, N_META, HD), lambda kh, n: (kh, 0, 0))

    def body(q_ref, kc_ref, kp_ref, vc_ref, vp_ref, km_ref, vm_ref, band_ref, bm_ref, sink_ref, o_ref):
        qv = q_ref[...].reshape(GRP * BLK, HD)
        kb = jnp.concatenate([kp_ref[0], kc_ref[0]], axis=0)
        vb = jnp.concatenate([vp_ref[0], vc_ref[0]], axis=0)
        sink = sink_ref[...]
        s_b = _dot(qv, kb, _NT) + band_ref[...].reshape(GRP * BLK, 2 * BLK)
        s_m = _dot(qv, km_ref[...], _NT) + bm_ref[...].reshape(GRP * BLK, N_META)
        m = jnp.maximum(jnp.maximum(jnp.max(s_b, axis=-1, keepdims=True), jnp.max(s_m, axis=-1, keepdims=True)), sink)
        e_b = jnp.exp(s_b - m)
        e_m = jnp.exp(s_m - m)
        inv = 1.0 / (jnp.sum(e_b, axis=-1, keepdims=True) + jnp.sum(e_m, axis=-1, keepdims=True) + jnp.exp(sink - m))
        o = _dot((e_b * inv).astype(BF16), vb) + _dot((e_m * inv).astype(BF16), vm_ref[...])
        o_ref[...] = o.reshape(GRP, BLK, HD).astype(BF16)

    return _carried_call(
        body, carry, name=name, grid=(KV, nb),
        in_specs=[sp['q'], sp['cur'], sp['prev'], sp['cur'], sp['prev'], meta, meta,
                  pl.BlockSpec((None, GRP, BLK, 2 * BLK), var), pl.BlockSpec((None, GRP, BLK, N_META), var),
                  sp['sink']],
        out_specs=[sp['qo']], out_shape=[jax.ShapeDtypeStruct((H, S, HD), BF16)], scratch_shapes=[],
        compiler_params=_params(2), operands=[q, k, k, v, v, km, vm, band, bmeta, sink])


def _attn_bwd(q, qt, k, km, kt, kmt, vt, vmt, bias, sink, do, dot_, name, carry=None):
    S = q.shape[1] - N_META
    nb = S // BLK
    sp = _attn_specs(nb)

    def body(q_ref, qt_ref, kc_ref, kp_ref, km_ref, kct_ref, kpt_ref, kmt_ref, vct_ref, vpt_ref, vmt_ref,
             bias_ref, sink_ref, do_ref, dot_ref, dq_ref, dkt_ref, dvt_ref, dkmt_ref, dvmt_ref, dbias_ref, ck, cv):
        n = pl.program_id(1)

        @pl.when(n < nb)
        def _():
            kbt = jnp.concatenate([kmt_ref[...], kpt_ref[...], kct_ref[...]], axis=1)
            vbt = jnp.concatenate([vmt_ref[...], vpt_ref[...], vct_ref[...]], axis=1)
            kb = jnp.concatenate([km_ref[...], kp_ref[0], kc_ref[0]], axis=0)
            p = _probs(q_ref[...].reshape(GRP * BLK, HD), kbt, bias_ref[...].reshape(GRP * BLK, NKEY), sink_ref[...])
            dp = _dot(do_ref[...].reshape(GRP * BLK, HD), vbt)
            ds = p * (dp - jnp.sum(p * dp, axis=-1, keepdims=True))
            _acc(dbias_ref, ds.reshape(GRP, BLK, NKEY), n <= 1)
            ds16 = ds.astype(BF16)
            dq_ref[...] = _dot(ds16, kb).reshape(GRP, BLK, HD).astype(BF16)
            qtv = jnp.concatenate([qt_ref[g] for g in range(GRP)], axis=1)
            dotv = jnp.concatenate([dot_ref[g] for g in range(GRP)], axis=1)
            dkt = _dot(qtv, ds16)
            dvt = _dot(dotv, p.astype(BF16))
            _acc(dkmt_ref, dkt[:, 0:BLK], n == 0)
            _acc(dvmt_ref, dvt[:, 0:BLK], n == 0)

            @pl.when(n >= 1)
            def _():
                dkt_ref[...] = (ck[...] + dkt[:, BLK:2 * BLK]).astype(BF16)
                dvt_ref[...] = (cv[...] + dvt[:, BLK:2 * BLK]).astype(BF16)

            ck[...] = dkt[:, 2 * BLK:3 * BLK]
            cv[...] = dvt[:, 2 * BLK:3 * BLK]

        @pl.when(n == nb)
        def _():
            dkt_ref[...] = ck[...].astype(BF16)
            dvt_ref[...] = cv[...].astype(BF16)

    kvout = pl.BlockSpec((None, HD, BLK), lambda kh, n: (kh, 0, jnp.maximum(n - 1, 0)))
    return _carried_call(
        body, carry, name=name, grid=(KV, nb + 1),
        in_specs=[sp['q'], sp['qt'], sp['cur'], sp['prev'], sp['meta'], sp['curt'], sp['prevt'], sp['metat'],
                  sp['curt'], sp['prevt'], sp['metat'], sp['bias'], sp['sink'], sp['q'], sp['qt']],
        out_specs=[sp['qo'], kvout, kvout, sp['metat'], sp['metat'], sp['bias']],
        out_shape=[jax.ShapeDtypeStruct((H, S, HD), BF16), jax.ShapeDtypeStruct((KV, HD, S), BF16),
                   jax.ShapeDtypeStruct((KV, HD, S), BF16), jax.ShapeDtypeStruct((KV, HD, BLK), F32),
                   jax.ShapeDtypeStruct((KV, HD, BLK), F32), jax.ShapeDtypeStruct((2, H, BLK, NKEY), F32)],
        scratch_shapes=[pltpu.VMEM((HD, BLK), F32), pltpu.VMEM((HD, BLK), F32)],
        compiler_params=_params(2), operands=[q, qt, k, k, km, kt, kt, kmt, vt, vt, vmt, bias, sink, do, dot_])


def _meta_softmax(q, k, bias, sink):
    s = _dot(q, k, _NT) + bias
    m = jnp.maximum(jnp.max(s, axis=-1, keepdims=True), sink)
    e = jnp.exp(s - m)
    return e * (1.0 / (jnp.sum(e, axis=-1, keepdims=True) + jnp.exp(sink - m)))


def _attn_meta_fwd(qm, km, vm, bias, sink, name):
    def body(q_ref, k_ref, v_ref, b_ref, s_ref, o_ref):
        for h in range(H):
            p = _meta_softmax(q_ref[h], k_ref[h // GRP], b_ref[h], s_ref[h])
            o_ref[h] = _dot(p.astype(BF16), v_ref[h // GRP]).astype(BF16)

    return pl.pallas_call(body, name=name, out_shape=jax.ShapeDtypeStruct((H, N_META, HD), BF16))(
        qm, km, vm, bias, sink)


def _attn_meta_bwd(qm, km, vm, bias, sink, do, name):
    def body(q_ref, k_ref, v_ref, b_ref, s_ref, do_ref, dq_ref, dk_ref, dv_ref, db_ref):
        for kh in range(KV):
            k, v = k_ref[kh], v_ref[kh]
            dk = jnp.zeros((N_META, HD), F32)
            dv = jnp.zeros((N_META, HD), F32)
            for g in range(GRP):
                h = kh * GRP + g
                q, dov = q_ref[h], do_ref[h]
                p = _meta_softmax(q, k, b_ref[h], s_ref[h])
                dp = _dot(dov, v, _NT)
                ds = p * (dp - jnp.sum(p * dp, axis=-1, keepdims=True))
                db_ref[h] = ds
                ds16 = ds.astype(BF16)
                dq_ref[h] = _dot(ds16, k).astype(BF16)
                dk = dk + _dot(ds16, q, _TN)
                dv = dv + _dot(p.astype(BF16), dov, _TN)
            dk_ref[kh] = dk
            dv_ref[kh] = dv

    return pl.pallas_call(
        body, name=name,
        out_shape=(jax.ShapeDtypeStruct((H, N_META, HD), BF16), jax.ShapeDtypeStruct((KV, N_META, HD), F32),
                   jax.ShapeDtypeStruct((KV, N_META, HD), F32), jax.ShapeDtypeStruct((H, N_META, N_META), F32)))(
        qm, km, vm, bias, sink, do)


def _unheads(t):
    return jnp.transpose(t, (1, 0, 2)).reshape(t.shape[1], t.shape[0] * HD)


def _unheads_t(t):
    return jnp.transpose(t, (2, 0, 1)).reshape(t.shape[2], t.shape[0] * HD)


def _pad_block(t, axis):
    pad = [(0, 0)] * t.ndim
    pad[axis] = (0, BLK - N_META)
    return jnp.pad(t, pad)


def _adam(w, g, m, v, name):
    lead, R, C = w.shape
    tr = _tile(R, 512) if R % 16 == 0 else R

    def body(w_ref, g_ref, m_ref, v_ref, d_ref, mo_ref, vo_ref, go_ref):
        gv = g_ref[...]
        go_ref[...] = gv
        mn = B1 * m_ref[...] + (1.0 - B1) * gv
        vn = B2 * v_ref[...] + (1.0 - B2) * (gv * gv)
        m_hat = mn / (1.0 - B1 ** STEP)
        v_hat = vn / (1.0 - B2 ** STEP)
        d_ref[...] = -LR * (m_hat / (jnp.sqrt(v_hat) + ADAM_EPS) + WD * w_ref[...])
        mo_ref[...] = mn
        vo_ref[...] = vn

    blk = pl.BlockSpec((None, tr, C), lambda l, i: (l, i, 0))
    shp = jax.ShapeDtypeStruct((lead, R, C), F32)
    return pl.pallas_call(body, name=name, grid=(lead, R // tr), in_specs=[blk] * 4, out_specs=(blk,) * 4,
                          out_shape=(shp,) * 4, compiler_params=_params(2))(w, g, m, v)


def _sum_leaf(lands, pends, ids, name):
    n = len(lands)
    _, R, C = lands[0].shape
    tr = _tile(R, 256)

    def body(ids_ref, *refs):
        out_ref = refs[2 * n]
        for k in range(n):
            acc = refs[n + k][...].astype(F32)
            for s in range(7):
                acc = acc + refs[k][s].astype(F32)
            out_ref[k] = acc

    grid_spec = pltpu.PrefetchScalarGridSpec(
        num_scalar_prefetch=1, grid=(R // tr,),
        in_specs=[pl.BlockSpec((7, tr, C), lambda i, ids: (0, i, 0))] * n
        + [pl.BlockSpec((None, None, tr, C), lambda i, ids: (ids[0], ids[1], i, 0))] * n,
        out_specs=pl.BlockSpec((n, None, tr, C), lambda i, ids: (0, ids[1], i, 0)))
    return pl.pallas_call(body, name=name, grid_spec=grid_spec, out_shape=jax.ShapeDtypeStruct((n, 2, R, C), F32),
                          compiler_params=_params(1))(ids, *lands, *pends)


def _place():
    x, y, c = lax.axis_index("x"), lax.axis_index("y"), lax.axis_index("c")
    chips = [(1 - x, y), (x, 1 - y), (1 - x, 1 - y)]
    return x, y, c, chips


def _rcopy(src, dst, ssem, rsem, dev):
    return pltpu.make_async_remote_copy(src_ref=src, dst_ref=dst, send_sem=ssem, recv_sem=rsem,
                                        device_id=dev, device_id_type=MESH)


class _Carry:
    def __init__(self, kind, arrays):
        self.kind, self.arrays, self.n = kind, list(arrays), len(arrays)
        self.per = 3 if kind == 'gather' else 7
        if kind == 'gather':
            self.out_shape = [jax.ShapeDtypeStruct((NSH,) + a.shape, a.dtype) for a in self.arrays]
        else:
            self.out_shape = [jax.ShapeDtypeStruct((7,) + a.shape[2:], a.dtype) for a in self.arrays]
        dma = pltpu.SemaphoreType.DMA
        self.scratch = [dma((self.per * self.n,)), dma((self.per * self.n,)), dma((self.n,))]
        if kind == 'gather':
            self.scratch += [pltpu.VMEM(a.shape, a.dtype) for a in self.arrays]

    def _copies(self, cin, cout, scr):
        ssem, rsem, loc = scr[:3]
        x, y, c, chips = _place()
        local, sends, recvs = [], [], []
        for k in range(self.n):
            if self.kind == 'gather':
                me = 2 * x + y
                local.append((pltpu.make_async_copy(cin[k], scr[3 + k], loc.at[k]),
                              pltpu.make_async_copy(scr[3 + k], cout[k].at[me], loc.at[k])))
                for j, (cx, cy) in enumerate(chips):
                    i = 3 * k + j
                    sends.append(_rcopy(cin[k], cout[k].at[me], ssem.at[i], rsem.at[i], (cx, cy, c)))
                    got = cout[k].at[2 * cx + cy]
                    recvs.append(_rcopy(got, got, ssem.at[i], rsem.at[i], (cx, cy, c)))
            else:
                for f in range(1, 8):
                    px = 1 - x if (f >> 2) & 1 else x
                    py = 1 - y if (f >> 1) & 1 else y
                    pc = 1 - c if f & 1 else c
                    i = 7 * k + f - 1
                    got = cout[k].at[f - 1]
                    sends.append(_rcopy(cin[k].at[2 * px + py, pc], got, ssem.at[i], rsem.at[i], (px, py, pc)))
                    recvs.append(_rcopy(got, got, ssem.at[i], rsem.at[i], (px, py, pc)))
        return local, sends, recvs

    def start(self, cin, cout, scr):
        local, sends, _ = self._copies(cin, cout, scr)
        for cp in [to_vmem for to_vmem, _ in local] + sends:
            cp.start()

    def finish(self, cin, cout, scr):
        local, sends, recvs = self._copies(cin, cout, scr)
        for to_vmem, to_slot in local:
            to_vmem.wait()
            to_slot.start()
        for cp in recvs:
            cp.wait_recv()
        for cp in sends:
            cp.wait_send()
        for _, to_slot in local:
            to_slot.wait()


def _carried_call(body, carry, *, name, grid, in_specs, out_specs, out_shape, scratch_shapes, compiler_params,
                  operands):
    n_in, n_out = len(in_specs), len(out_specs)
    if carry is None:
        return pl.pallas_call(body, name=name, grid=grid, in_specs=in_specs, out_specs=out_specs,
                              out_shape=out_shape, scratch_shapes=scratch_shapes,
                              compiler_params=compiler_params)(*operands), []
    m = carry.n

    def full(*refs):
        ins, cin = refs[:n_in], refs[n_in:n_in + m]
        outs, cout = refs[n_in + m:n_in + m + n_out], refs[n_in + m + n_out:n_in + 2 * m + n_out]
        own = len(refs) - len(carry.scratch)
        scr, sems = refs[n_in + 2 * m + n_out:own], refs[own:]
        ids = [pl.program_id(a) for a in range(len(grid))]
        first = functools.reduce(jnp.logical_and, [i == 0 for i in ids])
        last = functools.reduce(jnp.logical_and, [i == g - 1 for i, g in zip(ids, grid)])

        @pl.when(first)
        def _():
            carry.start(cin, cout, sems)

        body(*ins, *outs, *scr)

        @pl.when(last)
        def _():
            carry.finish(cin, cout, sems)

    res = pl.pallas_call(
        full, name=name, grid=grid, in_specs=list(in_specs) + [ANY] * m, out_specs=list(out_specs) + [ANY] * m,
        out_shape=list(out_shape) + carry.out_shape, scratch_shapes=list(scratch_shapes) + carry.scratch,
        compiler_params=compiler_params)(*operands, *carry.arrays)
    return res[:n_out], res[n_out:]


def _flush(carry, name):
    m = carry.n

    def body(*refs):
        cin, cout, sems = refs[:m], refs[m:2 * m], refs[2 * m:]
        carry.start(cin, cout, sems)
        carry.finish(cin, cout, sems)

    return pl.pallas_call(body, name=name, in_specs=[ANY] * m, out_specs=[ANY] * m, out_shape=carry.out_shape,
                          scratch_shapes=carry.scratch)(*carry.arrays)


def _pair_share(leaves):
    n = len(leaves)

    def body(*refs):
        ins, outs = refs[:n], refs[n:2 * n]
        ssem, rsem = refs[2 * n:]
        x, y, c, _ = _place()
        sib = (x, y, 1 - c)
        cps = [_rcopy(ins[k].at[:, c], outs[k].at[:, c], ssem.at[k], rsem.at[k], sib) for k in range(n)]
        for cp in cps:
            cp.start()
        for k in range(n):
            got = outs[k].at[:, 1 - c]
            _rcopy(got, got, ssem.at[k], rsem.at[k], sib).wait_recv()
        for cp in cps:
            cp.wait_send()

    dma = pltpu.SemaphoreType.DMA
    return pl.pallas_call(
        body, name="grad_pair_share", in_specs=[ANY] * n, out_specs=[ANY] * n,
        out_shape=[jax.ShapeDtypeStruct(a.shape, a.dtype) for a in leaves],
        input_output_aliases={k: k for k in range(n)},
        scratch_shapes=[dma((n,)), dma((n,))])(*leaves)


def _allreduce_small(pack):
    R = pack.shape[0]

    def body(in_ref, out_ref, buf, ssem, rsem):
        x, y, c, _ = _place()
        me = 4 * x + 2 * y + c
        buf[me] = in_ref[...]
        peers = []
        for k in range(1, 8):
            fx, fy, fc = (k >> 2) & 1, (k >> 1) & 1, k & 1
            peers.append((1 - x if fx else x, 1 - y if fy else y, 1 - c if fc else c))
        cps = [_rcopy(in_ref, buf.at[me], ssem.at[k], rsem.at[k], p) for k, p in enumerate(peers)]
        for cp in cps:
            cp.start()
        for k, (px, py, pc) in enumerate(peers):
            got = buf.at[4 * px + 2 * py + pc]
            _rcopy(got, got, ssem.at[k], rsem.at[k], (px, py, pc)).wait_recv()
        for cp in cps:
            cp.wait_send()
        acc = buf[0]
        for s in range(1, 8):
            acc = acc + buf[s]
        out_ref[...] = acc

    dma = pltpu.SemaphoreType.DMA
    return pl.pallas_call(
        body, name="allreduce_small", out_shape=jax.ShapeDtypeStruct(pack.shape, F32),
        in_specs=[pl.BlockSpec(memory_space=pltpu.VMEM)], out_specs=pl.BlockSpec(memory_space=pltpu.VMEM),
        scratch_shapes=[pltpu.VMEM((8, R, 128), F32), dma((7,)), dma((7,))])(pack)


def _pack(arrs):
    flat = jnp.concatenate([a.reshape(-1).astype(F32) for a in arrs])
    n = flat.shape[0]
    rows = -(-n // 1024) * 8
    return jnp.pad(flat, (0, rows * 128 - n)).reshape(rows, 128)


def _unpack(pack, shapes):
    flat, out, o = pack.reshape(-1), [], 0
    for s in shapes:
        n = int(np.prod(s))
        out.append(flat[o:o + n].reshape(s))
        o += n
    return out


def _ffn_keys(i):
    return [('ffn_w_gate', i), ('ffn_w_up', i), ('ffn_w_down', i)]


GATHER_PLAN = {'attn_fwd0': _ffn_keys(0) + [('conv_w_in', 0), ('conv_w_out', 0)],
               'ffn_fwd0': _ffn_keys(1),
               'ffn_fwd1': _ffn_keys(2) + [('pool_w', 0)],
               'ffn_fwd2': [('attn_w_qkv', 1), ('attn_w_o', 1)],
               'attn_fwd3': _ffn_keys(3)}
REDUCE_PLAN = {'attn_bwd3': _ffn_keys(3),
               'ffn_bwd_act2': [('attn_w_qkv', 1), ('attn_w_o', 1)],
               'ffn_bwd_act1': _ffn_keys(2) + [('pool_w', 0)],
               'ffn_bwd_act0': _ffn_keys(1),
               'ffn_bwd_w0': [('conv_w_in', 0), ('conv_w_out', 0)],
               'attn_bwd0': _ffn_keys(0) + [('attn_w_o', 0)],
               'qkv_bwd0': [('attn_w_qkv', 0)]}


def kernel(x, meta_tokens, rel_bias_table, norm_mix, norm_ffn, norm_final, attn_w_qkv, attn_b_qkv, attn_w_o, attn_b_o, attn_sinks, conv_w_in, conv_w, conv_w_out, pool_w, pool_scale, ffn_w_gate, ffn_w_up, ffn_w_down, loss_target, m_meta_tokens, m_rel_bias_table, m_norm_mix, m_norm_ffn, m_norm_final, m_attn_w_qkv, m_attn_b_qkv, m_attn_w_o, m_attn_b_o, m_attn_sinks, m_conv_w_in, m_conv_w, m_conv_w_out, m_pool_w, m_pool_scale, m_ffn_w_gate, m_ffn_w_up, m_ffn_w_down, v_meta_tokens, v_rel_bias_table, v_norm_mix, v_norm_ffn, v_norm_final, v_attn_w_qkv, v_attn_b_qkv, v_attn_w_o, v_attn_b_o, v_attn_sinks, v_conv_w_in, v_conv_w, v_conv_w_out, v_pool_w, v_pool_scale, v_ffn_w_gate, v_ffn_w_up, v_ffn_w_down):
    args = locals()
    w = {n: args[n] for n in WEIGHTS}
    mom = {n: args['m_' + n] for n in WEIGHTS}
    var = {n: args['v_' + n] for n in WEIGHTS}
    mx, my = lax.axis_index("x"), lax.axis_index("y")
    chip = 2 * mx + my
    S = x.shape[1]
    scale = jnp.asarray(HD ** -0.5, BF16)

    GW = {}
    pending = {}
    land = {}

    def gather_carry(call):
        keys = GATHER_PLAN.get(call)
        return _Carry('gather', [w[n][l].astype(BF16) for n, l in keys]) if keys else None

    def reduce_carry(call):
        keys = REDUCE_PLAN.get(call)
        return _Carry('reduce', [pending[k] for k in keys]) if keys else None

    def pend(n, l, g):
        pending[(n, l)] = g.reshape((NSH, 2) + HALF_SHAPE[n])

    small_in = jnp.concatenate([
        jnp.pad(w['meta_tokens'], ((0, 0), (0, 128))), w['attn_b_qkv'], jnp.pad(w['attn_b_o'], ((0, 0), (0, 128))),
        jnp.pad(w['conv_w'][0], ((0, 0), (0, 128))), jnp.pad(w['pool_scale'], ((0, 0), (0, 128)))], axis=0)
    gsmall, GW[('attn_w_qkv', 0)] = _flush(
        _Carry('gather', [small_in, w['attn_w_qkv'][0].astype(BF16)]), "gather_first")

    def cols(rows, width):
        return jnp.transpose(rows[:, :, :width], (1, 0, 2)).reshape(rows.shape[1], NSH * width)

    meta_full = cols(gsmall[:, 0:16], 256)
    b_qkv = cols(gsmall[:, 16:18], 384)
    b_o = cols(gsmall[:, 18:20], 256)
    conv_k = cols(gsmall[:, 20:23], 256)
    p_scale = cols(gsmall[:, 23:24], 256)

    idx_np, _ = _bias_index()
    idx = jnp.asarray(idx_np).reshape(1, -1)
    table_aug = jnp.concatenate([rel_bias_table.T, jnp.full((H, 1), NEG, F32),
                                 jnp.zeros((H, 127 - N_BUCKETS), F32)], axis=1)
    (bias_flat,), (GW[('attn_w_o', 0)],) = _bias_lookup(
        table_aug, idx, "bias_lookup", _Carry('gather', [w['attn_w_o'][0].astype(BF16)]))
    nblock = BLK * NKEY
    bias_blk = jnp.transpose(bias_flat[:, :2 * nblock].reshape(H, 2, BLK, NKEY), (1, 0, 2, 3))
    bias_mm = bias_flat[:, 2 * nblock:].reshape(H, N_META, N_META)
    bias_band, bias_meta = bias_blk[..., BLK:], bias_blk[..., :N_META]

    h = jnp.concatenate([meta_full, x[0]], axis=0)
    saved = []
    for i in range(DEPTH):
        kind, j = i % 3, i // 3
        gm = norm_mix[i:i + 1]
        st = dict(h=h)
        if kind == 0:
            w_o = GW[('attn_w_o', j)].reshape(1, D, D)
            q, k, v, a = _linear(h, GW[('attn_w_qkv', j)], gain=gm, bias=b_qkv[j:j + 1], qkv_heads=True,
                                 name=f"qkv{i}")
            tr = lambda t: jnp.swapaxes(t, 1, 2)
            lay = dict(q=q, qt=tr(q[:, N_META:]), qm=q[:, :N_META],
                       k=k, kt=tr(k[:, N_META:]), km=k[:, :N_META],
                       kmp=_pad_block(k[:, :N_META], 1), kmt=_pad_block(tr(k[:, :N_META]), 2),
                       v=v, vt=tr(v[:, N_META:]), vm=v[:, :N_META], vmt=_pad_block(tr(v[:, :N_META]), 2))
            sink_r = jnp.broadcast_to(attn_sinks[j].reshape(KV, GRP, 1, 1), (KV, GRP, BLK, 1)).reshape(KV, GRP * BLK, 1)
            sink_m = jnp.broadcast_to(attn_sinks[j].reshape(H, 1, 1), (H, N_META, 1))
            (o_r,), got = _attn_fwd(lay['q'], lay['k'], lay['km'], lay['v'], lay['vm'], bias_band, bias_meta, sink_r,
                                    f"attn_fwd{i}", gather_carry(f"attn_fwd{i}"))
            GW.update(zip(GATHER_PLAN.get(f"attn_fwd{i}", []), got))
            o_m = _attn_meta_fwd(lay['qm'], lay['km'], lay['vm'], bias_mm, sink_m, f"attn_meta_fwd{i}")
            o = _unheads(jnp.concatenate([o_m, o_r], axis=1))
            h1 = _linear(o, w_o, bias=b_o[j:j + 1], resid=h, out_dtype=F32, name=f"attn_out{i}")
            st.update(a=a, lay=lay, sinks=(sink_r, sink_m), o=o, w_o=w_o)
        elif kind == 1:
            z3, a = _linear(h, GW[('conv_w_in', j)], gain=gm, name=f"conv_in{i}")
            yv = _conv_fwd(z3, conv_k, f"conv_fwd{i}")
            w_cout = GW[('conv_w_out', j)].reshape(1, D, D)
            h1 = _linear(yv, w_cout, resid=h, out_dtype=F32, name=f"conv_out{i}")
            st.update(a=a, z3=z3, y=yv, w_cout=w_cout)
        else:
            w_pool = jnp.transpose(GW[('pool_w', j)], (1, 0, 2, 3)).reshape(4, PG, PG)
            mix = _pool_fwd(h, gm, f"pool_fwd{i}")
            h1 = _pool_out(mix, w_pool, p_scale, h, f"pool_out{i}")
            st.update(mix=mix, w_pool=w_pool)
        ffn_w = [GW[k] for k in _ffn_keys(i)]
        (hn, Gp, Up, Ap, Bp), got = _ffn_fwd(h1, norm_ffn[i:i + 1], *ffn_w, f"ffn_fwd{i}", gather_carry(f"ffn_fwd{i}"))
        GW.update(zip(GATHER_PLAN.get(f"ffn_fwd{i}", []), got))
        st.update(h1=h1, G=Gp, U=Up, A=Ap, B=Bp, ffn_w=ffn_w)
        saved.append(st)
        h = hn

    target = jnp.pad(loss_target[0], ((N_META, 0), (0, 0)))
    dh, d_nfinal, loss_part = _loss_head(h, norm_final.reshape(1, D), target)
    d_nmix, d_nffn = [None] * DEPTH, [None] * DEPTH
    d_bqkv, d_bo, dflats = [None, None], [None, None], [None, None]
    d_convk = d_pscale = None
    for i in reversed(range(DEPTH)):
        kind, j = i % 3, i // 3
        st = saved[i]
        call = f"ffn_bwd_act{i}"
        (DG, DU, dh1, d_nffn[i]), got = _ffn_bwd_act(dh, st['h1'], norm_ffn[i:i + 1], st['G'], st['U'], *st['ffn_w'],
                                                     call, reduce_carry(call))
        land.update(zip(REDUCE_PLAN.get(call, []), got))
        call = f"ffn_bwd_w{i}"
        (gw, uw, dw_), got = _ffn_bwd_w(st['B'], st['A'], DG, DU, dh, call, reduce_carry(call))
        land.update(zip(REDUCE_PLAN.get(call, []), got))
        pend('ffn_w_gate', i, gw)
        pend('ffn_w_up', i, uw)
        pend('ffn_w_down', i, dw_)
        gm = norm_mix[i:i + 1]
        if kind == 0:
            lay = st['lay']
            sink_r, sink_m = st['sinks']
            gwo, d_bo[j] = _wgrad(st['o'], dh1, mode='row', colsum=True, name=f"attn_out_wgrad{i}")
            pend('attn_w_o', j, gwo)
            do = _linear_t(dh1, st['w_o'], to_heads=True, name=f"attn_out_bwd{i}")
            call = f"attn_bwd{i}"
            (dq_r, dkt, dvt, dkmt, dvmt, dbias), got = _attn_bwd(
                lay['q'], lay['qt'], lay['k'], lay['kmp'], lay['kt'], lay['kmt'], lay['vt'], lay['vmt'], bias_blk,
                sink_r, do, jnp.swapaxes(do[:, N_META:], 1, 2), call, reduce_carry(call))
            land.update(zip(REDUCE_PLAN.get(call, []), got))
            dq_m, dkm2, dvm2, dbmm = _attn_meta_bwd(lay['qm'], lay['km'], lay['vm'], bias_mm, sink_m,
                                                    do[:, :N_META], f"attn_meta_bwd{i}")
            dflats[j] = jnp.concatenate([jnp.transpose(dbias, (1, 0, 2, 3)).reshape(H, -1), dbmm.reshape(H, -1)], axis=1)
            dkm = (jnp.transpose(dkmt[:, :, :N_META], (0, 2, 1)) + dkm2).astype(BF16)
            dvm = (jnp.transpose(dvmt[:, :, :N_META], (0, 2, 1)) + dvm2).astype(BF16)
            dqkv = jnp.concatenate([
                jnp.concatenate([_unheads(dq_m), _unheads(dq_r)], axis=0) * scale,
                jnp.concatenate([_unheads(dkm), _unheads_t(dkt)], axis=0),
                jnp.concatenate([_unheads(dvm), _unheads_t(dvt)], axis=0)], axis=1)
            gq, d_bqkv[j] = _wgrad(st['a'], dqkv, mode='col', colsum=True, name=f"qkv_wgrad{i}")
            pend('attn_w_qkv', j, gq)
            call = f"qkv_bwd{i}"
            res = _linear_t(dqkv, GW[('attn_w_qkv', j)], rms=(st['h'], gm, dh1), carry=reduce_carry(call), name=call)
            if call in REDUCE_PLAN:
                res, got = res
                land.update(zip(REDUCE_PLAN[call], got))
            dh, d_nmix[i] = res
        elif kind == 1:
            pend('conv_w_out', j, _wgrad(st['y'], dh1, mode='row', name=f"conv_out_wgrad{i}"))
            dy = _linear_t(dh1, st['w_cout'], out_dtype=F32, name=f"conv_out_bwd{i}")
            dz3, d_convk = _conv_bwd(st['z3'], conv_k, dy, f"conv_bwd{i}")
            pend('conv_w_in', j, _wgrad(st['a'], dz3, mode='col', name=f"conv_in_wgrad{i}"))
            dh, d_nmix[i] = _linear_t(dz3, GW[('conv_w_in', j)], rms=(st['h'], gm, dh1), name=f"conv_in_bwd{i}")
        else:
            dmix, dwp, d_pscale = _pool_out_bwd(dh1, st['mix'], st['w_pool'], p_scale, f"pool_out_bwd{i}")
            pend('pool_w', j, jnp.transpose(dwp.reshape(4, NSH, PG // NSH, PG), (1, 0, 2, 3)).astype(BF16))
            dh, d_nmix[i] = _pool_bwd(dmix, st['h'], gm, dh1, f"pool_bwd{i}")

    dts = _bias_lookup_bwd(jnp.concatenate(dflats, axis=0), idx, "bias_lookup_bwd")
    d_table = dts[:H, :N_BUCKETS].T + dts[H:, :N_BUCKETS].T
    d_sinks = dts[:, N_BUCKETS].reshape(2, H)

    ids = jnp.stack([chip, lax.axis_index("c")]).astype(jnp.int32)
    shared = _pair_share([_sum_leaf([land[(n, l)] for l in range(w[n].shape[0])],
                                    [pending[(n, l)] for l in range(w[n].shape[0])], ids, f"sum_{n}") for n in BIG])
    grads = {n: s.reshape(w[n].shape) for n, s in zip(BIG, shared)}

    small_full = [dh[:N_META], d_table, jnp.concatenate(d_nmix, axis=0), jnp.concatenate(d_nffn, axis=0), d_nfinal,
                  jnp.concatenate(d_bqkv, axis=0), jnp.concatenate(d_bo, axis=0), d_sinks, d_convk,
                  d_pscale, loss_part[:, 0:1]]
    red = _unpack(_allreduce_small(_pack(small_full)), [a.shape for a in small_full])
    g_meta, g_table, g_nmix, g_nffn, g_nfinal, g_bqkv, g_bo, g_sinks, g_convk, g_pscale, loss = red

    def shard(a, width):
        return lax.dynamic_slice_in_dim(a, chip * width, width, axis=1)

    grads.update(meta_tokens=shard(g_meta, 256), rel_bias_table=g_table, norm_mix=g_nmix, norm_ffn=g_nffn,
                 norm_final=g_nfinal.reshape(D), attn_b_qkv=shard(g_bqkv, 384), attn_b_o=shard(g_bo, 256),
                 attn_sinks=g_sinks, conv_w=shard(g_convk, 256)[None], pool_scale=shard(g_pscale, 256))

    delta, new_m, new_v = {}, {}, {}
    for n in BIG:
        shp = w[n].shape
        r3 = (int(np.prod(shp[:-2])),) + shp[-2:]
        swap = (lambda a: jnp.swapaxes(a, 1, 2)) if shp[-1] % 128 else (lambda a: a)
        res = _adam(*[swap(a.reshape(r3)) for a in (w[n], grads[n], mom[n], var[n])], f"adam_{n}")
        delta[n], new_m[n], new_v[n], grads[n] = [swap(a).reshape(shp) for a in res]
    shapes = [w[n].shape for n in SMALL]
    packed = [_pack([d[n] for n in SMALL])[None] for d in (w, grads, mom, var)]
    for dst, res in zip((delta, new_m, new_v), _adam(*packed, "adam_small")[:3]):
        dst.update(zip(SMALL, _unpack(res[0], shapes)))

    return (loss.reshape(()), dh[N_META:][None], *[grads[n] for n in WEIGHTS], *[delta[n] for n in WEIGHTS],
            *[new_m[n] for n in WEIGHTS], *[new_v[n] for n in WEIGHTS])
```

```python
import functools
import math

import numpy as np
import jax
import jax.numpy as jnp
from jax import lax
from jax.experimental import pallas as pl
from jax.experimental.pallas import tpu as pltpu

F32, BF16 = jnp.float32, jnp.bfloat16
D = 1024
N_META = 16
EPS = 1e-6
H, KV, GRP, HD = 16, 4, 4, 64
BLK = 128
NKEY = 3 * BLK
N_BUCKETS = 32
POOL_WINDOWS = (2, 4, 8, 16)
PG = 256
DFF = 2816
NSH = 4
FC = DFF // NSH
HALO = 16
NEG = -1e30
DEPTH = 4
LR, B1, B2, ADAM_EPS, WD, STEP = 0.001, 0.9, 0.999, 1e-08, 0.01, 10
MESH = pl.DeviceIdType.MESH
ANY = pl.BlockSpec(memory_space=pl.ANY)
VMEM_LIMIT = 48 * 1024 * 1024
VMEM_BIG = 58 * 1024 * 1024
BIG_TILE = 1024

_NN = (((1,), (0,)), ((), ()))
_NT = (((1,), (1,)), ((), ()))
_TN = (((0,), (0,)), ((), ()))

WEIGHTS = ['meta_tokens', 'rel_bias_table', 'norm_mix', 'norm_ffn', 'norm_final', 'attn_w_qkv', 'attn_b_qkv',
           'attn_w_o', 'attn_b_o', 'attn_sinks', 'conv_w_in', 'conv_w', 'conv_w_out', 'pool_w', 'pool_scale',
           'ffn_w_gate', 'ffn_w_up', 'ffn_w_down']
BIG = ['attn_w_qkv', 'attn_w_o', 'conv_w_in', 'conv_w_out', 'pool_w', 'ffn_w_gate', 'ffn_w_up', 'ffn_w_down']
SMALL = [w for w in WEIGHTS if w not in BIG]
HALF_SHAPE = {'attn_w_qkv': (512, 384), 'attn_w_o': (128, 1024), 'conv_w_in': (512, 768),
              'conv_w_out': (128, 1024), 'pool_w': (128, 256), 'ffn_w_gate': (512, 704),
              'ffn_w_up': (512, 704), 'ffn_w_down': (352, 1024)}


def _dot(a, b, dims=_NN):
    return lax.dot_general(a, b, dims, preferred_element_type=F32)


def _tile(n, cap=512):
    best = None
    for t in range(16, min(n, cap) + 1, 16):
        if n % t == 0:
            best = t
    assert best is not None, n
    return best


def _params(n_axes, vmem=VMEM_LIMIT):
    return pltpu.CompilerParams(dimension_semantics=("arbitrary",) * n_axes, vmem_limit_bytes=vmem)


def _rstd(x):
    return lax.rsqrt(jnp.mean(x * x, axis=-1, keepdims=True) + EPS)


def _rms_bwd(dy, x, g, r):
    u = dy * g
    dx = r * u - x * ((r * r * r) * (jnp.sum(x * u, axis=-1, keepdims=True) * (1.0 / D)))
    return dx, jnp.sum(dy * (x * r), axis=0, keepdims=True)


def _sigmoid(x):
    return 1.0 / (1.0 + jnp.exp(-x))


def _acc(ref, val, first):
    @pl.when(first)
    def _():
        ref[...] = val

    @pl.when(jnp.logical_not(first))
    def _():
        ref[...] += val


def _linear(x, w, *, gain=None, bias=None, resid=None, out_dtype=BF16, qkv_heads=False, cap=BIG_TILE, name):
    L, K = x.shape
    J, _, Nc = w.shape
    tm = _tile(L, cap)
    has_g, has_b, has_r = gain is not None, bias is not None, resid is not None
    n_main = 3 if qkv_heads else 1

    def body(*refs):
        refs = list(refs)
        x_ref, w_ref = refs[:2]
        i = 2
        g_ref = b_ref = r_ref = None
        if has_g:
            g_ref, i = refs[i], i + 1
        if has_b:
            b_ref, i = refs[i], i + 1
        if has_r:
            r_ref, i = refs[i], i + 1
        out_ref = refs[i]
        if has_g:
            xf = x_ref[...]
            xv = (xf * _rstd(xf) * g_ref[...]).astype(BF16)
            refs[i + n_main][...] = xv
        else:
            xv = x_ref[...]
        for s in range(J):
            sl = slice(s * Nc, (s + 1) * Nc)
            acc = _dot(xv, w_ref[s])
            if has_b:
                acc = acc + b_ref[:, sl]
            if has_r:
                acc = acc + r_ref[:, sl]
            if not qkv_heads:
                out_ref[:, sl] = acc.astype(out_dtype)
                continue
            for r in range(Nc // HD):
                hd, blk = s * (Nc // HD) + r, acc[:, r * HD:(r + 1) * HD]
                if hd < H:
                    refs[i][hd] = (blk * HD ** -0.5).astype(BF16)
                elif hd < H + KV:
                    refs[i + 1][hd - H] = blk.astype(BF16)
                else:
                    refs[i + 2][hd - H - KV] = blk.astype(BF16)

    row = lambda n: pl.BlockSpec((tm, n), lambda t: (t, 0))
    one = lambda n: pl.BlockSpec((1, n), lambda t: (0, 0))
    in_specs = [row(K), pl.BlockSpec((J, K, Nc), lambda t: (0, 0, 0))]
    ops = [x, w]
    if has_g:
        in_specs.append(one(K))
        ops.append(gain)
    if has_b:
        in_specs.append(one(J * Nc))
        ops.append(bias)
    if has_r:
        in_specs.append(row(J * Nc))
        ops.append(resid)
    if qkv_heads:
        heads = lambda n: pl.BlockSpec((n, tm, HD), lambda t: (0, t, 0))
        out_specs = [heads(H), heads(KV), heads(KV)]
        out_shape = [jax.ShapeDtypeStruct((n, L, HD), BF16) for n in (H, KV, KV)]
    else:
        out_specs, out_shape = [row(J * Nc)], [jax.ShapeDtypeStruct((L, J * Nc), out_dtype)]
    if has_g:
        out_specs, out_shape = out_specs + [row(K)], out_shape + [jax.ShapeDtypeStruct((L, K), BF16)]
    res = pl.pallas_call(body, name=name, grid=(L // tm,), in_specs=in_specs, out_specs=out_specs,
                         out_shape=out_shape, compiler_params=_params(1, VMEM_BIG))(*ops)
    return res[0] if len(res) == 1 else res


def _linear_t(dy, w, *, out_dtype=BF16, rms=None, to_heads=False, carry=None, cap=BIG_TILE, name):
    L = dy.shape[0]
    J, K, Nc = w.shape
    tm = _tile(L, cap)
    has_rms = rms is not None

    def body(*refs):
        t = pl.program_id(0)
        dy_ref, w_ref = refs[:2]
        acc = _dot(dy_ref[:, 0:Nc].astype(BF16), w_ref[0], _NT)
        for s in range(1, J):
            acc = acc + _dot(dy_ref[:, s * Nc:(s + 1) * Nc].astype(BF16), w_ref[s], _NT)
        if has_rms:
            h_ref, g_ref, r_ref, out_ref, dg_ref = refs[2:7]
            x = h_ref[...]
            dx, dg = _rms_bwd(acc, x, g_ref[...], _rstd(x))
            out_ref[...] = r_ref[...] + dx
            _acc(dg_ref, dg, t == 0)
        elif to_heads:
            for hd in range(K // HD):
                refs[2][hd] = acc[:, hd * HD:(hd + 1) * HD].astype(out_dtype)
        else:
            refs[2][...] = acc.astype(out_dtype)

    row = lambda n: pl.BlockSpec((tm, n), lambda t: (t, 0))
    in_specs = [row(J * Nc), pl.BlockSpec((J, K, Nc), lambda t: (0, 0, 0))]
    ops = [dy, w]
    if has_rms:
        in_specs += [row(K), pl.BlockSpec((1, K), lambda t: (0, 0)), row(K)]
        ops += list(rms)
        out_specs = [row(K), pl.BlockSpec((1, K), lambda t: (0, 0))]
        out_shape = [jax.ShapeDtypeStruct((L, K), F32), jax.ShapeDtypeStruct((1, K), F32)]
    elif to_heads:
        out_specs = [pl.BlockSpec((K // HD, tm, HD), lambda t: (0, t, 0))]
        out_shape = [jax.ShapeDtypeStruct((K // HD, L, HD), out_dtype)]
    else:
        out_specs = [row(K)]
        out_shape = [jax.ShapeDtypeStruct((L, K), out_dtype)]
    res, got = _carried_call(body, carry, name=name, grid=(L // tm,), in_specs=in_specs, out_specs=out_specs,
                             out_shape=out_shape, scratch_shapes=[], compiler_params=_params(1, VMEM_BIG),
                             operands=ops)
    res = res[0] if len(res) == 1 else res
    return res if carry is None else (res, got)


def _wgrad(x, dy, *, mode, colsum=False, cap=BIG_TILE, name):
    L, K = x.shape
    N = dy.shape[1]
    tm = _tile(L, cap)
    nt = L // tm
    oshape = (NSH, K, N // NSH) if mode == 'col' else (NSH, K // NSH, N)

    def body(*refs):
        t = pl.program_id(0)
        x_ref, dy_ref, out_ref = refs[:3]
        acc_ref = refs[-1]
        dyv = dy_ref[...]

        @pl.when(t == 0)
        def _():
            acc_ref[...] = jnp.zeros_like(acc_ref)

        acc_ref[...] += _dot(x_ref[...].astype(BF16), dyv.astype(BF16), _TN)
        if colsum:
            _acc(refs[3], jnp.sum(dyv.astype(F32), axis=0, keepdims=True), t == 0)

        @pl.when(t == nt - 1)
        def _():
            for s in range(NSH):
                if mode == 'col':
                    out_ref[s] = acc_ref[:, s * oshape[2]:(s + 1) * oshape[2]].astype(BF16)
                else:
                    out_ref[s] = acc_ref[s * oshape[1]:(s + 1) * oshape[1], :].astype(BF16)

    out_specs = pl.BlockSpec(oshape, lambda t: (0, 0, 0))
    out_shape = jax.ShapeDtypeStruct(oshape, BF16)
    if colsum:
        out_specs = (out_specs, pl.BlockSpec((1, N), lambda t: (0, 0)))
        out_shape = (out_shape, jax.ShapeDtypeStruct((1, N), F32))
    return pl.pallas_call(
        body, name=name, grid=(nt,),
        in_specs=[pl.BlockSpec((tm, K), lambda t: (t, 0)), pl.BlockSpec((tm, N), lambda t: (t, 0))],
        out_specs=out_specs, out_shape=out_shape,
        scratch_shapes=[pltpu.VMEM((K, N), F32)], compiler_params=_params(1, VMEM_BIG))(x, dy)


def _ffn_fwd(h, gain, wg, wu, wd, name, carry=None):
    L = h.shape[0]
    tm = _tile(L, BIG_TILE)

    def body(h_ref, g_ref, wg_ref, wu_ref, wd_ref, hn_ref, G_ref, U_ref, A_ref, B_ref, acc_ref, b_scr):
        j = pl.program_id(1)

        @pl.when(j == 0)
        def _():
            x = h_ref[...]
            b = (x * _rstd(x) * g_ref[...]).astype(BF16)
            b_scr[...] = b
            B_ref[...] = b
            acc_ref[...] = x

        b = b_scr[...]
        g = _dot(b, wg_ref[...])
        u = _dot(b, wu_ref[...])
        s = _sigmoid(g)
        silu = g * s
        G_ref[...] = (u * (s * (1.0 + g * (1.0 - s)))).astype(BF16)
        U_ref[...] = silu.astype(BF16)
        a = (silu * u).astype(BF16)
        A_ref[...] = a
        acc_ref[...] += _dot(a, wd_ref[...])

        @pl.when(j == NSH - 1)
        def _():
            hn_ref[...] = acc_ref[...]

    row = pl.BlockSpec((tm, D), lambda t, j: (t, 0))
    chunk = pl.BlockSpec((None, tm, FC), lambda t, j: (j, t, 0))
    cshape = jax.ShapeDtypeStruct((NSH, L, FC), BF16)
    return _carried_call(
        body, carry, name=name, grid=(L // tm, NSH),
        in_specs=[row, pl.BlockSpec((1, D), lambda t, j: (0, 0)),
                  pl.BlockSpec((None, D, FC), lambda t, j: (j, 0, 0)),
                  pl.BlockSpec((None, D, FC), lambda t, j: (j, 0, 0)),
                  pl.BlockSpec((None, FC, D), lambda t, j: (j, 0, 0))],
        out_specs=[row, chunk, chunk, chunk, row],
        out_shape=[jax.ShapeDtypeStruct((L, D), F32), cshape, cshape, cshape, jax.ShapeDtypeStruct((L, D), BF16)],
        scratch_shapes=[pltpu.VMEM((tm, D), F32), pltpu.VMEM((tm, D), BF16)],
        compiler_params=_params(2, VMEM_BIG), operands=[h, gain, wg, wu, wd])


def _ffn_bwd_act(dhn, h, gain, G, U, wg, wu, wd, name, carry=None):
    L = h.shape[0]
    tm = _tile(L, BIG_TILE)

    def body(dhn_ref, h_ref, g_ref, G_ref, U_ref, wg_ref, wu_ref, wd_ref, DG_ref, DU_ref, dh_ref, dgain_ref, db_ref):
        t, j = pl.program_id(0), pl.program_id(1)

        @pl.when(j == 0)
        def _():
            db_ref[...] = jnp.zeros_like(db_ref)

        d_act = _dot(dhn_ref[...].astype(BF16), wd_ref[...], _NT)
        dg = (d_act * G_ref[...].astype(F32)).astype(BF16)
        du = (d_act * U_ref[...].astype(F32)).astype(BF16)
        DG_ref[...] = dg
        DU_ref[...] = du
        db_ref[...] += _dot(dg, wg_ref[...], _NT) + _dot(du, wu_ref[...], _NT)

        @pl.when(j == NSH - 1)
        def _():
            x = h_ref[...]
            dx, dgn = _rms_bwd(db_ref[...], x, g_ref[...], _rstd(x))
            dh_ref[...] = dhn_ref[...] + dx
            _acc(dgain_ref, dgn, t == 0)

    row = pl.BlockSpec((tm, D), lambda t, j: (t, 0))
    one = pl.BlockSpec((1, D), lambda t, j: (0, 0))
    chunk = pl.BlockSpec((None, tm, FC), lambda t, j: (j, t, 0))
    cshape = jax.ShapeDtypeStruct((NSH, L, FC), BF16)
    return _carried_call(
        body, carry, name=name, grid=(L // tm, NSH),
        in_specs=[row, row, one, chunk, chunk,
                  pl.BlockSpec((None, D, FC), lambda t, j: (j, 0, 0)),
                  pl.BlockSpec((None, D, FC), lambda t, j: (j, 0, 0)),
                  pl.BlockSpec((None, FC, D), lambda t, j: (j, 0, 0))],
        out_specs=[chunk, chunk, row, one],
        out_shape=[cshape, cshape, jax.ShapeDtypeStruct((L, D), F32), jax.ShapeDtypeStruct((1, D), F32)],
        scratch_shapes=[pltpu.VMEM((tm, D), F32)],
        compiler_params=_params(2, VMEM_BIG), operands=[dhn, h, gain, G, U, wg, wu, wd])


def _ffn_bwd_w(B, A, DG, DU, dhn, name, carry=None):
    L = B.shape[0]
    tm = _tile(L, BIG_TILE)
    nt = L // tm

    def body(B_ref, A_ref, DG_ref, DU_ref, dhn_ref, dwg_ref, dwu_ref, dwd_ref, ag, au, ad):
        t = pl.program_id(1)

        @pl.when(t == 0)
        def _():
            ag[...] = jnp.zeros_like(ag)
            au[...] = jnp.zeros_like(au)
            ad[...] = jnp.zeros_like(ad)

        b = B_ref[...]
        ag[...] += _dot(b, DG_ref[...], _TN)
        au[...] += _dot(b, DU_ref[...], _TN)
        ad[...] += _dot(A_ref[...], dhn_ref[...].astype(BF16), _TN)

        @pl.when(t == nt - 1)
        def _():
            dwg_ref[...] = ag[...].astype(BF16)
            dwu_ref[...] = au[...].astype(BF16)
            dwd_ref[...] = ad[...].astype(BF16)

    row = pl.BlockSpec((tm, D), lambda j, t: (t, 0))
    chunk = pl.BlockSpec((None, tm, FC), lambda j, t: (j, t, 0))
    return _carried_call(
        body, carry, name=name, grid=(NSH, nt), in_specs=[row, chunk, chunk, chunk, row],
        out_specs=[pl.BlockSpec((None, D, FC), lambda j, t: (j, 0, 0)),
                   pl.BlockSpec((None, D, FC), lambda j, t: (j, 0, 0)),
                   pl.BlockSpec((None, FC, D), lambda j, t: (j, 0, 0))],
        out_shape=[jax.ShapeDtypeStruct((NSH, D, FC), BF16), jax.ShapeDtypeStruct((NSH, D, FC), BF16),
                   jax.ShapeDtypeStruct((NSH, FC, D), BF16)],
        scratch_shapes=[pltpu.VMEM((D, FC), F32), pltpu.VMEM((D, FC), F32), pltpu.VMEM((FC, D), F32)],
        compiler_params=_params(2, VMEM_BIG), operands=[B, A, DG, DU, dhn])


def _loss_head(h, gain, target):
    L = h.shape[0]
    tm = _tile(L)

    def body(h_ref, g_ref, tgt_ref, dh_ref, dg_ref, loss_ref):
        t = pl.program_id(0)
        x = h_ref[...]
        g = g_ref[...]
        r = _rstd(x)
        rows = t * tm + lax.broadcasted_iota(jnp.int32, (tm, 1), 0)
        diff = jnp.where(rows >= N_META, x * r * g - tgt_ref[...], 0.0)
        part = 0.5 * jnp.sum(jnp.sum(diff * diff, axis=-1, keepdims=True) * (1.0 / D), axis=0, keepdims=True)
        dx, dg = _rms_bwd(diff * (1.0 / D), x, g, r)
        dh_ref[...] = dx
        _acc(dg_ref, dg, t == 0)
        _acc(loss_ref, jnp.broadcast_to(part, (1, 128)), t == 0)

    row = pl.BlockSpec((tm, D), lambda t: (t, 0))
    one = pl.BlockSpec((1, D), lambda t: (0, 0))
    return pl.pallas_call(
        body, name="loss_head", grid=(L // tm,), in_specs=[row, one, row],
        out_specs=(row, one, pl.BlockSpec((1, 128), lambda t: (0, 0))),
        out_shape=(jax.ShapeDtypeStruct((L, D), F32), jax.ShapeDtypeStruct((1, D), F32),
                   jax.ShapeDtypeStruct((1, 128), F32)),
        compiler_params=_params(1))(h, gain, target)


def _conv_fwd(z3, cw, name):
    L = z3.shape[0]
    tm = _tile(L)

    def body(b_ref, c_ref, u_ref, ch_ref, uh_ref, w_ref, y_ref, buf):
        t = pl.program_id(0)
        z = c_ref[...].astype(F32) * u_ref[...].astype(F32)
        buf[pl.ds(0, HALO), :] = jnp.where(t > 0, ch_ref[...].astype(F32) * uh_ref[...].astype(F32), 0.0)
        buf[pl.ds(HALO, tm), :] = z
        w0, w1, w2 = w_ref[0:1, :], w_ref[1:2, :], w_ref[2:3, :]
        conv = w2 * z + w1 * buf[pl.ds(HALO - 1, tm), :] + w0 * buf[pl.ds(HALO - 2, tm), :]
        y_ref[...] = (b_ref[...].astype(F32) * conv).astype(BF16)

    def col(k):
        return pl.BlockSpec((tm, D), lambda t: (t, k))

    def hcol(k):
        return pl.BlockSpec((HALO, D), lambda t: (jnp.maximum(t * (tm // HALO) - 1, 0), k))

    return pl.pallas_call(
        body, name=name, grid=(L // tm,),
        in_specs=[col(0), col(1), col(2), hcol(1), hcol(2), pl.BlockSpec((3, D), lambda t: (0, 0))],
        out_specs=pl.BlockSpec((tm, D), lambda t: (t, 0)),
        out_shape=jax.ShapeDtypeStruct((L, D), BF16),
        scratch_shapes=[pltpu.VMEM((HALO + tm, D), F32)], compiler_params=_params(1))(z3, z3, z3, z3, z3, cw)


def _conv_bwd(z3, cw, dy, name):
    L = z3.shape[0]
    tm = _tile(L)
    nt = L // tm
    last_h = L // HALO - 1

    def body(b_ref, c_ref, u_ref, ch_ref, uh_ref, w_ref, dy_ref, bn_ref, dyn_ref, dz3_ref, dw_ref, zbuf, dbuf):
        t = pl.program_id(0)
        w0, w1, w2 = w_ref[0:1, :], w_ref[1:2, :], w_ref[2:3, :]
        bgate, cgate, u = b_ref[...].astype(F32), c_ref[...].astype(F32), u_ref[...].astype(F32)
        z = cgate * u
        zbuf[pl.ds(0, HALO), :] = jnp.where(t > 0, ch_ref[...].astype(F32) * uh_ref[...].astype(F32), 0.0)
        zbuf[pl.ds(HALO, tm), :] = z
        z1 = zbuf[pl.ds(HALO - 1, tm), :]
        z2 = zbuf[pl.ds(HALO - 2, tm), :]
        conv = w2 * z + w1 * z1 + w0 * z2
        dyv = dy_ref[...]
        dconv = dyv * bgate
        dbuf[pl.ds(0, tm), :] = dconv
        dbuf[pl.ds(tm, HALO), :] = jnp.where(t < nt - 1, dyn_ref[...] * bn_ref[...].astype(F32), 0.0)
        dz = w2 * dconv + w1 * dbuf[pl.ds(1, tm), :] + w0 * dbuf[pl.ds(2, tm), :]
        dz3_ref[:, 0:D] = (dyv * conv).astype(BF16)
        dz3_ref[:, D:2 * D] = (dz * u).astype(BF16)
        dz3_ref[:, 2 * D:3 * D] = (dz * cgate).astype(BF16)
        @pl.when(t == 0)
        def _():
            dw_ref[...] = jnp.zeros_like(dw_ref)

        for k, zk in enumerate((z2, z1, z)):
            dw_ref[k:k + 1, :] += jnp.sum(dconv * zk, axis=0, keepdims=True)

    def col(k):
        return pl.BlockSpec((tm, D), lambda t: (t, k))

    def hprev(k):
        return pl.BlockSpec((HALO, D), lambda t: (jnp.maximum(t * (tm // HALO) - 1, 0), k))

    def hnext(k):
        return pl.BlockSpec((HALO, D), lambda t: (jnp.minimum((t + 1) * (tm // HALO), last_h), k))

    return pl.pallas_call(
        body, name=name, grid=(nt,),
        in_specs=[col(0), col(1), col(2), hprev(1), hprev(2), pl.BlockSpec((3, D), lambda t: (0, 0)),
                  col(0), hnext(0), hnext(0)],
        out_specs=(pl.BlockSpec((tm, 3 * D), lambda t: (t, 0)), pl.BlockSpec((3, D), lambda t: (0, 0))),
        out_shape=(jax.ShapeDtypeStruct((L, 3 * D), BF16), jax.ShapeDtypeStruct((3, D), F32)),
        scratch_shapes=[pltpu.VMEM((HALO + tm, D), F32), pltpu.VMEM((tm + HALO, D), F32)],
        compiler_params=_params(1))(z3, z3, z3, z3, z3, cw, dy, z3, dy)


def _count_inv(pos, w):
    return 1.0 / jnp.minimum(pos + 1, w).astype(F32)


def _pool_fwd(h, gain, name):
    L = h.shape[0]
    tm = _tile(L)

    def body(h_ref, hh_ref, g_ref, mix_ref, buf):
        t = pl.program_id(0)
        g = g_ref[...]
        x = h_ref[...]
        a = x * _rstd(x) * g
        xh = hh_ref[...]
        buf[pl.ds(0, HALO), :] = jnp.where(t > 0, xh * _rstd(xh) * g, 0.0)
        buf[pl.ds(HALO, tm), :] = a
        pos = t * tm + lax.broadcasted_iota(jnp.int32, (tm, 1), 0)
        for gi, w in enumerate(POOL_WINDOWS):
            cols = pl.ds(gi * PG, PG)
            s = buf[pl.ds(HALO, tm), cols]
            for k in range(1, w):
                s = s + buf[pl.ds(HALO - k, tm), cols]
            mix_ref[:, gi * PG:(gi + 1) * PG] = (s / jnp.minimum(pos + 1, w).astype(F32)
                                                  - buf[pl.ds(HALO, tm), cols]).astype(BF16)

    return pl.pallas_call(
        body, name=name, grid=(L // tm,),
        in_specs=[pl.BlockSpec((tm, D), lambda t: (t, 0)),
                  pl.BlockSpec((HALO, D), lambda t: (jnp.maximum(t * (tm // HALO) - 1, 0), 0)),
                  pl.BlockSpec((1, D), lambda t: (0, 0))],
        out_specs=pl.BlockSpec((tm, D), lambda t: (t, 0)),
        out_shape=jax.ShapeDtypeStruct((L, D), BF16),
        scratch_shapes=[pltpu.VMEM((HALO + tm, D), F32)], compiler_params=_params(1))(h, h, gain)


def _pool_out(mix, wp, scale, h, name):
    L = mix.shape[0]
    tm = _tile(L)

    def body(mix_ref, wp_ref, s_ref, h_ref, out_ref):
        for gi in range(4):
            sl = slice(gi * PG, (gi + 1) * PG)
            pre = _dot(mix_ref[:, sl], wp_ref[gi])
            out_ref[:, sl] = h_ref[:, sl] + pre * s_ref[:, sl]

    row = pl.BlockSpec((tm, D), lambda t: (t, 0))
    return pl.pallas_call(
        body, name=name, grid=(L // tm,),
        in_specs=[row, pl.BlockSpec((4, PG, PG), lambda t: (0, 0, 0)), pl.BlockSpec((1, D), lambda t: (0, 0)), row],
        out_specs=row, out_shape=jax.ShapeDtypeStruct((L, D), F32), compiler_params=_params(1))(mix, wp, scale, h)


def _pool_out_bwd(dm, mix, wp, scale, name):
    L = mix.shape[0]
    tm = _tile(L)

    def body(dm_ref, mix_ref, wp_ref, s_ref, dmix_ref, dwp_ref, ds_ref):
        @pl.when(pl.program_id(0) == 0)
        def _():
            ds_ref[...] = jnp.zeros_like(ds_ref)
            dwp_ref[...] = jnp.zeros_like(dwp_ref)

        for gi in range(4):
            sl = slice(gi * PG, (gi + 1) * PG)
            mx = mix_ref[:, sl]
            dmv = dm_ref[:, sl]
            pre = _dot(mx, wp_ref[gi])
            ds_ref[:, sl] += jnp.sum(dmv * pre, axis=0, keepdims=True)
            dpre = (dmv * s_ref[:, sl]).astype(BF16)
            dwp_ref[gi] += _dot(mx, dpre, _TN)
            dmix_ref[:, sl] = _dot(dpre, wp_ref[gi], _NT)

    row = pl.BlockSpec((tm, D), lambda t: (t, 0))
    one = pl.BlockSpec((1, D), lambda t: (0, 0))
    wsp = pl.BlockSpec((4, PG, PG), lambda t: (0, 0, 0))
    return pl.pallas_call(
        body, name=name, grid=(L // tm,), in_specs=[row, row, wsp, one],
        out_specs=(row, wsp, one),
        out_shape=(jax.ShapeDtypeStruct((L, D), F32), jax.ShapeDtypeStruct((4, PG, PG), F32),
                   jax.ShapeDtypeStruct((1, D), F32)),
        compiler_params=_params(1))(dm, mix, wp, scale)


def _pool_bwd(dmix, h, gain, resid, name):
    L = h.shape[0]
    tm = _tile(L)
    nt = L // tm
    last_h = L // HALO - 1

    def body(dm_ref, dmn_ref, h_ref, g_ref, r_ref, dh_ref, dg_ref, buf):
        t = pl.program_id(0)
        pos = t * tm + lax.broadcasted_iota(jnp.int32, (tm, 1), 0)
        posn = (t + 1) * tm + lax.broadcasted_iota(jnp.int32, (HALO, 1), 0)
        dmv = dm_ref[...]
        dmn = dmn_ref[...]
        for gi, w in enumerate(POOL_WINDOWS):
            sl = slice(gi * PG, (gi + 1) * PG)
            buf[pl.ds(0, tm), sl] = dmv[:, sl] * _count_inv(pos, w)
            buf[pl.ds(tm, HALO), sl] = jnp.where(t < nt - 1, dmn[:, sl] * _count_inv(posn, w), 0.0)
        parts = []
        for gi, w in enumerate(POOL_WINDOWS):
            cols = pl.ds(gi * PG, PG)
            s = buf[pl.ds(0, tm), cols]
            for k in range(1, w):
                s = s + buf[pl.ds(k, tm), cols]
            parts.append(s)
        da = jnp.concatenate(parts, axis=1) - dmv
        x = h_ref[...]
        dx, dg = _rms_bwd(da, x, g_ref[...], _rstd(x))
        dh_ref[...] = r_ref[...] + dx
        _acc(dg_ref, dg, t == 0)

    row = pl.BlockSpec((tm, D), lambda t: (t, 0))
    one = pl.BlockSpec((1, D), lambda t: (0, 0))
    return pl.pallas_call(
        body, name=name, grid=(nt,),
        in_specs=[row, pl.BlockSpec((HALO, D), lambda t: (jnp.minimum((t + 1) * (tm // HALO), last_h), 0)),
                  row, one, row],
        out_specs=(row, one),
        out_shape=(jax.ShapeDtypeStruct((L, D), F32), jax.ShapeDtypeStruct((1, D), F32)),
        scratch_shapes=[pltpu.VMEM((tm + HALO, D), F32)], compiler_params=_params(1))(dmix, dmix, h, gain, resid)


def _bucket_np(d):
    d = np.maximum(d, 0)
    df = np.maximum(d, 1).astype(np.float32)
    large = 16 + (np.log(df / np.float32(16)) / np.float32(math.log(128 / 16)) * np.float32(16)).astype(np.int32)
    return np.where(d < 16, d, np.minimum(large, N_BUCKETS - 1))


def _bias_index():
    iq = np.arange(BLK)[:, None]
    jk = np.arange(2 * BLK)[None, :]
    dist = BLK + iq - jk
    band = _bucket_np(dist)
    ok = (dist >= 0) & (dist < BLK)
    band1 = np.where(ok, band, N_BUCKETS)
    band0 = np.where(ok & (jk >= BLK), band, N_BUCKETS)
    im = np.arange(N_META)[None, :]
    unused = np.full((BLK, BLK - N_META), N_BUCKETS)
    var0 = np.concatenate([_bucket_np(N_META + iq - im), unused, band0], axis=1)
    var1 = np.concatenate([_bucket_np(N_META + BLK + iq - im), unused, band1], axis=1)
    dm = np.arange(N_META)[:, None] - im
    mm = np.where(dm >= 0, _bucket_np(dm), N_BUCKETS)
    segs = [var0, var1, mm]
    return np.concatenate([s.reshape(-1) for s in segs]).astype(np.int32), [s.shape for s in segs]


P_CHUNK = 9856


def _onehot(idx_ref, grad):
    rows = lax.broadcasted_iota(jnp.int32, (128, P_CHUNK), 0)
    hit = (rows == idx_ref[...]).astype(F32)
    return jnp.where(rows == N_BUCKETS, -1.0, hit) if grad else hit


def _bias_lookup(table_aug, idx, name, carry=None):
    P = idx.shape[1]

    def body(t_ref, idx_ref, o_ref):
        o_ref[...] = lax.dot_general(t_ref[...], _onehot(idx_ref, False), _NN, precision=lax.Precision.HIGHEST,
                                     preferred_element_type=F32)

    return _carried_call(
        body, carry, name=name, grid=(P // P_CHUNK,),
        in_specs=[pl.BlockSpec((H, 128), lambda i: (0, 0)), pl.BlockSpec((1, P_CHUNK), lambda i: (0, i))],
        out_specs=[pl.BlockSpec((H, P_CHUNK), lambda i: (0, i))],
        out_shape=[jax.ShapeDtypeStruct((H, P), F32)], scratch_shapes=[], compiler_params=_params(1),
        operands=[table_aug, idx])


def _bias_lookup_bwd(dbias, idx, name):
    rows, P = dbias.shape

    def body(d_ref, idx_ref, o_ref):
        part = lax.dot_general(d_ref[...], _onehot(idx_ref, True), _NT, precision=lax.Precision.HIGHEST,
                               preferred_element_type=F32)
        _acc(o_ref, part, pl.program_id(0) == 0)

    return pl.pallas_call(
        body, name=name, grid=(P // P_CHUNK,),
        in_specs=[pl.BlockSpec((rows, P_CHUNK), lambda i: (0, i)), pl.BlockSpec((1, P_CHUNK), lambda i: (0, i))],
        out_specs=pl.BlockSpec((rows, 128), lambda i: (0, 0)),
        out_shape=jax.ShapeDtypeStruct((rows, 128), F32), compiler_params=_params(1))(dbias, idx)


def _probs(q, kbt, bias, sink):
    s = _dot(q, kbt) + bias
    m = jnp.maximum(jnp.max(s, axis=-1, keepdims=True), sink)
    e = jnp.exp(s - m)
    return e * (1.0 / (jnp.sum(e, axis=-1, keepdims=True) + jnp.exp(sink - m)))


def _attn_specs(nb):
    def cur(kh, n):
        return jnp.minimum(n, nb - 1)

    def prev(kh, n):
        return jnp.maximum(jnp.minimum(n, nb - 1) - 1, 0)

    def rows(heads, blk):
        return pl.BlockSpec((pl.Element(heads), pl.Element(BLK), pl.Element(HD)),
                            lambda kh, n: (kh * heads, pl.multiple_of(N_META + blk(kh, n) * BLK, N_META), 0))

    return dict(
        q=rows(GRP, cur),
        qo=pl.BlockSpec((GRP, BLK, HD), lambda kh, n: (kh, cur(kh, n), 0)),
        qt=pl.BlockSpec((GRP, HD, BLK), lambda kh, n: (kh, 0, cur(kh, n))),
        cur=rows(1, cur),
        prev=rows(1, prev),
        meta=pl.BlockSpec((None, BLK, HD), lambda kh, n: (kh, 0, 0)),
        curt=pl.BlockSpec((None, HD, BLK), lambda kh, n: (kh, 0, cur(kh, n))),
        prevt=pl.BlockSpec((None, HD, BLK), lambda kh, n: (kh, 0, prev(kh, n))),
        metat=pl.BlockSpec((None, HD, BLK), lambda kh, n: (kh, 0, 0)),
        bias=pl.BlockSpec((None, GRP, BLK, NKEY), lambda kh, n: (jnp.minimum(n, 1), kh, 0, 0)),
        sink=pl.BlockSpec((None, GRP * BLK, 1), lambda kh, n: (kh, 0, 0)))


def _attn_fwd(q, k, km, v, vm, band, bmeta, sink, name, carry=None):
    S = q.shape[1] - N_META
    nb = S // BLK
    sp = _attn_specs(nb)
    var = lambda kh, n: (jnp.minimum(n, 1), kh, 0, 0)
    meta = pl.BlockSpec((None, N_META, HD), lambda kh, n: (kh, 0, 0))

    def body(q_ref, kc_ref, kp_ref, vc_ref, vp_ref, km_ref, vm_ref, band_ref, bm_ref, sink_ref, o_ref):
        qv = q_ref[...].reshape(GRP * BLK, HD)
        kb = jnp.concatenate([kp_ref[0], kc_ref[0]], axis=0)
        vb = jnp.concatenate([vp_ref[0], vc_ref[0]], axis=0)
        sink = sink_ref[...]
        s_b = _dot(qv, kb, _NT) + band_ref[...].reshape(GRP * BLK, 2 * BLK)
        s_m = _dot(qv, km_ref[...], _NT) + bm_ref[...].reshape(GRP * BLK, N_META)
        m = jnp.maximum(jnp.maximum(jnp.max(s_b, axis=-1, keepdims=True), jnp.max(s_m, axis=-1, keepdims=True)), sink)
        e_b = jnp.exp(s_b - m)
        e_m = jnp.exp(s_m - m)
        inv = 1.0 / (jnp.sum(e_b, axis=-1, keepdims=True) + jnp.sum(e_m, axis=-1, keepdims=True) + jnp.exp(sink - m))
        o = _dot((e_b * inv).astype(BF16), vb) + _dot((e_m * inv).astype(BF16), vm_ref[...])
        o_ref[...] = o.reshape(GRP, BLK, HD).astype(BF16)

    return _carried_call(
        body, carry, name=name, grid=(KV, nb),
        in_specs=[sp['q'], sp['cur'], sp['prev'], sp['cur'], sp['prev'], meta, meta,
                  pl.BlockSpec((None, GRP, BLK, 2 * BLK), var), pl.BlockSpec((None, GRP, BLK, N_META), var),
                  sp['sink']],
        out_specs=[sp['qo']], out_shape=[jax.ShapeDtypeStruct((H, S, HD), BF16)], scratch_shapes=[],
        compiler_params=_params(2), operands=[q, k, k, v, v, km, vm, band, bmeta, sink])


def _attn_bwd(q, qt, k, km, kt, kmt, vt, vmt, bias, sink, do, dot_, name, carry=None):
    S = q.shape[1] - N_META
    nb = S // BLK
    sp = _attn_specs(nb)

    def body(q_ref, qt_ref, kc_ref, kp_ref, km_ref, kct_ref, kpt_ref, kmt_ref, vct_ref, vpt_ref, vmt_ref,
             bias_ref, sink_ref, do_ref, dot_ref, dq_ref, dkt_ref, dvt_ref, dkmt_ref, dvmt_ref, dbias_ref, ck, cv):
        n = pl.program_id(1)

        @pl.when(n <= 1)
        def _():
            dbias_ref[...] = jnp.zeros_like(dbias_ref)

        @pl.when(n == 0)
        def _():
            dkmt_ref[...] = jnp.zeros_like(dkmt_ref)
            dvmt_ref[...] = jnp.zeros_like(dvmt_ref)

        @pl.when(n < nb)
        def _():
            kbt = jnp.concatenate([kmt_ref[...], kpt_ref[...], kct_ref[...]], axis=1)
            vbt = jnp.concatenate([vmt_ref[...], vpt_ref[...], vct_ref[...]], axis=1)
            kb = jnp.concatenate([km_ref[...], kp_ref[0], kc_ref[0]], axis=0)
            p = _probs(q_ref[...].reshape(GRP * BLK, HD), kbt, bias_ref[...].reshape(GRP * BLK, NKEY), sink_ref[...])
            dp = _dot(do_ref[...].reshape(GRP * BLK, HD), vbt)
            ds = p * (dp - jnp.sum(p * dp, axis=-1, keepdims=True))
            dbias_ref[...] += ds.reshape(GRP, BLK, NKEY)
            ds16 = ds.astype(BF16)
            dq_ref[...] = _dot(ds16, kb).reshape(GRP, BLK, HD).astype(BF16)
            qtv = jnp.concatenate([qt_ref[g] for g in range(GRP)], axis=1)
            dotv = jnp.concatenate([dot_ref[g] for g in range(GRP)], axis=1)
            dkt = _dot(qtv, ds16)
            dvt = _dot(dotv, p.astype(BF16))
            dkmt_ref[...] += dkt[:, 0:BLK]
            dvmt_ref[...] += dvt[:, 0:BLK]

            @pl.when(n >= 1)
            def _():
                dkt_ref[...] = (ck[...] + dkt[:, BLK:2 * BLK]).astype(BF16)
                dvt_ref[...] = (cv[...] + dvt[:, BLK:2 * BLK]).astype(BF16)

            ck[...] = dkt[:, 2 * BLK:3 * BLK]
            cv[...] = dvt[:, 2 * BLK:3 * BLK]

        @pl.when(n == nb)
        def _():
            dkt_ref[...] = ck[...].astype(BF16)
            dvt_ref[...] = cv[...].astype(BF16)

    kvout = pl.BlockSpec((None, HD, BLK), lambda kh, n: (kh, 0, jnp.maximum(n - 1, 0)))
    return _carried_call(
        body, carry, name=name, grid=(KV, nb + 1),
        in_specs=[sp['q'], sp['qt'], sp['cur'], sp['prev'], sp['meta'], sp['curt'], sp['prevt'], sp['metat'],
                  sp['curt'], sp['prevt'], sp['metat'], sp['bias'], sp['sink'], sp['q'], sp['qt']],
        out_specs=[sp['qo'], kvout, kvout, sp['metat'], sp['metat'], sp['bias']],
        out_shape=[jax.ShapeDtypeStruct((H, S, HD), BF16), jax.ShapeDtypeStruct((KV, HD, S), BF16),
                   jax.ShapeDtypeStruct((KV, HD, S), BF16), jax.ShapeDtypeStruct((KV, HD, BLK), F32),
                   jax.ShapeDtypeStruct((KV, HD, BLK), F32), jax.ShapeDtypeStruct((2, H, BLK, NKEY), F32)],
        scratch_shapes=[pltpu.VMEM((HD, BLK), F32), pltpu.VMEM((HD, BLK), F32)],
        compiler_params=_params(2), operands=[q, qt, k, k, km, kt, kt, kmt, vt, vt, vmt, bias, sink, do, dot_])


def _meta_softmax(q, k, bias, sink):
    s = _dot(q, k, _NT) + bias
    m = jnp.maximum(jnp.max(s, axis=-1, keepdims=True), sink)
    e = jnp.exp(s - m)
    return e * (1.0 / (jnp.sum(e, axis=-1, keepdims=True) + jnp.exp(sink - m)))


def _attn_meta_fwd(qm, km, vm, bias, sink, name):
    def body(q_ref, k_ref, v_ref, b_ref, s_ref, o_ref):
        for h in range(H):
            p = _meta_softmax(q_ref[h], k_ref[h // GRP], b_ref[h], s_ref[h])
            o_ref[h] = _dot(p.astype(BF16), v_ref[h // GRP]).astype(BF16)

    return pl.pallas_call(body, name=name, out_shape=jax.ShapeDtypeStruct((H, N_META, HD), BF16))(
        qm, km, vm, bias, sink)


def _attn_meta_bwd(qm, km, vm, bias, sink, do, name):
    def body(q_ref, k_ref, v_ref, b_ref, s_ref, do_ref, dq_ref, dk_ref, dv_ref, db_ref):
        for kh in range(KV):
            k, v = k_ref[kh], v_ref[kh]
            dk = jnp.zeros((N_META, HD), F32)
            dv = jnp.zeros((N_META, HD), F32)
            for g in range(GRP):
                h = kh * GRP + g
                q, dov = q_ref[h], do_ref[h]
                p = _meta_softmax(q, k, b_ref[h], s_ref[h])
                dp = _dot(dov, v, _NT)
                ds = p * (dp - jnp.sum(p * dp, axis=-1, keepdims=True))
                db_ref[h] = ds
                ds16 = ds.astype(BF16)
                dq_ref[h] = _dot(ds16, k).astype(BF16)
                dk = dk + _dot(ds16, q, _TN)
                dv = dv + _dot(p.astype(BF16), dov, _TN)
            dk_ref[kh] = dk
            dv_ref[kh] = dv

    return pl.pallas_call(
        body, name=name,
        out_shape=(jax.ShapeDtypeStruct((H, N_META, HD), BF16), jax.ShapeDtypeStruct((KV, N_META, HD), F32),
                   jax.ShapeDtypeStruct((KV, N_META, HD), F32), jax.ShapeDtypeStruct((H, N_META, N_META), F32)))(
        qm, km, vm, bias, sink, do)


def _unheads(t):
    return jnp.transpose(t, (1, 0, 2)).reshape(t.shape[1], t.shape[0] * HD)


def _unheads_t(t):
    return jnp.transpose(t, (2, 0, 1)).reshape(t.shape[2], t.shape[0] * HD)


def _pad_block(t, axis):
    pad = [(0, 0)] * t.ndim
    pad[axis] = (0, BLK - N_META)
    return jnp.pad(t, pad)


def _adam(w, g, m, v, name):
    lead, R, C = w.shape
    tr = _tile(R, 512) if R % 16 == 0 else R

    def body(w_ref, g_ref, m_ref, v_ref, d_ref, mo_ref, vo_ref, go_ref):
        gv = g_ref[...]
        go_ref[...] = gv
        mn = B1 * m_ref[...] + (1.0 - B1) * gv
        vn = B2 * v_ref[...] + (1.0 - B2) * (gv * gv)
        m_hat = mn / (1.0 - B1 ** STEP)
        v_hat = vn / (1.0 - B2 ** STEP)
        d_ref[...] = -LR * (m_hat / (jnp.sqrt(v_hat) + ADAM_EPS) + WD * w_ref[...])
        mo_ref[...] = mn
        vo_ref[...] = vn

    blk = pl.BlockSpec((None, tr, C), lambda l, i: (l, i, 0))
    shp = jax.ShapeDtypeStruct((lead, R, C), F32)
    return pl.pallas_call(body, name=name, grid=(lead, R // tr), in_specs=[blk] * 4, out_specs=(blk,) * 4,
                          out_shape=(shp,) * 4, compiler_params=_params(2))(w, g, m, v)


def _sum_leaf(lands, pends, ids, name):
    n = len(lands)
    _, R, C = lands[0].shape
    tr = _tile(R, 256)

    def body(ids_ref, *refs):
        out_ref = refs[2 * n]
        for k in range(n):
            acc = refs[n + k][...].astype(F32)
            for s in range(7):
                acc = acc + refs[k][s].astype(F32)
            out_ref[k] = acc

    grid_spec = pltpu.PrefetchScalarGridSpec(
        num_scalar_prefetch=1, grid=(R // tr,),
        in_specs=[pl.BlockSpec((7, tr, C), lambda i, ids: (0, i, 0))] * n
        + [pl.BlockSpec((None, None, tr, C), lambda i, ids: (ids[0], ids[1], i, 0))] * n,
        out_specs=pl.BlockSpec((n, None, tr, C), lambda i, ids: (0, ids[1], i, 0)))
    return pl.pallas_call(body, name=name, grid_spec=grid_spec, out_shape=jax.ShapeDtypeStruct((n, 2, R, C), F32),
                          compiler_params=_params(1))(ids, *lands, *pends)


def _place():
    x, y, c = lax.axis_index("x"), lax.axis_index("y"), lax.axis_index("c")
    chips = [(1 - x, y), (x, 1 - y), (1 - x, 1 - y)]
    return x, y, c, chips


def _rcopy(src, dst, ssem, rsem, dev):
    return pltpu.make_async_remote_copy(src_ref=src, dst_ref=dst, send_sem=ssem, recv_sem=rsem,
                                        device_id=dev, device_id_type=MESH)


class _Carry:
    def __init__(self, kind, arrays):
        self.kind, self.arrays, self.n = kind, list(arrays), len(arrays)
        self.per = 3 if kind == 'gather' else 7
        if kind == 'gather':
            self.out_shape = [jax.ShapeDtypeStruct((NSH,) + a.shape, a.dtype) for a in self.arrays]
        else:
            self.out_shape = [jax.ShapeDtypeStruct((7,) + a.shape[2:], a.dtype) for a in self.arrays]
        dma = pltpu.SemaphoreType.DMA
        self.scratch = [dma((self.per * self.n,)), dma((self.per * self.n,)), dma((self.n,))]
        if kind == 'gather':
            self.scratch += [pltpu.VMEM(a.shape, a.dtype) for a in self.arrays]

    def _copies(self, cin, cout, scr):
        ssem, rsem, loc = scr[:3]
        x, y, c, chips = _place()
        local, sends, recvs = [], [], []
        for k in range(self.n):
            if self.kind == 'gather':
                me = 2 * x + y
                local.append((pltpu.make_async_copy(cin[k], scr[3 + k], loc.at[k]),
                              pltpu.make_async_copy(scr[3 + k], cout[k].at[me], loc.at[k])))
                for j, (cx, cy) in enumerate(chips):
                    i = 3 * k + j
                    sends.append(_rcopy(cin[k], cout[k].at[me], ssem.at[i], rsem.at[i], (cx, cy, c)))
                    got = cout[k].at[2 * cx + cy]
                    recvs.append(_rcopy(got, got, ssem.at[i], rsem.at[i], (cx, cy, c)))
            else:
                for f in range(1, 8):
                    px = 1 - x if (f >> 2) & 1 else x
                    py = 1 - y if (f >> 1) & 1 else y
                    pc = 1 - c if f & 1 else c
                    i = 7 * k + f - 1
                    got = cout[k].at[f - 1]
                    sends.append(_rcopy(cin[k].at[2 * px + py, pc], got, ssem.at[i], rsem.at[i], (px, py, pc)))
                    recvs.append(_rcopy(got, got, ssem.at[i], rsem.at[i], (px, py, pc)))
        return local, sends, recvs

    def start(self, cin, cout, scr):
        local, sends, _ = self._copies(cin, cout, scr)
        for cp in [to_vmem for to_vmem, _ in local] + sends:
            cp.start()

    def finish(self, cin, cout, scr):
        local, sends, recvs = self._copies(cin, cout, scr)
        for to_vmem, to_slot in local:
            to_vmem.wait()
            to_slot.start()
        for cp in recvs:
            cp.wait_recv()
        for cp in sends:
            cp.wait_send()
        for _, to_slot in local:
            to_slot.wait()


def _carried_call(body, carry, *, name, grid, in_specs, out_specs, out_shape, scratch_shapes, compiler_params,
                  operands):
    n_in, n_out = len(in_specs), len(out_specs)
    if carry is None:
        return pl.pallas_call(body, name=name, grid=grid, in_specs=in_specs, out_specs=out_specs,
                              out_shape=out_shape, scratch_shapes=scratch_shapes,
                              compiler_params=compiler_params)(*operands), []
    m = carry.n

    def full(*refs):
        ins, cin = refs[:n_in], refs[n_in:n_in + m]
        outs, cout = refs[n_in + m:n_in + m + n_out], refs[n_in + m + n_out:n_in + 2 * m + n_out]
        own = len(refs) - len(carry.scratch)
        scr, sems = refs[n_in + 2 * m + n_out:own], refs[own:]
        ids = [pl.program_id(a) for a in range(len(grid))]
        first = functools.reduce(jnp.logical_and, [i == 0 for i in ids])
        last = functools.reduce(jnp.logical_and, [i == g - 1 for i, g in zip(ids, grid)])

        @pl.when(first)
        def _():
            carry.start(cin, cout, sems)

        body(*ins, *outs, *scr)

        @pl.when(last)
        def _():
            carry.finish(cin, cout, sems)

    res = pl.pallas_call(
        full, name=name, grid=grid, in_specs=list(in_specs) + [ANY] * m, out_specs=list(out_specs) + [ANY] * m,
        out_shape=list(out_shape) + carry.out_shape, scratch_shapes=list(scratch_shapes) + carry.scratch,
        compiler_params=compiler_params)(*operands, *carry.arrays)
    return res[:n_out], res[n_out:]


def _flush(carry, name):
    m = carry.n

    def body(*refs):
        cin, cout, sems = refs[:m], refs[m:2 * m], refs[2 * m:]
        carry.start(cin, cout, sems)
        carry.finish(cin, cout, sems)

    return pl.pallas_call(body, name=name, in_specs=[ANY] * m, out_specs=[ANY] * m, out_shape=carry.out_shape,
                          scratch_shapes=carry.scratch)(*carry.arrays)


def _pair_share(leaves):
    n = len(leaves)

    def body(*refs):
        ins, outs = refs[:n], refs[n:2 * n]
        ssem, rsem = refs[2 * n:]
        x, y, c, _ = _place()
        sib = (x, y, 1 - c)
        cps = [_rcopy(ins[k].at[:, c], outs[k].at[:, c], ssem.at[k], rsem.at[k], sib) for k in range(n)]
        for cp in cps:
            cp.start()
        for k in range(n):
            got = outs[k].at[:, 1 - c]
            _rcopy(got, got, ssem.at[k], rsem.at[k], sib).wait_recv()
        for cp in cps:
            cp.wait_send()

    dma = pltpu.SemaphoreType.DMA
    return pl.pallas_call(
        body, name="grad_pair_share", in_specs=[ANY] * n, out_specs=[ANY] * n,
        out_shape=[jax.ShapeDtypeStruct(a.shape, a.dtype) for a in leaves],
        input_output_aliases={k: k for k in range(n)},
        scratch_shapes=[dma((n,)), dma((n,))])(*leaves)


def _allreduce_small(pack):
    R = pack.shape[0]

    def body(in_ref, out_ref, buf, ssem, rsem):
        x, y, c, _ = _place()
        me = 4 * x + 2 * y + c
        buf[me] = in_ref[...]
        peers = []
        for k in range(1, 8):
            fx, fy, fc = (k >> 2) & 1, (k >> 1) & 1, k & 1
            peers.append((1 - x if fx else x, 1 - y if fy else y, 1 - c if fc else c))
        cps = [_rcopy(in_ref, buf.at[me], ssem.at[k], rsem.at[k], p) for k, p in enumerate(peers)]
        for cp in cps:
            cp.start()
        for k, (px, py, pc) in enumerate(peers):
            got = buf.at[4 * px + 2 * py + pc]
            _rcopy(got, got, ssem.at[k], rsem.at[k], (px, py, pc)).wait_recv()
        for cp in cps:
            cp.wait_send()
        acc = buf[0]
        for s in range(1, 8):
            acc = acc + buf[s]
        out_ref[...] = acc

    dma = pltpu.SemaphoreType.DMA
    return pl.pallas_call(
        body, name="allreduce_small", out_shape=jax.ShapeDtypeStruct(pack.shape, F32),
        in_specs=[pl.BlockSpec(memory_space=pltpu.VMEM)], out_specs=pl.BlockSpec(memory_space=pltpu.VMEM),
        scratch_shapes=[pltpu.VMEM((8, R, 128), F32), dma((7,)), dma((7,))])(pack)


def _pack(arrs):
    flat = jnp.concatenate([a.reshape(-1).astype(F32) for a in arrs])
    n = flat.shape[0]
    rows = -(-n // 1024) * 8
    return jnp.pad(flat, (0, rows * 128 - n)).reshape(rows, 128)


def _unpack(pack, shapes):
    flat, out, o = pack.reshape(-1), [], 0
    for s in shapes:
        n = int(np.prod(s))
        out.append(flat[o:o + n].reshape(s))
        o += n
    return out


def _ffn_keys(i):
    return [('ffn_w_gate', i), ('ffn_w_up', i), ('ffn_w_down', i)]


GATHER_PLAN = {'attn_fwd0': _ffn_keys(0) + [('conv_w_in', 0), ('conv_w_out', 0)],
               'ffn_fwd0': _ffn_keys(1),
               'ffn_fwd1': _ffn_keys(2) + [('pool_w', 0)],
               'ffn_fwd2': [('attn_w_qkv', 1), ('attn_w_o', 1)],
               'attn_fwd3': _ffn_keys(3)}
REDUCE_PLAN = {'attn_bwd3': _ffn_keys(3),
               'ffn_bwd_act2': [('attn_w_qkv', 1), ('attn_w_o', 1)],
               'ffn_bwd_act1': _ffn_keys(2) + [('pool_w', 0)],
               'ffn_bwd_act0': _ffn_keys(1),
               'ffn_bwd_w0': [('conv_w_in', 0), ('conv_w_out', 0)],
               'attn_bwd0': _ffn_keys(0) + [('attn_w_o', 0)],
               'qkv_bwd0': [('attn_w_qkv', 0)]}


def kernel(x, meta_tokens, rel_bias_table, norm_mix, norm_ffn, norm_final, attn_w_qkv, attn_b_qkv, attn_w_o, attn_b_o, attn_sinks, conv_w_in, conv_w, conv_w_out, pool_w, pool_scale, ffn_w_gate, ffn_w_up, ffn_w_down, loss_target, m_meta_tokens, m_rel_bias_table, m_norm_mix, m_norm_ffn, m_norm_final, m_attn_w_qkv, m_attn_b_qkv, m_attn_w_o, m_attn_b_o, m_attn_sinks, m_conv_w_in, m_conv_w, m_conv_w_out, m_pool_w, m_pool_scale, m_ffn_w_gate, m_ffn_w_up, m_ffn_w_down, v_meta_tokens, v_rel_bias_table, v_norm_mix, v_norm_ffn, v_norm_final, v_attn_w_qkv, v_attn_b_qkv, v_attn_w_o, v_attn_b_o, v_attn_sinks, v_conv_w_in, v_conv_w, v_conv_w_out, v_pool_w, v_pool_scale, v_ffn_w_gate, v_ffn_w_up, v_ffn_w_down):
    args = locals()
    w = {n: args[n] for n in WEIGHTS}
    mom = {n: args['m_' + n] for n in WEIGHTS}
    var = {n: args['v_' + n] for n in WEIGHTS}
    mx, my = lax.axis_index("x"), lax.axis_index("y")
    chip = 2 * mx + my
    S = x.shape[1]
    scale = jnp.asarray(HD ** -0.5, BF16)

    GW = {}
    pending = {}
    land = {}

    def gather_carry(call):
        keys = GATHER_PLAN.get(call)
        return _Carry('gather', [w[n][l].astype(BF16) for n, l in keys]) if keys else None

    def reduce_carry(call):
        keys = REDUCE_PLAN.get(call)
        return _Carry('reduce', [pending[k] for k in keys]) if keys else None

    def pend(n, l, g):
        pending[(n, l)] = g.reshape((NSH, 2) + HALF_SHAPE[n])

    small_in = jnp.concatenate([
        jnp.pad(w['meta_tokens'], ((0, 0), (0, 128))), w['attn_b_qkv'], jnp.pad(w['attn_b_o'], ((0, 0), (0, 128))),
        jnp.pad(w['conv_w'][0], ((0, 0), (0, 128))), jnp.pad(w['pool_scale'], ((0, 0), (0, 128)))], axis=0)
    gsmall, GW[('attn_w_qkv', 0)] = _flush(
        _Carry('gather', [small_in, w['attn_w_qkv'][0].astype(BF16)]), "gather_first")

    def cols(rows, width):
        return jnp.transpose(rows[:, :, :width], (1, 0, 2)).reshape(rows.shape[1], NSH * width)

    meta_full = cols(gsmall[:, 0:16], 256)
    b_qkv = cols(gsmall[:, 16:18], 384)
    b_o = cols(gsmall[:, 18:20], 256)
    conv_k = cols(gsmall[:, 20:23], 256)
    p_scale = cols(gsmall[:, 23:24], 256)

    idx_np, _ = _bias_index()
    idx = jnp.asarray(idx_np).reshape(1, -1)
    table_aug = jnp.concatenate([rel_bias_table.T, jnp.full((H, 1), NEG, F32),
                                 jnp.zeros((H, 127 - N_BUCKETS), F32)], axis=1)
    (bias_flat,), (GW[('attn_w_o', 0)],) = _bias_lookup(
        table_aug, idx, "bias_lookup", _Carry('gather', [w['attn_w_o'][0].astype(BF16)]))
    nblock = BLK * NKEY
    bias_blk = jnp.transpose(bias_flat[:, :2 * nblock].reshape(H, 2, BLK, NKEY), (1, 0, 2, 3))
    bias_mm = bias_flat[:, 2 * nblock:].reshape(H, N_META, N_META)
    bias_band, bias_meta = bias_blk[..., BLK:], bias_blk[..., :N_META]

    h = jnp.concatenate([meta_full, x[0]], axis=0)
    saved = []
    for i in range(DEPTH):
        kind, j = i % 3, i // 3
        gm = norm_mix[i:i + 1]
        st = dict(h=h)
        if kind == 0:
            w_o = GW[('attn_w_o', j)].reshape(1, D, D)
            q, k, v, a = _linear(h, GW[('attn_w_qkv', j)], gain=gm, bias=b_qkv[j:j + 1], qkv_heads=True,
                                 name=f"qkv{i}")
            tr = lambda t: jnp.swapaxes(t, 1, 2)
            lay = dict(q=q, qt=tr(q[:, N_META:]), qm=q[:, :N_META],
                       k=k, kt=tr(k[:, N_META:]), km=k[:, :N_META],
                       kmp=_pad_block(k[:, :N_META], 1), kmt=_pad_block(tr(k[:, :N_META]), 2),
                       v=v, vt=tr(v[:, N_META:]), vm=v[:, :N_META], vmt=_pad_block(tr(v[:, :N_META]), 2))
            sink_r = jnp.broadcast_to(attn_sinks[j].reshape(KV, GRP, 1, 1), (KV, GRP, BLK, 1)).reshape(KV, GRP * BLK, 1)
            sink_m = jnp.broadcast_to(attn_sinks[j].reshape(H, 1, 1), (H, N_META, 1))
            (o_r,), got = _attn_fwd(lay['q'], lay['k'], lay['km'], lay['v'], lay['vm'], bias_band, bias_meta, sink_r,
                                    f"attn_fwd{i}", gather_carry(f"attn_fwd{i}"))
            GW.update(zip(GATHER_PLAN.get(f"attn_fwd{i}", []), got))
            o_m = _attn_meta_fwd(lay['qm'], lay['km'], lay['vm'], bias_mm, sink_m, f"attn_meta_fwd{i}")
            o = _unheads(jnp.concatenate([o_m, o_r], axis=1))
            h1 = _linear(o, w_o, bias=b_o[j:j + 1], resid=h, out_dtype=F32, name=f"attn_out{i}")
            st.update(a=a, lay=lay, sinks=(sink_r, sink_m), o=o, w_o=w_o)
        elif kind == 1:
            z3, a = _linear(h, GW[('conv_w_in', j)], gain=gm, name=f"conv_in{i}")
            yv = _conv_fwd(z3, conv_k, f"conv_fwd{i}")
            w_cout = GW[('conv_w_out', j)].reshape(1, D, D)
            h1 = _linear(yv, w_cout, resid=h, out_dtype=F32, name=f"conv_out{i}")
            st.update(a=a, z3=z3, y=yv, w_cout=w_cout)
        else:
            w_pool = jnp.transpose(GW[('pool_w', j)], (1, 0, 2, 3)).reshape(4, PG, PG)
            mix = _pool_fwd(h, gm, f"pool_fwd{i}")
            h1 = _pool_out(mix, w_pool, p_scale, h, f"pool_out{i}")
            st.update(mix=mix, w_pool=w_pool)
        ffn_w = [GW[k] for k in _ffn_keys(i)]
        (hn, Gp, Up, Ap, Bp), got = _ffn_fwd(h1, norm_ffn[i:i + 1], *ffn_w, f"ffn_fwd{i}", gather_carry(f"ffn_fwd{i}"))
        GW.update(zip(GATHER_PLAN.get(f"ffn_fwd{i}", []), got))
        st.update(h1=h1, G=Gp, U=Up, A=Ap, B=Bp, ffn_w=ffn_w)
        saved.append(st)
        h = hn

    target = jnp.pad(loss_target[0], ((N_META, 0), (0, 0)))
    dh, d_nfinal, loss_part = _loss_head(h, norm_final.reshape(1, D), target)
    d_nmix, d_nffn = [None] * DEPTH, [None] * DEPTH
    d_bqkv, d_bo, dflats = [None, None], [None, None], [None, None]
    d_convk = d_pscale = None
    for i in reversed(range(DEPTH)):
        kind, j = i % 3, i // 3
        st = saved[i]
        call = f"ffn_bwd_act{i}"
        (DG, DU, dh1, d_nffn[i]), got = _ffn_bwd_act(dh, st['h1'], norm_ffn[i:i + 1], st['G'], st['U'], *st['ffn_w'],
                                                     call, reduce_carry(call))
        land.update(zip(REDUCE_PLAN.get(call, []), got))
        call = f"ffn_bwd_w{i}"
        (gw, uw, dw_), got = _ffn_bwd_w(st['B'], st['A'], DG, DU, dh, call, reduce_carry(call))
        land.update(zip(REDUCE_PLAN.get(call, []), got))
        pend('ffn_w_gate', i, gw)
        pend('ffn_w_up', i, uw)
        pend('ffn_w_down', i, dw_)
        gm = norm_mix[i:i + 1]
        if kind == 0:
            lay = st['lay']
            sink_r, sink_m = st['sinks']
            gwo, d_bo[j] = _wgrad(st['o'], dh1, mode='row', colsum=True, name=f"attn_out_wgrad{i}")
            pend('attn_w_o', j, gwo)
            do = _linear_t(dh1, st['w_o'], to_heads=True, name=f"attn_out_bwd{i}")
            call = f"attn_bwd{i}"
            (dq_r, dkt, dvt, dkmt, dvmt, dbias), got = _attn_bwd(
                lay['q'], lay['qt'], lay['k'], lay['kmp'], lay['kt'], lay['kmt'], lay['vt'], lay['vmt'], bias_blk,
                sink_r, do, jnp.swapaxes(do[:, N_META:], 1, 2), call, reduce_carry(call))
            land.update(zip(REDUCE_PLAN.get(call, []), got))
            dq_m, dkm2, dvm2, dbmm = _attn_meta_bwd(lay['qm'], lay['km'], lay['vm'], bias_mm, sink_m,
                                                    do[:, :N_META], f"attn_meta_bwd{i}")
            dflats[j] = jnp.concatenate([jnp.transpose(dbias, (1, 0, 2, 3)).reshape(H, -1), dbmm.reshape(H, -1)], axis=1)
            dkm = (jnp.transpose(dkmt[:, :, :N_META], (0, 2, 1)) + dkm2).astype(BF16)
            dvm = (jnp.transpose(dvmt[:, :, :N_META], (0, 2, 1)) + dvm2).astype(BF16)
            dqkv = jnp.concatenate([
                jnp.concatenate([_unheads(dq_m), _unheads(dq_r)], axis=0) * scale,
                jnp.concatenate([_unheads(dkm), _unheads_t(dkt)], axis=0),
                jnp.concatenate([_unheads(dvm), _unheads_t(dvt)], axis=0)], axis=1)
            gq, d_bqkv[j] = _wgrad(st['a'], dqkv, mode='col', colsum=True, name=f"qkv_wgrad{i}")
            pend('attn_w_qkv', j, gq)
            call = f"qkv_bwd{i}"
            res = _linear_t(dqkv, GW[('attn_w_qkv', j)], rms=(st['h'], gm, dh1), carry=reduce_carry(call), name=call)
            if call in REDUCE_PLAN:
                res, got = res
                land.update(zip(REDUCE_PLAN[call], got))
            dh, d_nmix[i] = res
        elif kind == 1:
            pend('conv_w_out', j, _wgrad(st['y'], dh1, mode='row', name=f"conv_out_wgrad{i}"))
            dy = _linear_t(dh1, st['w_cout'], out_dtype=F32, name=f"conv_out_bwd{i}")
            dz3, d_convk = _conv_bwd(st['z3'], conv_k, dy, f"conv_bwd{i}")
            pend('conv_w_in', j, _wgrad(st['a'], dz3, mode='col', name=f"conv_in_wgrad{i}"))
            dh, d_nmix[i] = _linear_t(dz3, GW[('conv_w_in', j)], rms=(st['h'], gm, dh1), name=f"conv_in_bwd{i}")
        else:
            dmix, dwp, d_pscale = _pool_out_bwd(dh1, st['mix'], st['w_pool'], p_scale, f"pool_out_bwd{i}")
            pend('pool_w', j, jnp.transpose(dwp.reshape(4, NSH, PG // NSH, PG), (1, 0, 2, 3)).astype(BF16))
            dh, d_nmix[i] = _pool_bwd(dmix, st['h'], gm, dh1, f"pool_bwd{i}")

    dts = _bias_lookup_bwd(jnp.concatenate(dflats, axis=0), idx, "bias_lookup_bwd")
    d_table = dts[:H, :N_BUCKETS].T + dts[H:, :N_BUCKETS].T
    d_sinks = dts[:, N_BUCKETS].reshape(2, H)

    ids = jnp.stack([chip, lax.axis_index("c")]).astype(jnp.int32)
    shared = _pair_share([_sum_leaf([land[(n, l)] for l in range(w[n].shape[0])],
                                    [pending[(n, l)] for l in range(w[n].shape[0])], ids, f"sum_{n}") for n in BIG])
    grads = {n: s.reshape(w[n].shape) for n, s in zip(BIG, shared)}

    small_full = [dh[:N_META], d_table, jnp.concatenate(d_nmix, axis=0), jnp.concatenate(d_nffn, axis=0), d_nfinal,
                  jnp.concatenate(d_bqkv, axis=0), jnp.concatenate(d_bo, axis=0), d_sinks, d_convk,
                  d_pscale, loss_part[:, 0:1]]
    red = _unpack(_allreduce_small(_pack(small_full)), [a.shape for a in small_full])
    g_meta, g_table, g_nmix, g_nffn, g_nfinal, g_bqkv, g_bo, g_sinks, g_convk, g_pscale, loss = red

    def shard(a, width):
        return lax.dynamic_slice_in_dim(a, chip * width, width, axis=1)

    grads.update(meta_tokens=shard(g_meta, 256), rel_bias_table=g_table, norm_mix=g_nmix, norm_ffn=g_nffn,
                 norm_final=g_nfinal.reshape(D), attn_b_qkv=shard(g_bqkv, 384), attn_b_o=shard(g_bo, 256),
                 attn_sinks=g_sinks, conv_w=shard(g_convk, 256)[None], pool_scale=shard(g_pscale, 256))

    delta, new_m, new_v = {}, {}, {}
    for n in BIG:
        shp = w[n].shape
        r3 = (int(np.prod(shp[:-2])),) + shp[-2:]
        swap = (lambda a: jnp.swapaxes(a, 1, 2)) if shp[-1] % 128 else (lambda a: a)
        res = _adam(*[swap(a.reshape(r3)) for a in (w[n], grads[n], mom[n], var[n])], f"adam_{n}")
        delta[n], new_m[n], new_v[n], grads[n] = [swap(a).reshape(shp) for a in res]
    shapes = [w[n].shape for n in SMALL]
    packed = [_pack([d[n] for n in SMALL])[None] for d in (w, grads, mom, var)]
    for dst, res in zip((delta, new_m, new_v), _adam(*packed, "adam_small")[:3]):
        dst.update(zip(SMALL, _unpack(res[0], shapes)))

    return (loss.reshape(()), dh[N_META:][None], *[grads[n] for n in WEIGHTS], *[delta[n] for n in WEIGHTS],
            *[new_m[n] for n in WEIGHTS], *[new_v[n] for n in WEIGHTS])
```

```python
import functools
import math

import numpy as np
import jax
import jax.numpy as jnp
from jax import lax
from jax.experimental import pallas as pl
from jax.experimental.pallas import tpu as pltpu

F32, BF16 = jnp.float32, jnp.bfloat16
D = 1024
N_META = 16
EPS = 1e-6
H, KV, GRP, HD = 16, 4, 4, 64
BLK = 128
NKEY = 3 * BLK
N_BUCKETS = 32
POOL_WINDOWS = (2, 4, 8, 16)
PG = 256
DFF = 2816
NSH = 4
FC = DFF // NSH
HALO = 16
NEG = -1e30
DEPTH = 4
LR, B1, B2, ADAM_EPS, WD, STEP = 0.001, 0.9, 0.999, 1e-08, 0.01, 10
MESH = pl.DeviceIdType.MESH
ANY = pl.BlockSpec(memory_space=pl.ANY)
VMEM_LIMIT = 48 * 1024 * 1024
VMEM_BIG = 58 * 1024 * 1024
BIG_TILE = 1024
ACT_CHUNKS = 3

_NN = (((1,), (0,)), ((), ()))
_NT = (((1,), (1,)), ((), ()))
_TN = (((0,), (0,)), ((), ()))

WEIGHTS = ['meta_tokens', 'rel_bias_table', 'norm_mix', 'norm_ffn', 'norm_final', 'attn_w_qkv', 'attn_b_qkv',
           'attn_w_o', 'attn_b_o', 'attn_sinks', 'conv_w_in', 'conv_w', 'conv_w_out', 'pool_w', 'pool_scale',
           'ffn_w_gate', 'ffn_w_up', 'ffn_w_down']
BIG = ['attn_w_qkv', 'attn_w_o', 'conv_w_in', 'conv_w_out', 'pool_w', 'ffn_w_gate', 'ffn_w_up', 'ffn_w_down']
SMALL = [w for w in WEIGHTS if w not in BIG]
HALF_SHAPE = {'attn_w_qkv': (512, 384), 'attn_w_o': (128, 1024), 'conv_w_in': (512, 768),
              'conv_w_out': (128, 1024), 'pool_w': (128, 256), 'ffn_w_gate': (512, 704),
              'ffn_w_up': (512, 704), 'ffn_w_down': (352, 1024)}


def _dot(a, b, dims=_NN):
    return lax.dot_general(a, b, dims, preferred_element_type=F32)


def _tile(n, cap=512):
    best = None
    for t in range(16, min(n, cap) + 1, 16):
        if n % t == 0:
            best = t
    assert best is not None, n
    return best


def _params(n_axes, vmem=VMEM_LIMIT):
    return pltpu.CompilerParams(dimension_semantics=("arbitrary",) * n_axes, vmem_limit_bytes=vmem)


def _rstd(x):
    return lax.rsqrt(jnp.mean(x * x, axis=-1, keepdims=True) + EPS)


def _rms_bwd(dy, x, g, r):
    u = dy * g
    dx = r * u - x * ((r * r * r) * (jnp.sum(x * u, axis=-1, keepdims=True) * (1.0 / D)))
    return dx, jnp.sum(dy * (x * r), axis=0, keepdims=True)


def _sigmoid(x):
    return 1.0 / (1.0 + jnp.exp(-x))


def _acc(ref, val, first):
    @pl.when(first)
    def _():
        ref[...] = val

    @pl.when(jnp.logical_not(first))
    def _():
        ref[...] += val


def _linear(x, w, *, gain=None, bias=None, resid=None, out_dtype=BF16, qkv_heads=False, cap=BIG_TILE, name):
    L, K = x.shape
    J, _, Nc = w.shape
    tm = _tile(L, cap)
    has_g, has_b, has_r = gain is not None, bias is not None, resid is not None
    n_main = 3 if qkv_heads else 1

    def body(*refs):
        refs = list(refs)
        x_ref, w_ref = refs[:2]
        i = 2
        g_ref = b_ref = r_ref = None
        if has_g:
            g_ref, i = refs[i], i + 1
        if has_b:
            b_ref, i = refs[i], i + 1
        if has_r:
            r_ref, i = refs[i], i + 1
        out_ref = refs[i]
        if has_g:
            xf = x_ref[...]
            xv = (xf * _rstd(xf) * g_ref[...]).astype(BF16)
            refs[i + n_main][...] = xv
        else:
            xv = x_ref[...]
        for s in range(J):
            sl = slice(s * Nc, (s + 1) * Nc)
            acc = _dot(xv, w_ref[s])
            if has_b:
                acc = acc + b_ref[:, sl]
            if has_r:
                acc = acc + r_ref[:, sl]
            if not qkv_heads:
                out_ref[:, sl] = acc.astype(out_dtype)
                continue
            for r in range(Nc // HD):
                hd, blk = s * (Nc // HD) + r, acc[:, r * HD:(r + 1) * HD]
                if hd < H:
                    refs[i][hd] = (blk * HD ** -0.5).astype(BF16)
                elif hd < H + KV:
                    refs[i + 1][hd - H] = blk.astype(BF16)
                else:
                    refs[i + 2][hd - H - KV] = blk.astype(BF16)

    row = lambda n: pl.BlockSpec((tm, n), lambda t: (t, 0))
    one = lambda n: pl.BlockSpec((1, n), lambda t: (0, 0))
    in_specs = [row(K), pl.BlockSpec((J, K, Nc), lambda t: (0, 0, 0))]
    ops = [x, w]
    if has_g:
        in_specs.append(one(K))
        ops.append(gain)
    if has_b:
        in_specs.append(one(J * Nc))
        ops.append(bias)
    if has_r:
        in_specs.append(row(J * Nc))
        ops.append(resid)
    if qkv_heads:
        heads = lambda n: pl.BlockSpec((n, tm, HD), lambda t: (0, t, 0))
        out_specs = [heads(H), heads(KV), heads(KV)]
        out_shape = [jax.ShapeDtypeStruct((n, L, HD), BF16) for n in (H, KV, KV)]
    else:
        out_specs, out_shape = [row(J * Nc)], [jax.ShapeDtypeStruct((L, J * Nc), out_dtype)]
    if has_g:
        out_specs, out_shape = out_specs + [row(K)], out_shape + [jax.ShapeDtypeStruct((L, K), BF16)]
    res = pl.pallas_call(body, name=name, grid=(L // tm,), in_specs=in_specs, out_specs=out_specs,
                         out_shape=out_shape, compiler_params=_params(1, VMEM_BIG))(*ops)
    return res[0] if len(res) == 1 else res


def _linear_t(dy, w, *, out_dtype=BF16, rms=None, to_heads=False, carry=None, cap=BIG_TILE, name):
    L = dy.shape[0]
    J, K, Nc = w.shape
    tm = _tile(L, cap)
    has_rms = rms is not None

    def body(*refs):
        t = pl.program_id(0)
        dy_ref, w_ref = refs[:2]
        acc = _dot(dy_ref[:, 0:Nc].astype(BF16), w_ref[0], _NT)
        for s in range(1, J):
            acc = acc + _dot(dy_ref[:, s * Nc:(s + 1) * Nc].astype(BF16), w_ref[s], _NT)
        if has_rms:
            h_ref, g_ref, r_ref, out_ref, dg_ref = refs[2:7]
            x = h_ref[...]
            dx, dg = _rms_bwd(acc, x, g_ref[...], _rstd(x))
            out_ref[...] = r_ref[...] + dx
            _acc(dg_ref, dg, t == 0)
        elif to_heads:
            for hd in range(K // HD):
                refs[2][hd] = acc[:, hd * HD:(hd + 1) * HD].astype(out_dtype)
        else:
            refs[2][...] = acc.astype(out_dtype)

    row = lambda n: pl.BlockSpec((tm, n), lambda t: (t, 0))
    in_specs = [row(J * Nc), pl.BlockSpec((J, K, Nc), lambda t: (0, 0, 0))]
    ops = [dy, w]
    if has_rms:
        in_specs += [row(K), pl.BlockSpec((1, K), lambda t: (0, 0)), row(K)]
        ops += list(rms)
        out_specs = [row(K), pl.BlockSpec((1, K), lambda t: (0, 0))]
        out_shape = [jax.ShapeDtypeStruct((L, K), F32), jax.ShapeDtypeStruct((1, K), F32)]
    elif to_heads:
        out_specs = [pl.BlockSpec((K // HD, tm, HD), lambda t: (0, t, 0))]
        out_shape = [jax.ShapeDtypeStruct((K // HD, L, HD), out_dtype)]
    else:
        out_specs = [row(K)]
        out_shape = [jax.ShapeDtypeStruct((L, K), out_dtype)]
    res, got = _carried_call(body, carry, name=name, grid=(L // tm,), in_specs=in_specs, out_specs=out_specs,
                             out_shape=out_shape, scratch_shapes=[], compiler_params=_params(1, VMEM_BIG),
                             operands=ops)
    res = res[0] if len(res) == 1 else res
    return res if carry is None else (res, got)


def _wgrad(x, dy, *, mode, colsum=False, cap=BIG_TILE, name):
    L, K = x.shape
    N = dy.shape[1]
    tm = _tile(L, cap)
    nt = L // tm
    oshape = (NSH, K, N // NSH) if mode == 'col' else (NSH, K // NSH, N)

    def body(*refs):
        t = pl.program_id(0)
        x_ref, dy_ref, out_ref = refs[:3]
        acc_ref = refs[-1]
        dyv = dy_ref[...]

        @pl.when(t == 0)
        def _():
            acc_ref[...] = jnp.zeros_like(acc_ref)

        acc_ref[...] += _dot(x_ref[...].astype(BF16), dyv.astype(BF16), _TN)
        if colsum:
            _acc(refs[3], jnp.sum(dyv.astype(F32), axis=0, keepdims=True), t == 0)

        @pl.when(t == nt - 1)
        def _():
            for s in range(NSH):
                if mode == 'col':
                    out_ref[s] = acc_ref[:, s * oshape[2]:(s + 1) * oshape[2]].astype(BF16)
                else:
                    out_ref[s] = acc_ref[s * oshape[1]:(s + 1) * oshape[1], :].astype(BF16)

    out_specs = pl.BlockSpec(oshape, lambda t: (0, 0, 0))
    out_shape = jax.ShapeDtypeStruct(oshape, BF16)
    if colsum:
        out_specs = (out_specs, pl.BlockSpec((1, N), lambda t: (0, 0)))
        out_shape = (out_shape, jax.ShapeDtypeStruct((1, N), F32))
    return pl.pallas_call(
        body, name=name, grid=(nt,),
        in_specs=[pl.BlockSpec((tm, K), lambda t: (t, 0)), pl.BlockSpec((tm, N), lambda t: (t, 0))],
        out_specs=out_specs, out_shape=out_shape,
        scratch_shapes=[pltpu.VMEM((K, N), F32)], compiler_params=_params(1, VMEM_BIG))(x, dy)


def _ffn_fwd(h, gain, wg, wu, wd, name, carry=None):
    L = h.shape[0]
    tm = _tile(L, BIG_TILE)

    def body(h_ref, g_ref, wg_ref, wu_ref, wd_ref, hn_ref, G_ref, U_ref, A_ref, B_ref, acc_ref, b_scr):
        j = pl.program_id(1)

        @pl.when(j == 0)
        def _():
            x = h_ref[...]
            b = (x * _rstd(x) * g_ref[...]).astype(BF16)
            b_scr[...] = b
            B_ref[...] = b
            acc_ref[...] = x

        b = b_scr[...]
        g = _dot(b, wg_ref[...])
        u = _dot(b, wu_ref[...])
        s = _sigmoid(g)
        silu = g * s
        G_ref[...] = (u * (s * (1.0 + g * (1.0 - s)))).astype(BF16)
        U_ref[...] = silu.astype(BF16)
        a = (silu * u).astype(BF16)
        A_ref[...] = a
        acc_ref[...] += _dot(a, wd_ref[...])

        @pl.when(j == NSH - 1)
        def _():
            hn_ref[...] = acc_ref[...]

    row = pl.BlockSpec((tm, D), lambda t, j: (t, 0))
    chunk = pl.BlockSpec((None, tm, FC), lambda t, j: (j, t, 0))
    cshape = jax.ShapeDtypeStruct((NSH, L, FC), BF16)
    return _carried_call(
        body, carry, name=name, grid=(L // tm, NSH),
        in_specs=[row, pl.BlockSpec((1, D), lambda t, j: (0, 0)),
                  pl.BlockSpec((None, D, FC), lambda t, j: (j, 0, 0)),
                  pl.BlockSpec((None, D, FC), lambda t, j: (j, 0, 0)),
                  pl.BlockSpec((None, FC, D), lambda t, j: (j, 0, 0))],
        out_specs=[row, chunk, chunk, chunk, row],
        out_shape=[jax.ShapeDtypeStruct((L, D), F32), cshape, cshape, cshape, jax.ShapeDtypeStruct((L, D), BF16)],
        scratch_shapes=[pltpu.VMEM((tm, D), F32), pltpu.VMEM((tm, D), BF16)],
        compiler_params=_params(2, VMEM_BIG), operands=[h, gain, wg, wu, wd])


def _ffn_bwd_act(dhn, h, gain, G, U, wg, wu, wd, name, carry=None):
    L = h.shape[0]
    tm = _tile(L, BIG_TILE)
    step = tm // ACT_CHUNKS if tm % (16 * ACT_CHUNKS) == 0 else tm

    def body(dhn_ref, h_ref, g_ref, G_ref, U_ref, wg_ref, wu_ref, wd_ref, DG_ref, DU_ref, dh_ref, dgain_ref, db_ref):
        t, j = pl.program_id(0), pl.program_id(1)

        @pl.when(j == 0)
        def _():
            db_ref[...] = jnp.zeros_like(db_ref)

        for r0 in range(0, tm, step):
            rows = pl.ds(r0, step)
            d_act = _dot(dhn_ref[rows, :].astype(BF16), wd_ref[...], _NT)
            dg = (d_act * G_ref[rows, :].astype(F32)).astype(BF16)
            du = (d_act * U_ref[rows, :].astype(F32)).astype(BF16)
            DG_ref[rows, :] = dg
            DU_ref[rows, :] = du
            db_ref[rows, :] += _dot(dg, wg_ref[...], _NT) + _dot(du, wu_ref[...], _NT)

        @pl.when(j == NSH - 1)
        def _():
            x = h_ref[...]
            dx, dgn = _rms_bwd(db_ref[...], x, g_ref[...], _rstd(x))
            dh_ref[...] = dhn_ref[...] + dx
            _acc(dgain_ref, dgn, t == 0)

    row = pl.BlockSpec((tm, D), lambda t, j: (t, 0))
    one = pl.BlockSpec((1, D), lambda t, j: (0, 0))
    chunk = pl.BlockSpec((None, tm, FC), lambda t, j: (j, t, 0))
    cshape = jax.ShapeDtypeStruct((NSH, L, FC), BF16)
    return _carried_call(
        body, carry, name=name, grid=(L // tm, NSH),
        in_specs=[row, row, one, chunk, chunk,
                  pl.BlockSpec((None, D, FC), lambda t, j: (j, 0, 0)),
                  pl.BlockSpec((None, D, FC), lambda t, j: (j, 0, 0)),
                  pl.BlockSpec((None, FC, D), lambda t, j: (j, 0, 0))],
        out_specs=[chunk, chunk, row, one],
        out_shape=[cshape, cshape, jax.ShapeDtypeStruct((L, D), F32), jax.ShapeDtypeStruct((1, D), F32)],
        scratch_shapes=[pltpu.VMEM((tm, D), F32)],
        compiler_params=_params(2, VMEM_BIG), operands=[dhn, h, gain, G, U, wg, wu, wd])


def _ffn_bwd_w(B, A, DG, DU, dhn, name, carry=None):
    L = B.shape[0]
    tm = _tile(L, BIG_TILE)
    nt = L // tm

    def body(B_ref, A_ref, DG_ref, DU_ref, dhn_ref, dwg_ref, dwu_ref, dwd_ref, ag, au, ad):
        t = pl.program_id(1)

        @pl.when(t == 0)
        def _():
            ag[...] = jnp.zeros_like(ag)
            au[...] = jnp.zeros_like(au)
            ad[...] = jnp.zeros_like(ad)

        b = B_ref[...]
        ag[...] += _dot(b, DG_ref[...], _TN)
        au[...] += _dot(b, DU_ref[...], _TN)
        ad[...] += _dot(A_ref[...], dhn_ref[...].astype(BF16), _TN)

        @pl.when(t == nt - 1)
        def _():
            dwg_ref[...] = ag[...].astype(BF16)
            dwu_ref[...] = au[...].astype(BF16)
            dwd_ref[...] = ad[...].astype(BF16)

    row = pl.BlockSpec((tm, D), lambda j, t: (t, 0))
    chunk = pl.BlockSpec((None, tm, FC), lambda j, t: (j, t, 0))
    return _carried_call(
        body, carry, name=name, grid=(NSH, nt), in_specs=[row, chunk, chunk, chunk, row],
        out_specs=[pl.BlockSpec((None, D, FC), lambda j, t: (j, 0, 0)),
                   pl.BlockSpec((None, D, FC), lambda j, t: (j, 0, 0)),
                   pl.BlockSpec((None, FC, D), lambda j, t: (j, 0, 0))],
        out_shape=[jax.ShapeDtypeStruct((NSH, D, FC), BF16), jax.ShapeDtypeStruct((NSH, D, FC), BF16),
                   jax.ShapeDtypeStruct((NSH, FC, D), BF16)],
        scratch_shapes=[pltpu.VMEM((D, FC), F32), pltpu.VMEM((D, FC), F32), pltpu.VMEM((FC, D), F32)],
        compiler_params=_params(2, VMEM_BIG), operands=[B, A, DG, DU, dhn])


def _loss_head(h, gain, target):
    L = h.shape[0]
    tm = _tile(L)

    def body(h_ref, g_ref, tgt_ref, dh_ref, dg_ref, loss_ref):
        t = pl.program_id(0)
        x = h_ref[...]
        g = g_ref[...]
        r = _rstd(x)
        rows = t * tm + lax.broadcasted_iota(jnp.int32, (tm, 1), 0)
        diff = jnp.where(rows >= N_META, x * r * g - tgt_ref[...], 0.0)
        part = 0.5 * jnp.sum(jnp.sum(diff * diff, axis=-1, keepdims=True) * (1.0 / D), axis=0, keepdims=True)
        dx, dg = _rms_bwd(diff * (1.0 / D), x, g, r)
        dh_ref[...] = dx
        _acc(dg_ref, dg, t == 0)
        _acc(loss_ref, jnp.broadcast_to(part, (1, 128)), t == 0)

    row = pl.BlockSpec((tm, D), lambda t: (t, 0))
    one = pl.BlockSpec((1, D), lambda t: (0, 0))
    return pl.pallas_call(
        body, name="loss_head", grid=(L // tm,), in_specs=[row, one, row],
        out_specs=(row, one, pl.BlockSpec((1, 128), lambda t: (0, 0))),
        out_shape=(jax.ShapeDtypeStruct((L, D), F32), jax.ShapeDtypeStruct((1, D), F32),
                   jax.ShapeDtypeStruct((1, 128), F32)),
        compiler_params=_params(1))(h, gain, target)


def _conv_fwd(z3, cw, name):
    L = z3.shape[0]
    tm = _tile(L)

    def body(b_ref, c_ref, u_ref, ch_ref, uh_ref, w_ref, y_ref, buf):
        t = pl.program_id(0)
        z = c_ref[...].astype(F32) * u_ref[...].astype(F32)
        buf[pl.ds(0, HALO), :] = jnp.where(t > 0, ch_ref[...].astype(F32) * uh_ref[...].astype(F32), 0.0)
        buf[pl.ds(HALO, tm), :] = z
        w0, w1, w2 = w_ref[0:1, :], w_ref[1:2, :], w_ref[2:3, :]
        conv = w2 * z + w1 * buf[pl.ds(HALO - 1, tm), :] + w0 * buf[pl.ds(HALO - 2, tm), :]
        y_ref[...] = (b_ref[...].astype(F32) * conv).astype(BF16)

    def col(k):
        return pl.BlockSpec((tm, D), lambda t: (t, k))

    def hcol(k):
        return pl.BlockSpec((HALO, D), lambda t: (jnp.maximum(t * (tm // HALO) - 1, 0), k))

    return pl.pallas_call(
        body, name=name, grid=(L // tm,),
        in_specs=[col(0), col(1), col(2), hcol(1), hcol(2), pl.BlockSpec((3, D), lambda t: (0, 0))],
        out_specs=pl.BlockSpec((tm, D), lambda t: (t, 0)),
        out_shape=jax.ShapeDtypeStruct((L, D), BF16),
        scratch_shapes=[pltpu.VMEM((HALO + tm, D), F32)], compiler_params=_params(1))(z3, z3, z3, z3, z3, cw)


def _conv_bwd(z3, cw, dy, name):
    L = z3.shape[0]
    tm = _tile(L)
    nt = L // tm
    last_h = L // HALO - 1

    def body(b_ref, c_ref, u_ref, ch_ref, uh_ref, w_ref, dy_ref, bn_ref, dyn_ref, dz3_ref, dw_ref, zbuf, dbuf):
        t = pl.program_id(0)
        w0, w1, w2 = w_ref[0:1, :], w_ref[1:2, :], w_ref[2:3, :]
        bgate, cgate, u = b_ref[...].astype(F32), c_ref[...].astype(F32), u_ref[...].astype(F32)
        z = cgate * u
        zbuf[pl.ds(0, HALO), :] = jnp.where(t > 0, ch_ref[...].astype(F32) * uh_ref[...].astype(F32), 0.0)
        zbuf[pl.ds(HALO, tm), :] = z
        z1 = zbuf[pl.ds(HALO - 1, tm), :]
        z2 = zbuf[pl.ds(HALO - 2, tm), :]
        conv = w2 * z + w1 * z1 + w0 * z2
        dyv = dy_ref[...]
        dconv = dyv * bgate
        dbuf[pl.ds(0, tm), :] = dconv
        dbuf[pl.ds(tm, HALO), :] = jnp.where(t < nt - 1, dyn_ref[...] * bn_ref[...].astype(F32), 0.0)
        dz = w2 * dconv + w1 * dbuf[pl.ds(1, tm), :] + w0 * dbuf[pl.ds(2, tm), :]
        dz3_ref[:, 0:D] = (dyv * conv).astype(BF16)
        dz3_ref[:, D:2 * D] = (dz * u).astype(BF16)
        dz3_ref[:, 2 * D:3 * D] = (dz * cgate).astype(BF16)
        @pl.when(t == 0)
        def _():
            dw_ref[...] = jnp.zeros_like(dw_ref)

        for k, zk in enumerate((z2, z1, z)):
            dw_ref[k:k + 1, :] += jnp.sum(dconv * zk, axis=0, keepdims=True)

    def col(k):
        return pl.BlockSpec((tm, D), lambda t: (t, k))

    def hprev(k):
        return pl.BlockSpec((HALO, D), lambda t: (jnp.maximum(t * (tm // HALO) - 1, 0), k))

    def hnext(k):
        return pl.BlockSpec((HALO, D), lambda t: (jnp.minimum((t + 1) * (tm // HALO), last_h), k))

    return pl.pallas_call(
        body, name=name, grid=(nt,),
        in_specs=[col(0), col(1), col(2), hprev(1), hprev(2), pl.BlockSpec((3, D), lambda t: (0, 0)),
                  col(0), hnext(0), hnext(0)],
        out_specs=(pl.BlockSpec((tm, 3 * D), lambda t: (t, 0)), pl.BlockSpec((3, D), lambda t: (0, 0))),
        out_shape=(jax.ShapeDtypeStruct((L, 3 * D), BF16), jax.ShapeDtypeStruct((3, D), F32)),
        scratch_shapes=[pltpu.VMEM((HALO + tm, D), F32), pltpu.VMEM((tm + HALO, D), F32)],
        compiler_params=_params(1))(z3, z3, z3, z3, z3, cw, dy, z3, dy)


def _count_inv(pos, w):
    return 1.0 / jnp.minimum(pos + 1, w).astype(F32)


def _pool_fwd(h, gain, name):
    L = h.shape[0]
    tm = _tile(L)

    def body(h_ref, hh_ref, g_ref, mix_ref, buf):
        t = pl.program_id(0)
        g = g_ref[...]
        x = h_ref[...]
        a = x * _rstd(x) * g
        xh = hh_ref[...]
        buf[pl.ds(0, HALO), :] = jnp.where(t > 0, xh * _rstd(xh) * g, 0.0)
        buf[pl.ds(HALO, tm), :] = a
        pos = t * tm + lax.broadcasted_iota(jnp.int32, (tm, 1), 0)
        for gi, w in enumerate(POOL_WINDOWS):
            cols = pl.ds(gi * PG, PG)
            s = buf[pl.ds(HALO, tm), cols]
            for k in range(1, w):
                s = s + buf[pl.ds(HALO - k, tm), cols]
            mix_ref[:, gi * PG:(gi + 1) * PG] = (s / jnp.minimum(pos + 1, w).astype(F32)
                                                  - buf[pl.ds(HALO, tm), cols]).astype(BF16)

    return pl.pallas_call(
        body, name=name, grid=(L // tm,),
        in_specs=[pl.BlockSpec((tm, D), lambda t: (t, 0)),
                  pl.BlockSpec((HALO, D), lambda t: (jnp.maximum(t * (tm // HALO) - 1, 0), 0)),
                  pl.BlockSpec((1, D), lambda t: (0, 0))],
        out_specs=pl.BlockSpec((tm, D), lambda t: (t, 0)),
        out_shape=jax.ShapeDtypeStruct((L, D), BF16),
        scratch_shapes=[pltpu.VMEM((HALO + tm, D), F32)], compiler_params=_params(1))(h, h, gain)


def _pool_out(mix, wp, scale, h, name):
    L = mix.shape[0]
    tm = _tile(L)

    def body(mix_ref, wp_ref, s_ref, h_ref, out_ref):
        for gi in range(4):
            sl = slice(gi * PG, (gi + 1) * PG)
            pre = _dot(mix_ref[:, sl], wp_ref[gi])
            out_ref[:, sl] = h_ref[:, sl] + pre * s_ref[:, sl]

    row = pl.BlockSpec((tm, D), lambda t: (t, 0))
    return pl.pallas_call(
        body, name=name, grid=(L // tm,),
        in_specs=[row, pl.BlockSpec((4, PG, PG), lambda t: (0, 0, 0)), pl.BlockSpec((1, D), lambda t: (0, 0)), row],
        out_specs=row, out_shape=jax.ShapeDtypeStruct((L, D), F32), compiler_params=_params(1))(mix, wp, scale, h)


def _pool_out_bwd(dm, mix, wp, scale, name):
    L = mix.shape[0]
    tm = _tile(L)

    def body(dm_ref, mix_ref, wp_ref, s_ref, dmix_ref, dwp_ref, ds_ref):
        @pl.when(pl.program_id(0) == 0)
        def _():
            ds_ref[...] = jnp.zeros_like(ds_ref)
            dwp_ref[...] = jnp.zeros_like(dwp_ref)

        for gi in range(4):
            sl = slice(gi * PG, (gi + 1) * PG)
            mx = mix_ref[:, sl]
            dmv = dm_ref[:, sl]
            pre = _dot(mx, wp_ref[gi])
            ds_ref[:, sl] += jnp.sum(dmv * pre, axis=0, keepdims=True)
            dpre = (dmv * s_ref[:, sl]).astype(BF16)
            dwp_ref[gi] += _dot(mx, dpre, _TN)
            dmix_ref[:, sl] = _dot(dpre, wp_ref[gi], _NT)

    row = pl.BlockSpec((tm, D), lambda t: (t, 0))
    one = pl.BlockSpec((1, D), lambda t: (0, 0))
    wsp = pl.BlockSpec((4, PG, PG), lambda t: (0, 0, 0))
    return pl.pallas_call(
        body, name=name, grid=(L // tm,), in_specs=[row, row, wsp, one],
        out_specs=(row, wsp, one),
        out_shape=(jax.ShapeDtypeStruct((L, D), F32), jax.ShapeDtypeStruct((4, PG, PG), F32),
                   jax.ShapeDtypeStruct((1, D), F32)),
        compiler_params=_params(1))(dm, mix, wp, scale)


def _pool_bwd(dmix, h, gain, resid, name):
    L = h.shape[0]
    tm = _tile(L)
    nt = L // tm
    last_h = L // HALO - 1

    def body(dm_ref, dmn_ref, h_ref, g_ref, r_ref, dh_ref, dg_ref, buf):
        t = pl.program_id(0)
        pos = t * tm + lax.broadcasted_iota(jnp.int32, (tm, 1), 0)
        posn = (t + 1) * tm + lax.broadcasted_iota(jnp.int32, (HALO, 1), 0)
        dmv = dm_ref[...]
        dmn = dmn_ref[...]
        for gi, w in enumerate(POOL_WINDOWS):
            sl = slice(gi * PG, (gi + 1) * PG)
            buf[pl.ds(0, tm), sl] = dmv[:, sl] * _count_inv(pos, w)
            buf[pl.ds(tm, HALO), sl] = jnp.where(t < nt - 1, dmn[:, sl] * _count_inv(posn, w), 0.0)
        parts = []
        for gi, w in enumerate(POOL_WINDOWS):
            cols = pl.ds(gi * PG, PG)
            s = buf[pl.ds(0, tm), cols]
            for k in range(1, w):
                s = s + buf[pl.ds(k, tm), cols]
            parts.append(s)
        da = jnp.concatenate(parts, axis=1) - dmv
        x = h_ref[...]
        dx, dg = _rms_bwd(da, x, g_ref[...], _rstd(x))
        dh_ref[...] = r_ref[...] + dx
        _acc(dg_ref, dg, t == 0)

    row = pl.BlockSpec((tm, D), lambda t: (t, 0))
    one = pl.BlockSpec((1, D), lambda t: (0, 0))
    return pl.pallas_call(
        body, name=name, grid=(nt,),
        in_specs=[row, pl.BlockSpec((HALO, D), lambda t: (jnp.minimum((t + 1) * (tm // HALO), last_h), 0)),
                  row, one, row],
        out_specs=(row, one),
        out_shape=(jax.ShapeDtypeStruct((L, D), F32), jax.ShapeDtypeStruct((1, D), F32)),
        scratch_shapes=[pltpu.VMEM((tm + HALO, D), F32)], compiler_params=_params(1))(dmix, dmix, h, gain, resid)


def _bucket_np(d):
    d = np.maximum(d, 0)
    df = np.maximum(d, 1).astype(np.float32)
    large = 16 + (np.log(df / np.float32(16)) / np.float32(math.log(128 / 16)) * np.float32(16)).astype(np.int32)
    return np.where(d < 16, d, np.minimum(large, N_BUCKETS - 1))


def _bias_index():
    iq = np.arange(BLK)[:, None]
    jk = np.arange(2 * BLK)[None, :]
    dist = BLK + iq - jk
    band = _bucket_np(dist)
    ok = (dist >= 0) & (dist < BLK)
    band1 = np.where(ok, band, N_BUCKETS)
    band0 = np.where(ok & (jk >= BLK), band, N_BUCKETS)
    im = np.arange(N_META)[None, :]
    unused = np.full((BLK, BLK - N_META), N_BUCKETS)
    var0 = np.concatenate([_bucket_np(N_META + iq - im), unused, band0], axis=1)
    var1 = np.concatenate([_bucket_np(N_META + BLK + iq - im), unused, band1], axis=1)
    dm = np.arange(N_META)[:, None] - im
    mm = np.where(dm >= 0, _bucket_np(dm), N_BUCKETS)
    segs = [var0, var1, mm]
    return np.concatenate([s.reshape(-1) for s in segs]).astype(np.int32), [s.shape for s in segs]


P_CHUNK = 9856


def _onehot(idx_ref, grad):
    rows = lax.broadcasted_iota(jnp.int32, (128, P_CHUNK), 0)
    hit = (rows == idx_ref[...]).astype(F32)
    return jnp.where(rows == N_BUCKETS, -1.0, hit) if grad else hit


def _bias_lookup(table_aug, idx, name, carry=None):
    P = idx.shape[1]

    def body(t_ref, idx_ref, o_ref):
        o_ref[...] = lax.dot_general(t_ref[...], _onehot(idx_ref, False), _NN, precision=lax.Precision.HIGHEST,
                                     preferred_element_type=F32)

    return _carried_call(
        body, carry, name=name, grid=(P // P_CHUNK,),
        in_specs=[pl.BlockSpec((H, 128), lambda i: (0, 0)), pl.BlockSpec((1, P_CHUNK), lambda i: (0, i))],
        out_specs=[pl.BlockSpec((H, P_CHUNK), lambda i: (0, i))],
        out_shape=[jax.ShapeDtypeStruct((H, P), F32)], scratch_shapes=[], compiler_params=_params(1),
        operands=[table_aug, idx])


def _bias_lookup_bwd(dbias, idx, name):
    rows, P = dbias.shape

    def body(d_ref, idx_ref, o_ref):
        part = lax.dot_general(d_ref[...], _onehot(idx_ref, True), _NT, precision=lax.Precision.HIGHEST,
                               preferred_element_type=F32)
        _acc(o_ref, part, pl.program_id(0) == 0)

    return pl.pallas_call(
        body, name=name, grid=(P // P_CHUNK,),
        in_specs=[pl.BlockSpec((rows, P_CHUNK), lambda i: (0, i)), pl.BlockSpec((1, P_CHUNK), lambda i: (0, i))],
        out_specs=pl.BlockSpec((rows, 128), lambda i: (0, 0)),
        out_shape=jax.ShapeDtypeStruct((rows, 128), F32), compiler_params=_params(1))(dbias, idx)


def _probs(q, kbt, bias, sink):
    s = _dot(q, kbt) + bias
    m = jnp.maximum(jnp.max(s, axis=-1, keepdims=True), sink)
    e = jnp.exp(s - m)
    return e * (1.0 / (jnp.sum(e, axis=-1, keepdims=True) + jnp.exp(sink - m)))


def _attn_specs(nb):
    def cur(kh, n):
        return jnp.minimum(n, nb - 1)

    def prev(kh, n):
        return jnp.maximum(jnp.minimum(n, nb - 1) - 1, 0)

    def rows(heads, blk):
        return pl.BlockSpec((pl.Element(heads), pl.Element(BLK), pl.Element(HD)),
                            lambda kh, n: (kh * heads, pl.multiple_of(N_META + blk(kh, n) * BLK, N_META), 0))

    return dict(
        q=rows(GRP, cur),
        qo=pl.BlockSpec((GRP, BLK, HD), lambda kh, n: (kh, cur(kh, n), 0)),
        qt=pl.BlockSpec((GRP, HD, BLK), lambda kh, n: (kh, 0, cur(kh, n))),
        cur=rows(1, cur),
        prev=rows(1, prev),
        meta=pl.BlockSpec((None, BLK, HD), lambda kh, n: (kh, 0, 0)),
        curt=pl.BlockSpec((None, HD, BLK), lambda kh, n: (kh, 0, cur(kh, n))),
        prevt=pl.BlockSpec((None, HD, BLK), lambda kh, n: (kh, 0, prev(kh, n))),
        metat=pl.BlockSpec((None, HD, BLK), lambda kh, n: (kh, 0, 0)),
        bias=pl.BlockSpec((None, GRP, BLK, NKEY), lambda kh, n: (jnp.minimum(n, 1), kh, 0, 0)),
        sink=pl.BlockSpec((None, GRP * BLK, 1), lambda kh, n: (kh, 0, 0)))


def _attn_fwd(q, k, km, v, vm, band, bmeta, sink, name, carry=None):
    S = q.shape[1] - N_META
    nb = S // BLK
    sp = _attn_specs(nb)
    var = lambda kh, n: (jnp.minimum(n, 1), kh, 0, 0)
    meta = pl.BlockSpec((None, N_META, HD), lambda kh, n: (kh, 0, 0))

    def body(q_ref, kc_ref, kp_ref, vc_ref, vp_ref, km_ref, vm_ref, band_ref, bm_ref, sink_ref, o_ref):
        qv = q_ref[...].reshape(GRP * BLK, HD)
        kb = jnp.concatenate([kp_ref[0], kc_ref[0]], axis=0)
        vb = jnp.concatenate([vp_ref[0], vc_ref[0]], axis=0)
        sink = sink_ref[...]
        s_b = _dot(qv, kb, _NT) + band_ref[...].reshape(GRP * BLK, 2 * BLK)
        s_m = _dot(qv, km_ref[...], _NT) + bm_ref[...].reshape(GRP * BLK, N_META)
        m = jnp.maximum(jnp.maximum(jnp.max(s_b, axis=-1, keepdims=True), jnp.max(s_m, axis=-1, keepdims=True)), sink)
        e_b = jnp.exp(s_b - m)
        e_m = jnp.exp(s_m - m)
        inv = 1.0 / (jnp.sum(e_b, axis=-1, keepdims=True) + jnp.sum(e_m, axis=-1, keepdims=True) + jnp.exp(sink - m))
        o = _dot((e_b * inv).astype(BF16), vb) + _dot((e_m * inv).astype(BF16), vm_ref[...])
        o_ref[...] = o.reshape(GRP, BLK, HD).astype(BF16)

    return _carried_call(
        body, carry, name=name, grid=(KV, nb),
        in_specs=[sp['q'], sp['cur'], sp['prev'], sp['cur'], sp['prev'], meta, meta,
                  pl.BlockSpec((None, GRP, BLK, 2 * BLK), var), pl.BlockSpec((None, GRP, BLK, N_META), var),
                  sp['sink']],
        out_specs=[sp['qo']], out_shape=[jax.ShapeDtypeStruct((H, S, HD), BF16)], scratch_shapes=[],
        compiler_params=_params(2), operands=[q, k, k, v, v, km, vm, band, bmeta, sink])


def _attn_bwd(q, qt, k, km, kt, kmt, vt, vmt, bias, sink, do, dot_, name, carry=None):
    S = q.shape[1] - N_META
    nb = S // BLK
    sp = _attn_specs(nb)

    def body(q_ref, qt_ref, kc_ref, kp_ref, km_ref, kct_ref, kpt_ref, kmt_ref, vct_ref, vpt_ref, vmt_ref,
             bias_ref, sink_ref, do_ref, dot_ref, dq_ref, dkt_ref, dvt_ref, dkmt_ref, dvmt_ref, dbias_ref, ck, cv):
        n = pl.program_id(1)

        @pl.when(n <= 1)
        def _():
            dbias_ref[...] = jnp.zeros_like(dbias_ref)

        @pl.when(n == 0)
        def _():
            dkmt_ref[...] = jnp.zeros_like(dkmt_ref)
            dvmt_ref[...] = jnp.zeros_like(dvmt_ref)

        @pl.when(n < nb)
        def _():
            kbt = jnp.concatenate([kmt_ref[...], kpt_ref[...], kct_ref[...]], axis=1)
            vbt = jnp.concatenate([vmt_ref[...], vpt_ref[...], vct_ref[...]], axis=1)
            kb = jnp.concatenate([km_ref[...], kp_ref[0], kc_ref[0]], axis=0)
            p = _probs(q_ref[...].reshape(GRP * BLK, HD), kbt, bias_ref[...].reshape(GRP * BLK, NKEY), sink_ref[...])
            dp = _dot(do_ref[...].reshape(GRP * BLK, HD), vbt)
            ds = p * (dp - jnp.sum(p * dp, axis=-1, keepdims=True))
            dbias_ref[...] += ds.reshape(GRP, BLK, NKEY)
            ds16 = ds.astype(BF16)
            dq_ref[...] = _dot(ds16, kb).reshape(GRP, BLK, HD).astype(BF16)
            qtv = jnp.concatenate([qt_ref[g] for g in range(GRP)], axis=1)
            dotv = jnp.concatenate([dot_ref[g] for g in range(GRP)], axis=1)
            dkt = _dot(qtv, ds16)
            dvt = _dot(dotv, p.astype(BF16))
            dkmt_ref[...] += dkt[:, 0:BLK]
            dvmt_ref[...] += dvt[:, 0:BLK]

            @pl.when(n >= 1)
            def _():
                dkt_ref[...] = (ck[...] + dkt[:, BLK:2 * BLK]).astype(BF16)
                dvt_ref[...] = (cv[...] + dvt[:, BLK:2 * BLK]).astype(BF16)

            ck[...] = dkt[:, 2 * BLK:3 * BLK]
            cv[...] = dvt[:, 2 * BLK:3 * BLK]

        @pl.when(n == nb)
        def _():
            dkt_ref[...] = ck[...].astype(BF16)
            dvt_ref[...] = cv[...].astype(BF16)

    kvout = pl.BlockSpec((None, HD, BLK), lambda kh, n: (kh, 0, jnp.maximum(n - 1, 0)))
    return _carried_call(
        body, carry, name=name, grid=(KV, nb + 1),
        in_specs=[sp['q'], sp['qt'], sp['cur'], sp['prev'], sp['meta'], sp['curt'], sp['prevt'], sp['metat'],
                  sp['curt'], sp['prevt'], sp['metat'], sp['bias'], sp['sink'], sp['q'], sp['qt']],
        out_specs=[sp['qo'], kvout, kvout, sp['metat'], sp['metat'], sp['bias']],
        out_shape=[jax.ShapeDtypeStruct((H, S, HD), BF16), jax.ShapeDtypeStruct((KV, HD, S), BF16),
                   jax.ShapeDtypeStruct((KV, HD, S), BF16), jax.ShapeDtypeStruct((KV, HD, BLK), F32),
                   jax.ShapeDtypeStruct((KV, HD, BLK), F32), jax.ShapeDtypeStruct((2, H, BLK, NKEY), F32)],
        scratch_shapes=[pltpu.VMEM((HD, BLK), F32), pltpu.VMEM((HD, BLK), F32)],
        compiler_params=_params(2), operands=[q, qt, k, k, km, kt, kt, kmt, vt, vt, vmt, bias, sink, do, dot_])


def _meta_softmax(q, k, bias, sink):
    s = _dot(q, k, _NT) + bias
    m = jnp.maximum(jnp.max(s, axis=-1, keepdims=True), sink)
    e = jnp.exp(s - m)
    return e * (1.0 / (jnp.sum(e, axis=-1, keepdims=True) + jnp.exp(sink - m)))


def _attn_meta_fwd(qm, km, vm, bias, sink, name):
    def body(q_ref, k_ref, v_ref, b_ref, s_ref, o_ref):
        for h in range(H):
            p = _meta_softmax(q_ref[h], k_ref[h // GRP], b_ref[h], s_ref[h])
            o_ref[h] = _dot(p.astype(BF16), v_ref[h // GRP]).astype(BF16)

    return pl.pallas_call(body, name=name, out_shape=jax.ShapeDtypeStruct((H, N_META, HD), BF16))(
        qm, km, vm, bias, sink)


def _attn_meta_bwd(qm, km, vm, bias, sink, do, name):
    def body(q_ref, k_ref, v_ref, b_ref, s_ref, do_ref, dq_ref, dk_ref, dv_ref, db_ref):
        for kh in range(KV):
            k, v = k_ref[kh], v_ref[kh]
            dk = jnp.zeros((N_META, HD), F32)
            dv = jnp.zeros((N_META, HD), F32)
            for g in range(GRP):
                h = kh * GRP + g
                q, dov = q_ref[h], do_ref[h]
                p = _meta_softmax(q, k, b_ref[h], s_ref[h])
                dp = _dot(dov, v, _NT)
                ds = p * (dp - jnp.sum(p * dp, axis=-1, keepdims=True))
                db_ref[h] = ds
                ds16 = ds.astype(BF16)
                dq_ref[h] = _dot(ds16, k).astype(BF16)
                dk = dk + _dot(ds16, q, _TN)
                dv = dv + _dot(p.astype(BF16), dov, _TN)
            dk_ref[kh] = dk
            dv_ref[kh] = dv

    return pl.pallas_call(
        body, name=name,
        out_shape=(jax.ShapeDtypeStruct((H, N_META, HD), BF16), jax.ShapeDtypeStruct((KV, N_META, HD), F32),
                   jax.ShapeDtypeStruct((KV, N_META, HD), F32), jax.ShapeDtypeStruct((H, N_META, N_META), F32)))(
        qm, km, vm, bias, sink, do)


def _unheads(t):
    return jnp.transpose(t, (1, 0, 2)).reshape(t.shape[1], t.shape[0] * HD)


def _unheads_t(t):
    return jnp.transpose(t, (2, 0, 1)).reshape(t.shape[2], t.shape[0] * HD)


def _pad_block(t, axis):
    pad = [(0, 0)] * t.ndim
    pad[axis] = (0, BLK - N_META)
    return jnp.pad(t, pad)


def _adam(w, g, m, v, name):
    lead, R, C = w.shape
    tr = _tile(R, 512) if R % 16 == 0 else R

    def body(w_ref, g_ref, m_ref, v_ref, d_ref, mo_ref, vo_ref, go_ref):
        gv = g_ref[...]
        go_ref[...] = gv
        mn = B1 * m_ref[...] + (1.0 - B1) * gv
        vn = B2 * v_ref[...] + (1.0 - B2) * (gv * gv)
        m_hat = mn / (1.0 - B1 ** STEP)
        v_hat = vn / (1.0 - B2 ** STEP)
        d_ref[...] = -LR * (m_hat / (jnp.sqrt(v_hat) + ADAM_EPS) + WD * w_ref[...])
        mo_ref[...] = mn
        vo_ref[...] = vn

    blk = pl.BlockSpec((None, tr, C), lambda l, i: (l, i, 0))
    shp = jax.ShapeDtypeStruct((lead, R, C), F32)
    return pl.pallas_call(body, name=name, grid=(lead, R // tr), in_specs=[blk] * 4, out_specs=(blk,) * 4,
                          out_shape=(shp,) * 4, compiler_params=_params(2))(w, g, m, v)


def _sum_leaf(lands, pends, ids, name):
    n = len(lands)
    _, R, C = lands[0].shape
    tr = _tile(R, 256)

    def body(ids_ref, *refs):
        out_ref = refs[2 * n]
        for k in range(n):
            acc = refs[n + k][...].astype(F32)
            for s in range(7):
                acc = acc + refs[k][s].astype(F32)
            out_ref[k] = acc

    grid_spec = pltpu.PrefetchScalarGridSpec(
        num_scalar_prefetch=1, grid=(R // tr,),
        in_specs=[pl.BlockSpec((7, tr, C), lambda i, ids: (0, i, 0))] * n
        + [pl.BlockSpec((None, None, tr, C), lambda i, ids: (ids[0], ids[1], i, 0))] * n,
        out_specs=pl.BlockSpec((n, None, tr, C), lambda i, ids: (0, ids[1], i, 0)))
    return pl.pallas_call(body, name=name, grid_spec=grid_spec, out_shape=jax.ShapeDtypeStruct((n, 2, R, C), F32),
                          compiler_params=_params(1))(ids, *lands, *pends)


def _place():
    x, y, c = lax.axis_index("x"), lax.axis_index("y"), lax.axis_index("c")
    chips = [(1 - x, y), (x, 1 - y), (1 - x, 1 - y)]
    return x, y, c, chips


def _rcopy(src, dst, ssem, rsem, dev):
    return pltpu.make_async_remote_copy(src_ref=src, dst_ref=dst, send_sem=ssem, recv_sem=rsem,
                                        device_id=dev, device_id_type=MESH)


class _Carry:
    def __init__(self, kind, arrays):
        self.kind, self.arrays, self.n = kind, list(arrays), len(arrays)
        self.per = 3 if kind == 'gather' else 7
        if kind == 'gather':
            self.out_shape = [jax.ShapeDtypeStruct((NSH,) + a.shape, a.dtype) for a in self.arrays]
        else:
            self.out_shape = [jax.ShapeDtypeStruct((7,) + a.shape[2:], a.dtype) for a in self.arrays]
        dma = pltpu.SemaphoreType.DMA
        self.scratch = [dma((self.per * self.n,)), dma((self.per * self.n,)), dma((self.n,))]
        if kind == 'gather':
            self.scratch += [pltpu.VMEM(a.shape, a.dtype) for a in self.arrays]

    def _copies(self, cin, cout, scr):
        ssem, rsem, loc = scr[:3]
        x, y, c, chips = _place()
        local, sends, recvs = [], [], []
        for k in range(self.n):
            if self.kind == 'gather':
                me = 2 * x + y
                local.append((pltpu.make_async_copy(cin[k], scr[3 + k], loc.at[k]),
                              pltpu.make_async_copy(scr[3 + k], cout[k].at[me], loc.at[k])))
                for j, (cx, cy) in enumerate(chips):
                    i = 3 * k + j
                    sends.append(_rcopy(cin[k], cout[k].at[me], ssem.at[i], rsem.at[i], (cx, cy, c)))
                    got = cout[k].at[2 * cx + cy]
                    recvs.append(_rcopy(got, got, ssem.at[i], rsem.at[i], (cx, cy, c)))
            else:
                for f in range(1, 8):
                    px = 1 - x if (f >> 2) & 1 else x
                    py = 1 - y if (f >> 1) & 1 else y
                    pc = 1 - c if f & 1 else c
                    i = 7 * k + f - 1
                    got = cout[k].at[f - 1]
                    sends.append(_rcopy(cin[k].at[2 * px + py, pc], got, ssem.at[i], rsem.at[i], (px, py, pc)))
                    recvs.append(_rcopy(got, got, ssem.at[i], rsem.at[i], (px, py, pc)))
        return local, sends, recvs

    def start(self, cin, cout, scr):
        local, sends, _ = self._copies(cin, cout, scr)
        for cp in [to_vmem for to_vmem, _ in local] + sends:
            cp.start()

    def finish(self, cin, cout, scr):
        local, sends, recvs = self._copies(cin, cout, scr)
        for to_vmem, to_slot in local:
            to_vmem.wait()
            to_slot.start()
        for cp in recvs:
            cp.wait_recv()
        for cp in sends:
            cp.wait_send()
        for _, to_slot in local:
            to_slot.wait()


def _carried_call(body, carry, *, name, grid, in_specs, out_specs, out_shape, scratch_shapes, compiler_params,
                  operands):
    n_in, n_out = len(in_specs), len(out_specs)
    if carry is None:
        return pl.pallas_call(body, name=name, grid=grid, in_specs=in_specs, out_specs=out_specs,
                              out_shape=out_shape, scratch_shapes=scratch_shapes,
                              compiler_params=compiler_params)(*operands), []
    m = carry.n

    def full(*refs):
        ins, cin = refs[:n_in], refs[n_in:n_in + m]
        outs, cout = refs[n_in + m:n_in + m + n_out], refs[n_in + m + n_out:n_in + 2 * m + n_out]
        own = len(refs) - len(carry.scratch)
        scr, sems = refs[n_in + 2 * m + n_out:own], refs[own:]
        ids = [pl.program_id(a) for a in range(len(grid))]
        first = functools.reduce(jnp.logical_and, [i == 0 for i in ids])
        last = functools.reduce(jnp.logical_and, [i == g - 1 for i, g in zip(ids, grid)])

        @pl.when(first)
        def _():
            carry.start(cin, cout, sems)

        body(*ins, *outs, *scr)

        @pl.when(last)
        def _():
            carry.finish(cin, cout, sems)

    res = pl.pallas_call(
        full, name=name, grid=grid, in_specs=list(in_specs) + [ANY] * m, out_specs=list(out_specs) + [ANY] * m,
        out_shape=list(out_shape) + carry.out_shape, scratch_shapes=list(scratch_shapes) + carry.scratch,
        compiler_params=compiler_params)(*operands, *carry.arrays)
    return res[:n_out], res[n_out:]


def _flush(carry, name):
    m = carry.n

    def body(*refs):
        cin, cout, sems = refs[:m], refs[m:2 * m], refs[2 * m:]
        carry.start(cin, cout, sems)
        carry.finish(cin, cout, sems)

    return pl.pallas_call(body, name=name, in_specs=[ANY] * m, out_specs=[ANY] * m, out_shape=carry.out_shape,
                          scratch_shapes=carry.scratch)(*carry.arrays)


def _pair_share(leaves):
    n = len(leaves)

    def body(*refs):
        ins, outs = refs[:n], refs[n:2 * n]
        ssem, rsem = refs[2 * n:]
        x, y, c, _ = _place()
        sib = (x, y, 1 - c)
        cps = [_rcopy(ins[k].at[:, c], outs[k].at[:, c], ssem.at[k], rsem.at[k], sib) for k in range(n)]
        for cp in cps:
            cp.start()
        for k in range(n):
            got = outs[k].at[:, 1 - c]
            _rcopy(got, got, ssem.at[k], rsem.at[k], sib).wait_recv()
        for cp in cps:
            cp.wait_send()

    dma = pltpu.SemaphoreType.DMA
    return pl.pallas_call(
        body, name="grad_pair_share", in_specs=[ANY] * n, out_specs=[ANY] * n,
        out_shape=[jax.ShapeDtypeStruct(a.shape, a.dtype) for a in leaves],
        input_output_aliases={k: k for k in range(n)},
        scratch_shapes=[dma((n,)), dma((n,))])(*leaves)


def _allreduce_small(pack):
    R = pack.shape[0]

    def body(in_ref, out_ref, buf, ssem, rsem):
        x, y, c, _ = _place()
        me = 4 * x + 2 * y + c
        buf[me] = in_ref[...]
        peers = []
        for k in range(1, 8):
            fx, fy, fc = (k >> 2) & 1, (k >> 1) & 1, k & 1
            peers.append((1 - x if fx else x, 1 - y if fy else y, 1 - c if fc else c))
        cps = [_rcopy(in_ref, buf.at[me], ssem.at[k], rsem.at[k], p) for k, p in enumerate(peers)]
        for cp in cps:
            cp.start()
        for k, (px, py, pc) in enumerate(peers):
            got = buf.at[4 * px + 2 * py + pc]
            _rcopy(got, got, ssem.at[k], rsem.at[k], (px, py, pc)).wait_recv()
        for cp in cps:
            cp.wait_send()
        acc = buf[0]
        for s in range(1, 8):
            acc = acc + buf[s]
        out_ref[...] = acc

    dma = pltpu.SemaphoreType.DMA
    return pl.pallas_call(
        body, name="allreduce_small", out_shape=jax.ShapeDtypeStruct(pack.shape, F32),
        in_specs=[pl.BlockSpec(memory_space=pltpu.VMEM)], out_specs=pl.BlockSpec(memory_space=pltpu.VMEM),
        scratch_shapes=[pltpu.VMEM((8, R, 128), F32), dma((7,)), dma((7,))])(pack)


def _pack(arrs):
    flat = jnp.concatenate([a.reshape(-1).astype(F32) for a in arrs])
    n = flat.shape[0]
    rows = -(-n // 1024) * 8
    return jnp.pad(flat, (0, rows * 128 - n)).reshape(rows, 128)


def _unpack(pack, shapes):
    flat, out, o = pack.reshape(-1), [], 0
    for s in shapes:
        n = int(np.prod(s))
        out.append(flat[o:o + n].reshape(s))
        o += n
    return out


def _ffn_keys(i):
    return [('ffn_w_gate', i), ('ffn_w_up', i), ('ffn_w_down', i)]


GATHER_PLAN = {'attn_fwd0': _ffn_keys(0) + [('conv_w_in', 0), ('conv_w_out', 0)],
               'ffn_fwd0': _ffn_keys(1),
               'ffn_fwd1': _ffn_keys(2) + [('pool_w', 0)],
               'ffn_fwd2': [('attn_w_qkv', 1), ('attn_w_o', 1)],
               'attn_fwd3': _ffn_keys(3)}
REDUCE_PLAN = {'attn_bwd3': _ffn_keys(3),
               'ffn_bwd_act2': [('attn_w_qkv', 1), ('attn_w_o', 1)],
               'ffn_bwd_act1': _ffn_keys(2) + [('pool_w', 0)],
               'ffn_bwd_act0': _ffn_keys(1),
               'ffn_bwd_w0': [('conv_w_in', 0), ('conv_w_out', 0)],
               'attn_bwd0': _ffn_keys(0) + [('attn_w_o', 0)],
               'qkv_bwd0': [('attn_w_qkv', 0)]}


def kernel(x, meta_tokens, rel_bias_table, norm_mix, norm_ffn, norm_final, attn_w_qkv, attn_b_qkv, attn_w_o, attn_b_o, attn_sinks, conv_w_in, conv_w, conv_w_out, pool_w, pool_scale, ffn_w_gate, ffn_w_up, ffn_w_down, loss_target, m_meta_tokens, m_rel_bias_table, m_norm_mix, m_norm_ffn, m_norm_final, m_attn_w_qkv, m_attn_b_qkv, m_attn_w_o, m_attn_b_o, m_attn_sinks, m_conv_w_in, m_conv_w, m_conv_w_out, m_pool_w, m_pool_scale, m_ffn_w_gate, m_ffn_w_up, m_ffn_w_down, v_meta_tokens, v_rel_bias_table, v_norm_mix, v_norm_ffn, v_norm_final, v_attn_w_qkv, v_attn_b_qkv, v_attn_w_o, v_attn_b_o, v_attn_sinks, v_conv_w_in, v_conv_w, v_conv_w_out, v_pool_w, v_pool_scale, v_ffn_w_gate, v_ffn_w_up, v_ffn_w_down):
    args = locals()
    w = {n: args[n] for n in WEIGHTS}
    mom = {n: args['m_' + n] for n in WEIGHTS}
    var = {n: args['v_' + n] for n in WEIGHTS}
    mx, my = lax.axis_index("x"), lax.axis_index("y")
    chip = 2 * mx + my
    S = x.shape[1]
    scale = jnp.asarray(HD ** -0.5, BF16)

    GW = {}
    pending = {}
    land = {}

    def gather_carry(call):
        keys = GATHER_PLAN.get(call)
        return _Carry('gather', [w[n][l].astype(BF16) for n, l in keys]) if keys else None

    def reduce_carry(call):
        keys = REDUCE_PLAN.get(call)
        return _Carry('reduce', [pending[k] for k in keys]) if keys else None

    def pend(n, l, g):
        pending[(n, l)] = g.reshape((NSH, 2) + HALF_SHAPE[n])

    small_in = jnp.concatenate([
        jnp.pad(w['meta_tokens'], ((0, 0), (0, 128))), w['attn_b_qkv'], jnp.pad(w['attn_b_o'], ((0, 0), (0, 128))),
        jnp.pad(w['conv_w'][0], ((0, 0), (0, 128))), jnp.pad(w['pool_scale'], ((0, 0), (0, 128)))], axis=0)
    gsmall, GW[('attn_w_qkv', 0)] = _flush(
        _Carry('gather', [small_in, w['attn_w_qkv'][0].astype(BF16)]), "gather_first")

    def cols(rows, width):
        return jnp.transpose(rows[:, :, :width], (1, 0, 2)).reshape(rows.shape[1], NSH * width)

    meta_full = cols(gsmall[:, 0:16], 256)
    b_qkv = cols(gsmall[:, 16:18], 384)
    b_o = cols(gsmall[:, 18:20], 256)
    conv_k = cols(gsmall[:, 20:23], 256)
    p_scale = cols(gsmall[:, 23:24], 256)

    idx_np, _ = _bias_index()
    idx = jnp.asarray(idx_np).reshape(1, -1)
    table_aug = jnp.concatenate([rel_bias_table.T, jnp.full((H, 1), NEG, F32),
                                 jnp.zeros((H, 127 - N_BUCKETS), F32)], axis=1)
    (bias_flat,), (GW[('attn_w_o', 0)],) = _bias_lookup(
        table_aug, idx, "bias_lookup", _Carry('gather', [w['attn_w_o'][0].astype(BF16)]))
    nblock = BLK * NKEY
    bias_blk = jnp.transpose(bias_flat[:, :2 * nblock].reshape(H, 2, BLK, NKEY), (1, 0, 2, 3))
    bias_mm = bias_flat[:, 2 * nblock:].reshape(H, N_META, N_META)
    bias_band, bias_meta = bias_blk[..., BLK:], bias_blk[..., :N_META]

    h = jnp.concatenate([meta_full, x[0]], axis=0)
    saved = []
    for i in range(DEPTH):
        kind, j = i % 3, i // 3
        gm = norm_mix[i:i + 1]
        st = dict(h=h)
        if kind == 0:
            w_o = GW[('attn_w_o', j)].reshape(1, D, D)
            q, k, v, a = _linear(h, GW[('attn_w_qkv', j)], gain=gm, bias=b_qkv[j:j + 1], qkv_heads=True,
                                 name=f"qkv{i}")
            tr = lambda t: jnp.swapaxes(t, 1, 2)
            lay = dict(q=q, qt=tr(q[:, N_META:]), qm=q[:, :N_META],
                       k=k, kt=tr(k[:, N_META:]), km=k[:, :N_META],
                       kmp=_pad_block(k[:, :N_META], 1), kmt=_pad_block(tr(k[:, :N_META]), 2),
                       v=v, vt=tr(v[:, N_META:]), vm=v[:, :N_META], vmt=_pad_block(tr(v[:, :N_META]), 2))
            sink_r = jnp.broadcast_to(attn_sinks[j].reshape(KV, GRP, 1, 1), (KV, GRP, BLK, 1)).reshape(KV, GRP * BLK, 1)
            sink_m = jnp.broadcast_to(attn_sinks[j].reshape(H, 1, 1), (H, N_META, 1))
            (o_r,), got = _attn_fwd(lay['q'], lay['k'], lay['km'], lay['v'], lay['vm'], bias_band, bias_meta, sink_r,
                                    f"attn_fwd{i}", gather_carry(f"attn_fwd{i}"))
            GW.update(zip(GATHER_PLAN.get(f"attn_fwd{i}", []), got))
            o_m = _attn_meta_fwd(lay['qm'], lay['km'], lay['vm'], bias_mm, sink_m, f"attn_meta_fwd{i}")
            o = _unheads(jnp.concatenate([o_m, o_r], axis=1))
            h1 = _linear(o, w_o, bias=b_o[j:j + 1], resid=h, out_dtype=F32, name=f"attn_out{i}")
            st.update(a=a, lay=lay, sinks=(sink_r, sink_m), o=o, w_o=w_o)
        elif kind == 1:
            z3, a = _linear(h, GW[('conv_w_in', j)], gain=gm, name=f"conv_in{i}")
            yv = _conv_fwd(z3, conv_k, f"conv_fwd{i}")
            w_cout = GW[('conv_w_out', j)].reshape(1, D, D)
            h1 = _linear(yv, w_cout, resid=h, out_dtype=F32, name=f"conv_out{i}")
            st.update(a=a, z3=z3, y=yv, w_cout=w_cout)
        else:
            w_pool = jnp.transpose(GW[('pool_w', j)], (1, 0, 2, 3)).reshape(4, PG, PG)
            mix = _pool_fwd(h, gm, f"pool_fwd{i}")
            h1 = _pool_out(mix, w_pool, p_scale, h, f"pool_out{i}")
            st.update(mix=mix, w_pool=w_pool)
        ffn_w = [GW[k] for k in _ffn_keys(i)]
        (hn, Gp, Up, Ap, Bp), got = _ffn_fwd(h1, norm_ffn[i:i + 1], *ffn_w, f"ffn_fwd{i}", gather_carry(f"ffn_fwd{i}"))
        GW.update(zip(GATHER_PLAN.get(f"ffn_fwd{i}", []), got))
        st.update(h1=h1, G=Gp, U=Up, A=Ap, B=Bp, ffn_w=ffn_w)
        saved.append(st)
        h = hn

    target = jnp.pad(loss_target[0], ((N_META, 0), (0, 0)))
    dh, d_nfinal, loss_part = _loss_head(h, norm_final.reshape(1, D), target)
    d_nmix, d_nffn = [None] * DEPTH, [None] * DEPTH
    d_bqkv, d_bo, dflats = [None, None], [None, None], [None, None]
    d_convk = d_pscale = None
    for i in reversed(range(DEPTH)):
        kind, j = i % 3, i // 3
        st = saved[i]
        call = f"ffn_bwd_act{i}"
        (DG, DU, dh1, d_nffn[i]), got = _ffn_bwd_act(dh, st['h1'], norm_ffn[i:i + 1], st['G'], st['U'], *st['ffn_w'],
                                                     call, reduce_carry(call))
        land.update(zip(REDUCE_PLAN.get(call, []), got))
        call = f"ffn_bwd_w{i}"
        (gw, uw, dw_), got = _ffn_bwd_w(st['B'], st['A'], DG, DU, dh, call, reduce_carry(call))
        land.update(zip(REDUCE_PLAN.get(call, []), got))
        pend('ffn_w_gate', i, gw)
        pend('ffn_w_up', i, uw)
        pend('ffn_w_down', i, dw_)
        gm = norm_mix[i:i + 1]
        if kind == 0:
            lay = st['lay']
            sink_r, sink_m = st['sinks']
            gwo, d_bo[j] = _wgrad(st['o'], dh1, mode='row', colsum=True, name=f"attn_out_wgrad{i}")
            pend('attn_w_o', j, gwo)
            do = _linear_t(dh1, st['w_o'], to_heads=True, name=f"attn_out_bwd{i}")
            call = f"attn_bwd{i}"
            (dq_r, dkt, dvt, dkmt, dvmt, dbias), got = _attn_bwd(
                lay['q'], lay['qt'], lay['k'], lay['kmp'], lay['kt'], lay['kmt'], lay['vt'], lay['vmt'], bias_blk,
                sink_r, do, jnp.swapaxes(do[:, N_META:], 1, 2), call, reduce_carry(call))
            land.update(zip(REDUCE_PLAN.get(call, []), got))
            dq_m, dkm2, dvm2, dbmm = _attn_meta_bwd(lay['qm'], lay['km'], lay['vm'], bias_mm, sink_m,
                                                    do[:, :N_META], f"attn_meta_bwd{i}")
            dflats[j] = jnp.concatenate([jnp.transpose(dbias, (1, 0, 2, 3)).reshape(H, -1), dbmm.reshape(H, -1)], axis=1)
            dkm = (jnp.transpose(dkmt[:, :, :N_META], (0, 2, 1)) + dkm2).astype(BF16)
            dvm = (jnp.transpose(dvmt[:, :, :N_META], (0, 2, 1)) + dvm2).astype(BF16)
            dqkv = jnp.concatenate([
                jnp.concatenate([_unheads(dq_m), _unheads(dq_r)], axis=0) * scale,
                jnp.concatenate([_unheads(dkm), _unheads_t(dkt)], axis=0),
                jnp.concatenate([_unheads(dvm), _unheads_t(dvt)], axis=0)], axis=1)
            gq, d_bqkv[j] = _wgrad(st['a'], dqkv, mode='col', colsum=True, name=f"qkv_wgrad{i}")
            pend('attn_w_qkv', j, gq)
            call = f"qkv_bwd{i}"
            res = _linear_t(dqkv, GW[('attn_w_qkv', j)], rms=(st['h'], gm, dh1), carry=reduce_carry(call), name=call)
            if call in REDUCE_PLAN:
                res, got = res
                land.update(zip(REDUCE_PLAN[call], got))
            dh, d_nmix[i] = res
        elif kind == 1:
            pend('conv_w_out', j, _wgrad(st['y'], dh1, mode='row', name=f"conv_out_wgrad{i}"))
            dy = _linear_t(dh1, st['w_cout'], out_dtype=F32, name=f"conv_out_bwd{i}")
            dz3, d_convk = _conv_bwd(st['z3'], conv_k, dy, f"conv_bwd{i}")
            pend('conv_w_in', j, _wgrad(st['a'], dz3, mode='col', name=f"conv_in_wgrad{i}"))
            dh, d_nmix[i] = _linear_t(dz3, GW[('conv_w_in', j)], rms=(st['h'], gm, dh1), name=f"conv_in_bwd{i}")
        else:
            dmix, dwp, d_pscale = _pool_out_bwd(dh1, st['mix'], st['w_pool'], p_scale, f"pool_out_bwd{i}")
            pend('pool_w', j, jnp.transpose(dwp.reshape(4, NSH, PG // NSH, PG), (1, 0, 2, 3)).astype(BF16))
            dh, d_nmix[i] = _pool_bwd(dmix, st['h'], gm, dh1, f"pool_bwd{i}")

    dts = _bias_lookup_bwd(jnp.concatenate(dflats, axis=0), idx, "bias_lookup_bwd")
    d_table = dts[:H, :N_BUCKETS].T + dts[H:, :N_BUCKETS].T
    d_sinks = dts[:, N_BUCKETS].reshape(2, H)

    ids = jnp.stack([chip, lax.axis_index("c")]).astype(jnp.int32)
    shared = _pair_share([_sum_leaf([land[(n, l)] for l in range(w[n].shape[0])],
                                    [pending[(n, l)] for l in range(w[n].shape[0])], ids, f"sum_{n}") for n in BIG])
    grads = {n: s.reshape(w[n].shape) for n, s in zip(BIG, shared)}

    small_full = [dh[:N_META], d_table, jnp.concatenate(d_nmix, axis=0), jnp.concatenate(d_nffn, axis=0), d_nfinal,
                  jnp.concatenate(d_bqkv, axis=0), jnp.concatenate(d_bo, axis=0), d_sinks, d_convk,
                  d_pscale, loss_part[:, 0:1]]
    red = _unpack(_allreduce_small(_pack(small_full)), [a.shape for a in small_full])
    g_meta, g_table, g_nmix, g_nffn, g_nfinal, g_bqkv, g_bo, g_sinks, g_convk, g_pscale, loss = red

    def shard(a, width):
        return lax.dynamic_slice_in_dim(a, chip * width, width, axis=1)

    grads.update(meta_tokens=shard(g_meta, 256), rel_bias_table=g_table, norm_mix=g_nmix, norm_ffn=g_nffn,
                 norm_final=g_nfinal.reshape(D), attn_b_qkv=shard(g_bqkv, 384), attn_b_o=shard(g_bo, 256),
                 attn_sinks=g_sinks, conv_w=shard(g_convk, 256)[None], pool_scale=shard(g_pscale, 256))

    delta, new_m, new_v = {}, {}, {}
    for n in BIG:
        shp = w[n].shape
        r3 = (int(np.prod(shp[:-2])),) + shp[-2:]
        swap = (lambda a: jnp.swapaxes(a, 1, 2)) if shp[-1] % 128 else (lambda a: a)
        res = _adam(*[swap(a.reshape(r3)) for a in (w[n], grads[n], mom[n], var[n])], f"adam_{n}")
        delta[n], new_m[n], new_v[n], grads[n] = [swap(a).reshape(shp) for a in res]
    shapes = [w[n].shape for n in SMALL]
    packed = [_pack([d[n] for n in SMALL])[None] for d in (w, grads, mom, var)]
    for dst, res in zip((delta, new_m, new_v), _adam(*packed, "adam_small")[:3]):
        dst.update(zip(SMALL, _unpack(res[0], shapes)))

    return (loss.reshape(()), dh[N_META:][None], *[grads[n] for n in WEIGHTS], *[delta[n] for n in WEIGHTS],
            *[new_m[n] for n in WEIGHTS], *[new_v[n] for n in WEIGHTS])
```

```python
import functools
import math

import numpy as np
import jax
import jax.numpy as jnp
from jax import lax
from jax.experimental import pallas as pl
from jax.experimental.pallas import tpu as pltpu

F32, BF16 = jnp.float32, jnp.bfloat16
D = 1024
N_META = 16
EPS = 1e-6
H, KV, GRP, HD = 16, 4, 4, 64
BLK = 128
NKEY = 3 * BLK
N_BUCKETS = 32
POOL_WINDOWS = (2, 4, 8, 16)
PG = 256
DFF = 2816
NSH = 4
FC = DFF // NSH
HALO = 16
NEG = -1e30
DEPTH = 4
LR, B1, B2, ADAM_EPS, WD, STEP = 0.001, 0.9, 0.999, 1e-08, 0.01, 10
MESH = pl.DeviceIdType.MESH
ANY = pl.BlockSpec(memory_space=pl.ANY)
VMEM_LIMIT = 48 * 1024 * 1024
VMEM_BIG = 58 * 1024 * 1024
BIG_TILE = 1024
ACT_CHUNKS = 3

_NN = (((1,), (0,)), ((), ()))
_NT = (((1,), (1,)), ((), ()))
_TN = (((0,), (0,)), ((), ()))

WEIGHTS = ['meta_tokens', 'rel_bias_table', 'norm_mix', 'norm_ffn', 'norm_final', 'attn_w_qkv', 'attn_b_qkv',
           'attn_w_o', 'attn_b_o', 'attn_sinks', 'conv_w_in', 'conv_w', 'conv_w_out', 'pool_w', 'pool_scale',
           'ffn_w_gate', 'ffn_w_up', 'ffn_w_down']
BIG = ['attn_w_qkv', 'attn_w_o', 'conv_w_in', 'conv_w_out', 'pool_w', 'ffn_w_gate', 'ffn_w_up', 'ffn_w_down']
SMALL = [w for w in WEIGHTS if w not in BIG]
HALF_SHAPE = {'attn_w_qkv': (512, 384), 'attn_w_o': (128, 1024), 'conv_w_in': (512, 768),
              'conv_w_out': (128, 1024), 'pool_w': (128, 256), 'ffn_w_gate': (512, 704),
              'ffn_w_up': (512, 704), 'ffn_w_down': (352, 1024)}


def _dot(a, b, dims=_NN):
    return lax.dot_general(a, b, dims, preferred_element_type=F32)


def _tile(n, cap=512):
    best = None
    for t in range(16, min(n, cap) + 1, 16):
        if n % t == 0:
            best = t
    assert best is not None, n
    return best


def _params(n_axes, vmem=VMEM_LIMIT):
    return pltpu.CompilerParams(dimension_semantics=("arbitrary",) * n_axes, vmem_limit_bytes=vmem)


def _rstd(x):
    return lax.rsqrt(jnp.mean(x * x, axis=-1, keepdims=True) + EPS)


def _rms_bwd(dy, x, g, r):
    u = dy * g
    dx = r * u - x * ((r * r * r) * (jnp.sum(x * u, axis=-1, keepdims=True) * (1.0 / D)))
    return dx, jnp.sum(dy * (x * r), axis=0, keepdims=True)


def _sigmoid(x):
    return 1.0 / (1.0 + jnp.exp(-x))


def _acc(ref, val, first):
    @pl.when(first)
    def _():
        ref[...] = val

    @pl.when(jnp.logical_not(first))
    def _():
        ref[...] += val


def _linear(x, w, *, gain=None, bias=None, resid=None, out_dtype=BF16, qkv_heads=False, cap=BIG_TILE, name):
    L, K = x.shape
    J, _, Nc = w.shape
    tm = _tile(L, cap)
    has_g, has_b, has_r = gain is not None, bias is not None, resid is not None
    n_main = 3 if qkv_heads else 1

    def body(*refs):
        refs = list(refs)
        x_ref, w_ref = refs[:2]
        i = 2
        g_ref = b_ref = r_ref = None
        if has_g:
            g_ref, i = refs[i], i + 1
        if has_b:
            b_ref, i = refs[i], i + 1
        if has_r:
            r_ref, i = refs[i], i + 1
        out_ref = refs[i]
        if has_g:
            xf = x_ref[...]
            xv = (xf * _rstd(xf) * g_ref[...]).astype(BF16)
            refs[i + n_main][...] = xv
        else:
            xv = x_ref[...]
        for s in range(J):
            sl = slice(s * Nc, (s + 1) * Nc)
            acc = _dot(xv, w_ref[s])
            if has_b:
                acc = acc + b_ref[:, sl]
            if has_r:
                acc = acc + r_ref[:, sl]
            if not qkv_heads:
                out_ref[:, sl] = acc.astype(out_dtype)
                continue
            for r in range(Nc // HD):
                hd, blk = s * (Nc // HD) + r, acc[:, r * HD:(r + 1) * HD]
                if hd < H:
                    refs[i][hd] = (blk * HD ** -0.5).astype(BF16)
                elif hd < H + KV:
                    refs[i + 1][hd - H] = blk.astype(BF16)
                else:
                    refs[i + 2][hd - H - KV] = blk.astype(BF16)

    row = lambda n: pl.BlockSpec((tm, n), lambda t: (t, 0))
    one = lambda n: pl.BlockSpec((1, n), lambda t: (0, 0))
    in_specs = [row(K), pl.BlockSpec((J, K, Nc), lambda t: (0, 0, 0))]
    ops = [x, w]
    if has_g:
        in_specs.append(one(K))
        ops.append(gain)
    if has_b:
        in_specs.append(one(J * Nc))
        ops.append(bias)
    if has_r:
        in_specs.append(row(J * Nc))
        ops.append(resid)
    if qkv_heads:
        heads = lambda n: pl.BlockSpec((n, tm, HD), lambda t: (0, t, 0))
        out_specs = [heads(H), heads(KV), heads(KV)]
        out_shape = [jax.ShapeDtypeStruct((n, L, HD), BF16) for n in (H, KV, KV)]
    else:
        out_specs, out_shape = [row(J * Nc)], [jax.ShapeDtypeStruct((L, J * Nc), out_dtype)]
    if has_g:
        out_specs, out_shape = out_specs + [row(K)], out_shape + [jax.ShapeDtypeStruct((L, K), BF16)]
    res = pl.pallas_call(body, name=name, grid=(L // tm,), in_specs=in_specs, out_specs=out_specs,
                         out_shape=out_shape, compiler_params=_params(1, VMEM_BIG))(*ops)
    return res[0] if len(res) == 1 else res


def _linear_t(dy, w, *, out_dtype=BF16, rms=None, to_heads=False, carry=None, cap=BIG_TILE, name):
    L = dy.shape[0]
    J, K, Nc = w.shape
    tm = _tile(L, cap)
    has_rms = rms is not None
    step = tm // ACT_CHUNKS if tm % (16 * ACT_CHUNKS) == 0 else tm

    def body(*refs):
        t = pl.program_id(0)
        dy_ref, w_ref = refs[:2]
        if has_rms:
            h_ref, g_ref, r_ref, out_ref, dg_ref = refs[2:7]
            gain_row = g_ref[...]
            dg = jnp.zeros((1, K), F32)
            for r0 in range(0, tm, step):
                rows = pl.ds(r0, step)
                acc = _dot(dy_ref[rows, 0:Nc].astype(BF16), w_ref[0], _NT)
                for s in range(1, J):
                    acc = acc + _dot(dy_ref[rows, s * Nc:(s + 1) * Nc].astype(BF16), w_ref[s], _NT)
                x = h_ref[rows, :]
                dx, part = _rms_bwd(acc, x, gain_row, _rstd(x))
                out_ref[rows, :] = r_ref[rows, :] + dx
                dg = dg + part
            _acc(dg_ref, dg, t == 0)
            return
        acc = _dot(dy_ref[:, 0:Nc].astype(BF16), w_ref[0], _NT)
        for s in range(1, J):
            acc = acc + _dot(dy_ref[:, s * Nc:(s + 1) * Nc].astype(BF16), w_ref[s], _NT)
        if to_heads:
            for hd in range(K // HD):
                refs[2][hd] = acc[:, hd * HD:(hd + 1) * HD].astype(out_dtype)
        else:
            refs[2][...] = acc.astype(out_dtype)

    row = lambda n: pl.BlockSpec((tm, n), lambda t: (t, 0))
    in_specs = [row(J * Nc), pl.BlockSpec((J, K, Nc), lambda t: (0, 0, 0))]
    ops = [dy, w]
    if has_rms:
        in_specs += [row(K), pl.BlockSpec((1, K), lambda t: (0, 0)), row(K)]
        ops += list(rms)
        out_specs = [row(K), pl.BlockSpec((1, K), lambda t: (0, 0))]
        out_shape = [jax.ShapeDtypeStruct((L, K), F32), jax.ShapeDtypeStruct((1, K), F32)]
    elif to_heads:
        out_specs = [pl.BlockSpec((K // HD, tm, HD), lambda t: (0, t, 0))]
        out_shape = [jax.ShapeDtypeStruct((K // HD, L, HD), out_dtype)]
    else:
        out_specs = [row(K)]
        out_shape = [jax.ShapeDtypeStruct((L, K), out_dtype)]
    res, got = _carried_call(body, carry, name=name, grid=(L // tm,), in_specs=in_specs, out_specs=out_specs,
                             out_shape=out_shape, scratch_shapes=[], compiler_params=_params(1, VMEM_BIG),
                             operands=ops)
    res = res[0] if len(res) == 1 else res
    return res if carry is None else (res, got)


def _wgrad(x, dy, *, mode, colsum=False, cap=BIG_TILE, name):
    L, K = x.shape
    N = dy.shape[1]
    tm = _tile(L, cap)
    nt = L // tm
    oshape = (NSH, K, N // NSH) if mode == 'col' else (NSH, K // NSH, N)

    def body(*refs):
        t = pl.program_id(0)
        x_ref, dy_ref, out_ref = refs[:3]
        acc_ref = refs[-1]
        dyv = dy_ref[...]

        @pl.when(t == 0)
        def _():
            acc_ref[...] = jnp.zeros_like(acc_ref)

        acc_ref[...] += _dot(x_ref[...].astype(BF16), dyv.astype(BF16), _TN)
        if colsum:
            _acc(refs[3], jnp.sum(dyv.astype(F32), axis=0, keepdims=True), t == 0)

        @pl.when(t == nt - 1)
        def _():
            for s in range(NSH):
                if mode == 'col':
                    out_ref[s] = acc_ref[:, s * oshape[2]:(s + 1) * oshape[2]].astype(BF16)
                else:
                    out_ref[s] = acc_ref[s * oshape[1]:(s + 1) * oshape[1], :].astype(BF16)

    out_specs = pl.BlockSpec(oshape, lambda t: (0, 0, 0))
    out_shape = jax.ShapeDtypeStruct(oshape, BF16)
    if colsum:
        out_specs = (out_specs, pl.BlockSpec((1, N), lambda t: (0, 0)))
        out_shape = (out_shape, jax.ShapeDtypeStruct((1, N), F32))
    return pl.pallas_call(
        body, name=name, grid=(nt,),
        in_specs=[pl.BlockSpec((tm, K), lambda t: (t, 0)), pl.BlockSpec((tm, N), lambda t: (t, 0))],
        out_specs=out_specs, out_shape=out_shape,
        scratch_shapes=[pltpu.VMEM((K, N), F32)], compiler_params=_params(1, VMEM_BIG))(x, dy)


def _ffn_fwd(h, gain, wg, wu, wd, name, carry=None):
    L = h.shape[0]
    tm = _tile(L, BIG_TILE)

    def body(h_ref, g_ref, wg_ref, wu_ref, wd_ref, hn_ref, G_ref, U_ref, A_ref, B_ref, acc_ref, b_scr):
        j = pl.program_id(1)

        @pl.when(j == 0)
        def _():
            x = h_ref[...]
            b = (x * _rstd(x) * g_ref[...]).astype(BF16)
            b_scr[...] = b
            B_ref[...] = b
            acc_ref[...] = x

        b = b_scr[...]
        g = _dot(b, wg_ref[...])
        u = _dot(b, wu_ref[...])
        s = _sigmoid(g)
        silu = g * s
        G_ref[...] = (u * (s * (1.0 + g * (1.0 - s)))).astype(BF16)
        U_ref[...] = silu.astype(BF16)
        a = (silu * u).astype(BF16)
        A_ref[...] = a
        acc_ref[...] += _dot(a, wd_ref[...])

        @pl.when(j == NSH - 1)
        def _():
            hn_ref[...] = acc_ref[...]

    row = pl.BlockSpec((tm, D), lambda t, j: (t, 0))
    chunk = pl.BlockSpec((None, tm, FC), lambda t, j: (j, t, 0))
    cshape = jax.ShapeDtypeStruct((NSH, L, FC), BF16)
    return _carried_call(
        body, carry, name=name, grid=(L // tm, NSH),
        in_specs=[row, pl.BlockSpec((1, D), lambda t, j: (0, 0)),
                  pl.BlockSpec((None, D, FC), lambda t, j: (j, 0, 0)),
                  pl.BlockSpec((None, D, FC), lambda t, j: (j, 0, 0)),
                  pl.BlockSpec((None, FC, D), lambda t, j: (j, 0, 0))],
        out_specs=[row, chunk, chunk, chunk, row],
        out_shape=[jax.ShapeDtypeStruct((L, D), F32), cshape, cshape, cshape, jax.ShapeDtypeStruct((L, D), BF16)],
        scratch_shapes=[pltpu.VMEM((tm, D), F32), pltpu.VMEM((tm, D), BF16)],
        compiler_params=_params(2, VMEM_BIG), operands=[h, gain, wg, wu, wd])


def _ffn_bwd_act(dhn, h, gain, G, U, wg, wu, wd, name, carry=None):
    L = h.shape[0]
    tm = _tile(L, BIG_TILE)
    step = tm // ACT_CHUNKS if tm % (16 * ACT_CHUNKS) == 0 else tm

    def body(dhn_ref, h_ref, g_ref, G_ref, U_ref, wg_ref, wu_ref, wd_ref, DG_ref, DU_ref, dh_ref, dgain_ref, db_ref):
        t, j = pl.program_id(0), pl.program_id(1)

        @pl.when(j == 0)
        def _():
            db_ref[...] = jnp.zeros_like(db_ref)

        for r0 in range(0, tm, step):
            rows = pl.ds(r0, step)
            d_act = _dot(dhn_ref[rows, :].astype(BF16), wd_ref[...], _NT)
            dg = (d_act * G_ref[rows, :].astype(F32)).astype(BF16)
            du = (d_act * U_ref[rows, :].astype(F32)).astype(BF16)
            DG_ref[rows, :] = dg
            DU_ref[rows, :] = du
            db_ref[rows, :] += _dot(dg, wg_ref[...], _NT) + _dot(du, wu_ref[...], _NT)

        @pl.when(j == NSH - 1)
        def _():
            x = h_ref[...]
            dx, dgn = _rms_bwd(db_ref[...], x, g_ref[...], _rstd(x))
            dh_ref[...] = dhn_ref[...] + dx
            _acc(dgain_ref, dgn, t == 0)

    row = pl.BlockSpec((tm, D), lambda t, j: (t, 0))
    one = pl.BlockSpec((1, D), lambda t, j: (0, 0))
    chunk = pl.BlockSpec((None, tm, FC), lambda t, j: (j, t, 0))
    cshape = jax.ShapeDtypeStruct((NSH, L, FC), BF16)
    return _carried_call(
        body, carry, name=name, grid=(L // tm, NSH),
        in_specs=[row, row, one, chunk, chunk,
                  pl.BlockSpec((None, D, FC), lambda t, j: (j, 0, 0)),
                  pl.BlockSpec((None, D, FC), lambda t, j: (j, 0, 0)),
                  pl.BlockSpec((None, FC, D), lambda t, j: (j, 0, 0))],
        out_specs=[chunk, chunk, row, one],
        out_shape=[cshape, cshape, jax.ShapeDtypeStruct((L, D), F32), jax.ShapeDtypeStruct((1, D), F32)],
        scratch_shapes=[pltpu.VMEM((tm, D), F32)],
        compiler_params=_params(2, VMEM_BIG), operands=[dhn, h, gain, G, U, wg, wu, wd])


def _ffn_bwd_w(B, A, DG, DU, dhn, name, carry=None):
    L = B.shape[0]
    tm = _tile(L, BIG_TILE)
    nt = L // tm

    def body(B_ref, A_ref, DG_ref, DU_ref, dhn_ref, dwg_ref, dwu_ref, dwd_ref, ag, au, ad):
        t = pl.program_id(1)

        @pl.when(t == 0)
        def _():
            ag[...] = jnp.zeros_like(ag)
            au[...] = jnp.zeros_like(au)
            ad[...] = jnp.zeros_like(ad)

        b = B_ref[...]
        ag[...] += _dot(b, DG_ref[...], _TN)
        au[...] += _dot(b, DU_ref[...], _TN)
        ad[...] += _dot(A_ref[...], dhn_ref[...].astype(BF16), _TN)

        @pl.when(t == nt - 1)
        def _():
            dwg_ref[...] = ag[...].astype(BF16)
            dwu_ref[...] = au[...].astype(BF16)
            dwd_ref[...] = ad[...].astype(BF16)

    row = pl.BlockSpec((tm, D), lambda j, t: (t, 0))
    chunk = pl.BlockSpec((None, tm, FC), lambda j, t: (j, t, 0))
    return _carried_call(
        body, carry, name=name, grid=(NSH, nt), in_specs=[row, chunk, chunk, chunk, row],
        out_specs=[pl.BlockSpec((None, D, FC), lambda j, t: (j, 0, 0)),
                   pl.BlockSpec((None, D, FC), lambda j, t: (j, 0, 0)),
                   pl.BlockSpec((None, FC, D), lambda j, t: (j, 0, 0))],
        out_shape=[jax.ShapeDtypeStruct((NSH, D, FC), BF16), jax.ShapeDtypeStruct((NSH, D, FC), BF16),
                   jax.ShapeDtypeStruct((NSH, FC, D), BF16)],
        scratch_shapes=[pltpu.VMEM((D, FC), F32), pltpu.VMEM((D, FC), F32), pltpu.VMEM((FC, D), F32)],
        compiler_params=_params(2, VMEM_BIG), operands=[B, A, DG, DU, dhn])


def _loss_head(h, gain, target):
    L = h.shape[0]
    tm = _tile(L)

    def body(h_ref, g_ref, tgt_ref, dh_ref, dg_ref, loss_ref):
        t = pl.program_id(0)
        x = h_ref[...]
        g = g_ref[...]
        r = _rstd(x)
        rows = t * tm + lax.broadcasted_iota(jnp.int32, (tm, 1), 0)
        diff = jnp.where(rows >= N_META, x * r * g - tgt_ref[...], 0.0)
        part = 0.5 * jnp.sum(jnp.sum(diff * diff, axis=-1, keepdims=True) * (1.0 / D), axis=0, keepdims=True)
        dx, dg = _rms_bwd(diff * (1.0 / D), x, g, r)
        dh_ref[...] = dx
        _acc(dg_ref, dg, t == 0)
        _acc(loss_ref, jnp.broadcast_to(part, (1, 128)), t == 0)

    row = pl.BlockSpec((tm, D), lambda t: (t, 0))
    one = pl.BlockSpec((1, D), lambda t: (0, 0))
    return pl.pallas_call(
        body, name="loss_head", grid=(L // tm,), in_specs=[row, one, row],
        out_specs=(row, one, pl.BlockSpec((1, 128), lambda t: (0, 0))),
        out_shape=(jax.ShapeDtypeStruct((L, D), F32), jax.ShapeDtypeStruct((1, D), F32),
                   jax.ShapeDtypeStruct((1, 128), F32)),
        compiler_params=_params(1))(h, gain, target)


def _conv_fwd(z3, cw, name):
    L = z3.shape[0]
    tm = _tile(L)

    def body(b_ref, c_ref, u_ref, ch_ref, uh_ref, w_ref, y_ref, buf):
        t = pl.program_id(0)
        z = c_ref[...].astype(F32) * u_ref[...].astype(F32)
        buf[pl.ds(0, HALO), :] = jnp.where(t > 0, ch_ref[...].astype(F32) * uh_ref[...].astype(F32), 0.0)
        buf[pl.ds(HALO, tm), :] = z
        w0, w1, w2 = w_ref[0:1, :], w_ref[1:2, :], w_ref[2:3, :]
        conv = w2 * z + w1 * buf[pl.ds(HALO - 1, tm), :] + w0 * buf[pl.ds(HALO - 2, tm), :]
        y_ref[...] = (b_ref[...].astype(F32) * conv).astype(BF16)

    def col(k):
        return pl.BlockSpec((tm, D), lambda t: (t, k))

    def hcol(k):
        return pl.BlockSpec((HALO, D), lambda t: (jnp.maximum(t * (tm // HALO) - 1, 0), k))

    return pl.pallas_call(
        body, name=name, grid=(L // tm,),
        in_specs=[col(0), col(1), col(2), hcol(1), hcol(2), pl.BlockSpec((3, D), lambda t: (0, 0))],
        out_specs=pl.BlockSpec((tm, D), lambda t: (t, 0)),
        out_shape=jax.ShapeDtypeStruct((L, D), BF16),
        scratch_shapes=[pltpu.VMEM((HALO + tm, D), F32)], compiler_params=_params(1))(z3, z3, z3, z3, z3, cw)


def _conv_bwd(z3, cw, dy, name):
    L = z3.shape[0]
    tm = _tile(L)
    nt = L // tm
    last_h = L // HALO - 1

    def body(b_ref, c_ref, u_ref, ch_ref, uh_ref, w_ref, dy_ref, bn_ref, dyn_ref, dz3_ref, dw_ref, zbuf, dbuf):
        t = pl.program_id(0)
        w0, w1, w2 = w_ref[0:1, :], w_ref[1:2, :], w_ref[2:3, :]
        bgate, cgate, u = b_ref[...].astype(F32), c_ref[...].astype(F32), u_ref[...].astype(F32)
        z = cgate * u
        zbuf[pl.ds(0, HALO), :] = jnp.where(t > 0, ch_ref[...].astype(F32) * uh_ref[...].astype(F32), 0.0)
        zbuf[pl.ds(HALO, tm), :] = z
        z1 = zbuf[pl.ds(HALO - 1, tm), :]
        z2 = zbuf[pl.ds(HALO - 2, tm), :]
        conv = w2 * z + w1 * z1 + w0 * z2
        dyv = dy_ref[...]
        dconv = dyv * bgate
        dbuf[pl.ds(0, tm), :] = dconv
        dbuf[pl.ds(tm, HALO), :] = jnp.where(t < nt - 1, dyn_ref[...] * bn_ref[...].astype(F32), 0.0)
        dz = w2 * dconv + w1 * dbuf[pl.ds(1, tm), :] + w0 * dbuf[pl.ds(2, tm), :]
        dz3_ref[:, 0:D] = (dyv * conv).astype(BF16)
        dz3_ref[:, D:2 * D] = (dz * u).astype(BF16)
        dz3_ref[:, 2 * D:3 * D] = (dz * cgate).astype(BF16)
        @pl.when(t == 0)
        def _():
            dw_ref[...] = jnp.zeros_like(dw_ref)

        for k, zk in enumerate((z2, z1, z)):
            dw_ref[k:k + 1, :] += jnp.sum(dconv * zk, axis=0, keepdims=True)

    def col(k):
        return pl.BlockSpec((tm, D), lambda t: (t, k))

    def hprev(k):
        return pl.BlockSpec((HALO, D), lambda t: (jnp.maximum(t * (tm // HALO) - 1, 0), k))

    def hnext(k):
        return pl.BlockSpec((HALO, D), lambda t: (jnp.minimum((t + 1) * (tm // HALO), last_h), k))

    return pl.pallas_call(
        body, name=name, grid=(nt,),
        in_specs=[col(0), col(1), col(2), hprev(1), hprev(2), pl.BlockSpec((3, D), lambda t: (0, 0)),
                  col(0), hnext(0), hnext(0)],
        out_specs=(pl.BlockSpec((tm, 3 * D), lambda t: (t, 0)), pl.BlockSpec((3, D), lambda t: (0, 0))),
        out_shape=(jax.ShapeDtypeStruct((L, 3 * D), BF16), jax.ShapeDtypeStruct((3, D), F32)),
        scratch_shapes=[pltpu.VMEM((HALO + tm, D), F32), pltpu.VMEM((tm + HALO, D), F32)],
        compiler_params=_params(1))(z3, z3, z3, z3, z3, cw, dy, z3, dy)


def _count_inv(pos, w):
    return 1.0 / jnp.minimum(pos + 1, w).astype(F32)


def _pool_fwd(h, gain, name):
    L = h.shape[0]
    tm = _tile(L)

    def body(h_ref, hh_ref, g_ref, mix_ref, buf):
        t = pl.program_id(0)
        g = g_ref[...]
        x = h_ref[...]
        a = x * _rstd(x) * g
        xh = hh_ref[...]
        buf[pl.ds(0, HALO), :] = jnp.where(t > 0, xh * _rstd(xh) * g, 0.0)
        buf[pl.ds(HALO, tm), :] = a
        pos = t * tm + lax.broadcasted_iota(jnp.int32, (tm, 1), 0)
        for gi, w in enumerate(POOL_WINDOWS):
            cols = pl.ds(gi * PG, PG)
            s = buf[pl.ds(HALO, tm), cols]
            for k in range(1, w):
                s = s + buf[pl.ds(HALO - k, tm), cols]
            mix_ref[:, gi * PG:(gi + 1) * PG] = (s / jnp.minimum(pos + 1, w).astype(F32)
                                                  - buf[pl.ds(HALO, tm), cols]).astype(BF16)

    return pl.pallas_call(
        body, name=name, grid=(L // tm,),
        in_specs=[pl.BlockSpec((tm, D), lambda t: (t, 0)),
                  pl.BlockSpec((HALO, D), lambda t: (jnp.maximum(t * (tm // HALO) - 1, 0), 0)),
                  pl.BlockSpec((1, D), lambda t: (0, 0))],
        out_specs=pl.BlockSpec((tm, D), lambda t: (t, 0)),
        out_shape=jax.ShapeDtypeStruct((L, D), BF16),
        scratch_shapes=[pltpu.VMEM((HALO + tm, D), F32)], compiler_params=_params(1))(h, h, gain)


def _pool_out(mix, wp, scale, h, name):
    L = mix.shape[0]
    tm = _tile(L)

    def body(mix_ref, wp_ref, s_ref, h_ref, out_ref):
        for gi in range(4):
            sl = slice(gi * PG, (gi + 1) * PG)
            pre = _dot(mix_ref[:, sl], wp_ref[gi])
            out_ref[:, sl] = h_ref[:, sl] + pre * s_ref[:, sl]

    row = pl.BlockSpec((tm, D), lambda t: (t, 0))
    return pl.pallas_call(
        body, name=name, grid=(L // tm,),
        in_specs=[row, pl.BlockSpec((4, PG, PG), lambda t: (0, 0, 0)), pl.BlockSpec((1, D), lambda t: (0, 0)), row],
        out_specs=row, out_shape=jax.ShapeDtypeStruct((L, D), F32), compiler_params=_params(1))(mix, wp, scale, h)


def _pool_out_bwd(dm, mix, wp, scale, name):
    L = mix.shape[0]
    tm = _tile(L)

    def body(dm_ref, mix_ref, wp_ref, s_ref, dmix_ref, dwp_ref, ds_ref):
        @pl.when(pl.program_id(0) == 0)
        def _():
            ds_ref[...] = jnp.zeros_like(ds_ref)
            dwp_ref[...] = jnp.zeros_like(dwp_ref)

        for gi in range(4):
            sl = slice(gi * PG, (gi + 1) * PG)
            mx = mix_ref[:, sl]
            dmv = dm_ref[:, sl]
            pre = _dot(mx, wp_ref[gi])
            ds_ref[:, sl] += jnp.sum(dmv * pre, axis=0, keepdims=True)
            dpre = (dmv * s_ref[:, sl]).astype(BF16)
            dwp_ref[gi] += _dot(mx, dpre, _TN)
            dmix_ref[:, sl] = _dot(dpre, wp_ref[gi], _NT)

    row = pl.BlockSpec((tm, D), lambda t: (t, 0))
    one = pl.BlockSpec((1, D), lambda t: (0, 0))
    wsp = pl.BlockSpec((4, PG, PG), lambda t: (0, 0, 0))
    return pl.pallas_call(
        body, name=name, grid=(L // tm,), in_specs=[row, row, wsp, one],
        out_specs=(row, wsp, one),
        out_shape=(jax.ShapeDtypeStruct((L, D), F32), jax.ShapeDtypeStruct((4, PG, PG), F32),
                   jax.ShapeDtypeStruct((1, D), F32)),
        compiler_params=_params(1))(dm, mix, wp, scale)


def _pool_bwd(dmix, h, gain, resid, name):
    L = h.shape[0]
    tm = _tile(L)
    nt = L // tm
    last_h = L // HALO - 1

    def body(dm_ref, dmn_ref, h_ref, g_ref, r_ref, dh_ref, dg_ref, buf):
        t = pl.program_id(0)
        pos = t * tm + lax.broadcasted_iota(jnp.int32, (tm, 1), 0)
        posn = (t + 1) * tm + lax.broadcasted_iota(jnp.int32, (HALO, 1), 0)
        dmv = dm_ref[...]
        dmn = dmn_ref[...]
        for gi, w in enumerate(POOL_WINDOWS):
            sl = slice(gi * PG, (gi + 1) * PG)
            buf[pl.ds(0, tm), sl] = dmv[:, sl] * _count_inv(pos, w)
            buf[pl.ds(tm, HALO), sl] = jnp.where(t < nt - 1, dmn[:, sl] * _count_inv(posn, w), 0.0)
        parts = []
        for gi, w in enumerate(POOL_WINDOWS):
            cols = pl.ds(gi * PG, PG)
            s = buf[pl.ds(0, tm), cols]
            for k in range(1, w):
                s = s + buf[pl.ds(k, tm), cols]
            parts.append(s)
        da = jnp.concatenate(parts, axis=1) - dmv
        x = h_ref[...]
        dx, dg = _rms_bwd(da, x, g_ref[...], _rstd(x))
        dh_ref[...] = r_ref[...] + dx
        _acc(dg_ref, dg, t == 0)

    row = pl.BlockSpec((tm, D), lambda t: (t, 0))
    one = pl.BlockSpec((1, D), lambda t: (0, 0))
    return pl.pallas_call(
        body, name=name, grid=(nt,),
        in_specs=[row, pl.BlockSpec((HALO, D), lambda t: (jnp.minimum((t + 1) * (tm // HALO), last_h), 0)),
                  row, one, row],
        out_specs=(row, one),
        out_shape=(jax.ShapeDtypeStruct((L, D), F32), jax.ShapeDtypeStruct((1, D), F32)),
        scratch_shapes=[pltpu.VMEM((tm + HALO, D), F32)], compiler_params=_params(1))(dmix, dmix, h, gain, resid)


def _bucket_np(d):
    d = np.maximum(d, 0)
    df = np.maximum(d, 1).astype(np.float32)
    large = 16 + (np.log(df / np.float32(16)) / np.float32(math.log(128 / 16)) * np.float32(16)).astype(np.int32)
    return np.where(d < 16, d, np.minimum(large, N_BUCKETS - 1))


def _bias_index():
    iq = np.arange(BLK)[:, None]
    jk = np.arange(2 * BLK)[None, :]
    dist = BLK + iq - jk
    band = _bucket_np(dist)
    ok = (dist >= 0) & (dist < BLK)
    band1 = np.where(ok, band, N_BUCKETS)
    band0 = np.where(ok & (jk >= BLK), band, N_BUCKETS)
    im = np.arange(N_META)[None, :]
    unused = np.full((BLK, BLK - N_META), N_BUCKETS)
    var0 = np.concatenate([_bucket_np(N_META + iq - im), unused, band0], axis=1)
    var1 = np.concatenate([_bucket_np(N_META + BLK + iq - im), unused, band1], axis=1)
    dm = np.arange(N_META)[:, None] - im
    mm = np.where(dm >= 0, _bucket_np(dm), N_BUCKETS)
    segs = [var0, var1, mm]
    return np.concatenate([s.reshape(-1) for s in segs]).astype(np.int32), [s.shape for s in segs]


P_CHUNK = 9856


def _onehot(idx_ref, grad):
    rows = lax.broadcasted_iota(jnp.int32, (128, P_CHUNK), 0)
    hit = (rows == idx_ref[...]).astype(F32)
    return jnp.where(rows == N_BUCKETS, -1.0, hit) if grad else hit


def _bias_lookup(table_aug, idx, name, carry=None):
    P = idx.shape[1]

    def body(t_ref, idx_ref, o_ref):
        o_ref[...] = lax.dot_general(t_ref[...], _onehot(idx_ref, False), _NN, precision=lax.Precision.HIGHEST,
                                     preferred_element_type=F32)

    return _carried_call(
        body, carry, name=name, grid=(P // P_CHUNK,),
        in_specs=[pl.BlockSpec((H, 128), lambda i: (0, 0)), pl.BlockSpec((1, P_CHUNK), lambda i: (0, i))],
        out_specs=[pl.BlockSpec((H, P_CHUNK), lambda i: (0, i))],
        out_shape=[jax.ShapeDtypeStruct((H, P), F32)], scratch_shapes=[], compiler_params=_params(1),
        operands=[table_aug, idx])


def _bias_lookup_bwd(dbias, idx, name):
    rows, P = dbias.shape

    def body(d_ref, idx_ref, o_ref):
        part = lax.dot_general(d_ref[...], _onehot(idx_ref, True), _NT, precision=lax.Precision.HIGHEST,
                               preferred_element_type=F32)
        _acc(o_ref, part, pl.program_id(0) == 0)

    return pl.pallas_call(
        body, name=name, grid=(P // P_CHUNK,),
        in_specs=[pl.BlockSpec((rows, P_CHUNK), lambda i: (0, i)), pl.BlockSpec((1, P_CHUNK), lambda i: (0, i))],
        out_specs=pl.BlockSpec((rows, 128), lambda i: (0, 0)),
        out_shape=jax.ShapeDtypeStruct((rows, 128), F32), compiler_params=_params(1))(dbias, idx)


def _probs(q, kbt, bias, sink):
    s = _dot(q, kbt) + bias
    m = jnp.maximum(jnp.max(s, axis=-1, keepdims=True), sink)
    e = jnp.exp(s - m)
    return e * (1.0 / (jnp.sum(e, axis=-1, keepdims=True) + jnp.exp(sink - m)))


def _attn_specs(nb):
    def cur(kh, n):
        return jnp.minimum(n, nb - 1)

    def prev(kh, n):
        return jnp.maximum(jnp.minimum(n, nb - 1) - 1, 0)

    def rows(heads, blk):
        return pl.BlockSpec((pl.Element(heads), pl.Element(BLK), pl.Element(HD)),
                            lambda kh, n: (kh * heads, pl.multiple_of(N_META + blk(kh, n) * BLK, N_META), 0))

    return dict(
        q=rows(GRP, cur),
        qo=pl.BlockSpec((GRP, BLK, HD), lambda kh, n: (kh, cur(kh, n), 0)),
        qt=pl.BlockSpec((GRP, HD, BLK), lambda kh, n: (kh, 0, cur(kh, n))),
        cur=rows(1, cur),
        prev=rows(1, prev),
        meta=pl.BlockSpec((None, BLK, HD), lambda kh, n: (kh, 0, 0)),
        curt=pl.BlockSpec((None, HD, BLK), lambda kh, n: (kh, 0, cur(kh, n))),
        prevt=pl.BlockSpec((None, HD, BLK), lambda kh, n: (kh, 0, prev(kh, n))),
        metat=pl.BlockSpec((None, HD, BLK), lambda kh, n: (kh, 0, 0)),
        bias=pl.BlockSpec((None, GRP, BLK, NKEY), lambda kh, n: (jnp.minimum(n, 1), kh, 0, 0)),
        sink=pl.BlockSpec((None, GRP * BLK, 1), lambda kh, n: (kh, 0, 0)))


def _attn_fwd(q, k, km, v, vm, band, bmeta, sink, name, carry=None):
    S = q.shape[1] - N_META
    nb = S // BLK
    sp = _attn_specs(nb)
    var = lambda kh, n: (jnp.minimum(n, 1), kh, 0, 0)
    meta = pl.BlockSpec((None, N_META, HD), lambda kh, n: (kh, 0, 0))

    def body(q_ref, kc_ref, kp_ref, vc_ref, vp_ref, km_ref, vm_ref, band_ref, bm_ref, sink_ref, o_ref):
        qv = q_ref[...].reshape(GRP * BLK, HD)
        kb = jnp.concatenate([kp_ref[0], kc_ref[0]], axis=0)
        vb = jnp.concatenate([vp_ref[0], vc_ref[0]], axis=0)
        sink = sink_ref[...]
        s_b = _dot(qv, kb, _NT) + band_ref[...].reshape(GRP * BLK, 2 * BLK)
        s_m = _dot(qv, km_ref[...], _NT) + bm_ref[...].reshape(GRP * BLK, N_META)
        m = jnp.maximum(jnp.maximum(jnp.max(s_b, axis=-1, keepdims=True), jnp.max(s_m, axis=-1, keepdims=True)), sink)
        e_b = jnp.exp(s_b - m)
        e_m = jnp.exp(s_m - m)
        inv = 1.0 / (jnp.sum(e_b, axis=-1, keepdims=True) + jnp.sum(e_m, axis=-1, keepdims=True) + jnp.exp(sink - m))
        o = _dot((e_b * inv).astype(BF16), vb) + _dot((e_m * inv).astype(BF16), vm_ref[...])
        o_ref[...] = o.reshape(GRP, BLK, HD).astype(BF16)

    return _carried_call(
        body, carry, name=name, grid=(KV, nb),
        in_specs=[sp['q'], sp['cur'], sp['prev'], sp['cur'], sp['prev'], meta, meta,
                  pl.BlockSpec((None, GRP, BLK, 2 * BLK), var), pl.BlockSpec((None, GRP, BLK, N_META), var),
                  sp['sink']],
        out_specs=[sp['qo']], out_shape=[jax.ShapeDtypeStruct((H, S, HD), BF16)], scratch_shapes=[],
        compiler_params=_params(2), operands=[q, k, k, v, v, km, vm, band, bmeta, sink])


def _attn_bwd(q, qt, k, km, kt, kmt, vt, vmt, bias, sink, do, dot_, name, carry=None):
    S = q.shape[1] - N_META
    nb = S // BLK
    sp = _attn_specs(nb)

    def body(q_ref, qt_ref, kc_ref, kp_ref, km_ref, kct_ref, kpt_ref, kmt_ref, vct_ref, vpt_ref, vmt_ref,
             bias_ref, sink_ref, do_ref, dot_ref, dq_ref, dkt_ref, dvt_ref, dkmt_ref, dvmt_ref, dbias_ref, ck, cv):
        n = pl.program_id(1)

        @pl.when(n <= 1)
        def _():
            dbias_ref[...] = jnp.zeros_like(dbias_ref)

        @pl.when(n == 0)
        def _():
            dkmt_ref[...] = jnp.zeros_like(dkmt_ref)
            dvmt_ref[...] = jnp.zeros_like(dvmt_ref)

        @pl.when(n < nb)
        def _():
            kbt = jnp.concatenate([kmt_ref[...], kpt_ref[...], kct_ref[...]], axis=1)
            vbt = jnp.concatenate([vmt_ref[...], vpt_ref[...], vct_ref[...]], axis=1)
            kb = jnp.concatenate([km_ref[...], kp_ref[0], kc_ref[0]], axis=0)
            p = _probs(q_ref[...].reshape(GRP * BLK, HD), kbt, bias_ref[...].reshape(GRP * BLK, NKEY), sink_ref[...])
            dp = _dot(do_ref[...].reshape(GRP * BLK, HD), vbt)
            ds = p * (dp - jnp.sum(p * dp, axis=-1, keepdims=True))
            dbias_ref[...] += ds.reshape(GRP, BLK, NKEY)
            ds16 = ds.astype(BF16)
            dq_ref[...] = _dot(ds16, kb).reshape(GRP, BLK, HD).astype(BF16)
            qtv = jnp.concatenate([qt_ref[g] for g in range(GRP)], axis=1)
            dotv = jnp.concatenate([dot_ref[g] for g in range(GRP)], axis=1)
            dkt = _dot(qtv, ds16)
            dvt = _dot(dotv, p.astype(BF16))
            dkmt_ref[...] += dkt[:, 0:BLK]
            dvmt_ref[...] += dvt[:, 0:BLK]

            @pl.when(n >= 1)
            def _():
                dkt_ref[...] = (ck[...] + dkt[:, BLK:2 * BLK]).astype(BF16)
                dvt_ref[...] = (cv[...] + dvt[:, BLK:2 * BLK]).astype(BF16)

            ck[...] = dkt[:, 2 * BLK:3 * BLK]
            cv[...] = dvt[:, 2 * BLK:3 * BLK]

        @pl.when(n == nb)
        def _():
            dkt_ref[...] = ck[...].astype(BF16)
            dvt_ref[...] = cv[...].astype(BF16)

    kvout = pl.BlockSpec((None, HD, BLK), lambda kh, n: (kh, 0, jnp.maximum(n - 1, 0)))
    return _carried_call(
        body, carry, name=name, grid=(KV, nb + 1),
        in_specs=[sp['q'], sp['qt'], sp['cur'], sp['prev'], sp['meta'], sp['curt'], sp['prevt'], sp['metat'],
                  sp['curt'], sp['prevt'], sp['metat'], sp['bias'], sp['sink'], sp['q'], sp['qt']],
        out_specs=[sp['qo'], kvout, kvout, sp['metat'], sp['metat'], sp['bias']],
        out_shape=[jax.ShapeDtypeStruct((H, S, HD), BF16), jax.ShapeDtypeStruct((KV, HD, S), BF16),
                   jax.ShapeDtypeStruct((KV, HD, S), BF16), jax.ShapeDtypeStruct((KV, HD, BLK), F32),
                   jax.ShapeDtypeStruct((KV, HD, BLK), F32), jax.ShapeDtypeStruct((2, H, BLK, NKEY), F32)],
        scratch_shapes=[pltpu.VMEM((HD, BLK), F32), pltpu.VMEM((HD, BLK), F32)],
        compiler_params=_params(2), operands=[q, qt, k, k, km, kt, kt, kmt, vt, vt, vmt, bias, sink, do, dot_])


def _meta_softmax(q, k, bias, sink):
    s = _dot(q, k, _NT) + bias
    m = jnp.maximum(jnp.max(s, axis=-1, keepdims=True), sink)
    e = jnp.exp(s - m)
    return e * (1.0 / (jnp.sum(e, axis=-1, keepdims=True) + jnp.exp(sink - m)))


def _attn_meta_fwd(qm, km, vm, bias, sink, name):
    def body(q_ref, k_ref, v_ref, b_ref, s_ref, o_ref):
        for h in range(H):
            p = _meta_softmax(q_ref[h], k_ref[h // GRP], b_ref[h], s_ref[h])
            o_ref[h] = _dot(p.astype(BF16), v_ref[h // GRP]).astype(BF16)

    return pl.pallas_call(body, name=name, out_shape=jax.ShapeDtypeStruct((H, N_META, HD), BF16))(
        qm, km, vm, bias, sink)


def _attn_meta_bwd(qm, km, vm, bias, sink, do, name):
    def body(q_ref, k_ref, v_ref, b_ref, s_ref, do_ref, dq_ref, dk_ref, dv_ref, db_ref):
        for kh in range(KV):
            k, v = k_ref[kh], v_ref[kh]
            dk = jnp.zeros((N_META, HD), F32)
            dv = jnp.zeros((N_META, HD), F32)
            for g in range(GRP):
                h = kh * GRP + g
                q, dov = q_ref[h], do_ref[h]
                p = _meta_softmax(q, k, b_ref[h], s_ref[h])
                dp = _dot(dov, v, _NT)
                ds = p * (dp - jnp.sum(p * dp, axis=-1, keepdims=True))
                db_ref[h] = ds
                ds16 = ds.astype(BF16)
                dq_ref[h] = _dot(ds16, k).astype(BF16)
                dk = dk + _dot(ds16, q, _TN)
                dv = dv + _dot(p.astype(BF16), dov, _TN)
            dk_ref[kh] = dk
            dv_ref[kh] = dv

    return pl.pallas_call(
        body, name=name,
        out_shape=(jax.ShapeDtypeStruct((H, N_META, HD), BF16), jax.ShapeDtypeStruct((KV, N_META, HD), F32),
                   jax.ShapeDtypeStruct((KV, N_META, HD), F32), jax.ShapeDtypeStruct((H, N_META, N_META), F32)))(
        qm, km, vm, bias, sink, do)


def _unheads(t):
    return jnp.transpose(t, (1, 0, 2)).reshape(t.shape[1], t.shape[0] * HD)


def _unheads_t(t):
    return jnp.transpose(t, (2, 0, 1)).reshape(t.shape[2], t.shape[0] * HD)


def _pad_block(t, axis):
    pad = [(0, 0)] * t.ndim
    pad[axis] = (0, BLK - N_META)
    return jnp.pad(t, pad)


def _adam(w, g, m, v, name):
    lead, R, C = w.shape
    tr = _tile(R, 512) if R % 16 == 0 else R

    def body(w_ref, g_ref, m_ref, v_ref, d_ref, mo_ref, vo_ref, go_ref):
        gv = g_ref[...]
        go_ref[...] = gv
        mn = B1 * m_ref[...] + (1.0 - B1) * gv
        vn = B2 * v_ref[...] + (1.0 - B2) * (gv * gv)
        m_hat = mn / (1.0 - B1 ** STEP)
        v_hat = vn / (1.0 - B2 ** STEP)
        d_ref[...] = -LR * (m_hat / (jnp.sqrt(v_hat) + ADAM_EPS) + WD * w_ref[...])
        mo_ref[...] = mn
        vo_ref[...] = vn

    blk = pl.BlockSpec((None, tr, C), lambda l, i: (l, i, 0))
    shp = jax.ShapeDtypeStruct((lead, R, C), F32)
    return pl.pallas_call(body, name=name, grid=(lead, R // tr), in_specs=[blk] * 4, out_specs=(blk,) * 4,
                          out_shape=(shp,) * 4, compiler_params=_params(2))(w, g, m, v)


def _sum_leaf(lands, pends, ids, name):
    n = len(lands)
    _, R, C = lands[0].shape
    tr = _tile(R, 256)

    def body(ids_ref, *refs):
        out_ref = refs[2 * n]
        for k in range(n):
            acc = refs[n + k][...].astype(F32)
            for s in range(7):
                acc = acc + refs[k][s].astype(F32)
            out_ref[k] = acc

    grid_spec = pltpu.PrefetchScalarGridSpec(
        num_scalar_prefetch=1, grid=(R // tr,),
        in_specs=[pl.BlockSpec((7, tr, C), lambda i, ids: (0, i, 0))] * n
        + [pl.BlockSpec((None, None, tr, C), lambda i, ids: (ids[0], ids[1], i, 0))] * n,
        out_specs=pl.BlockSpec((n, None, tr, C), lambda i, ids: (0, ids[1], i, 0)))
    return pl.pallas_call(body, name=name, grid_spec=grid_spec, out_shape=jax.ShapeDtypeStruct((n, 2, R, C), F32),
                          compiler_params=_params(1))(ids, *lands, *pends)


def _place():
    x, y, c = lax.axis_index("x"), lax.axis_index("y"), lax.axis_index("c")
    chips = [(1 - x, y), (x, 1 - y), (1 - x, 1 - y)]
    return x, y, c, chips


def _rcopy(src, dst, ssem, rsem, dev):
    return pltpu.make_async_remote_copy(src_ref=src, dst_ref=dst, send_sem=ssem, recv_sem=rsem,
                                        device_id=dev, device_id_type=MESH)


class _Carry:
    def __init__(self, kind, arrays):
        self.kind, self.arrays, self.n = kind, list(arrays), len(arrays)
        self.per = 3 if kind == 'gather' else 7
        if kind == 'gather':
            self.out_shape = [jax.ShapeDtypeStruct((NSH,) + a.shape, a.dtype) for a in self.arrays]
        else:
            self.out_shape = [jax.ShapeDtypeStruct((7,) + a.shape[2:], a.dtype) for a in self.arrays]
        dma = pltpu.SemaphoreType.DMA
        self.scratch = [dma((self.per * self.n,)), dma((self.per * self.n,)), dma((self.n,))]
        if kind == 'gather':
            self.scratch += [pltpu.VMEM(a.shape, a.dtype) for a in self.arrays]

    def _copies(self, cin, cout, scr):
        ssem, rsem, loc = scr[:3]
        x, y, c, chips = _place()
        local, sends, recvs = [], [], []
        for k in range(self.n):
            if self.kind == 'gather':
                me = 2 * x + y
                local.append((pltpu.make_async_copy(cin[k], scr[3 + k], loc.at[k]),
                              pltpu.make_async_copy(scr[3 + k], cout[k].at[me], loc.at[k])))
                for j, (cx, cy) in enumerate(chips):
                    i = 3 * k + j
                    sends.append(_rcopy(cin[k], cout[k].at[me], ssem.at[i], rsem.at[i], (cx, cy, c)))
                    got = cout[k].at[2 * cx + cy]
                    recvs.append(_rcopy(got, got, ssem.at[i], rsem.at[i], (cx, cy, c)))
            else:
                for f in range(1, 8):
                    px = 1 - x if (f >> 2) & 1 else x
                    py = 1 - y if (f >> 1) & 1 else y
                    pc = 1 - c if f & 1 else c
                    i = 7 * k + f - 1
                    got = cout[k].at[f - 1]
                    sends.append(_rcopy(cin[k].at[2 * px + py, pc], got, ssem.at[i], rsem.at[i], (px, py, pc)))
                    recvs.append(_rcopy(got, got, ssem.at[i], rsem.at[i], (px, py, pc)))
        return local, sends, recvs

    def start(self, cin, cout, scr):
        local, sends, _ = self._copies(cin, cout, scr)
        for cp in [to_vmem for to_vmem, _ in local] + sends:
            cp.start()

    def finish(self, cin, cout, scr):
        local, sends, recvs = self._copies(cin, cout, scr)
        for to_vmem, to_slot in local:
            to_vmem.wait()
            to_slot.start()
        for cp in recvs:
            cp.wait_recv()
        for cp in sends:
            cp.wait_send()
        for _, to_slot in local:
            to_slot.wait()


def _carried_call(body, carry, *, name, grid, in_specs, out_specs, out_shape, scratch_shapes, compiler_params,
                  operands):
    n_in, n_out = len(in_specs), len(out_specs)
    if carry is None:
        return pl.pallas_call(body, name=name, grid=grid, in_specs=in_specs, out_specs=out_specs,
                              out_shape=out_shape, scratch_shapes=scratch_shapes,
                              compiler_params=compiler_params)(*operands), []
    m = carry.n

    def full(*refs):
        ins, cin = refs[:n_in], refs[n_in:n_in + m]
        outs, cout = refs[n_in + m:n_in + m + n_out], refs[n_in + m + n_out:n_in + 2 * m + n_out]
        own = len(refs) - len(carry.scratch)
        scr, sems = refs[n_in + 2 * m + n_out:own], refs[own:]
        ids = [pl.program_id(a) for a in range(len(grid))]
        first = functools.reduce(jnp.logical_and, [i == 0 for i in ids])
        last = functools.reduce(jnp.logical_and, [i == g - 1 for i, g in zip(ids, grid)])

        @pl.when(first)
        def _():
            carry.start(cin, cout, sems)

        body(*ins, *outs, *scr)

        @pl.when(last)
        def _():
            carry.finish(cin, cout, sems)

    res = pl.pallas_call(
        full, name=name, grid=grid, in_specs=list(in_specs) + [ANY] * m, out_specs=list(out_specs) + [ANY] * m,
        out_shape=list(out_shape) + carry.out_shape, scratch_shapes=list(scratch_shapes) + carry.scratch,
        compiler_params=compiler_params)(*operands, *carry.arrays)
    return res[:n_out], res[n_out:]


def _flush(carry, name):
    m = carry.n

    def body(*refs):
        cin, cout, sems = refs[:m], refs[m:2 * m], refs[2 * m:]
        carry.start(cin, cout, sems)
        carry.finish(cin, cout, sems)

    return pl.pallas_call(body, name=name, in_specs=[ANY] * m, out_specs=[ANY] * m, out_shape=carry.out_shape,
                          scratch_shapes=carry.scratch)(*carry.arrays)


def _pair_share(leaves):
    n = len(leaves)

    def body(*refs):
        ins, outs = refs[:n], refs[n:2 * n]
        ssem, rsem = refs[2 * n:]
        x, y, c, _ = _place()
        sib = (x, y, 1 - c)
        cps = [_rcopy(ins[k].at[:, c], outs[k].at[:, c], ssem.at[k], rsem.at[k], sib) for k in range(n)]
        for cp in cps:
            cp.start()
        for k in range(n):
            got = outs[k].at[:, 1 - c]
            _rcopy(got, got, ssem.at[k], rsem.at[k], sib).wait_recv()
        for cp in cps:
            cp.wait_send()

    dma = pltpu.SemaphoreType.DMA
    return pl.pallas_call(
        body, name="grad_pair_share", in_specs=[ANY] * n, out_specs=[ANY] * n,
        out_shape=[jax.ShapeDtypeStruct(a.shape, a.dtype) for a in leaves],
        input_output_aliases={k: k for k in range(n)},
        scratch_shapes=[dma((n,)), dma((n,))])(*leaves)


def _allreduce_small(pack):
    R = pack.shape[0]

    def body(in_ref, out_ref, buf, ssem, rsem):
        x, y, c, _ = _place()
        me = 4 * x + 2 * y + c
        buf[me] = in_ref[...]
        peers = []
        for k in range(1, 8):
            fx, fy, fc = (k >> 2) & 1, (k >> 1) & 1, k & 1
            peers.append((1 - x if fx else x, 1 - y if fy else y, 1 - c if fc else c))
        cps = [_rcopy(in_ref, buf.at[me], ssem.at[k], rsem.at[k], p) for k, p in enumerate(peers)]
        for cp in cps:
            cp.start()
        for k, (px, py, pc) in enumerate(peers):
            got = buf.at[4 * px + 2 * py + pc]
            _rcopy(got, got, ssem.at[k], rsem.at[k], (px, py, pc)).wait_recv()
        for cp in cps:
            cp.wait_send()
        acc = buf[0]
        for s in range(1, 8):
            acc = acc + buf[s]
        out_ref[...] = acc

    dma = pltpu.SemaphoreType.DMA
    return pl.pallas_call(
        body, name="allreduce_small", out_shape=jax.ShapeDtypeStruct(pack.shape, F32),
        in_specs=[pl.BlockSpec(memory_space=pltpu.VMEM)], out_specs=pl.BlockSpec(memory_space=pltpu.VMEM),
        scratch_shapes=[pltpu.VMEM((8, R, 128), F32), dma((7,)), dma((7,))])(pack)


def _pack(arrs):
    flat = jnp.concatenate([a.reshape(-1).astype(F32) for a in arrs])
    n = flat.shape[0]
    rows = -(-n // 1024) * 8
    return jnp.pad(flat, (0, rows * 128 - n)).reshape(rows, 128)


def _unpack(pack, shapes):
    flat, out, o = pack.reshape(-1), [], 0
    for s in shapes:
        n = int(np.prod(s))
        out.append(flat[o:o + n].reshape(s))
        o += n
    return out


def _ffn_keys(i):
    return [('ffn_w_gate', i), ('ffn_w_up', i), ('ffn_w_down', i)]


GATHER_PLAN = {'attn_fwd0': _ffn_keys(0) + [('conv_w_in', 0), ('conv_w_out', 0)],
               'ffn_fwd0': _ffn_keys(1),
               'ffn_fwd1': _ffn_keys(2) + [('pool_w', 0)],
               'ffn_fwd2': [('attn_w_qkv', 1), ('attn_w_o', 1)],
               'attn_fwd3': _ffn_keys(3)}
REDUCE_PLAN = {'attn_bwd3': _ffn_keys(3),
               'ffn_bwd_act2': [('attn_w_qkv', 1), ('attn_w_o', 1)],
               'ffn_bwd_act1': _ffn_keys(2) + [('pool_w', 0)],
               'ffn_bwd_act0': _ffn_keys(1),
               'ffn_bwd_w0': [('conv_w_in', 0), ('conv_w_out', 0)],
               'attn_bwd0': _ffn_keys(0) + [('attn_w_o', 0)],
               'qkv_bwd0': [('attn_w_qkv', 0)]}


def kernel(x, meta_tokens, rel_bias_table, norm_mix, norm_ffn, norm_final, attn_w_qkv, attn_b_qkv, attn_w_o, attn_b_o, attn_sinks, conv_w_in, conv_w, conv_w_out, pool_w, pool_scale, ffn_w_gate, ffn_w_up, ffn_w_down, loss_target, m_meta_tokens, m_rel_bias_table, m_norm_mix, m_norm_ffn, m_norm_final, m_attn_w_qkv, m_attn_b_qkv, m_attn_w_o, m_attn_b_o, m_attn_sinks, m_conv_w_in, m_conv_w, m_conv_w_out, m_pool_w, m_pool_scale, m_ffn_w_gate, m_ffn_w_up, m_ffn_w_down, v_meta_tokens, v_rel_bias_table, v_norm_mix, v_norm_ffn, v_norm_final, v_attn_w_qkv, v_attn_b_qkv, v_attn_w_o, v_attn_b_o, v_attn_sinks, v_conv_w_in, v_conv_w, v_conv_w_out, v_pool_w, v_pool_scale, v_ffn_w_gate, v_ffn_w_up, v_ffn_w_down):
    args = locals()
    w = {n: args[n] for n in WEIGHTS}
    mom = {n: args['m_' + n] for n in WEIGHTS}
    var = {n: args['v_' + n] for n in WEIGHTS}
    mx, my = lax.axis_index("x"), lax.axis_index("y")
    chip = 2 * mx + my
    S = x.shape[1]
    scale = jnp.asarray(HD ** -0.5, BF16)

    GW = {}
    pending = {}
    land = {}

    def gather_carry(call):
        keys = GATHER_PLAN.get(call)
        return _Carry('gather', [w[n][l].astype(BF16) for n, l in keys]) if keys else None

    def reduce_carry(call):
        keys = REDUCE_PLAN.get(call)
        return _Carry('reduce', [pending[k] for k in keys]) if keys else None

    def pend(n, l, g):
        pending[(n, l)] = g.reshape((NSH, 2) + HALF_SHAPE[n])

    small_in = jnp.concatenate([
        jnp.pad(w['meta_tokens'], ((0, 0), (0, 128))), w['attn_b_qkv'], jnp.pad(w['attn_b_o'], ((0, 0), (0, 128))),
        jnp.pad(w['conv_w'][0], ((0, 0), (0, 128))), jnp.pad(w['pool_scale'], ((0, 0), (0, 128)))], axis=0)
    gsmall, GW[('attn_w_qkv', 0)] = _flush(
        _Carry('gather', [small_in, w['attn_w_qkv'][0].astype(BF16)]), "gather_first")

    def cols(rows, width):
        return jnp.transpose(rows[:, :, :width], (1, 0, 2)).reshape(rows.shape[1], NSH * width)

    meta_full = cols(gsmall[:, 0:16], 256)
    b_qkv = cols(gsmall[:, 16:18], 384)
    b_o = cols(gsmall[:, 18:20], 256)
    conv_k = cols(gsmall[:, 20:23], 256)
    p_scale = cols(gsmall[:, 23:24], 256)

    idx_np, _ = _bias_index()
    idx = jnp.asarray(idx_np).reshape(1, -1)
    table_aug = jnp.concatenate([rel_bias_table.T, jnp.full((H, 1), NEG, F32),
                                 jnp.zeros((H, 127 - N_BUCKETS), F32)], axis=1)
    (bias_flat,), (GW[('attn_w_o', 0)],) = _bias_lookup(
        table_aug, idx, "bias_lookup", _Carry('gather', [w['attn_w_o'][0].astype(BF16)]))
    nblock = BLK * NKEY
    bias_blk = jnp.transpose(bias_flat[:, :2 * nblock].reshape(H, 2, BLK, NKEY), (1, 0, 2, 3))
    bias_mm = bias_flat[:, 2 * nblock:].reshape(H, N_META, N_META)
    bias_band, bias_meta = bias_blk[..., BLK:], bias_blk[..., :N_META]

    h = jnp.concatenate([meta_full, x[0]], axis=0)
    saved = []
    for i in range(DEPTH):
        kind, j = i % 3, i // 3
        gm = norm_mix[i:i + 1]
        st = dict(h=h)
        if kind == 0:
            w_o = GW[('attn_w_o', j)].reshape(1, D, D)
            q, k, v, a = _linear(h, GW[('attn_w_qkv', j)], gain=gm, bias=b_qkv[j:j + 1], qkv_heads=True,
                                 name=f"qkv{i}")
            tr = lambda t: jnp.swapaxes(t, 1, 2)
            lay = dict(q=q, qt=tr(q[:, N_META:]), qm=q[:, :N_META],
                       k=k, kt=tr(k[:, N_META:]), km=k[:, :N_META],
                       kmp=_pad_block(k[:, :N_META], 1), kmt=_pad_block(tr(k[:, :N_META]), 2),
                       v=v, vt=tr(v[:, N_META:]), vm=v[:, :N_META], vmt=_pad_block(tr(v[:, :N_META]), 2))
            sink_r = jnp.broadcast_to(attn_sinks[j].reshape(KV, GRP, 1, 1), (KV, GRP, BLK, 1)).reshape(KV, GRP * BLK, 1)
            sink_m = jnp.broadcast_to(attn_sinks[j].reshape(H, 1, 1), (H, N_META, 1))
            (o_r,), got = _attn_fwd(lay['q'], lay['k'], lay['km'], lay['v'], lay['vm'], bias_band, bias_meta, sink_r,
                                    f"attn_fwd{i}", gather_carry(f"attn_fwd{i}"))
            GW.update(zip(GATHER_PLAN.get(f"attn_fwd{i}", []), got))
            o_m = _attn_meta_fwd(lay['qm'], lay['km'], lay['vm'], bias_mm, sink_m, f"attn_meta_fwd{i}")
            o = _unheads(jnp.concatenate([o_m, o_r], axis=1))
            h1 = _linear(o, w_o, bias=b_o[j:j + 1], resid=h, out_dtype=F32, name=f"attn_out{i}")
            st.update(a=a, lay=lay, sinks=(sink_r, sink_m), o=o, w_o=w_o)
        elif kind == 1:
            z3, a = _linear(h, GW[('conv_w_in', j)], gain=gm, name=f"conv_in{i}")
            yv = _conv_fwd(z3, conv_k, f"conv_fwd{i}")
            w_cout = GW[('conv_w_out', j)].reshape(1, D, D)
            h1 = _linear(yv, w_cout, resid=h, out_dtype=F32, name=f"conv_out{i}")
            st.update(a=a, z3=z3, y=yv, w_cout=w_cout)
        else:
            w_pool = jnp.transpose(GW[('pool_w', j)], (1, 0, 2, 3)).reshape(4, PG, PG)
            mix = _pool_fwd(h, gm, f"pool_fwd{i}")
            h1 = _pool_out(mix, w_pool, p_scale, h, f"pool_out{i}")
            st.update(mix=mix, w_pool=w_pool)
        ffn_w = [GW[k] for k in _ffn_keys(i)]
        (hn, Gp, Up, Ap, Bp), got = _ffn_fwd(h1, norm_ffn[i:i + 1], *ffn_w, f"ffn_fwd{i}", gather_carry(f"ffn_fwd{i}"))
        GW.update(zip(GATHER_PLAN.get(f"ffn_fwd{i}", []), got))
        st.update(h1=h1, G=Gp, U=Up, A=Ap, B=Bp, ffn_w=ffn_w)
        saved.append(st)
        h = hn

    target = jnp.pad(loss_target[0], ((N_META, 0), (0, 0)))
    dh, d_nfinal, loss_part = _loss_head(h, norm_final.reshape(1, D), target)
    d_nmix, d_nffn = [None] * DEPTH, [None] * DEPTH
    d_bqkv, d_bo, dflats = [None, None], [None, None], [None, None]
    d_convk = d_pscale = None
    for i in reversed(range(DEPTH)):
        kind, j = i % 3, i // 3
        st = saved[i]
        call = f"ffn_bwd_act{i}"
        (DG, DU, dh1, d_nffn[i]), got = _ffn_bwd_act(dh, st['h1'], norm_ffn[i:i + 1], st['G'], st['U'], *st['ffn_w'],
                                                     call, reduce_carry(call))
        land.update(zip(REDUCE_PLAN.get(call, []), got))
        call = f"ffn_bwd_w{i}"
        (gw, uw, dw_), got = _ffn_bwd_w(st['B'], st['A'], DG, DU, dh, call, reduce_carry(call))
        land.update(zip(REDUCE_PLAN.get(call, []), got))
        pend('ffn_w_gate', i, gw)
        pend('ffn_w_up', i, uw)
        pend('ffn_w_down', i, dw_)
        gm = norm_mix[i:i + 1]
        if kind == 0:
            lay = st['lay']
            sink_r, sink_m = st['sinks']
            gwo, d_bo[j] = _wgrad(st['o'], dh1, mode='row', colsum=True, name=f"attn_out_wgrad{i}")
            pend('attn_w_o', j, gwo)
            do = _linear_t(dh1, st['w_o'], to_heads=True, name=f"attn_out_bwd{i}")
            call = f"attn_bwd{i}"
            (dq_r, dkt, dvt, dkmt, dvmt, dbias), got = _attn_bwd(
                lay['q'], lay['qt'], lay['k'], lay['kmp'], lay['kt'], lay['kmt'], lay['vt'], lay['vmt'], bias_blk,
                sink_r, do, jnp.swapaxes(do[:, N_META:], 1, 2), call, reduce_carry(call))
            land.update(zip(REDUCE_PLAN.get(call, []), got))
            dq_m, dkm2, dvm2, dbmm = _attn_meta_bwd(lay['qm'], lay['km'], lay['vm'], bias_mm, sink_m,
                                                    do[:, :N_META], f"attn_meta_bwd{i}")
            dflats[j] = jnp.concatenate([jnp.transpose(dbias, (1, 0, 2, 3)).reshape(H, -1), dbmm.reshape(H, -1)], axis=1)
            dkm = (jnp.transpose(dkmt[:, :, :N_META], (0, 2, 1)) + dkm2).astype(BF16)
            dvm = (jnp.transpose(dvmt[:, :, :N_META], (0, 2, 1)) + dvm2).astype(BF16)
            dqkv = jnp.concatenate([
                jnp.concatenate([_unheads(dq_m), _unheads(dq_r)], axis=0) * scale,
                jnp.concatenate([_unheads(dkm), _unheads_t(dkt)], axis=0),
                jnp.concatenate([_unheads(dvm), _unheads_t(dvt)], axis=0)], axis=1)
            gq, d_bqkv[j] = _wgrad(st['a'], dqkv, mode='col', colsum=True, name=f"qkv_wgrad{i}")
            pend('attn_w_qkv', j, gq)
            call = f"qkv_bwd{i}"
            res = _linear_t(dqkv, GW[('attn_w_qkv', j)], rms=(st['h'], gm, dh1), carry=reduce_carry(call), name=call)
            if call in REDUCE_PLAN:
                res, got = res
                land.update(zip(REDUCE_PLAN[call], got))
            dh, d_nmix[i] = res
        elif kind == 1:
            pend('conv_w_out', j, _wgrad(st['y'], dh1, mode='row', name=f"conv_out_wgrad{i}"))
            dy = _linear_t(dh1, st['w_cout'], out_dtype=F32, name=f"conv_out_bwd{i}")
            dz3, d_convk = _conv_bwd(st['z3'], conv_k, dy, f"conv_bwd{i}")
            pend('conv_w_in', j, _wgrad(st['a'], dz3, mode='col', name=f"conv_in_wgrad{i}"))
            dh, d_nmix[i] = _linear_t(dz3, GW[('conv_w_in', j)], rms=(st['h'], gm, dh1), name=f"conv_in_bwd{i}")
        else:
            dmix, dwp, d_pscale = _pool_out_bwd(dh1, st['mix'], st['w_pool'], p_scale, f"pool_out_bwd{i}")
            pend('pool_w', j, jnp.transpose(dwp.reshape(4, NSH, PG // NSH, PG), (1, 0, 2, 3)).astype(BF16))
            dh, d_nmix[i] = _pool_bwd(dmix, st['h'], gm, dh1, f"pool_bwd{i}")

    dts = _bias_lookup_bwd(jnp.concatenate(dflats, axis=0), idx, "bias_lookup_bwd")
    d_table = dts[:H, :N_BUCKETS].T + dts[H:, :N_BUCKETS].T
    d_sinks = dts[:, N_BUCKETS].reshape(2, H)

    ids = jnp.stack([chip, lax.axis_index("c")]).astype(jnp.int32)
    shared = _pair_share([_sum_leaf([land[(n, l)] for l in range(w[n].shape[0])],
                                    [pending[(n, l)] for l in range(w[n].shape[0])], ids, f"sum_{n}") for n in BIG])
    grads = {n: s.reshape(w[n].shape) for n, s in zip(BIG, shared)}

    small_full = [dh[:N_META], d_table, jnp.concatenate(d_nmix, axis=0), jnp.concatenate(d_nffn, axis=0), d_nfinal,
                  jnp.concatenate(d_bqkv, axis=0), jnp.concatenate(d_bo, axis=0), d_sinks, d_convk,
                  d_pscale, loss_part[:, 0:1]]
    red = _unpack(_allreduce_small(_pack(small_full)), [a.shape for a in small_full])
    g_meta, g_table, g_nmix, g_nffn, g_nfinal, g_bqkv, g_bo, g_sinks, g_convk, g_pscale, loss = red

    def shard(a, width):
        return lax.dynamic_slice_in_dim(a, chip * width, width, axis=1)

    grads.update(meta_tokens=shard(g_meta, 256), rel_bias_table=g_table, norm_mix=g_nmix, norm_ffn=g_nffn,
                 norm_final=g_nfinal.reshape(D), attn_b_qkv=shard(g_bqkv, 384), attn_b_o=shard(g_bo, 256),
                 attn_sinks=g_sinks, conv_w=shard(g_convk, 256)[None], pool_scale=shard(g_pscale, 256))

    delta, new_m, new_v = {}, {}, {}
    for n in BIG:
        shp = w[n].shape
        r3 = (int(np.prod(shp[:-2])),) + shp[-2:]
        swap = (lambda a: jnp.swapaxes(a, 1, 2)) if shp[-1] % 128 else (lambda a: a)
        res = _adam(*[swap(a.reshape(r3)) for a in (w[n], grads[n], mom[n], var[n])], f"adam_{n}")
        delta[n], new_m[n], new_v[n], grads[n] = [swap(a).reshape(shp) for a in res]
    shapes = [w[n].shape for n in SMALL]
    packed = [_pack([d[n] for n in SMALL])[None] for d in (w, grads, mom, var)]
    for dst, res in zip((delta, new_m, new_v), _adam(*packed, "adam_small")[:3]):
        dst.update(zip(SMALL, _unpack(res[0], shapes)))

    return (loss.reshape(()), dh[N_META:][None], *[grads[n] for n in WEIGHTS], *[delta[n] for n in WEIGHTS],
            *[new_m[n] for n in WEIGHTS], *[new_v[n] for n in WEIGHTS])
```
